```python
import math
import jax
import jax.numpy as jnp
from jax import lax
import numpy as np

D_MODEL = 1024
BATCH = 4
SEQ = 4096
DEPTH = 1

GRID_W = 64
CTX_LEN = 256
N_HEADS = 8
HEAD_DIM = 64
D_ATTN = N_HEADS * HEAD_DIM
D_HYENA = 512
HYENA_ORDER = 2
FILTER_HIDDEN = 64
POS_BANDS = 16
POS_FEATS = 1 + 2 * POS_BANDS
DECAY_TARGET = 1e-2
DECAY_FAST_PCT = 0.3
DECAY_SLOW_PCT = 1.5
D_FF = 2816
CONV_WIDTH = 3
WIN_ROWS = 8
WIN_COLS = 16
ROPE_BASE = 10000.0
RMS_EPS = 1e-6
N_BRANCHES = 2
N_MOD = 6
D_IN = 3 * D_ATTN + (HYENA_ORDER + 1) * D_HYENA + N_BRANCHES * D_MODEL

kernel_name = 'hybrid_na_hyena_convffn_dit_block'


def rmsnorm(x, g):
    xf = x.astype(jnp.float32)
    y = xf * lax.rsqrt(jnp.mean(xf * xf, axis=-1, keepdims=True) + RMS_EPS)
    return (y * g.astype(jnp.float32)).astype(x.dtype)


def adaln(cvec, w, b, n_chunks):
    m = (jax.nn.silu(cvec) @ w + b)[:, None, :]
    return jnp.split(m, n_chunks, axis=-1)


def modulate(x, shift, scale):
    return x * (1 + scale) + shift


def heads(t):
    return t.reshape(t.shape[0], t.shape[1], N_HEADS, HEAD_DIM)


def split_proj(p):
    q = p[..., :D_ATTN]
    k = p[..., D_ATTN:2 * D_ATTN]
    v = p[..., 2 * D_ATTN:3 * D_ATTN]
    hy = p[..., 3 * D_ATTN:3 * D_ATTN + (HYENA_ORDER + 1) * D_HYENA]
    gl = p[..., 3 * D_ATTN + (HYENA_ORDER + 1) * D_HYENA:]
    return q, k, v, hy, gl


def dwconv(u, w, b):
    pad = CONV_WIDTH // 2
    L = u.shape[1]
    up = jnp.pad(u, ((0, 0), (pad, pad), (0, 0)))
    return sum(up[:, j:j + L] * w[j] for j in range(CONV_WIDTH)) + b


def axial_rope(t):
    L, Dh = t.shape[1], t.shape[3]
    pos = jnp.arange(L)
    row = (pos // GRID_W).astype(jnp.float32)
    col = (pos % GRID_W).astype(jnp.float32)
    n_pairs = Dh // 4
    inv_freq = ROPE_BASE ** (-jnp.arange(n_pairs, dtype=jnp.float32) / n_pairs)

    def rot(u, p):
        ang = p[:, None] * inv_freq[None, :]
        cos = jnp.cos(ang)[None, :, None, :].astype(t.dtype)
        sin = jnp.sin(ang)[None, :, None, :].astype(t.dtype)
        u1, u2 = u[..., :n_pairs], u[..., n_pairs:]
        return jnp.concatenate([u1 * cos - u2 * sin, u2 * cos + u1 * sin], axis=-1)

    half = Dh // 2
    return jnp.concatenate([rot(t[..., :half], row), rot(t[..., half:], col)], axis=-1)


def neighbourhood_attention(q_rot, k_rot, v, q_plain, k_ctx, v_ctx, rpb):
    B, S, H, Dh = q_rot.shape
    rows = S // GRID_W
    win_r = min(WIN_ROWS, rows)
    n_nb = win_r * GRID_W
    scale = Dh ** -0.5

    def grid(t):
        return t.reshape(B, rows, GRID_W, H, Dh)

    k_grid, v_grid = grid(k_rot), grid(v)
    q_rows = jnp.moveaxis(grid(q_rot), 1, 0)
    qp_rows = jnp.moveaxis(grid(q_plain), 1, 0)
    qc = jnp.arange(GRID_W)[:, None]
    kc = jnp.arange(GRID_W)[None, :]
    c_start = jnp.clip(qc - WIN_COLS // 2, 0, GRID_W - WIN_COLS)
    col_in = (kc >= c_start) & (kc < c_start + WIN_COLS)
    dc_idx = jnp.clip(kc - qc, 1 - WIN_COLS, WIN_COLS - 1) + (WIN_COLS - 1)
    band_mask = jnp.broadcast_to(col_in[:, None, :], (GRID_W, win_r, GRID_W)).reshape(GRID_W, n_nb)

    def one_row(args):
        r, q_r, qp_r = args
        r_start = jnp.clip(r - win_r // 2, 0, rows - win_r)
        kb = lax.dynamic_slice_in_dim(k_grid, r_start, win_r, axis=1).reshape(B, n_nb, H, Dh)
        vb = lax.dynamic_slice_in_dim(v_grid, r_start, win_r, axis=1).reshape(B, n_nb, H, Dh)
        dr_idx = r_start + jnp.arange(win_r) - r + (WIN_ROWS - 1)
        bias = rpb[:, dr_idx[None, :, None], dc_idx[:, None, :]].reshape(H, GRID_W, n_nb)
        s_nb = jnp.einsum('bqhd,bkhd->bhqk', q_r, kb).astype(jnp.float32) * scale + bias.astype(jnp.float32)
        s_nb = jnp.where(band_mask, s_nb, -jnp.inf)
        s_cx = jnp.einsum('bqhd,bchd->bhqc', qp_r, k_ctx).astype(jnp.float32) * scale
        probs = jax.nn.softmax(jnp.concatenate([s_nb, s_cx], axis=-1), axis=-1).astype(v.dtype)
        return (jnp.einsum('bhqk,bkhd->bqhd', probs[..., :n_nb], vb)
                + jnp.einsum('bhqc,bchd->bqhd', probs[..., n_nb:], v_ctx))

    out = lax.map(one_row, (jnp.arange(rows), q_rows, qp_rows))
    return jnp.moveaxis(out, 0, 1).reshape(B, S, H * Dh)


def context_attention(q, k, v):
    B, C, H, Dh = q.shape
    s = jnp.einsum('bqhd,bkhd->bhqk', q, k).astype(jnp.float32) * (Dh ** -0.5)
    p = jax.nn.softmax(s, axis=-1).astype(v.dtype)
    return jnp.einsum('bhqk,bkhd->bqhd', p, v).reshape(B, C, H * Dh)


def hyena_filter(L, w1, b1, w2, b2, w3, b3, sin_freq, decay):
    pos = jnp.arange(L, dtype=jnp.float32)
    t = pos / max(L - 1, 1)
    bands = jnp.linspace(1e-4, POS_BANDS - 1, POS_BANDS, dtype=jnp.float32)
    ang = (2.0 * math.pi / L) * pos[:, None] * bands[None, :]
    z = jnp.concatenate([t[:, None], jnp.cos(ang), -jnp.sin(ang)], axis=-1)
    h = jnp.sin(sin_freq[0] * (z @ w1 + b1))
    h = jnp.sin(sin_freq[1] * (h @ w2 + b2))
    h = (h @ w3 + b3).reshape(L, 2, D_HYENA)
    h = (h * jnp.exp(-t[:, None, None] * jnp.abs(decay)[None])).astype(jnp.float32)
    k_circ = jnp.concatenate([h[:, 0], jnp.zeros((1, D_HYENA), jnp.float32), h[:0:-1, 1]], axis=0)
    return k_circ / jnp.sum(jnp.abs(k_circ), axis=0, keepdims=True)


def long_conv(u, k_circ, skip):
    L = u.shape[1]
    uf = u.astype(jnp.float32)
    spec = jnp.fft.rfft(uf, n=2 * L, axis=1) * jnp.fft.rfft(k_circ, axis=0)[None]
    y = jnp.fft.irfft(spec, n=2 * L, axis=1)[:, :L]
    return (y + uf * skip.astype(jnp.float32)).astype(u.dtype)


def hyena_branch(hy, conv_w, conv_b, k_circ, skip):
    x0, x1, v = jnp.split(dwconv(hy, conv_w, conv_b), HYENA_ORDER + 1, axis=-1)
    return x0 * long_conv(x1 * v, k_circ, skip)


def merge_branches(y_na, y_hy, gate_logits, w_o_na, w_o_hy, w_out):
    g_na, g_hy = jnp.split(jax.nn.sigmoid(gate_logits), N_BRANCHES, axis=-1)
    return (g_na * (y_na @ w_o_na) + g_hy * (y_hy @ w_o_hy)) @ w_out


def conv_ffn(h, w_up, conv_w, conv_b, w_down):
    u = dwconv(h @ w_up, conv_w, conv_b)
    a, val = jnp.split(u, 2, axis=-1)
    return (jax.nn.gelu(a) * val) @ w_down


def setup_inputs(seed: int = 0) -> dict:
    key = jax.random.key(seed)
    keys = jax.random.split(key, 32)
    counter = [0]

    def nk():
        k = keys[counter[0]]
        counter[0] += 1
        return k

    def nrm(shape, scale):
        return jax.random.normal(nk(), shape, jnp.float32) * scale

    L = DEPTH
    decay_lo = -math.log(DECAY_TARGET) / DECAY_SLOW_PCT
    decay_hi = -math.log(DECAY_TARGET) / DECAY_FAST_PCT
    return {
        'x': nrm((BATCH, SEQ, D_MODEL), 1.0),
        'c': nrm((BATCH, D_MODEL), 1.0),
        'ctx': nrm((BATCH, CTX_LEN, D_MODEL), 1.0),
        'c_ctx': nrm((D_MODEL,), 1.0),
        'w_mod': nrm((L, D_MODEL, N_MOD * D_MODEL), 0.5 * D_MODEL ** -0.5),
        'b_mod': nrm((L, N_MOD * D_MODEL), 0.02),
        'norm_mix_pre': 1.0 + nrm((L, D_MODEL), 0.1),
        'norm_mix_post': 1.0 + nrm((L, D_MODEL), 0.1),
        'norm_ffn_pre': 1.0 + nrm((L, D_MODEL), 0.1),
        'norm_ffn_post': 1.0 + nrm((L, D_MODEL), 0.1),
        'w_in': nrm((L, D_MODEL, D_IN), D_MODEL ** -0.5),
        'b_in': nrm((L, D_IN), 0.02),
        'na_rpb': nrm((L, N_HEADS, 2 * WIN_ROWS - 1, 2 * WIN_COLS - 1), 0.2),
        'hy_conv_w': nrm((L, CONV_WIDTH, (HYENA_ORDER + 1) * D_HYENA), CONV_WIDTH ** -0.5),
        'hy_conv_b': nrm((L, (HYENA_ORDER + 1) * D_HYENA), 0.02),
        'hy_filt_w1': nrm((L, POS_FEATS, FILTER_HIDDEN), POS_FEATS ** -0.5),
        'hy_filt_b1': nrm((L, FILTER_HIDDEN), 0.02),
        'hy_filt_w2': nrm((L, FILTER_HIDDEN, FILTER_HIDDEN), FILTER_HIDDEN ** -0.5),
        'hy_filt_b2': nrm((L, FILTER_HIDDEN), 0.02),
        'hy_filt_w3': nrm((L, FILTER_HIDDEN, 2 * D_HYENA), FILTER_HIDDEN ** -0.5),
        'hy_filt_b3': nrm((L, 2 * D_HYENA), 0.02),
        'hy_sin_freq': 1.0 + nrm((L, 2, FILTER_HIDDEN), 0.1),
        'hy_decay': jax.random.uniform(nk(), (L, 2, D_HYENA), jnp.float32, decay_lo, decay_hi),
        'hy_skip': nrm((L, D_HYENA), 1.0),
        'w_o_na': nrm((L, D_ATTN, D_MODEL), D_ATTN ** -0.5),
        'w_o_hy': nrm((L, D_HYENA, D_MODEL), D_HYENA ** -0.5),
        'w_out': nrm((L, D_MODEL, D_MODEL), D_MODEL ** -0.5),
        'ffn_w_up': nrm((L, D_MODEL, 2 * D_FF), D_MODEL ** -0.5),
        'ffn_conv_w': nrm((L, CONV_WIDTH, 2 * D_FF), CONV_WIDTH ** -0.5),
        'ffn_conv_b': nrm((L, 2 * D_FF), 0.02),
        'ffn_w_down': nrm((L, D_FF, D_MODEL), D_FF ** -0.5),
    }


def reference(x, c, ctx, c_ctx, w_mod, b_mod, norm_mix_pre, norm_mix_post, norm_ffn_pre, norm_ffn_post,
              w_in, b_in, na_rpb, hy_conv_w, hy_conv_b, hy_filt_w1, hy_filt_b1, hy_filt_w2, hy_filt_b2,
              hy_filt_w3, hy_filt_b3, hy_sin_freq, hy_decay, hy_skip, w_o_na, w_o_hy, w_out,
              ffn_w_up, ffn_conv_w, ffn_conv_b, ffn_w_down):
    seq_lat = x.shape[1]
    seq_ctx = ctx.shape[1]
    for layer in range(DEPTH):
        update_ctx = layer + 1 < DEPTH
        w_in_l, b_in_l = w_in[layer], b_in[layer]
        filt_params = (hy_filt_w1[layer], hy_filt_b1[layer], hy_filt_w2[layer], hy_filt_b2[layer],
                       hy_filt_w3[layer], hy_filt_b3[layer], hy_sin_freq[layer], hy_decay[layer])
        out_params = (w_o_na[layer], w_o_hy[layer], w_out[layer])
        ffn_params = (ffn_w_up[layer], ffn_conv_w[layer], ffn_conv_b[layer], ffn_w_down[layer])

        sh_mix, sc_mix, g_mix, sh_ffn, sc_ffn, g_ffn = adaln(c, w_mod[layer], b_mod[layer], N_MOD)
        n_ctx_mod = N_MOD if update_ctx else 2
        ctx_mod = adaln(c_ctx[None, :], w_mod[layer][:, :n_ctx_mod * D_MODEL],
                        b_mod[layer][:n_ctx_mod * D_MODEL], n_ctx_mod)

        hc = modulate(rmsnorm(ctx, norm_mix_pre[layer]), ctx_mod[0], ctx_mod[1])
        if update_ctx:
            q_c, k_c, v_c, hy_c, gl_c = split_proj(hc @ w_in_l + b_in_l)
        else:
            k_c, v_c = jnp.split(hc @ w_in_l[:, D_ATTN:3 * D_ATTN] + b_in_l[D_ATTN:3 * D_ATTN], 2, axis=-1)
        k_c, v_c = heads(k_c), heads(v_c)

        h = modulate(rmsnorm(x, norm_mix_pre[layer]), sh_mix, sc_mix)
        q, k, v, hy, gl = split_proj(h @ w_in_l + b_in_l)
        q, k, v = heads(q), heads(k), heads(v)
        y_na = neighbourhood_attention(axial_rope(q), axial_rope(k), v, q, k_c, v_c, na_rpb[layer])
        y_hy = hyena_branch(hy, hy_conv_w[layer], hy_conv_b[layer], hyena_filter(seq_lat, *filt_params), hy_skip[layer])
        x = x + g_mix * rmsnorm(merge_branches(y_na, y_hy, gl, *out_params), norm_mix_post[layer])

        h = modulate(rmsnorm(x, norm_ffn_pre[layer]), sh_ffn, sc_ffn)
        x = x + g_ffn * rmsnorm(conv_ffn(h, *ffn_params), norm_ffn_post[layer])

        if update_ctx:
            yc_na = context_attention(heads(q_c), k_c, v_c)
            yc_hy = hyena_branch(hy_c, hy_conv_w[layer], hy_conv_b[layer], hyena_filter(seq_ctx, *filt_params), hy_skip[layer])
            ctx = ctx + ctx_mod[2] * rmsnorm(merge_branches(yc_na, yc_hy, gl_c, *out_params), norm_mix_post[layer])
            hc = modulate(rmsnorm(ctx, norm_ffn_pre[layer]), ctx_mod[3], ctx_mod[4])
            ctx = ctx + ctx_mod[5] * rmsnorm(conv_ffn(hc, *ffn_params), norm_ffn_post[layer])
    return x
```

```python
import functools
import math

import jax
import jax.numpy as jnp
import numpy as np
from jax import lax
from jax.experimental import pallas as pl
from jax.experimental.pallas import tpu as pltpu

F32 = jnp.float32
BF16 = jnp.bfloat16

D_MODEL = 1024
N_HEADS = 8
HEAD_DIM = 64
D_ATTN = N_HEADS * HEAD_DIM
D_HYENA = 512
GRID_W = 64
WIN_ROWS = 8
WIN_COLS = 16
POS_BANDS = 16
POS_FEATS = 1 + 2 * POS_BANDS
FILTER_HIDDEN = 64
D_FF = 2816
N_MOD = 6
ROPE_BASE = 10000.0
RMS_EPS = 1e-6
NEG_BIAS = -1e30

LANES = 128
VMEM_LIMIT_BYTES = 56 * 1024 * 1024

FFT_N1 = 64
FFT_N2 = 128
FFT_N = FFT_N1 * FFT_N2
FFT_HALF_N1 = FFT_N1 // 2
FFT_PITCH = 2 * FFT_N1 + 8
FFT_K1_PER_STEP = 8
FFT_STEPS = FFT_N1 // FFT_K1_PER_STEP

TOKEN_TILE = 512
ATTN_ROWS_PER_STEP = 4
ATTN_Q = ATTN_ROWS_PER_STEP * GRID_W
ATTN_KEY_ROWS = 12
ATTN_KEYS = ATTN_KEY_ROWS * GRID_W
FF_CHUNK = 256
FFN_HALO = 8


def _cparams(sem):
    return pltpu.CompilerParams(dimension_semantics=sem, vmem_limit_bytes=VMEM_LIMIT_BYTES)


@functools.lru_cache(maxsize=None)
def _rope_tables(seq):
    pos = np.arange(seq)
    row = pos // GRID_W
    col = pos % GRID_W
    n_pairs = HEAD_DIM // 4
    inv = ROPE_BASE ** (-np.arange(n_pairs, dtype=np.float64) / n_pairs)
    lane = np.arange(LANES) % HEAD_DIM
    p = np.where(lane[None, :] < HEAD_DIM // 2, row[:, None], col[:, None]).astype(np.float64)
    ang = p * inv[lane % n_pairs][None, :]
    sign = np.where((lane % (2 * n_pairs)) < n_pairs, -1.0, 1.0)
    return np.cos(ang).astype(np.float32), (np.sin(ang) * sign[None, :]).astype(np.float32)


@functools.lru_cache(maxsize=None)
def _filter_tables(seq):
    assert 2 * seq == FFT_N
    rows = np.arange(FFT_N)
    n = FFT_N2 * (rows % FFT_N1) + rows // FFT_N1
    fwd = n < seq
    m = n - seq
    pos = np.where(fwd, n, seq - m).astype(np.float64)
    valid = fwd | (m >= 1)
    pos = np.where(valid, pos, 0.0)
    t = pos / max(seq - 1, 1)
    bands = np.linspace(1e-4, POS_BANDS - 1, POS_BANDS)
    ang = (2.0 * math.pi / seq) * pos[:, None] * bands[None, :]
    z = np.zeros((FFT_N, 64), np.float64)
    z[:, 0] = t
    z[:, 1:1 + POS_BANDS] = np.cos(ang)
    z[:, 1 + POS_BANDS:POS_FEATS] = -np.sin(ang)
    aux = np.zeros((FFT_N, 8), np.float64)
    aux[:, 0] = t
    aux[:, 1] = fwd
    aux[:, 2] = valid
    return z.astype(np.float32), aux.astype(np.float32)


def _realify(m):
    return np.block([[m.real, -m.imag], [m.imag, m.real]])


@functools.lru_cache(maxsize=None)
def _fft_matrices():
    n1 = np.arange(FFT_N1)
    n2 = np.arange(FFT_N2)
    k1 = np.arange(FFT_N1)
    k2 = np.arange(FFT_N2)
    f1 = np.exp(-2j * np.pi * np.outer(k1, n1) / FFT_N1)
    w1_data = _realify(f1[:, :FFT_HALF_N1])
    w1_real = np.concatenate([f1.real, f1.imag], axis=0)
    v1 = _realify(np.conj(f1.T)[:FFT_HALF_N1, :] / FFT_N)
    f2 = np.exp(-2j * np.pi * np.outer(k2, n2) / FFT_N2)
    tw = np.exp(-2j * np.pi * np.outer(k1, n2) / FFT_N)
    w2 = np.stack([_realify(f2 * tw[a][None, :]) for a in range(FFT_N1)])
    v2 = np.stack([_realify(np.conj(f2.T) * np.conj(tw[a])[:, None]) for a in range(FFT_N1)])
    pad = np.zeros((FFT_PITCH - 2 * FFT_N1, FFT_N1))
    w1_data = np.concatenate([w1_data, pad], axis=0)
    w1_real = np.concatenate([w1_real, pad], axis=0)
    return {k: v.astype(np.float32).astype(BF16) for k, v in
            dict(w1_data=w1_data, w1_real=w1_real, v1=v1, w2=w2, v2=v2).items()}


def _attn_bias_indices():
    rows = GRID_W
    groups = (0, 2, rows // ATTN_ROWS_PER_STEP - 1)
    dr = np.zeros((3, ATTN_ROWS_PER_STEP, ATTN_KEY_ROWS), np.int32)
    vr = np.zeros((3, ATTN_ROWS_PER_STEP, ATTN_KEY_ROWS), bool)
    for v, g in enumerate(groups):
        ws = min(max(ATTN_ROWS_PER_STEP * g - WIN_ROWS // 2, 0), rows - ATTN_KEY_ROWS)
        for i in range(ATTN_ROWS_PER_STEP):
            r = ATTN_ROWS_PER_STEP * g + i
            r_start = min(max(r - WIN_ROWS // 2, 0), rows - WIN_ROWS)
            for j in range(ATTN_KEY_ROWS):
                kr = ws + j
                ok = r_start <= kr < r_start + WIN_ROWS
                vr[v, i, j] = ok
                dr[v, i, j] = kr - r + (WIN_ROWS - 1) if ok else 0
    qc = np.arange(GRID_W)[:, None]
    kc = np.arange(GRID_W)[None, :]
    c_start = np.clip(qc - WIN_COLS // 2, 0, GRID_W - WIN_COLS)
    vc = (kc >= c_start) & (kc < c_start + WIN_COLS)
    dc = np.clip(kc - qc, 1 - WIN_COLS, WIN_COLS - 1) + (WIN_COLS - 1)
    return dr, vr, dc.astype(np.int32), vc


def _attn_bias_table(rpb):
    dr, vr, dc, vc = _attn_bias_indices()
    g = rpb[:, dr[:, :, None, :, None], dc[None, None, :, None, :]]
    ok = vr[:, :, None, :, None] & vc[None, None, :, None, :]
    g = jnp.where(ok[None], g, NEG_BIAS)
    g = jnp.transpose(g, (1, 0, 2, 3, 4, 5))
    return g.reshape(3, N_HEADS, ATTN_Q, ATTN_KEYS).astype(BF16)


def _mod_kernel(c_ref, w_ref, b_ref, o_ref):
    c = c_ref[...]
    s = c * jax.nn.sigmoid(c)
    o_ref[...] = jnp.dot(s, w_ref[...], precision=lax.Precision.HIGHEST,
                         preferred_element_type=F32) + b_ref[...]


def _norm_modulate(x, gain, shift, scale):
    ms = jnp.mean(x * x, axis=-1, keepdims=True)
    y = x * lax.rsqrt(ms + RMS_EPS) * gain
    return y * (1.0 + scale) + shift


def _rope(t, cos, sin_signed):
    n_pairs = HEAD_DIM // 4
    lane = lax.broadcasted_iota(jnp.int32, t.shape, 1)
    first = (lane % (2 * n_pairs)) < n_pairs
    partner = jnp.where(first, pltpu.roll(t, LANES - n_pairs, 1), pltpu.roll(t, n_pairs, 1))
    return t * cos + partner * sin_signed


def _in_proj_kernel(x_ref, mod_ref, g_ref, w_ref, b_ref, cos_ref, sin_ref,
                    qr_ref, qp_ref, kr_ref, v_ref, hy_ref, gt_ref):
    h = _norm_modulate(x_ref[0], g_ref[...], mod_ref[0, 0:1, :], mod_ref[0, 1:2, :]).astype(BF16)
    cos = cos_ref[...]
    sin = sin_ref[...]

    def proj(lo, hi):
        return jnp.dot(h, w_ref[:, lo:hi], preferred_element_type=F32) + b_ref[:, lo:hi]

    for c in range(D_ATTN // LANES):
        lo = c * LANES
        q = proj(lo, lo + LANES) * (HEAD_DIM ** -0.5)
        qp_ref[0, :, lo:lo + LANES] = q.astype(BF16)
        qr_ref[0, :, lo:lo + LANES] = _rope(q, cos, sin).astype(BF16)
        k = proj(D_ATTN + lo, D_ATTN + lo + LANES)
        kr_ref[0, :, lo:lo + LANES] = _rope(k, cos, sin).astype(BF16)
    v_ref[0] = proj(2 * D_ATTN, 3 * D_ATTN).astype(BF16)
    hy_lo = 3 * D_ATTN
    for c in range(3):
        hy_ref[0, :, c * D_HYENA:(c + 1) * D_HYENA] = proj(
            hy_lo + c * D_HYENA, hy_lo + (c + 1) * D_HYENA).astype(BF16)
    gl_lo = hy_lo + 3 * D_HYENA
    for c in range(4):
        w = D_MODEL // 2
        gt_ref[0, :, c * w:(c + 1) * w] = jax.nn.sigmoid(
            proj(gl_lo + c * w, gl_lo + (c + 1) * w)).astype(BF16)


def _ctx_kv_kernel(x_ref, mod_ref, g_ref, w_ref, b_ref, k_ref, v_ref):
    h = _norm_modulate(x_ref[0], g_ref[...], mod_ref[0:1, :], mod_ref[1:2, :]).astype(BF16)
    kv = jnp.dot(h, w_ref[...], preferred_element_type=F32) + b_ref[...]
    k_ref[0] = kv[:, :D_ATTN].astype(BF16)
    v_ref[0] = kv[:, D_ATTN:].astype(BF16)


def _attn_window_start(g):
    return jnp.clip(ATTN_ROWS_PER_STEP * g - WIN_ROWS // 2, 0, GRID_W - ATTN_KEY_ROWS)


def _attn_kernel(qr_ref, qp_ref, k_ref, v_ref, kc_ref, vc_ref, bias_ref, o_ref):
    g = pl.program_id(1)
    key0 = pl.multiple_of(_attn_window_start(g) * GRID_W, GRID_W)
    nt = (((1,), (1,)), ((), ()))
    quad_w = 4 * HEAD_DIM
    lane = lax.broadcasted_iota(jnp.int32, (1, quad_w), 1)
    for quad in range(N_HEADS // 4):
        ql = slice(quad * quad_w, (quad + 1) * quad_w)
        q_rot = qr_ref[0, :, ql]
        q_plain = qp_ref[0, :, ql]
        k_win = k_ref[0, pl.ds(key0, ATTN_KEYS), ql]
        v_win = v_ref[0, pl.ds(key0, ATTN_KEYS), ql]
        k_ctx = kc_ref[0, :, ql]
        v_ctx = vc_ref[0, :, ql]
        acc = jnp.zeros((ATTN_Q, quad_w), F32)
        for hh in range(4):
            head = quad * 4 + hh
            mine = (lane >= hh * HEAD_DIM) & (lane < (hh + 1) * HEAD_DIM)
            zero = jnp.zeros((), BF16)
            s_nb = lax.dot_general(jnp.where(mine, q_rot, zero), k_win, nt, preferred_element_type=F32)
            s_nb = s_nb + bias_ref[0, head].astype(F32)
            s_cx = lax.dot_general(jnp.where(mine, q_plain, zero), k_ctx, nt, preferred_element_type=F32)
            m = jnp.maximum(jnp.max(s_nb, axis=-1, keepdims=True), jnp.max(s_cx, axis=-1, keepdims=True))
            p_nb = jnp.exp(s_nb - m)
            p_cx = jnp.exp(s_cx - m)
            denom = jnp.sum(p_nb, axis=-1, keepdims=True) + jnp.sum(p_cx, axis=-1, keepdims=True)
            o = (jnp.dot(p_nb.astype(BF16), v_win, preferred_element_type=F32)
                 + jnp.dot(p_cx.astype(BF16), v_ctx, preferred_element_type=F32))
            acc = jnp.where(mine, o / denom, acc)
        o_ref[0, :, ql] = acc.astype(BF16)


def _shift_rows(x, first_row, last_row):
    n = x.shape[0]
    row = lax.broadcasted_iota(jnp.int32, x.shape, 0)
    prev = jnp.where(row == 0, first_row, pltpu.roll(x, 1, 0))
    nxt = jnp.where(row == n - 1, last_row, pltpu.roll(x, n - 1, 0))
    return prev, nxt


def _hy_pre_kernel(hy_ref, prev_ref, next_ref, w_ref, b_ref, u_ref, x0_ref):
    i = pl.program_id(1)
    n_tiles = pl.num_programs(1)
    x = hy_ref[0].astype(F32)
    halo = prev_ref.shape[1]
    first = jnp.where(i > 0, prev_ref[0, halo - 1:halo, :].astype(F32), 0.0)
    last = jnp.where(i < n_tiles - 1, next_ref[0, 0:1, :].astype(F32), 0.0)
    prev, nxt = _shift_rows(x, first, last)
    y = prev * w_ref[0:1, :] + x * w_ref[1:2, :] + nxt * w_ref[2:3, :] + b_ref[...]
    x0 = y[:, :D_HYENA].astype(BF16)
    u = (y[:, D_HYENA:2 * D_HYENA] * y[:, 2 * D_HYENA:]).astype(BF16)
    for j in range(TOKEN_TILE // FFT_N2):
        rows = slice(j * FFT_N2, (j + 1) * FFT_N2)
        cols = slice(j * D_HYENA, (j + 1) * D_HYENA)
        u_ref[0, :, cols] = u[rows]
        x0_ref[0, :, cols] = x0[rows]


def _filt_kernel(z_ref, aux_ref, w1_ref, b1_ref, w2_ref, b2_ref, w3_ref, b3_ref, freq_ref, decay_ref,
                 o_ref, asum_ref):
    i = pl.program_id(0)
    rows = z_ref.shape[0]
    hp = lax.Precision.HIGHEST
    h = jnp.sin(freq_ref[0:1, :] * (jnp.dot(z_ref[...], w1_ref[...], precision=hp,
                                            preferred_element_type=F32) + b1_ref[...]))
    h = jnp.sin(freq_ref[1:2, :] * (jnp.dot(h, w2_ref[...], precision=hp,
                                            preferred_element_type=F32) + b2_ref[...]))
    taps = jnp.dot(h, w3_ref[...], precision=hp, preferred_element_type=F32) + b3_ref[...]
    t = aux_ref[:, 0:1]
    is_fwd = aux_ref[:, 1:2] > 0.5
    valid = aux_ref[:, 2:3] > 0.5
    rate = jnp.abs(decay_ref[...])
    k_fwd = taps[:, :D_HYENA] * jnp.exp(-t * rate[0:1, :])
    k_bwd = taps[:, D_HYENA:] * jnp.exp(-t * rate[1:2, :])
    k = jnp.where(valid, jnp.where(is_fwd, k_fwd, k_bwd), 0.0)
    part = jnp.sum(jnp.abs(k), axis=0, keepdims=True)

    @pl.when(i == 0)
    def _():
        asum_ref[...] = jnp.zeros_like(asum_ref)

    asum_ref[...] += part
    o_ref[pl.ds(pl.multiple_of(i * rows, rows), rows), :] = k

    @pl.when(i == pl.num_programs(0) - 1)
    def _():
        o_ref[...] = o_ref[...] / asum_ref[...]


def _fft_stage1(load_group, w1_ref, a_ref):
    n_slabs = a_ref.shape[0]

    def body(n2, carry):
        x = load_group(n2)
        a = jnp.dot(w1_ref[...], x, preferred_element_type=F32)
        base = pl.multiple_of(n2 * FFT_PITCH, 8)
        for s in range(n_slabs):
            a_ref[s, pl.ds(base, FFT_PITCH), :] = a[:, s * LANES:(s + 1) * LANES]
        return carry

    lax.fori_loop(0, FFT_N2, body, 0)


def _fft_load_k1(a_ref, k1):
    parts = []
    for off in (0, FFT_N1):
        parts.append(jnp.concatenate(
            [a_ref[s, pl.ds(k1 + off, FFT_N2, stride=FFT_PITCH), :] for s in range(a_ref.shape[0])], axis=1))
    return jnp.concatenate(parts, axis=0)


def _filt_fft_kernel(k_ref, w1_ref, w2_ref, o_ref, a_ref):
    step = pl.program_id(1)

    @pl.when(step == 0)
    def _():
        def load_group(n2):
            return k_ref[pl.ds(pl.multiple_of(n2 * FFT_N1, FFT_N1), FFT_N1), :].astype(BF16)
        _fft_stage1(load_group, w1_ref, a_ref)

    for j in range(FFT_K1_PER_STEP):
        b = _fft_load_k1(a_ref, step * FFT_K1_PER_STEP + j).astype(BF16)
        o_ref[j] = jnp.dot(w2_ref[j], b, preferred_element_type=F32)


def _hy_conv_kernel(u_ref, kf_ref, w1_ref, w2_ref, v2_ref, v1_ref, o_ref, a_ref):
    step = pl.program_id(1)
    n_slabs = a_ref.shape[0]

    @pl.when(step == 0)
    def _():
        def load_group(n2):
            r0 = pl.multiple_of(n2 * FFT_N1, FFT_N1)
            return jnp.concatenate([u_ref[s, pl.ds(r0, FFT_N1), :] for s in range(n_slabs)], axis=1)
        _fft_stage1(load_group, w1_ref, a_ref)

    for j in range(FFT_K1_PER_STEP):
        k1 = step * FFT_K1_PER_STEP + j
        b = _fft_load_k1(a_ref, k1).astype(BF16)
        x = jnp.dot(w2_ref[j], b, preferred_element_type=F32)
        kf = kf_ref[j]
        kf = jnp.concatenate([kf] * n_slabs, axis=1)
        xr, xi = x[:FFT_N2], x[FFT_N2:]
        kr, ki = kf[:FFT_N2], kf[FFT_N2:]
        y = jnp.concatenate([xr * kr - xi * ki, xr * ki + xi * kr], axis=0).astype(BF16)
        d = jnp.dot(v2_ref[j], y, preferred_element_type=F32)
        for s in range(n_slabs):
            lanes = slice(s * LANES, (s + 1) * LANES)
            a_ref[s, pl.ds(k1, FFT_N2, stride=FFT_PITCH), :] = d[:FFT_N2, lanes]
            a_ref[s, pl.ds(k1 + FFT_N1, FFT_N2, stride=FFT_PITCH), :] = d[FFT_N2:, lanes]

    @pl.when(step == FFT_STEPS - 1)
    def _():
        def body(n2, carry):
            base = pl.multiple_of(n2 * FFT_PITCH, 8)
            d = jnp.concatenate([a_ref[s, pl.ds(base, 2 * FFT_N1), :] for s in range(n_slabs)], axis=1)
            y = jnp.dot(v1_ref[...], d.astype(BF16), preferred_element_type=F32).astype(BF16)
            r0 = pl.multiple_of(n2 * FFT_N1, FFT_N1)
            for s in range(n_slabs):
                o_ref[s, pl.ds(r0, FFT_N1), :] = y[:, s * LANES:(s + 1) * LANES]
            return carry

        lax.fori_loop(0, FFT_N2, body, 0)


def _from_fft_rows(ref):
    width = ref.shape[2] // (TOKEN_TILE // FFT_N2)
    return jnp.concatenate([ref[0, :, j * width:(j + 1) * width]
                            for j in range(TOKEN_TILE // FFT_N2)], axis=0)


def _merge_kernel(x_ref, yna_ref, x0_ref, u_ref, yc_ref, gt_ref, mod_ref, skip_ref, gpost_ref,
                  wna_ref, why_ref, wout_ref, o_ref):
    x0 = _from_fft_rows(x0_ref).astype(F32)
    u = _from_fft_rows(u_ref).astype(F32)
    yc = _from_fft_rows(yc_ref).astype(F32)
    y_hy = (x0 * (yc + u * skip_ref[...])).astype(BF16)
    a = jnp.dot(yna_ref[0], wna_ref[...], preferred_element_type=F32)
    b = jnp.dot(y_hy, why_ref[...], preferred_element_type=F32)
    g_na = gt_ref[0, :, :D_MODEL].astype(F32)
    g_hy = gt_ref[0, :, D_MODEL:].astype(F32)
    m = (g_na * a + g_hy * b).astype(BF16)
    o = jnp.dot(m, wout_ref[...], preferred_element_type=F32)
    ms = jnp.mean(o * o, axis=-1, keepdims=True)
    o = o * lax.rsqrt(ms + RMS_EPS) * gpost_ref[...]
    o_ref[0] = x_ref[0] + mod_ref[0, 2:3, :] * o


def _gelu_tanh(a):
    return 0.5 * a * (1.0 + jnp.tanh(math.sqrt(2.0 / math.pi) * (a + 0.044715 * (a * a * a))))


def _ffn_kernel(x_ref, prev_ref, next_ref, mod_ref, gpre_ref, gpost_ref, wup_ref, cw_ref, cb_ref, wdn_ref,
                o_ref, acc_ref):
    i = pl.program_id(1)
    n_tiles = pl.num_programs(1)
    x = x_ref[0]
    xx = jnp.concatenate([prev_ref[0], x, next_ref[0]], axis=0)
    h = _norm_modulate(xx, gpre_ref[...], mod_ref[0, 3:4, :], mod_ref[0, 4:5, :])
    row = lax.broadcasted_iota(jnp.int32, (xx.shape[0], 1), 0)
    inside = ((row >= FFN_HALO) | (i > 0)) & ((row < FFN_HALO + TOKEN_TILE) | (i < n_tiles - 1))
    h = jnp.where(inside, h, 0.0).astype(BF16)
    n_rows = xx.shape[0]
    n_chunks = D_FF // FF_CHUNK
    acc_ref[...] = jnp.zeros_like(acc_ref)

    def conv(half, c):
        idx = half * n_chunks + c
        u = jnp.dot(h, wup_ref[idx], preferred_element_type=F32)
        w = cw_ref[idx]
        y = (pltpu.roll(u, 1, 0) * w[0:1, :] + u * w[1:2, :] + pltpu.roll(u, n_rows - 1, 0) * w[2:3, :])
        return y[FFN_HALO:FFN_HALO + TOKEN_TILE] + cb_ref[idx]

    def body(c, carry):
        act = (_gelu_tanh(conv(0, c)) * conv(1, c)).astype(BF16)
        acc_ref[...] += jnp.dot(act, wdn_ref[c], preferred_element_type=F32)
        return carry

    lax.fori_loop(0, n_chunks, body, 0)
    y = acc_ref[...]
    ms = jnp.mean(y * y, axis=-1, keepdims=True)
    y = y * lax.rsqrt(ms + RMS_EPS) * gpost_ref[...]
    o_ref[0] = x + mod_ref[0, 5:6, :] * y


def _const_spec(shape):
    nd = len(shape)
    return pl.BlockSpec(shape, lambda *_: (0,) * nd)


def kernel(x, c, ctx, c_ctx, w_mod, b_mod, norm_mix_pre, norm_mix_post, norm_ffn_pre, norm_ffn_post, w_in, b_in, na_rpb, hy_conv_w, hy_conv_b, hy_filt_w1, hy_filt_b1, hy_filt_w2, hy_filt_b2, hy_filt_w3, hy_filt_b3, hy_sin_freq, hy_decay, hy_skip, w_o_na, w_o_hy, w_out, ffn_w_up, ffn_conv_w, ffn_conv_b, ffn_w_down):
    batch, seq, d = x.shape
    n_ctx = ctx.shape[1]
    assert d == D_MODEL and 2 * seq == FFT_N and seq == GRID_W * GRID_W and batch % 2 == 0
    assert w_mod.shape[0] == 1, "single-layer block"
    n_tiles = seq // TOKEN_TILE
    d_in = w_in.shape[2]
    row2 = lambda a: a.reshape(1, -1)

    c_all = jnp.zeros((8, d), F32).at[:batch].set(c).at[batch].set(c_ctx)
    mod_n = 1024
    mod = pl.pallas_call(
        _mod_kernel,
        grid=(N_MOD * d // mod_n,),
        in_specs=[_const_spec((8, d)),
                  pl.BlockSpec((d, mod_n), lambda j: (0, j)),
                  pl.BlockSpec((1, mod_n), lambda j: (0, j))],
        out_specs=pl.BlockSpec((8, mod_n), lambda j: (0, j)),
        out_shape=jax.ShapeDtypeStruct((8, N_MOD * d), F32),
        compiler_params=_cparams(("arbitrary",)),
        name="mod",
    )(c_all, w_mod[0], row2(b_mod[0]))
    mod_lat = jnp.pad(mod[:batch].reshape(batch, N_MOD, d), ((0, 0), (0, 8 - N_MOD), (0, 0)))
    mod_ctx = jnp.pad(mod[batch].reshape(N_MOD, d), ((0, 8 - N_MOD), (0, 0)))

    w_in_b = w_in[0].astype(BF16)
    b_in_r = row2(b_in[0])
    g_mix_pre = row2(norm_mix_pre[0])

    k_ctx, v_ctx = pl.pallas_call(
        _ctx_kv_kernel,
        grid=(batch,),
        in_specs=[pl.BlockSpec((1, n_ctx, d), lambda b: (b, 0, 0)),
                  _const_spec((8, d)), _const_spec((1, d)),
                  _const_spec((d, 2 * D_ATTN)), _const_spec((1, 2 * D_ATTN))],
        out_specs=[pl.BlockSpec((1, n_ctx, D_ATTN), lambda b: (b, 0, 0))] * 2,
        out_shape=[jax.ShapeDtypeStruct((batch, n_ctx, D_ATTN), BF16)] * 2,
        compiler_params=_cparams(("arbitrary",)),
        name="ctx_kv",
    )(ctx, mod_ctx, g_mix_pre, w_in_b[:, D_ATTN:3 * D_ATTN], b_in_r[:, D_ATTN:3 * D_ATTN])

    cos_t, sin_t = _rope_tables(seq)
    tok = lambda w: pl.BlockSpec((1, TOKEN_TILE, w), lambda b, i: (b, i, 0))
    mod_spec = pl.BlockSpec((1, 8, d), lambda b, i: (b, 0, 0))
    rope_spec = pl.BlockSpec((TOKEN_TILE, LANES), lambda b, i: (i, 0))
    q_rot, q_plain, k_rot, v_lat, hy, gates = pl.pallas_call(
        _in_proj_kernel,
        grid=(batch, n_tiles),
        in_specs=[tok(d), mod_spec, _const_spec((1, d)), _const_spec((d, d_in)), _const_spec((1, d_in)),
                  rope_spec, rope_spec],
        out_specs=[tok(D_ATTN)] * 4 + [tok(3 * D_HYENA), tok(2 * d)],
        out_shape=[jax.ShapeDtypeStruct((batch, seq, D_ATTN), BF16)] * 4
        + [jax.ShapeDtypeStruct((batch, seq, 3 * D_HYENA), BF16),
           jax.ShapeDtypeStruct((batch, seq, 2 * d), BF16)],
        compiler_params=_cparams(("arbitrary", "arbitrary")),
        name="in_proj",
    )(x, mod_lat, g_mix_pre, w_in_b, b_in_r, jnp.asarray(cos_t), jnp.asarray(sin_t))

    bias = _attn_bias_table(na_rpb[0])
    n_groups = GRID_W // ATTN_ROWS_PER_STEP
    q_spec = pl.BlockSpec((1, ATTN_Q, D_ATTN), lambda b, g: (b, g, 0))
    full = lambda n: pl.BlockSpec((1, n, D_ATTN), lambda b, g: (b, 0, 0))
    bias_spec = pl.BlockSpec(
        (1, N_HEADS, ATTN_Q, ATTN_KEYS),
        lambda b, g: ((g > 0).astype(jnp.int32) + (g == n_groups - 1).astype(jnp.int32), 0, 0, 0))
    y_na = pl.pallas_call(
        _attn_kernel,
        grid=(batch, n_groups),
        in_specs=[q_spec, q_spec, full(seq), full(seq), full(n_ctx), full(n_ctx), bias_spec],
        out_specs=q_spec,
        out_shape=jax.ShapeDtypeStruct((batch, seq, D_ATTN), BF16),
        compiler_params=_cparams(("arbitrary", "arbitrary")),
        name="attn",
    )(q_rot, q_plain, k_rot, v_lat, k_ctx, v_ctx, bias)

    halo = 16
    blocks_per_tile = TOKEN_TILE // halo
    n_halo_blocks = seq // halo
    tiles_per_pair_row = TOKEN_TILE // FFT_N2
    fft_cols = FFT_N1 * D_HYENA
    fft_spec = pl.BlockSpec((1, FFT_N2, tiles_per_pair_row * D_HYENA),
                            lambda b, i: (b // 2, 0, (b % 2) * n_tiles + i))
    fft_shape = jax.ShapeDtypeStruct((batch // 2, FFT_N2, fft_cols), BF16)
    u_f, x0_f = pl.pallas_call(
        _hy_pre_kernel,
        grid=(batch, n_tiles),
        in_specs=[tok(3 * D_HYENA),
                  pl.BlockSpec((1, halo, 3 * D_HYENA),
                               lambda b, i: (b, jnp.maximum(i * blocks_per_tile - 1, 0), 0)),
                  pl.BlockSpec((1, halo, 3 * D_HYENA),
                               lambda b, i: (b, jnp.minimum((i + 1) * blocks_per_tile, n_halo_blocks - 1), 0)),
                  _const_spec((3, 3 * D_HYENA)), _const_spec((1, 3 * D_HYENA))],
        out_specs=[fft_spec, fft_spec],
        out_shape=[fft_shape, fft_shape],
        compiler_params=_cparams(("arbitrary", "arbitrary")),
        name="hy_pre",
    )(hy, hy, hy, hy_conv_w[0], row2(hy_conv_b[0]))

    z_t, aux_t = _filter_tables(seq)
    filt_rows = 1024
    w1_pad = jnp.pad(hy_filt_w1[0], ((0, z_t.shape[1] - POS_FEATS), (0, 0)))
    k_circ = pl.pallas_call(
        _filt_kernel,
        grid=(FFT_N // filt_rows,),
        in_specs=[pl.BlockSpec((filt_rows, z_t.shape[1]), lambda i: (i, 0)),
                  pl.BlockSpec((filt_rows, aux_t.shape[1]), lambda i: (i, 0)),
                  _const_spec(w1_pad.shape), _const_spec((1, FILTER_HIDDEN)),
                  _const_spec((FILTER_HIDDEN, FILTER_HIDDEN)), _const_spec((1, FILTER_HIDDEN)),
                  _const_spec((FILTER_HIDDEN, 2 * D_HYENA)), _const_spec((1, 2 * D_HYENA)),
                  _const_spec((2, FILTER_HIDDEN)), _const_spec((2, D_HYENA))],
        out_specs=_const_spec((FFT_N, D_HYENA)),
        out_shape=jax.ShapeDtypeStruct((FFT_N, D_HYENA), F32),
        scratch_shapes=[pltpu.VMEM((1, D_HYENA), F32)],
        compiler_params=_cparams(("arbitrary",)),
        name="filt",
    )(jnp.asarray(z_t), jnp.asarray(aux_t), w1_pad, row2(hy_filt_b1[0]), hy_filt_w2[0], row2(hy_filt_b2[0]),
      hy_filt_w3[0], row2(hy_filt_b3[0]), hy_sin_freq[0], hy_decay[0])

    mats = _fft_matrices()
    w2_all = jnp.asarray(mats["w2"])
    v2_all = jnp.asarray(mats["v2"])
    step_mat_spec = pl.BlockSpec((FFT_K1_PER_STEP, 2 * FFT_N2, 2 * FFT_N2), lambda cb, s: (s, 0, 0))
    filt_slabs = 2
    kf = pl.pallas_call(
        _filt_fft_kernel,
        grid=(D_HYENA // (filt_slabs * LANES), FFT_STEPS),
        in_specs=[pl.BlockSpec((FFT_N, filt_slabs * LANES), lambda cb, s: (0, cb)),
                  _const_spec((FFT_PITCH, FFT_N1)), step_mat_spec],
        out_specs=pl.BlockSpec((FFT_K1_PER_STEP, 2 * FFT_N2, filt_slabs * LANES), lambda cb, s: (s, 0, cb)),
        out_shape=jax.ShapeDtypeStruct((FFT_N1, 2 * FFT_N2, D_HYENA), F32),
        scratch_shapes=[pltpu.VMEM((filt_slabs, FFT_N2 * FFT_PITCH, LANES), F32)],
        compiler_params=_cparams(("arbitrary", "arbitrary")),
        name="filt_fft",
    )(k_circ, jnp.asarray(mats["w1_real"]), w2_all)

    n_pairs = batch // 2
    u_rows = u_f.reshape(n_pairs, FFT_N, D_HYENA)
    pair_spec = pl.BlockSpec((n_pairs, FFT_N, LANES), lambda cb, s: (0, 0, cb))
    y_conv = pl.pallas_call(
        _hy_conv_kernel,
        grid=(D_HYENA // LANES, FFT_STEPS),
        in_specs=[pair_spec,
                  pl.BlockSpec((FFT_K1_PER_STEP, 2 * FFT_N2, LANES), lambda cb, s: (s, 0, cb)),
                  _const_spec((FFT_PITCH, FFT_N1)), step_mat_spec, step_mat_spec,
                  _const_spec((FFT_N1, 2 * FFT_N1))],
        out_specs=pair_spec,
        out_shape=jax.ShapeDtypeStruct((n_pairs, FFT_N, D_HYENA), BF16),
        scratch_shapes=[pltpu.VMEM((n_pairs, FFT_N2 * FFT_PITCH, LANES), F32)],
        compiler_params=_cparams(("arbitrary", "arbitrary")),
        name="hy_conv",
    )(u_rows, kf, jnp.asarray(mats["w1_data"]), w2_all, v2_all, jnp.asarray(mats["v1"]))
    y_conv = y_conv.reshape(n_pairs, FFT_N2, fft_cols)

    x1 = pl.pallas_call(
        _merge_kernel,
        grid=(batch, n_tiles),
        in_specs=[tok(d), tok(D_ATTN), fft_spec, fft_spec, fft_spec, tok(2 * d), mod_spec,
                  _const_spec((1, D_HYENA)), _const_spec((1, d)),
                  _const_spec((D_ATTN, d)), _const_spec((D_HYENA, d)), _const_spec((d, d))],
        out_specs=tok(d),
        out_shape=jax.ShapeDtypeStruct((batch, seq, d), F32),
        compiler_params=_cparams(("arbitrary", "arbitrary")),
        name="merge",
    )(x, y_na, x0_f, u_f, y_conv, gates, mod_lat, row2(hy_skip[0]), row2(norm_mix_post[0]),
      w_o_na[0].astype(BF16), w_o_hy[0].astype(BF16), w_out[0].astype(BF16))

    n_chunks = D_FF // FF_CHUNK
    chunked = lambda a: a.reshape(a.shape[0], 2 * n_chunks, FF_CHUNK).transpose(1, 0, 2)
    w_up_c = chunked(ffn_w_up[0].astype(BF16))
    conv_w_c = chunked(ffn_conv_w[0])
    conv_b_c = ffn_conv_b[0].reshape(2 * n_chunks, 1, FF_CHUNK)
    w_dn_c = ffn_w_down[0].astype(BF16).reshape(n_chunks, FF_CHUNK, d)
    ffn_blocks = TOKEN_TILE // FFN_HALO
    n_ffn_blocks = seq // FFN_HALO
    out = pl.pallas_call(
        _ffn_kernel,
        grid=(batch, n_tiles),
        in_specs=[tok(d),
                  pl.BlockSpec((1, FFN_HALO, d), lambda b, i: (b, jnp.maximum(i * ffn_blocks - 1, 0), 0)),
                  pl.BlockSpec((1, FFN_HALO, d),
                               lambda b, i: (b, jnp.minimum((i + 1) * ffn_blocks, n_ffn_blocks - 1), 0)),
                  mod_spec, _const_spec((1, d)), _const_spec((1, d)),
                  _const_spec(w_up_c.shape), _const_spec(conv_w_c.shape), _const_spec(conv_b_c.shape),
                  _const_spec(w_dn_c.shape)],
        out_specs=tok(d),
        out_shape=jax.ShapeDtypeStruct((batch, seq, d), F32),
        scratch_shapes=[pltpu.VMEM((TOKEN_TILE, d), F32)],
        compiler_params=_cparams(("arbitrary", "arbitrary")),
        name="ffn",
    )(x1, x1, x1, mod_lat, row2(norm_ffn_pre[0]), row2(norm_ffn_post[0]), w_up_c, conv_w_c, conv_b_c, w_dn_c)
    return out
```

```python
import functools
import math

import jax
import jax.numpy as jnp
import numpy as np
from jax import lax
from jax.experimental import pallas as pl
from jax.experimental.pallas import tpu as pltpu

F32 = jnp.float32
BF16 = jnp.bfloat16

D_MODEL = 1024
N_HEADS = 8
HEAD_DIM = 64
D_ATTN = N_HEADS * HEAD_DIM
D_HYENA = 512
GRID_W = 64
WIN_ROWS = 8
WIN_COLS = 16
POS_BANDS = 16
POS_FEATS = 1 + 2 * POS_BANDS
FILTER_HIDDEN = 64
D_FF = 2816
N_MOD = 6
ROPE_BASE = 10000.0
RMS_EPS = 1e-6
NEG_BIAS = -1e30

LANES = 128
VMEM_LIMIT_BYTES = 56 * 1024 * 1024

FFT_N1 = 64
FFT_N2 = 128
FFT_N = FFT_N1 * FFT_N2
FFT_HALF_N1 = FFT_N1 // 2
FFT_PITCH = 2 * FFT_N1 + 8
FFT_K1_PER_STEP = 8
FFT_STEPS = FFT_N1 // FFT_K1_PER_STEP

TOKEN_TILE = 512
SEQ_PITCH = FFT_N2 + 8
PITCHED_ROWS = FFT_HALF_N1 * SEQ_PITCH
TILE_GROUPS = TOKEN_TILE // FFT_N2
ATTN_ROWS_PER_STEP = 4
ATTN_Q = ATTN_ROWS_PER_STEP * GRID_W
ATTN_KEY_ROWS = 12
ATTN_KEYS = ATTN_KEY_ROWS * GRID_W
FF_CHUNK = 256
FFN_HALO = 8


def _cparams(sem):
    return pltpu.CompilerParams(dimension_semantics=sem, vmem_limit_bytes=VMEM_LIMIT_BYTES)


@functools.lru_cache(maxsize=None)
def _rope_tables(seq):
    pos = np.arange(seq)
    row = pos // GRID_W
    col = pos % GRID_W
    n_pairs = HEAD_DIM // 4
    inv = ROPE_BASE ** (-np.arange(n_pairs, dtype=np.float64) / n_pairs)
    lane = np.arange(LANES) % HEAD_DIM
    p = np.where(lane[None, :] < HEAD_DIM // 2, row[:, None], col[:, None]).astype(np.float64)
    ang = p * inv[lane % n_pairs][None, :]
    sign = np.where((lane % (2 * n_pairs)) < n_pairs, -1.0, 1.0)
    return np.cos(ang).astype(np.float32), (np.sin(ang) * sign[None, :]).astype(np.float32)


@functools.lru_cache(maxsize=None)
def _filter_tables(seq):
    assert 2 * seq == FFT_N
    half, n2, n1 = np.meshgrid(np.arange(2), np.arange(FFT_N2), np.arange(FFT_HALF_N1), indexing="ij")
    n = FFT_N2 * (half * FFT_HALF_N1 + n1) + n2
    fwd = n < seq
    m = n - seq
    valid = fwd | (m >= 1)
    pos = np.where(valid, np.where(fwd, n, seq - m), 0).astype(np.float64).reshape(2, seq)
    t = pos / max(seq - 1, 1)
    bands = np.linspace(1e-4, POS_BANDS - 1, POS_BANDS)
    ang = (2.0 * math.pi / seq) * pos[..., None] * bands
    z = np.zeros((2, seq, 64), np.float64)
    z[..., 0] = t
    z[..., 1:1 + POS_BANDS] = np.cos(ang)
    z[..., 1 + POS_BANDS:POS_FEATS] = -np.sin(ang)
    aux = np.zeros((2, seq, 8), np.float64)
    aux[..., 0] = t
    aux[..., 1] = valid.reshape(2, seq)
    return z.astype(np.float32), aux.astype(np.float32)


def _realify(m):
    return np.block([[m.real, -m.imag], [m.imag, m.real]])


@functools.lru_cache(maxsize=None)
def _fft_matrices():
    n1 = np.arange(FFT_N1)
    n2 = np.arange(FFT_N2)
    k1 = np.arange(FFT_N1)
    k2 = np.arange(FFT_N2)
    f1 = np.exp(-2j * np.pi * np.outer(k1, n1) / FFT_N1)
    w1_data = _realify(f1[:, :FFT_HALF_N1])
    w1_real = np.concatenate([f1.real, f1.imag], axis=0)
    v1 = _realify(np.conj(f1.T)[:FFT_HALF_N1, :] / FFT_N)
    f2 = np.exp(-2j * np.pi * np.outer(k2, n2) / FFT_N2)
    tw = np.exp(-2j * np.pi * np.outer(k1, n2) / FFT_N)
    w2 = np.stack([_realify(f2 * tw[a][None, :]) for a in range(FFT_N1)])
    v2 = np.stack([_realify(np.conj(f2.T) * np.conj(tw[a])[:, None]) for a in range(FFT_N1)])
    pad = np.zeros((FFT_PITCH - 2 * FFT_N1, FFT_N1))
    w1_data = np.concatenate([w1_data, pad], axis=0)
    w1_real = np.concatenate([w1_real, pad], axis=0)
    return {k: v.astype(np.float32) for k, v in
            dict(w1_data=w1_data, w1_real=w1_real, v1=v1, w2=w2, v2=v2).items()}


N_BIAS_ROWS = 2 * WIN_ROWS - 1
N_BIAS_COLS = 2 * WIN_COLS - 1


def _attn_bias_row_index():
    rows = GRID_W
    groups = (0, 2, rows // ATTN_ROWS_PER_STEP - 1)
    dr = np.full((3, ATTN_ROWS_PER_STEP, ATTN_KEY_ROWS), -1, np.int32)
    for v, g in enumerate(groups):
        ws = min(max(ATTN_ROWS_PER_STEP * g - WIN_ROWS // 2, 0), rows - ATTN_KEY_ROWS)
        for i in range(ATTN_ROWS_PER_STEP):
            r = ATTN_ROWS_PER_STEP * g + i
            r_start = min(max(r - WIN_ROWS // 2, 0), rows - WIN_ROWS)
            for j in range(ATTN_KEY_ROWS):
                kr = ws + j
                if r_start <= kr < r_start + WIN_ROWS:
                    dr[v, i, j] = kr - r + (WIN_ROWS - 1)
    return dr


def _attn_bias_kernel(rpb_ref, o_ref, t_ref):
    head = pl.program_id(0)
    qc = lax.broadcasted_iota(jnp.int32, (GRID_W, LANES), 0)
    lane = lax.broadcasted_iota(jnp.int32, (GRID_W, LANES), 1)
    kc = lane % GRID_W
    c_start = jnp.clip(qc - WIN_COLS // 2, 0, GRID_W - WIN_COLS)
    col_in = (kc >= c_start) & (kc < c_start + WIN_COLS)
    dc = jnp.clip(kc - qc, 1 - WIN_COLS, WIN_COLS - 1) + (WIN_COLS - 1)
    base = head * (N_BIAS_ROWS * N_BIAS_COLS)
    for r in range(N_BIAS_ROWS):
        t = jnp.full((GRID_W, LANES), NEG_BIAS, F32)
        for cidx in range(N_BIAS_COLS):
            t = jnp.where(col_in & (dc == cidx), rpb_ref[base + r * N_BIAS_COLS + cidx], t)
        t_ref[r] = t
    dr = _attn_bias_row_index()
    low_half = lane < GRID_W
    masked = jnp.full((GRID_W, LANES), NEG_BIAS, F32)
    for v in range(3):
        for i in range(ATTN_ROWS_PER_STEP):
            for j in range(0, ATTN_KEY_ROWS, 2):
                lo = t_ref[int(dr[v, i, j])] if dr[v, i, j] >= 0 else masked
                hi = t_ref[int(dr[v, i, j + 1])] if dr[v, i, j + 1] >= 0 else masked
                o_ref[v, 0, i * GRID_W:(i + 1) * GRID_W, j * GRID_W:(j + 2) * GRID_W] = jnp.where(
                    low_half, lo, hi).astype(BF16)


def _mod_kernel(c_ref, w_ref, b_ref, o_ref):
    c = c_ref[...]
    s = c * jax.nn.sigmoid(c)
    o_ref[...] = jnp.dot(s, w_ref[...], precision=lax.Precision.HIGHEST,
                         preferred_element_type=F32) + b_ref[...]


def _norm_modulate(x, gain, shift, scale):
    ms = jnp.mean(x * x, axis=-1, keepdims=True)
    y = x * lax.rsqrt(ms + RMS_EPS) * gain
    return y * (1.0 + scale) + shift


def _rope(t, cos, sin_signed):
    n_pairs = HEAD_DIM // 4
    lane = lax.broadcasted_iota(jnp.int32, t.shape, 1)
    first = (lane % (2 * n_pairs)) < n_pairs
    partner = jnp.where(first, pltpu.roll(t, LANES - n_pairs, 1), pltpu.roll(t, n_pairs, 1))
    return t * cos + partner * sin_signed


def _in_proj_kernel(x_ref, mod_ref, g_ref, w_ref, b_ref, cos_ref, sin_ref,
                    qr_ref, qp_ref, kr_ref, v_ref, hy_ref, gt_ref):
    h = _norm_modulate(x_ref[0], g_ref[...], mod_ref[0, 0:1, :], mod_ref[0, 1:2, :]).astype(BF16)
    cos = cos_ref[...]
    sin = sin_ref[...]

    def proj(lo, hi):
        return jnp.dot(h, w_ref[:, lo:hi], preferred_element_type=F32) + b_ref[:, lo:hi]

    for c in range(D_ATTN // LANES):
        lo = c * LANES
        q = proj(lo, lo + LANES) * (HEAD_DIM ** -0.5)
        qp_ref[0, :, lo:lo + LANES] = q.astype(BF16)
        qr_ref[0, :, lo:lo + LANES] = _rope(q, cos, sin).astype(BF16)
        k = proj(D_ATTN + lo, D_ATTN + lo + LANES)
        kr_ref[0, :, lo:lo + LANES] = _rope(k, cos, sin).astype(BF16)
    v_ref[0] = proj(2 * D_ATTN, 3 * D_ATTN).astype(BF16)
    hy_lo = 3 * D_ATTN
    for c in range(3):
        hy_ref[0, :, c * D_HYENA:(c + 1) * D_HYENA] = proj(
            hy_lo + c * D_HYENA, hy_lo + (c + 1) * D_HYENA).astype(BF16)
    gl_lo = hy_lo + 3 * D_HYENA
    for c in range(4):
        w = D_MODEL // 2
        gt_ref[0, :, c * w:(c + 1) * w] = jax.nn.sigmoid(
            proj(gl_lo + c * w, gl_lo + (c + 1) * w)).astype(BF16)


def _ctx_kv_kernel(x_ref, mod_ref, g_ref, w_ref, b_ref, k_ref, v_ref):
    h = _norm_modulate(x_ref[0], g_ref[...], mod_ref[0:1, :], mod_ref[1:2, :]).astype(BF16)
    kv = jnp.dot(h, w_ref[...], preferred_element_type=F32) + b_ref[...]
    k_ref[0] = kv[:, :D_ATTN].astype(BF16)
    v_ref[0] = kv[:, D_ATTN:].astype(BF16)


def _attn_window_start(g):
    return jnp.clip(ATTN_ROWS_PER_STEP * g - WIN_ROWS // 2, 0, GRID_W - ATTN_KEY_ROWS)


def _attn_kernel(qr_ref, qp_ref, k_ref, v_ref, kc_ref, vc_ref, bias_ref, o_ref):
    g = pl.program_id(1)
    key0 = pl.multiple_of(_attn_window_start(g) * GRID_W, GRID_W)
    nt = (((1,), (1,)), ((), ()))
    quad_w = 4 * HEAD_DIM
    lane = lax.broadcasted_iota(jnp.int32, (1, quad_w), 1)
    for quad in range(N_HEADS // 4):
        ql = slice(quad * quad_w, (quad + 1) * quad_w)
        q_rot = qr_ref[0, :, ql]
        q_plain = qp_ref[0, :, ql]
        k_win = k_ref[0, pl.ds(key0, ATTN_KEYS), ql]
        v_win = v_ref[0, pl.ds(key0, ATTN_KEYS), ql]
        k_ctx = kc_ref[0, :, ql]
        v_ctx = vc_ref[0, :, ql]
        acc = jnp.zeros((ATTN_Q, quad_w), F32)
        for hh in range(4):
            head = quad * 4 + hh
            mine = (lane >= hh * HEAD_DIM) & (lane < (hh + 1) * HEAD_DIM)
            zero = jnp.zeros((), BF16)
            s_nb = lax.dot_general(jnp.where(mine, q_rot, zero), k_win, nt, preferred_element_type=F32)
            s_nb = s_nb + bias_ref[0, head].astype(F32)
            s_cx = lax.dot_general(jnp.where(mine, q_plain, zero), k_ctx, nt, preferred_element_type=F32)
            m = jnp.maximum(jnp.max(s_nb, axis=-1, keepdims=True), jnp.max(s_cx, axis=-1, keepdims=True))
            p_nb = jnp.exp(s_nb - m)
            p_cx = jnp.exp(s_cx - m)
            denom = jnp.sum(p_nb, axis=-1, keepdims=True) + jnp.sum(p_cx, axis=-1, keepdims=True)
            o = (jnp.dot(p_nb.astype(BF16), v_win, preferred_element_type=F32)
                 + jnp.dot(p_cx.astype(BF16), v_ctx, preferred_element_type=F32))
            acc = jnp.where(mine, o / denom, acc)
        o_ref[0, :, ql] = acc.astype(BF16)


def _shift_rows(x, first_row, last_row):
    n = x.shape[0]
    row = lax.broadcasted_iota(jnp.int32, x.shape, 0)
    prev = jnp.where(row == 0, first_row, pltpu.roll(x, 1, 0))
    nxt = jnp.where(row == n - 1, last_row, pltpu.roll(x, n - 1, 0))
    return prev, nxt


def _hy_pre_kernel(hy_ref, prev_ref, next_ref, w_ref, b_ref, u_ref, x0_ref):
    i = pl.program_id(1)
    n_tiles = pl.num_programs(1)
    x = hy_ref[0].astype(F32)
    halo = prev_ref.shape[1]
    first = jnp.where(i > 0, prev_ref[0, halo - 1:halo, :].astype(F32), 0.0)
    last = jnp.where(i < n_tiles - 1, next_ref[0, 0:1, :].astype(F32), 0.0)
    prev, nxt = _shift_rows(x, first, last)
    y = prev * w_ref[0:1, :] + x * w_ref[1:2, :] + nxt * w_ref[2:3, :] + b_ref[...]
    x0_ref[0] = y[:, :D_HYENA].astype(BF16)
    u = y[:, D_HYENA:2 * D_HYENA] * y[:, 2 * D_HYENA:]
    for j in range(TILE_GROUPS):
        u_ref[0, j * SEQ_PITCH:j * SEQ_PITCH + FFT_N2, :] = u[j * FFT_N2:(j + 1) * FFT_N2]
        u_ref[0, j * SEQ_PITCH + FFT_N2:(j + 1) * SEQ_PITCH, :] = jnp.zeros((SEQ_PITCH - FFT_N2, D_HYENA), F32)


def _filt_kernel(z_ref, aux_ref, w1_ref, b1_ref, w2_ref, b2_ref, w3_ref, b3_ref, freq_ref, decay_ref,
                 o_ref, asum_ref):
    i = pl.program_id(1)
    hp = lax.Precision.HIGHEST
    h = jnp.sin(freq_ref[0:1, :] * (jnp.dot(z_ref[0], w1_ref[...], precision=hp,
                                            preferred_element_type=F32) + b1_ref[...]))
    h = jnp.sin(freq_ref[1:2, :] * (jnp.dot(h, w2_ref[...], precision=hp,
                                            preferred_element_type=F32) + b2_ref[...]))
    taps = jnp.dot(h, w3_ref[...], precision=hp, preferred_element_type=F32) + b3_ref[...]
    t = aux_ref[0, :, 0:1]
    valid = aux_ref[0, :, 1:2] > 0.5
    k = jnp.where(valid, taps * jnp.exp(-t * jnp.abs(decay_ref[0])), 0.0)
    part = jnp.sum(jnp.abs(k), axis=0, keepdims=True)

    @pl.when(i == 0)
    def _():
        asum_ref[...] = jnp.zeros_like(asum_ref)

    asum_ref[0] += jnp.broadcast_to(part, asum_ref.shape[1:])
    o_ref[...] = k.reshape(o_ref.shape)


def _fft_stage1(load_group, w1_ref, a_ref):
    n_slabs = a_ref.shape[0]

    def body(n2, carry):
        x = load_group(n2)
        a = jnp.dot(w1_ref[...], x, preferred_element_type=F32)
        base = pl.multiple_of(n2 * FFT_PITCH, 8)
        for s in range(n_slabs):
            a_ref[s, pl.ds(base, FFT_PITCH), :] = a[:, s * LANES:(s + 1) * LANES]
        return carry

    lax.fori_loop(0, FFT_N2, body, 0)


def _fft_load_k1(a_ref, k1):
    parts = []
    for off in (0, FFT_N1):
        parts.append(jnp.concatenate(
            [a_ref[s, pl.ds(k1 + off, FFT_N2, stride=FFT_PITCH), :] for s in range(a_ref.shape[0])], axis=1))
    return jnp.concatenate(parts, axis=0)


def _filt_fft_kernel(k_ref, asum_ref, w1_ref, w2_ref, o_ref, a_ref):
    step = pl.program_id(1)

    @pl.when(step == 0)
    def _():
        norm = asum_ref[0, 0:1, :] + asum_ref[1, 0:1, :]

        def load_group(n2):
            return (k_ref[pl.ds(pl.multiple_of(n2 * FFT_N1, FFT_N1), FFT_N1), :] / norm).astype(BF16)
        _fft_stage1(load_group, w1_ref, a_ref)

    for j in range(FFT_K1_PER_STEP):
        b = _fft_load_k1(a_ref, step * FFT_K1_PER_STEP + j).astype(BF16)
        o_ref[j] = jnp.dot(w2_ref[j], b, preferred_element_type=F32)


def _hy_conv_kernel(u_ref, kf_ref, w1_ref, w2_ref, v2_ref, v1_ref, o_ref, a_ref):
    step = pl.program_id(1)
    n_slabs = a_ref.shape[0]

    def tokens_of(member, s, n2):
        return (2 * s + member, pl.ds(n2, FFT_HALF_N1, stride=SEQ_PITCH), slice(None))

    @pl.when(step == 0)
    def _():
        o_ref[...] = jnp.zeros_like(o_ref)

        def load_group(n2):
            return jnp.concatenate(
                [jnp.concatenate([u_ref[tokens_of(member, s, n2)] for s in range(n_slabs)], axis=1)
                 for member in range(2)], axis=0).astype(BF16)
        _fft_stage1(load_group, w1_ref, a_ref)

    for j in range(FFT_K1_PER_STEP):
        k1 = step * FFT_K1_PER_STEP + j
        b = _fft_load_k1(a_ref, k1).astype(BF16)
        x = jnp.dot(w2_ref[j], b, preferred_element_type=F32)
        kf = kf_ref[j]
        kf = jnp.concatenate([kf] * n_slabs, axis=1)
        xr, xi = x[:FFT_N2], x[FFT_N2:]
        kr, ki = kf[:FFT_N2], kf[FFT_N2:]
        y = jnp.concatenate([xr * kr - xi * ki, xr * ki + xi * kr], axis=0).astype(BF16)
        d = jnp.dot(v2_ref[j], y, preferred_element_type=F32)
        for s in range(n_slabs):
            lanes = slice(s * LANES, (s + 1) * LANES)
            a_ref[s, pl.ds(k1, FFT_N2, stride=FFT_PITCH), :] = d[:FFT_N2, lanes]
            a_ref[s, pl.ds(k1 + FFT_N1, FFT_N2, stride=FFT_PITCH), :] = d[FFT_N2:, lanes]

    @pl.when(step == FFT_STEPS - 1)
    def _():
        def body(n2, carry):
            base = pl.multiple_of(n2 * FFT_PITCH, 8)
            d = jnp.concatenate([a_ref[s, pl.ds(base, 2 * FFT_N1), :] for s in range(n_slabs)], axis=1)
            y = jnp.dot(v1_ref[...], d.astype(BF16), preferred_element_type=F32)
            for s in range(n_slabs):
                for member in range(2):
                    o_ref[tokens_of(member, s, n2)] = y[member * FFT_HALF_N1:(member + 1) * FFT_HALF_N1,
                                                        s * LANES:(s + 1) * LANES]
            return carry

        lax.fori_loop(0, FFT_N2, body, 0)


def _from_pitched(ref):
    return jnp.concatenate([ref[0, j * SEQ_PITCH:j * SEQ_PITCH + FFT_N2, :] for j in range(TILE_GROUPS)], axis=0)


def _merge_kernel(x_ref, yna_ref, x0_ref, u_ref, yc_ref, gt_ref, mod_ref, skip_ref, gpost_ref,
                  wna_ref, why_ref, wout_ref, o_ref):
    y_hy = (x0_ref[0].astype(F32) * (_from_pitched(yc_ref) + _from_pitched(u_ref) * skip_ref[...])).astype(BF16)
    a = jnp.dot(yna_ref[0], wna_ref[...], preferred_element_type=F32)
    b = jnp.dot(y_hy, why_ref[...], preferred_element_type=F32)
    g_na = gt_ref[0, :, :D_MODEL].astype(F32)
    g_hy = gt_ref[0, :, D_MODEL:].astype(F32)
    m = (g_na * a + g_hy * b).astype(BF16)
    o = jnp.dot(m, wout_ref[...], preferred_element_type=F32)
    ms = jnp.mean(o * o, axis=-1, keepdims=True)
    o = o * lax.rsqrt(ms + RMS_EPS) * gpost_ref[...]
    o_ref[0] = x_ref[0] + mod_ref[0, 2:3, :] * o


def _gelu_tanh(a):
    return 0.5 * a * (1.0 + jnp.tanh(math.sqrt(2.0 / math.pi) * (a + 0.044715 * (a * a * a))))


def _ffn_kernel(x_ref, prev_ref, next_ref, mod_ref, gpre_ref, gpost_ref, wup_ref, cw_ref, cb_ref, wdn_ref,
                o_ref, acc_ref):
    i = pl.program_id(1)
    n_tiles = pl.num_programs(1)
    x = x_ref[0]
    xx = jnp.concatenate([prev_ref[0], x, next_ref[0]], axis=0)
    h = _norm_modulate(xx, gpre_ref[...], mod_ref[0, 3:4, :], mod_ref[0, 4:5, :])
    row = lax.broadcasted_iota(jnp.int32, (xx.shape[0], 1), 0)
    inside = ((row >= FFN_HALO) | (i > 0)) & ((row < FFN_HALO + TOKEN_TILE) | (i < n_tiles - 1))
    h = jnp.where(inside, h, 0.0).astype(BF16)
    n_rows = xx.shape[0]
    n_chunks = D_FF // FF_CHUNK
    acc_ref[...] = jnp.zeros_like(acc_ref)

    def conv(half, c):
        cols = slice(half * D_FF + c * FF_CHUNK, half * D_FF + (c + 1) * FF_CHUNK)
        u = jnp.dot(h, wup_ref[:, cols], preferred_element_type=F32)
        y = (pltpu.roll(u, 1, 0) * cw_ref[0:1, cols] + u * cw_ref[1:2, cols]
             + pltpu.roll(u, n_rows - 1, 0) * cw_ref[2:3, cols])
        return y[FFN_HALO:FFN_HALO + TOKEN_TILE] + cb_ref[:, cols]

    for c in range(n_chunks):
        act = (_gelu_tanh(conv(0, c)) * conv(1, c)).astype(BF16)
        acc_ref[...] += jnp.dot(act, wdn_ref[c * FF_CHUNK:(c + 1) * FF_CHUNK, :], preferred_element_type=F32)
    y = acc_ref[...]
    ms = jnp.mean(y * y, axis=-1, keepdims=True)
    y = y * lax.rsqrt(ms + RMS_EPS) * gpost_ref[...]
    o_ref[0] = x + mod_ref[0, 5:6, :] * y


def _const_spec(shape):
    nd = len(shape)
    return pl.BlockSpec(shape, lambda *_: (0,) * nd, pipeline_mode=pl.Buffered(1))


def kernel(x, c, ctx, c_ctx, w_mod, b_mod, norm_mix_pre, norm_mix_post, norm_ffn_pre, norm_ffn_post, w_in, b_in, na_rpb, hy_conv_w, hy_conv_b, hy_filt_w1, hy_filt_b1, hy_filt_w2, hy_filt_b2, hy_filt_w3, hy_filt_b3, hy_sin_freq, hy_decay, hy_skip, w_o_na, w_o_hy, w_out, ffn_w_up, ffn_conv_w, ffn_conv_b, ffn_w_down):
    batch, seq, d = x.shape
    n_ctx = ctx.shape[1]
    assert d == D_MODEL and 2 * seq == FFT_N and seq == GRID_W * GRID_W and batch % 2 == 0
    assert w_mod.shape[0] == 1, "single-layer block"
    n_tiles = seq // TOKEN_TILE
    d_in = w_in.shape[2]
    row2 = lambda a: a.reshape(1, -1)

    c_all = jnp.zeros((8, d), F32).at[:batch].set(c).at[batch].set(c_ctx)
    mod_n = 1024
    mod = pl.pallas_call(
        _mod_kernel,
        grid=(N_MOD * d // mod_n,),
        in_specs=[_const_spec((8, d)),
                  pl.BlockSpec((d, mod_n), lambda j: (0, j)),
                  pl.BlockSpec((1, mod_n), lambda j: (0, j))],
        out_specs=pl.BlockSpec((8, mod_n), lambda j: (0, j)),
        out_shape=jax.ShapeDtypeStruct((8, N_MOD * d), F32),
        compiler_params=_cparams(("arbitrary",)),
        name="mod",
    )(c_all, w_mod[0], row2(b_mod[0]))
    mod_lat = jnp.pad(mod[:batch].reshape(batch, N_MOD, d), ((0, 0), (0, 8 - N_MOD), (0, 0)))
    mod_ctx = jnp.pad(mod[batch].reshape(N_MOD, d), ((0, 8 - N_MOD), (0, 0)))

    w_in_b = w_in[0].astype(BF16)
    b_in_r = row2(b_in[0])
    g_mix_pre = row2(norm_mix_pre[0])

    k_ctx, v_ctx = pl.pallas_call(
        _ctx_kv_kernel,
        grid=(batch,),
        in_specs=[pl.BlockSpec((1, n_ctx, d), lambda b: (b, 0, 0)),
                  _const_spec((8, d)), _const_spec((1, d)),
                  _const_spec((d, 2 * D_ATTN)), _const_spec((1, 2 * D_ATTN))],
        out_specs=[pl.BlockSpec((1, n_ctx, D_ATTN), lambda b: (b, 0, 0))] * 2,
        out_shape=[jax.ShapeDtypeStruct((batch, n_ctx, D_ATTN), BF16)] * 2,
        compiler_params=_cparams(("arbitrary",)),
        name="ctx_kv",
    )(ctx, mod_ctx, g_mix_pre, w_in_b[:, D_ATTN:3 * D_ATTN], b_in_r[:, D_ATTN:3 * D_ATTN])

    cos_t, sin_t = _rope_tables(seq)
    tok = lambda w: pl.BlockSpec((1, TOKEN_TILE, w), lambda b, i: (b, i, 0))
    mod_spec = pl.BlockSpec((1, 8, d), lambda b, i: (b, 0, 0))
    rope_spec = pl.BlockSpec((TOKEN_TILE, LANES), lambda b, i: (i, 0))
    q_rot, q_plain, k_rot, v_lat, hy, gates = pl.pallas_call(
        _in_proj_kernel,
        grid=(batch, n_tiles),
        in_specs=[tok(d), mod_spec, _const_spec((1, d)), _const_spec((d, d_in)), _const_spec((1, d_in)),
                  rope_spec, rope_spec],
        out_specs=[tok(D_ATTN)] * 4 + [tok(3 * D_HYENA), tok(2 * d)],
        out_shape=[jax.ShapeDtypeStruct((batch, seq, D_ATTN), BF16)] * 4
        + [jax.ShapeDtypeStruct((batch, seq, 3 * D_HYENA), BF16),
           jax.ShapeDtypeStruct((batch, seq, 2 * d), BF16)],
        compiler_params=_cparams(("arbitrary", "arbitrary")),
        name="in_proj",
    )(x, mod_lat, g_mix_pre, w_in_b, b_in_r, jnp.asarray(cos_t), jnp.asarray(sin_t))

    bias = pl.pallas_call(
        _attn_bias_kernel,
        grid=(N_HEADS,),
        in_specs=[pl.BlockSpec(memory_space=pltpu.SMEM)],
        out_specs=pl.BlockSpec((3, 1, ATTN_Q, ATTN_KEYS), lambda h: (0, h, 0, 0)),
        out_shape=jax.ShapeDtypeStruct((3, N_HEADS, ATTN_Q, ATTN_KEYS), BF16),
        scratch_shapes=[pltpu.VMEM((N_BIAS_ROWS, GRID_W, LANES), F32)],
        compiler_params=_cparams(("arbitrary",)),
        name="attn_bias",
    )(na_rpb[0].reshape(-1))
    n_groups = GRID_W // ATTN_ROWS_PER_STEP
    q_spec = pl.BlockSpec((1, ATTN_Q, D_ATTN), lambda b, g: (b, g, 0))
    full = lambda n: pl.BlockSpec((1, n, D_ATTN), lambda b, g: (b, 0, 0))
    bias_spec = pl.BlockSpec(
        (1, N_HEADS, ATTN_Q, ATTN_KEYS),
        lambda b, g: ((g > 0).astype(jnp.int32) + (g == n_groups - 1).astype(jnp.int32), 0, 0, 0))
    y_na = pl.pallas_call(
        _attn_kernel,
        grid=(batch, n_groups),
        in_specs=[q_spec, q_spec, full(seq), full(seq), full(n_ctx), full(n_ctx), bias_spec],
        out_specs=q_spec,
        out_shape=jax.ShapeDtypeStruct((batch, seq, D_ATTN), BF16),
        compiler_params=_cparams(("arbitrary", "arbitrary")),
        name="attn",
    )(q_rot, q_plain, k_rot, v_lat, k_ctx, v_ctx, bias)

    halo = 16
    blocks_per_tile = TOKEN_TILE // halo
    n_halo_blocks = seq // halo
    pitched_spec = pl.BlockSpec((1, TILE_GROUPS * SEQ_PITCH, D_HYENA), lambda b, i: (b, i, 0))
    pitched_shape = jax.ShapeDtypeStruct((batch, PITCHED_ROWS, D_HYENA), F32)
    u_p, x0 = pl.pallas_call(
        _hy_pre_kernel,
        grid=(batch, n_tiles),
        in_specs=[tok(3 * D_HYENA),
                  pl.BlockSpec((1, halo, 3 * D_HYENA),
                               lambda b, i: (b, jnp.maximum(i * blocks_per_tile - 1, 0), 0)),
                  pl.BlockSpec((1, halo, 3 * D_HYENA),
                               lambda b, i: (b, jnp.minimum((i + 1) * blocks_per_tile, n_halo_blocks - 1), 0)),
                  _const_spec((3, 3 * D_HYENA)), _const_spec((1, 3 * D_HYENA))],
        out_specs=[pitched_spec, tok(D_HYENA)],
        out_shape=[pitched_shape, jax.ShapeDtypeStruct((batch, seq, D_HYENA), BF16)],
        compiler_params=_cparams(("arbitrary", "arbitrary")),
        name="hy_pre",
    )(hy, hy, hy, hy_conv_w[0], row2(hy_conv_b[0]))

    z_t, aux_t = _filter_tables(seq)
    filt_n2 = 32
    filt_rows = filt_n2 * FFT_HALF_N1
    w1_pad = jnp.pad(hy_filt_w1[0], ((0, z_t.shape[2] - POS_FEATS), (0, 0)))
    k_circ, k_asum = pl.pallas_call(
        _filt_kernel,
        grid=(2, seq // filt_rows),
        in_specs=[pl.BlockSpec((1, filt_rows, z_t.shape[2]), lambda hf, i: (hf, i, 0)),
                  pl.BlockSpec((1, filt_rows, aux_t.shape[2]), lambda hf, i: (hf, i, 0)),
                  _const_spec(w1_pad.shape), _const_spec((1, FILTER_HIDDEN)),
                  _const_spec((FILTER_HIDDEN, FILTER_HIDDEN)), _const_spec((1, FILTER_HIDDEN)),
                  pl.BlockSpec((FILTER_HIDDEN, D_HYENA), lambda hf, i: (0, hf)),
                  pl.BlockSpec((1, D_HYENA), lambda hf, i: (0, hf)),
                  _const_spec((2, FILTER_HIDDEN)),
                  pl.BlockSpec((1, 1, D_HYENA), lambda hf, i: (hf, 0, 0))],
        out_specs=[pl.BlockSpec((filt_n2, FFT_HALF_N1, D_HYENA), lambda hf, i: (i, hf, 0)),
                   pl.BlockSpec((1, 8, D_HYENA), lambda hf, i: (hf, 0, 0))],
        out_shape=[jax.ShapeDtypeStruct((FFT_N2, FFT_N1, D_HYENA), F32),
                   jax.ShapeDtypeStruct((2, 8, D_HYENA), F32)],
        compiler_params=_cparams(("arbitrary", "arbitrary")),
        name="filt",
    )(jnp.asarray(z_t), jnp.asarray(aux_t), w1_pad, row2(hy_filt_b1[0]), hy_filt_w2[0], row2(hy_filt_b2[0]),
      hy_filt_w3[0], row2(hy_filt_b3[0]), hy_sin_freq[0], hy_decay[0].reshape(2, 1, D_HYENA))
    k_circ = k_circ.reshape(FFT_N, D_HYENA)

    mats = _fft_matrices()
    mats = {k: jnp.asarray(v).astype(BF16) for k, v in mats.items()}
    w2_all = mats["w2"]
    v2_all = mats["v2"]
    step_mat_spec = pl.BlockSpec((FFT_K1_PER_STEP, 2 * FFT_N2, 2 * FFT_N2), lambda cb, s: (s, 0, 0))
    filt_slabs = 2
    kf = pl.pallas_call(
        _filt_fft_kernel,
        grid=(D_HYENA // (filt_slabs * LANES), FFT_STEPS),
        in_specs=[pl.BlockSpec((FFT_N, filt_slabs * LANES), lambda cb, s: (0, cb)),
                  pl.BlockSpec((2, 8, filt_slabs * LANES), lambda cb, s: (0, 0, cb)),
                  _const_spec((FFT_PITCH, FFT_N1)), step_mat_spec],
        out_specs=pl.BlockSpec((FFT_K1_PER_STEP, 2 * FFT_N2, filt_slabs * LANES), lambda cb, s: (s, 0, cb)),
        out_shape=jax.ShapeDtypeStruct((FFT_N1, 2 * FFT_N2, D_HYENA), F32),
        scratch_shapes=[pltpu.VMEM((filt_slabs, FFT_N2 * FFT_PITCH, LANES), F32)],
        compiler_params=_cparams(("arbitrary", "arbitrary")),
        name="filt_fft",
    )(k_circ, k_asum, mats["w1_real"], w2_all)

    n_pairs = batch // 2
    pair_spec = pl.BlockSpec((batch, PITCHED_ROWS, LANES), lambda cb, s: (0, 0, cb),
                             pipeline_mode=pl.Buffered(1))
    y_conv = pl.pallas_call(
        _hy_conv_kernel,
        grid=(D_HYENA // LANES, FFT_STEPS),
        in_specs=[pair_spec,
                  pl.BlockSpec((FFT_K1_PER_STEP, 2 * FFT_N2, LANES), lambda cb, s: (s, 0, cb)),
                  _const_spec((FFT_PITCH, FFT_N1)), step_mat_spec, step_mat_spec,
                  _const_spec((FFT_N1, 2 * FFT_N1))],
        out_specs=pair_spec,
        out_shape=pitched_shape,
        scratch_shapes=[pltpu.VMEM((n_pairs, FFT_N2 * FFT_PITCH, LANES), F32)],
        compiler_params=_cparams(("arbitrary", "arbitrary")),
        name="hy_conv",
    )(u_p, kf, mats["w1_data"], w2_all, v2_all, mats["v1"])

    x1 = pl.pallas_call(
        _merge_kernel,
        grid=(batch, n_tiles),
        in_specs=[tok(d), tok(D_ATTN), tok(D_HYENA), pitched_spec, pitched_spec, tok(2 * d), mod_spec,
                  _const_spec((1, D_HYENA)), _const_spec((1, d)),
                  _const_spec((D_ATTN, d)), _const_spec((D_HYENA, d)), _const_spec((d, d))],
        out_specs=tok(d),
        out_shape=jax.ShapeDtypeStruct((batch, seq, d), F32),
        compiler_params=_cparams(("arbitrary", "arbitrary")),
        name="merge",
    )(x, y_na, x0, u_p, y_conv, gates, mod_lat, row2(hy_skip[0]), row2(norm_mix_post[0]),
      w_o_na[0].astype(BF16), w_o_hy[0].astype(BF16), w_out[0].astype(BF16))

    w_up_c = ffn_w_up[0].astype(BF16)
    conv_w_c = ffn_conv_w[0]
    conv_b_c = row2(ffn_conv_b[0])
    w_dn_c = ffn_w_down[0].astype(BF16)
    ffn_blocks = TOKEN_TILE // FFN_HALO
    n_ffn_blocks = seq // FFN_HALO
    out = pl.pallas_call(
        _ffn_kernel,
        grid=(batch, n_tiles),
        in_specs=[tok(d),
                  pl.BlockSpec((1, FFN_HALO, d), lambda b, i: (b, jnp.maximum(i * ffn_blocks - 1, 0), 0)),
                  pl.BlockSpec((1, FFN_HALO, d),
                               lambda b, i: (b, jnp.minimum((i + 1) * ffn_blocks, n_ffn_blocks - 1), 0)),
                  mod_spec, _const_spec((1, d)), _const_spec((1, d)),
                  _const_spec(w_up_c.shape), _const_spec(conv_w_c.shape), _const_spec(conv_b_c.shape),
                  _const_spec(w_dn_c.shape)],
        out_specs=tok(d),
        out_shape=jax.ShapeDtypeStruct((batch, seq, d), F32),
        scratch_shapes=[pltpu.VMEM((TOKEN_TILE, d), F32)],
        compiler_params=_cparams(("arbitrary", "arbitrary")),
        name="ffn",
    )(x1, x1, x1, mod_lat, row2(norm_ffn_pre[0]), row2(norm_ffn_post[0]), w_up_c, conv_w_c, conv_b_c, w_dn_c)
    return out
```

```python
import functools
import math

import jax
import jax.numpy as jnp
import numpy as np
from jax import lax
from jax.experimental import pallas as pl
from jax.experimental.pallas import tpu as pltpu

F32 = jnp.float32
BF16 = jnp.bfloat16

D_MODEL = 1024
N_HEADS = 8
HEAD_DIM = 64
D_ATTN = N_HEADS * HEAD_DIM
D_HYENA = 512
GRID_W = 64
WIN_ROWS = 8
WIN_COLS = 16
POS_BANDS = 16
POS_FEATS = 1 + 2 * POS_BANDS
FILTER_HIDDEN = 64
D_FF = 2816
N_MOD = 6
ROPE_BASE = 10000.0
RMS_EPS = 1e-6
NEG_BIAS = -1e30

LANES = 128
VMEM_LIMIT_BYTES = 56 * 1024 * 1024

FFT_N1 = 64
FFT_N2 = 128
FFT_N = FFT_N1 * FFT_N2
FFT_HALF_N1 = FFT_N1 // 2
FFT_PITCH = 2 * FFT_N1 + 8
FFT_K1_PER_STEP = 8
FFT_UNROLL = 8
FFT_STEPS = FFT_N1 // FFT_K1_PER_STEP

TOKEN_TILE = 512
SEQ_PITCH = FFT_N2 + 8
PITCHED_ROWS = FFT_HALF_N1 * SEQ_PITCH
TILE_GROUPS = TOKEN_TILE // FFT_N2
ATTN_ROWS_PER_STEP = 4
ATTN_Q = ATTN_ROWS_PER_STEP * GRID_W
ATTN_KEY_ROWS = 12
ATTN_KEYS = ATTN_KEY_ROWS * GRID_W
FF_CHUNK = 256
FFN_HALO = 8
FFN_ROW_BLOCK = 64


def _cparams(sem):
    return pltpu.CompilerParams(dimension_semantics=sem, vmem_limit_bytes=VMEM_LIMIT_BYTES)


@functools.lru_cache(maxsize=None)
def _rope_tables(seq):
    pos = np.arange(seq)
    row = pos // GRID_W
    col = pos % GRID_W
    n_pairs = HEAD_DIM // 4
    inv = ROPE_BASE ** (-np.arange(n_pairs, dtype=np.float64) / n_pairs)
    lane = np.arange(LANES) % HEAD_DIM
    p = np.where(lane[None, :] < HEAD_DIM // 2, row[:, None], col[:, None]).astype(np.float64)
    ang = p * inv[lane % n_pairs][None, :]
    sign = np.where((lane % (2 * n_pairs)) < n_pairs, -1.0, 1.0)
    return np.cos(ang).astype(np.float32), (np.sin(ang) * sign[None, :]).astype(np.float32)


@functools.lru_cache(maxsize=None)
def _filter_tables(seq):
    assert 2 * seq == FFT_N
    half, n2, n1 = np.meshgrid(np.arange(2), np.arange(FFT_N2), np.arange(FFT_HALF_N1), indexing="ij")
    n = FFT_N2 * (half * FFT_HALF_N1 + n1) + n2
    fwd = n < seq
    m = n - seq
    valid = fwd | (m >= 1)
    pos = np.where(valid, np.where(fwd, n, seq - m), 0).astype(np.float64).reshape(2, seq)
    t = pos / max(seq - 1, 1)
    bands = np.linspace(1e-4, POS_BANDS - 1, POS_BANDS)
    ang = (2.0 * math.pi / seq) * pos[..., None] * bands
    z = np.zeros((2, seq, 64), np.float64)
    z[..., 0] = t
    z[..., 1:1 + POS_BANDS] = np.cos(ang)
    z[..., 1 + POS_BANDS:POS_FEATS] = -np.sin(ang)
    aux = np.zeros((2, seq, 8), np.float64)
    aux[..., 0] = t
    aux[..., 1] = valid.reshape(2, seq)
    return z.astype(np.float32), aux.astype(np.float32)


def _realify(m):
    return np.block([[m.real, -m.imag], [m.imag, m.real]])


@functools.lru_cache(maxsize=None)
def _fft_matrices():
    n1 = np.arange(FFT_N1)
    n2 = np.arange(FFT_N2)
    k1 = np.arange(FFT_N1)
    k2 = np.arange(FFT_N2)
    f1 = np.exp(-2j * np.pi * np.outer(k1, n1) / FFT_N1)
    w1_data = _realify(f1[:, :FFT_HALF_N1])
    w1_real = np.concatenate([f1.real, f1.imag], axis=0)
    v1 = _realify(np.conj(f1.T)[:FFT_HALF_N1, :] / FFT_N)
    f2 = np.exp(-2j * np.pi * np.outer(k2, n2) / FFT_N2)
    tw = np.exp(-2j * np.pi * np.outer(k1, n2) / FFT_N)
    w2 = np.stack([_realify(f2 * tw[a][None, :]) for a in range(FFT_N1)])
    v2 = np.stack([_realify(np.conj(f2.T) * np.conj(tw[a])[:, None]) for a in range(FFT_N1)])
    pad = np.zeros((FFT_PITCH - 2 * FFT_N1, FFT_N1))
    w1_data = np.concatenate([w1_data, pad], axis=0)
    w1_real = np.concatenate([w1_real, pad], axis=0)
    return {k: v.astype(np.float32) for k, v in
            dict(w1_data=w1_data, w1_real=w1_real, v1=v1, w2=w2, v2=v2).items()}


N_BIAS_ROWS = 2 * WIN_ROWS - 1
N_BIAS_COLS = 2 * WIN_COLS - 1


def _attn_bias_row_index():
    rows = GRID_W
    groups = (0, 2, rows // ATTN_ROWS_PER_STEP - 1)
    dr = np.full((3, ATTN_ROWS_PER_STEP, ATTN_KEY_ROWS), -1, np.int32)
    for v, g in enumerate(groups):
        ws = min(max(ATTN_ROWS_PER_STEP * g - WIN_ROWS // 2, 0), rows - ATTN_KEY_ROWS)
        for i in range(ATTN_ROWS_PER_STEP):
            r = ATTN_ROWS_PER_STEP * g + i
            r_start = min(max(r - WIN_ROWS // 2, 0), rows - WIN_ROWS)
            for j in range(ATTN_KEY_ROWS):
                kr = ws + j
                if r_start <= kr < r_start + WIN_ROWS:
                    dr[v, i, j] = kr - r + (WIN_ROWS - 1)
    return dr


def _attn_bias_kernel(rpb_ref, o_ref, t_ref):
    head = pl.program_id(0)
    qc = lax.broadcasted_iota(jnp.int32, (GRID_W, LANES), 0)
    lane = lax.broadcasted_iota(jnp.int32, (GRID_W, LANES), 1)
    kc = lane % GRID_W
    c_start = jnp.clip(qc - WIN_COLS // 2, 0, GRID_W - WIN_COLS)
    col_in = (kc >= c_start) & (kc < c_start + WIN_COLS)
    dc = jnp.clip(kc - qc, 1 - WIN_COLS, WIN_COLS - 1) + (WIN_COLS - 1)
    base = head * (N_BIAS_ROWS * N_BIAS_COLS)
    for r in range(N_BIAS_ROWS):
        t = jnp.full((GRID_W, LANES), NEG_BIAS, F32)
        for cidx in range(N_BIAS_COLS):
            t = jnp.where(col_in & (dc == cidx), rpb_ref[base + r * N_BIAS_COLS + cidx], t)
        t_ref[r] = t
    dr = _attn_bias_row_index()
    low_half = lane < GRID_W
    masked = jnp.full((GRID_W, LANES), NEG_BIAS, F32)
    for v in range(3):
        for i in range(ATTN_ROWS_PER_STEP):
            for j in range(0, ATTN_KEY_ROWS, 2):
                lo = t_ref[int(dr[v, i, j])] if dr[v, i, j] >= 0 else masked
                hi = t_ref[int(dr[v, i, j + 1])] if dr[v, i, j + 1] >= 0 else masked
                o_ref[v, 0, i * GRID_W:(i + 1) * GRID_W, j * GRID_W:(j + 2) * GRID_W] = jnp.where(
                    low_half, lo, hi).astype(BF16)


def _mod_kernel(c_ref, w_ref, b_ref, o_ref):
    c = c_ref[...]
    s = c * jax.nn.sigmoid(c)
    o_ref[...] = jnp.dot(s, w_ref[...], precision=lax.Precision.HIGHEST,
                         preferred_element_type=F32) + b_ref[...]


def _norm_modulate(x, gain, shift, scale):
    ms = jnp.mean(x * x, axis=-1, keepdims=True)
    y = x * lax.rsqrt(ms + RMS_EPS) * gain
    return y * (1.0 + scale) + shift


def _rope(t, cos, sin_signed):
    n_pairs = HEAD_DIM // 4
    lane = lax.broadcasted_iota(jnp.int32, t.shape, 1)
    first = (lane % (2 * n_pairs)) < n_pairs
    partner = jnp.where(first, pltpu.roll(t, LANES - n_pairs, 1), pltpu.roll(t, n_pairs, 1))
    return t * cos + partner * sin_signed


def _in_proj_kernel(x_ref, mod_ref, g_ref, w_ref, b_ref, cos_ref, sin_ref,
                    qr_ref, qp_ref, kr_ref, v_ref, hy_ref, gt_ref):
    h = _norm_modulate(x_ref[0], g_ref[...], mod_ref[0, 0:1, :], mod_ref[0, 1:2, :]).astype(BF16)
    cos = cos_ref[...]
    sin = sin_ref[...]

    def proj(lo, hi):
        return jnp.dot(h, w_ref[:, lo:hi], preferred_element_type=F32) + b_ref[:, lo:hi]

    for c in range(D_ATTN // LANES):
        lo = c * LANES
        q = proj(lo, lo + LANES) * (HEAD_DIM ** -0.5)
        qp_ref[0, :, lo:lo + LANES] = q.astype(BF16)
        qr_ref[0, :, lo:lo + LANES] = _rope(q, cos, sin).astype(BF16)
        k = proj(D_ATTN + lo, D_ATTN + lo + LANES)
        kr_ref[0, :, lo:lo + LANES] = _rope(k, cos, sin).astype(BF16)
    v_ref[0] = proj(2 * D_ATTN, 3 * D_ATTN).astype(BF16)
    hy_lo = 3 * D_ATTN
    for c in range(3):
        hy_ref[0, :, c * D_HYENA:(c + 1) * D_HYENA] = proj(
            hy_lo + c * D_HYENA, hy_lo + (c + 1) * D_HYENA).astype(BF16)
    gl_lo = hy_lo + 3 * D_HYENA
    for c in range(4):
        w = D_MODEL // 2
        gt_ref[0, :, c * w:(c + 1) * w] = jax.nn.sigmoid(
            proj(gl_lo + c * w, gl_lo + (c + 1) * w)).astype(BF16)


def _ctx_kv_kernel(x_ref, mod_ref, g_ref, w_ref, b_ref, k_ref, v_ref):
    h = _norm_modulate(x_ref[0], g_ref[...], mod_ref[0:1, :], mod_ref[1:2, :]).astype(BF16)
    kv = jnp.dot(h, w_ref[...], preferred_element_type=F32) + b_ref[...]
    k_ref[0] = kv[:, :D_ATTN].astype(BF16)
    v_ref[0] = kv[:, D_ATTN:].astype(BF16)


def _attn_window_start(g):
    return jnp.clip(ATTN_ROWS_PER_STEP * g - WIN_ROWS // 2, 0, GRID_W - ATTN_KEY_ROWS)


def _attn_kernel(qr_ref, qp_ref, k_ref, v_ref, kc_ref, vc_ref, bias_ref, o_ref):
    g = pl.program_id(1)
    key0 = pl.multiple_of(_attn_window_start(g) * GRID_W, GRID_W)
    nt = (((1,), (1,)), ((), ()))
    quad_w = 4 * HEAD_DIM
    lane = lax.broadcasted_iota(jnp.int32, (1, quad_w), 1)
    for quad in range(N_HEADS // 4):
        ql = slice(quad * quad_w, (quad + 1) * quad_w)
        q_rot = qr_ref[0, :, ql]
        q_plain = qp_ref[0, :, ql]
        k_win = k_ref[0, pl.ds(key0, ATTN_KEYS), ql]
        v_win = v_ref[0, pl.ds(key0, ATTN_KEYS), ql]
        k_ctx = kc_ref[0, :, ql]
        v_ctx = vc_ref[0, :, ql]
        acc = jnp.zeros((ATTN_Q, quad_w), F32)
        for hh in range(4):
            head = quad * 4 + hh
            mine = (lane >= hh * HEAD_DIM) & (lane < (hh + 1) * HEAD_DIM)
            zero = jnp.zeros((), BF16)
            s_nb = lax.dot_general(jnp.where(mine, q_rot, zero), k_win, nt, preferred_element_type=F32)
            s_nb = s_nb + bias_ref[0, head].astype(F32)
            s_cx = lax.dot_general(jnp.where(mine, q_plain, zero), k_ctx, nt, preferred_element_type=F32)
            m = jnp.maximum(jnp.max(s_nb, axis=-1, keepdims=True), jnp.max(s_cx, axis=-1, keepdims=True))
            p_nb = jnp.exp(s_nb - m)
            p_cx = jnp.exp(s_cx - m)
            denom = jnp.sum(p_nb, axis=-1, keepdims=True) + jnp.sum(p_cx, axis=-1, keepdims=True)
            o = (jnp.dot(p_nb.astype(BF16), v_win, preferred_element_type=F32)
                 + jnp.dot(p_cx.astype(BF16), v_ctx, preferred_element_type=F32))
            acc = jnp.where(mine, o / denom, acc)
        o_ref[0, :, ql] = acc.astype(BF16)


def _shift_rows(x, first_row, last_row):
    n = x.shape[0]
    row = lax.broadcasted_iota(jnp.int32, x.shape, 0)
    prev = jnp.where(row == 0, first_row, pltpu.roll(x, 1, 0))
    nxt = jnp.where(row == n - 1, last_row, pltpu.roll(x, n - 1, 0))
    return prev, nxt


def _hy_pre_kernel(hy_ref, prev_ref, next_ref, w_ref, b_ref, u_ref, x0_ref):
    i = pl.program_id(1)
    n_tiles = pl.num_programs(1)
    x = hy_ref[0].astype(F32)
    halo = prev_ref.shape[1]
    first = jnp.where(i > 0, prev_ref[0, halo - 1:halo, :].astype(F32), 0.0)
    last = jnp.where(i < n_tiles - 1, next_ref[0, 0:1, :].astype(F32), 0.0)
    prev, nxt = _shift_rows(x, first, last)
    y = prev * w_ref[0:1, :] + x * w_ref[1:2, :] + nxt * w_ref[2:3, :] + b_ref[...]
    x0_ref[0] = y[:, :D_HYENA].astype(BF16)
    u = y[:, D_HYENA:2 * D_HYENA] * y[:, 2 * D_HYENA:]
    for j in range(TILE_GROUPS):
        u_ref[0, j * SEQ_PITCH:j * SEQ_PITCH + FFT_N2, :] = u[j * FFT_N2:(j + 1) * FFT_N2]
        u_ref[0, j * SEQ_PITCH + FFT_N2:(j + 1) * SEQ_PITCH, :] = jnp.zeros((SEQ_PITCH - FFT_N2, D_HYENA), F32)


def _filt_kernel(z_ref, aux_ref, w1_ref, b1_ref, w2_ref, b2_ref, w3_ref, b3_ref, freq_ref, decay_ref,
                 o_ref, asum_ref):
    i = pl.program_id(1)
    hp = lax.Precision.HIGHEST
    h = jnp.sin(freq_ref[0:1, :] * (jnp.dot(z_ref[0], w1_ref[...], precision=hp,
                                            preferred_element_type=F32) + b1_ref[...]))
    h = jnp.sin(freq_ref[1:2, :] * (jnp.dot(h, w2_ref[...], precision=hp,
                                            preferred_element_type=F32) + b2_ref[...]))
    taps = jnp.dot(h, w3_ref[...], precision=hp, preferred_element_type=F32) + b3_ref[...]
    t = aux_ref[0, :, 0:1]
    valid = aux_ref[0, :, 1:2] > 0.5
    k = jnp.where(valid, taps * jnp.exp(-t * jnp.abs(decay_ref[0])), 0.0)
    part = jnp.sum(jnp.abs(k), axis=0, keepdims=True)

    @pl.when(i == 0)
    def _():
        asum_ref[...] = jnp.zeros_like(asum_ref)

    asum_ref[0] += jnp.broadcast_to(part, asum_ref.shape[1:])
    o_ref[...] = k.reshape(o_ref.shape)


def _fft_stage1(load_group, w1_ref, a_ref):
    n_slabs = a_ref.shape[0]

    def body(n2, carry):
        x = load_group(n2)
        a = jnp.dot(w1_ref[...], x, preferred_element_type=F32)
        base = pl.multiple_of(n2 * FFT_PITCH, 8)
        for s in range(n_slabs):
            a_ref[s, pl.ds(base, FFT_PITCH), :] = a[:, s * LANES:(s + 1) * LANES]
        return carry

    lax.fori_loop(0, FFT_N2, body, 0, unroll=FFT_UNROLL)


def _fft_load_k1(a_ref, k1):
    parts = []
    for off in (0, FFT_N1):
        parts.append(jnp.concatenate(
            [a_ref[s, pl.ds(k1 + off, FFT_N2, stride=FFT_PITCH), :] for s in range(a_ref.shape[0])], axis=1))
    return jnp.concatenate(parts, axis=0)


def _filt_fft_kernel(k_ref, asum_ref, w1_ref, w2_ref, o_ref, a_ref):
    step = pl.program_id(1)

    @pl.when(step == 0)
    def _():
        norm = asum_ref[0, 0:1, :] + asum_ref[1, 0:1, :]

        def load_group(n2):
            return (k_ref[pl.ds(pl.multiple_of(n2 * FFT_N1, FFT_N1), FFT_N1), :] / norm).astype(BF16)
        _fft_stage1(load_group, w1_ref, a_ref)

    for j in range(FFT_K1_PER_STEP):
        b = _fft_load_k1(a_ref, step * FFT_K1_PER_STEP + j).astype(BF16)
        o_ref[j] = jnp.dot(w2_ref[j], b, preferred_element_type=F32)


def _hy_conv_kernel(u_ref, kf_ref, w1_ref, w2_ref, v2_ref, v1_ref, o_ref, a_ref):
    step = pl.program_id(1)
    n_slabs = a_ref.shape[0]

    def tokens_of(member, s, n2):
        return (2 * s + member, pl.ds(n2, FFT_HALF_N1, stride=SEQ_PITCH), slice(None))

    @pl.when(step == 0)
    def _():
        o_ref[...] = jnp.zeros_like(o_ref)

        def load_group(n2):
            return jnp.concatenate(
                [jnp.concatenate([u_ref[tokens_of(member, s, n2)] for s in range(n_slabs)], axis=1)
                 for member in range(2)], axis=0).astype(BF16)
        _fft_stage1(load_group, w1_ref, a_ref)

    for j in range(FFT_K1_PER_STEP):
        k1 = step * FFT_K1_PER_STEP + j
        b = _fft_load_k1(a_ref, k1).astype(BF16)
        x = jnp.dot(w2_ref[j], b, preferred_element_type=F32)
        kf = kf_ref[j]
        kf = jnp.concatenate([kf] * n_slabs, axis=1)
        xr, xi = x[:FFT_N2], x[FFT_N2:]
        kr, ki = kf[:FFT_N2], kf[FFT_N2:]
        y = jnp.concatenate([xr * kr - xi * ki, xr * ki + xi * kr], axis=0).astype(BF16)
        d = jnp.dot(v2_ref[j], y, preferred_element_type=F32)
        for s in range(n_slabs):
            lanes = slice(s * LANES, (s + 1) * LANES)
            a_ref[s, pl.ds(k1, FFT_N2, stride=FFT_PITCH), :] = d[:FFT_N2, lanes]
            a_ref[s, pl.ds(k1 + FFT_N1, FFT_N2, stride=FFT_PITCH), :] = d[FFT_N2:, lanes]

    @pl.when(step == FFT_STEPS - 1)
    def _():
        def body(n2, carry):
            base = pl.multiple_of(n2 * FFT_PITCH, 8)
            d = jnp.concatenate([a_ref[s, pl.ds(base, 2 * FFT_N1), :] for s in range(n_slabs)], axis=1)
            y = jnp.dot(v1_ref[...], d.astype(BF16), preferred_element_type=F32)
            for s in range(n_slabs):
                for member in range(2):
                    o_ref[tokens_of(member, s, n2)] = y[member * FFT_HALF_N1:(member + 1) * FFT_HALF_N1,
                                                        s * LANES:(s + 1) * LANES]
            return carry

        lax.fori_loop(0, FFT_N2, body, 0, unroll=FFT_UNROLL)


def _from_pitched(ref):
    return jnp.concatenate([ref[0, j * SEQ_PITCH:j * SEQ_PITCH + FFT_N2, :] for j in range(TILE_GROUPS)], axis=0)


def _merge_kernel(x_ref, yna_ref, x0_ref, u_ref, yc_ref, gt_ref, mod_ref, skip_ref, gpost_ref,
                  wna_ref, why_ref, wout_ref, o_ref):
    y_hy = (x0_ref[0].astype(F32) * (_from_pitched(yc_ref) + _from_pitched(u_ref) * skip_ref[...])).astype(BF16)
    a = jnp.dot(yna_ref[0], wna_ref[...], preferred_element_type=F32)
    b = jnp.dot(y_hy, why_ref[...], preferred_element_type=F32)
    g_na = gt_ref[0, :, :D_MODEL].astype(F32)
    g_hy = gt_ref[0, :, D_MODEL:].astype(F32)
    m = (g_na * a + g_hy * b).astype(BF16)
    o = jnp.dot(m, wout_ref[...], preferred_element_type=F32)
    ms = jnp.mean(o * o, axis=-1, keepdims=True)
    o = o * lax.rsqrt(ms + RMS_EPS) * gpost_ref[...]
    o_ref[0] = x_ref[0] + mod_ref[0, 2:3, :] * o


def _gelu_tanh(a):
    return 0.5 * a * (1.0 + jnp.tanh(math.sqrt(2.0 / math.pi) * (a + 0.044715 * (a * a * a))))


def _ffn_kernel(x_ref, prev_ref, next_ref, mod_ref, gpre_ref, gpost_ref, wup_ref, cw_ref, cb_ref, wdn_ref,
                o_ref, u_ref, act_ref):
    i = pl.program_id(1)
    n_tiles = pl.num_programs(1)
    x = x_ref[0]
    xx = jnp.concatenate([prev_ref[0], x, next_ref[0]], axis=0)
    h = _norm_modulate(xx, gpre_ref[...], mod_ref[0, 3:4, :], mod_ref[0, 4:5, :])
    row = lax.broadcasted_iota(jnp.int32, (xx.shape[0], 1), 0)
    inside = ((row >= FFN_HALO) | (i > 0)) & ((row < FFN_HALO + TOKEN_TILE) | (i < n_tiles - 1))
    h = jnp.where(inside, h, 0.0).astype(BF16)
    n_chunks = D_FF // FF_CHUNK
    n_slabs = FF_CHUNK // LANES

    def conv(buf, half, s, lo, r0):
        lanes = slice(half * D_FF + lo, half * D_FF + lo + LANES)
        tap = lambda j: u_ref[buf, half, s, pl.ds(FFN_HALO - 1 + j + r0, FFN_ROW_BLOCK, stride=1), :]
        return (tap(0) * cw_ref[0:1, lanes] + tap(1) * cw_ref[1:2, lanes] + tap(2) * cw_ref[2:3, lanes]
                + cb_ref[:, lanes])

    for c in range(n_chunks):
        buf = c % 2
        for half in range(2):
            lo = half * D_FF + c * FF_CHUNK
            u = jnp.dot(h, wup_ref[:, lo:lo + FF_CHUNK], preferred_element_type=F32)
            for s in range(n_slabs):
                u_ref[buf, half, s] = u[:, s * LANES:(s + 1) * LANES]
        for s in range(n_slabs):
            lo = c * FF_CHUNK + s * LANES
            for r0 in range(0, TOKEN_TILE, FFN_ROW_BLOCK):
                act_ref[r0:r0 + FFN_ROW_BLOCK, lo:lo + LANES] = (
                    _gelu_tanh(conv(buf, 0, s, lo, r0)) * conv(buf, 1, s, lo, r0)).astype(BF16)
    y = jnp.dot(act_ref[...], wdn_ref[...], preferred_element_type=F32)
    ms = jnp.mean(y * y, axis=-1, keepdims=True)
    y = y * lax.rsqrt(ms + RMS_EPS) * gpost_ref[...]
    o_ref[0] = x + mod_ref[0, 5:6, :] * y


def _const_spec(shape):
    nd = len(shape)
    return pl.BlockSpec(shape, lambda *_: (0,) * nd, pipeline_mode=pl.Buffered(1))


def kernel(x, c, ctx, c_ctx, w_mod, b_mod, norm_mix_pre, norm_mix_post, norm_ffn_pre, norm_ffn_post, w_in, b_in, na_rpb, hy_conv_w, hy_conv_b, hy_filt_w1, hy_filt_b1, hy_filt_w2, hy_filt_b2, hy_filt_w3, hy_filt_b3, hy_sin_freq, hy_decay, hy_skip, w_o_na, w_o_hy, w_out, ffn_w_up, ffn_conv_w, ffn_conv_b, ffn_w_down):
    batch, seq, d = x.shape
    n_ctx = ctx.shape[1]
    assert d == D_MODEL and 2 * seq == FFT_N and seq == GRID_W * GRID_W and batch % 2 == 0
    assert w_mod.shape[0] == 1, "single-layer block"
    n_tiles = seq // TOKEN_TILE
    d_in = w_in.shape[2]
    row2 = lambda a: a.reshape(1, -1)

    c_all = jnp.zeros((8, d), F32).at[:batch].set(c).at[batch].set(c_ctx)
    mod_n = 1024
    mod = pl.pallas_call(
        _mod_kernel,
        grid=(N_MOD * d // mod_n,),
        in_specs=[_const_spec((8, d)),
                  pl.BlockSpec((d, mod_n), lambda j: (0, j)),
                  pl.BlockSpec((1, mod_n), lambda j: (0, j))],
        out_specs=pl.BlockSpec((8, mod_n), lambda j: (0, j)),
        out_shape=jax.ShapeDtypeStruct((8, N_MOD * d), F32),
        compiler_params=_cparams(("arbitrary",)),
        name="mod",
    )(c_all, w_mod[0], row2(b_mod[0]))
    mod_lat = jnp.pad(mod[:batch].reshape(batch, N_MOD, d), ((0, 0), (0, 8 - N_MOD), (0, 0)))
    mod_ctx = jnp.pad(mod[batch].reshape(N_MOD, d), ((0, 8 - N_MOD), (0, 0)))

    w_in_b = w_in[0].astype(BF16)
    b_in_r = row2(b_in[0])
    g_mix_pre = row2(norm_mix_pre[0])

    k_ctx, v_ctx = pl.pallas_call(
        _ctx_kv_kernel,
        grid=(batch,),
        in_specs=[pl.BlockSpec((1, n_ctx, d), lambda b: (b, 0, 0)),
                  _const_spec((8, d)), _const_spec((1, d)),
                  _const_spec((d, 2 * D_ATTN)), _const_spec((1, 2 * D_ATTN))],
        out_specs=[pl.BlockSpec((1, n_ctx, D_ATTN), lambda b: (b, 0, 0))] * 2,
        out_shape=[jax.ShapeDtypeStruct((batch, n_ctx, D_ATTN), BF16)] * 2,
        compiler_params=_cparams(("arbitrary",)),
        name="ctx_kv",
    )(ctx, mod_ctx, g_mix_pre, w_in_b[:, D_ATTN:3 * D_ATTN], b_in_r[:, D_ATTN:3 * D_ATTN])

    cos_t, sin_t = _rope_tables(seq)
    tok = lambda w: pl.BlockSpec((1, TOKEN_TILE, w), lambda b, i: (b, i, 0))
    mod_spec = pl.BlockSpec((1, 8, d), lambda b, i: (b, 0, 0))
    rope_spec = pl.BlockSpec((TOKEN_TILE, LANES), lambda b, i: (i, 0))
    q_rot, q_plain, k_rot, v_lat, hy, gates = pl.pallas_call(
        _in_proj_kernel,
        grid=(batch, n_tiles),
        in_specs=[tok(d), mod_spec, _const_spec((1, d)), _const_spec((d, d_in)), _const_spec((1, d_in)),
                  rope_spec, rope_spec],
        out_specs=[tok(D_ATTN)] * 4 + [tok(3 * D_HYENA), tok(2 * d)],
        out_shape=[jax.ShapeDtypeStruct((batch, seq, D_ATTN), BF16)] * 4
        + [jax.ShapeDtypeStruct((batch, seq, 3 * D_HYENA), BF16),
           jax.ShapeDtypeStruct((batch, seq, 2 * d), BF16)],
        compiler_params=_cparams(("arbitrary", "arbitrary")),
        name="in_proj",
    )(x, mod_lat, g_mix_pre, w_in_b, b_in_r, jnp.asarray(cos_t), jnp.asarray(sin_t))

    bias = pl.pallas_call(
        _attn_bias_kernel,
        grid=(N_HEADS,),
        in_specs=[pl.BlockSpec(memory_space=pltpu.SMEM)],
        out_specs=pl.BlockSpec((3, 1, ATTN_Q, ATTN_KEYS), lambda h: (0, h, 0, 0)),
        out_shape=jax.ShapeDtypeStruct((3, N_HEADS, ATTN_Q, ATTN_KEYS), BF16),
        scratch_shapes=[pltpu.VMEM((N_BIAS_ROWS, GRID_W, LANES), F32)],
        compiler_params=_cparams(("arbitrary",)),
        name="attn_bias",
    )(na_rpb[0].reshape(-1))
    n_groups = GRID_W // ATTN_ROWS_PER_STEP
    q_spec = pl.BlockSpec((1, ATTN_Q, D_ATTN), lambda b, g: (b, g, 0))
    full = lambda n: pl.BlockSpec((1, n, D_ATTN), lambda b, g: (b, 0, 0))
    bias_spec = pl.BlockSpec(
        (1, N_HEADS, ATTN_Q, ATTN_KEYS),
        lambda b, g: ((g > 0).astype(jnp.int32) + (g == n_groups - 1).astype(jnp.int32), 0, 0, 0))
    y_na = pl.pallas_call(
        _attn_kernel,
        grid=(batch, n_groups),
        in_specs=[q_spec, q_spec, full(seq), full(seq), full(n_ctx), full(n_ctx), bias_spec],
        out_specs=q_spec,
        out_shape=jax.ShapeDtypeStruct((batch, seq, D_ATTN), BF16),
        compiler_params=_cparams(("arbitrary", "arbitrary")),
        name="attn",
    )(q_rot, q_plain, k_rot, v_lat, k_ctx, v_ctx, bias)

    halo = 16
    blocks_per_tile = TOKEN_TILE // halo
    n_halo_blocks = seq // halo
    pitched_spec = pl.BlockSpec((1, TILE_GROUPS * SEQ_PITCH, D_HYENA), lambda b, i: (b, i, 0))
    pitched_shape = jax.ShapeDtypeStruct((batch, PITCHED_ROWS, D_HYENA), F32)
    u_p, x0 = pl.pallas_call(
        _hy_pre_kernel,
        grid=(batch, n_tiles),
        in_specs=[tok(3 * D_HYENA),
                  pl.BlockSpec((1, halo, 3 * D_HYENA),
                               lambda b, i: (b, jnp.maximum(i * blocks_per_tile - 1, 0), 0)),
                  pl.BlockSpec((1, halo, 3 * D_HYENA),
                               lambda b, i: (b, jnp.minimum((i + 1) * blocks_per_tile, n_halo_blocks - 1), 0)),
                  _const_spec((3, 3 * D_HYENA)), _const_spec((1, 3 * D_HYENA))],
        out_specs=[pitched_spec, tok(D_HYENA)],
        out_shape=[pitched_shape, jax.ShapeDtypeStruct((batch, seq, D_HYENA), BF16)],
        compiler_params=_cparams(("arbitrary", "arbitrary")),
        name="hy_pre",
    )(hy, hy, hy, hy_conv_w[0], row2(hy_conv_b[0]))

    z_t, aux_t = _filter_tables(seq)
    filt_n2 = 32
    filt_rows = filt_n2 * FFT_HALF_N1
    w1_pad = jnp.pad(hy_filt_w1[0], ((0, z_t.shape[2] - POS_FEATS), (0, 0)))
    k_circ, k_asum = pl.pallas_call(
        _filt_kernel,
        grid=(2, seq // filt_rows),
        in_specs=[pl.BlockSpec((1, filt_rows, z_t.shape[2]), lambda hf, i: (hf, i, 0)),
                  pl.BlockSpec((1, filt_rows, aux_t.shape[2]), lambda hf, i: (hf, i, 0)),
                  _const_spec(w1_pad.shape), _const_spec((1, FILTER_HIDDEN)),
                  _const_spec((FILTER_HIDDEN, FILTER_HIDDEN)), _const_spec((1, FILTER_HIDDEN)),
                  pl.BlockSpec((FILTER_HIDDEN, D_HYENA), lambda hf, i: (0, hf)),
                  pl.BlockSpec((1, D_HYENA), lambda hf, i: (0, hf)),
                  _const_spec((2, FILTER_HIDDEN)),
                  pl.BlockSpec((1, 1, D_HYENA), lambda hf, i: (hf, 0, 0))],
        out_specs=[pl.BlockSpec((filt_n2, FFT_HALF_N1, D_HYENA), lambda hf, i: (i, hf, 0)),
                   pl.BlockSpec((1, 8, D_HYENA), lambda hf, i: (hf, 0, 0))],
        out_shape=[jax.ShapeDtypeStruct((FFT_N2, FFT_N1, D_HYENA), F32),
                   jax.ShapeDtypeStruct((2, 8, D_HYENA), F32)],
        compiler_params=_cparams(("arbitrary", "arbitrary")),
        name="filt",
    )(jnp.asarray(z_t), jnp.asarray(aux_t), w1_pad, row2(hy_filt_b1[0]), hy_filt_w2[0], row2(hy_filt_b2[0]),
      hy_filt_w3[0], row2(hy_filt_b3[0]), hy_sin_freq[0], hy_decay[0].reshape(2, 1, D_HYENA))
    k_circ = k_circ.reshape(FFT_N, D_HYENA)

    mats = _fft_matrices()
    mats = {k: jnp.asarray(v).astype(BF16) for k, v in mats.items()}
    w2_all = mats["w2"]
    v2_all = mats["v2"]
    step_mat_spec = pl.BlockSpec((FFT_K1_PER_STEP, 2 * FFT_N2, 2 * FFT_N2), lambda cb, s: (s, 0, 0))
    filt_slabs = 2
    kf = pl.pallas_call(
        _filt_fft_kernel,
        grid=(D_HYENA // (filt_slabs * LANES), FFT_STEPS),
        in_specs=[pl.BlockSpec((FFT_N, filt_slabs * LANES), lambda cb, s: (0, cb)),
                  pl.BlockSpec((2, 8, filt_slabs * LANES), lambda cb, s: (0, 0, cb)),
                  _const_spec((FFT_PITCH, FFT_N1)), step_mat_spec],
        out_specs=pl.BlockSpec((FFT_K1_PER_STEP, 2 * FFT_N2, filt_slabs * LANES), lambda cb, s: (s, 0, cb)),
        out_shape=jax.ShapeDtypeStruct((FFT_N1, 2 * FFT_N2, D_HYENA), F32),
        scratch_shapes=[pltpu.VMEM((filt_slabs, FFT_N2 * FFT_PITCH, LANES), F32)],
        compiler_params=_cparams(("arbitrary", "arbitrary")),
        name="filt_fft",
    )(k_circ, k_asum, mats["w1_real"], w2_all)

    n_pairs = batch // 2
    pair_spec = pl.BlockSpec((batch, PITCHED_ROWS, LANES), lambda cb, s: (0, 0, cb),
                             pipeline_mode=pl.Buffered(1))
    y_conv = pl.pallas_call(
        _hy_conv_kernel,
        grid=(D_HYENA // LANES, FFT_STEPS),
        in_specs=[pair_spec,
                  pl.BlockSpec((FFT_K1_PER_STEP, 2 * FFT_N2, LANES), lambda cb, s: (s, 0, cb)),
                  _const_spec((FFT_PITCH, FFT_N1)), step_mat_spec, step_mat_spec,
                  _const_spec((FFT_N1, 2 * FFT_N1))],
        out_specs=pair_spec,
        out_shape=pitched_shape,
        scratch_shapes=[pltpu.VMEM((n_pairs, FFT_N2 * FFT_PITCH, LANES), F32)],
        compiler_params=_cparams(("arbitrary", "arbitrary")),
        name="hy_conv",
    )(u_p, kf, mats["w1_data"], w2_all, v2_all, mats["v1"])

    x1 = pl.pallas_call(
        _merge_kernel,
        grid=(batch, n_tiles),
        in_specs=[tok(d), tok(D_ATTN), tok(D_HYENA), pitched_spec, pitched_spec, tok(2 * d), mod_spec,
                  _const_spec((1, D_HYENA)), _const_spec((1, d)),
                  _const_spec((D_ATTN, d)), _const_spec((D_HYENA, d)), _const_spec((d, d))],
        out_specs=tok(d),
        out_shape=jax.ShapeDtypeStruct((batch, seq, d), F32),
        compiler_params=_cparams(("arbitrary", "arbitrary")),
        name="merge",
    )(x, y_na, x0, u_p, y_conv, gates, mod_lat, row2(hy_skip[0]), row2(norm_mix_post[0]),
      w_o_na[0].astype(BF16), w_o_hy[0].astype(BF16), w_out[0].astype(BF16))

    w_up_c = ffn_w_up[0].astype(BF16)
    conv_w_c = ffn_conv_w[0]
    conv_b_c = row2(ffn_conv_b[0])
    w_dn_c = ffn_w_down[0].astype(BF16)
    ffn_blocks = TOKEN_TILE // FFN_HALO
    n_ffn_blocks = seq // FFN_HALO
    out = pl.pallas_call(
        _ffn_kernel,
        grid=(batch, n_tiles),
        in_specs=[tok(d),
                  pl.BlockSpec((1, FFN_HALO, d), lambda b, i: (b, jnp.maximum(i * ffn_blocks - 1, 0), 0)),
                  pl.BlockSpec((1, FFN_HALO, d),
                               lambda b, i: (b, jnp.minimum((i + 1) * ffn_blocks, n_ffn_blocks - 1), 0)),
                  mod_spec, _const_spec((1, d)), _const_spec((1, d)),
                  _const_spec(w_up_c.shape), _const_spec(conv_w_c.shape), _const_spec(conv_b_c.shape),
                  _const_spec(w_dn_c.shape)],
        out_specs=tok(d),
        out_shape=jax.ShapeDtypeStruct((batch, seq, d), F32),
        scratch_shapes=[pltpu.VMEM((2, 2, FF_CHUNK // LANES, TOKEN_TILE + 2 * FFN_HALO, LANES), F32),
                        pltpu.VMEM((TOKEN_TILE, D_FF), BF16)],
        compiler_params=_cparams(("arbitrary", "arbitrary")),
        name="ffn",
    )(x1, x1, x1, mod_lat, row2(norm_ffn_pre[0]), row2(norm_ffn_post[0]), w_up_c, conv_w_c, conv_b_c, w_dn_c)
    return out
```

```python
import functools
import math

import jax
import jax.numpy as jnp
import numpy as np
from jax import lax
from jax.experimental import pallas as pl
from jax.experimental.pallas import tpu as pltpu

F32 = jnp.float32
BF16 = jnp.bfloat16

D_MODEL = 1024
N_HEADS = 8
HEAD_DIM = 64
D_ATTN = N_HEADS * HEAD_DIM
D_HYENA = 512
GRID_W = 64
WIN_ROWS = 8
WIN_COLS = 16
POS_BANDS = 16
POS_FEATS = 1 + 2 * POS_BANDS
FILTER_HIDDEN = 64
D_FF = 2816
N_MOD = 6
ROPE_BASE = 10000.0
RMS_EPS = 1e-6
NEG_BIAS = -1e30

LANES = 128
VMEM_LIMIT_BYTES = 56 * 1024 * 1024

FFT_N1 = 64
FFT_N2 = 128
FFT_N = FFT_N1 * FFT_N2
FFT_HALF_N1 = FFT_N1 // 2
FFT_PITCH = 2 * FFT_N1 + 8
FFT_K1_PER_STEP = 8
FFT_UNROLL = 8
FFT_STEPS = FFT_N1 // FFT_K1_PER_STEP

TOKEN_TILE = 512
SEQ_PITCH = FFT_N2 + 8
PITCHED_ROWS = FFT_HALF_N1 * SEQ_PITCH
TILE_GROUPS = TOKEN_TILE // FFT_N2
ATTN_ROWS_PER_STEP = 4
ATTN_Q = ATTN_ROWS_PER_STEP * GRID_W
ATTN_KEY_ROWS = 12
ATTN_KEYS = ATTN_KEY_ROWS * GRID_W
FF_CHUNK = 256
HALO = 8
FFN_ROW_BLOCK = 64


def _cparams(sem):
    return pltpu.CompilerParams(dimension_semantics=sem, vmem_limit_bytes=VMEM_LIMIT_BYTES)


@functools.lru_cache(maxsize=None)
def _rope_tables(seq):
    pos = np.arange(seq)
    row = pos // GRID_W
    col = pos % GRID_W
    n_pairs = HEAD_DIM // 4
    inv = ROPE_BASE ** (-np.arange(n_pairs, dtype=np.float64) / n_pairs)
    lane = np.arange(LANES) % HEAD_DIM
    p = np.where(lane[None, :] < HEAD_DIM // 2, row[:, None], col[:, None]).astype(np.float64)
    ang = p * inv[lane % n_pairs][None, :]
    sign = np.where((lane % (2 * n_pairs)) < n_pairs, -1.0, 1.0)
    return np.cos(ang).astype(np.float32), (np.sin(ang) * sign[None, :]).astype(np.float32)


@functools.lru_cache(maxsize=None)
def _filter_tables(seq):
    assert 2 * seq == FFT_N
    half, n2, n1 = np.meshgrid(np.arange(2), np.arange(FFT_N2), np.arange(FFT_HALF_N1), indexing="ij")
    n = FFT_N2 * (half * FFT_HALF_N1 + n1) + n2
    fwd = n < seq
    m = n - seq
    valid = fwd | (m >= 1)
    pos = np.where(valid, np.where(fwd, n, seq - m), 0).astype(np.float64).reshape(2, seq)
    t = pos / max(seq - 1, 1)
    bands = np.linspace(1e-4, POS_BANDS - 1, POS_BANDS)
    ang = (2.0 * math.pi / seq) * pos[..., None] * bands
    z = np.zeros((2, seq, 64), np.float64)
    z[..., 0] = t
    z[..., 1:1 + POS_BANDS] = np.cos(ang)
    z[..., 1 + POS_BANDS:POS_FEATS] = -np.sin(ang)
    aux = np.zeros((2, seq, 8), np.float64)
    aux[..., 0] = t
    aux[..., 1] = valid.reshape(2, seq)
    return z.astype(np.float32), aux.astype(np.float32)


def _realify(m):
    return np.block([[m.real, -m.imag], [m.imag, m.real]])


@functools.lru_cache(maxsize=None)
def _fft_matrices():
    n1 = np.arange(FFT_N1)
    n2 = np.arange(FFT_N2)
    k1 = np.arange(FFT_N1)
    k2 = np.arange(FFT_N2)
    f1 = np.exp(-2j * np.pi * np.outer(k1, n1) / FFT_N1)
    w1_data = _realify(f1[:, :FFT_HALF_N1])
    w1_real = np.concatenate([f1.real, f1.imag], axis=0)
    v1 = _realify(np.conj(f1.T)[:FFT_HALF_N1, :] / FFT_N)
    f2 = np.exp(-2j * np.pi * np.outer(k2, n2) / FFT_N2)
    tw = np.exp(-2j * np.pi * np.outer(k1, n2) / FFT_N)
    w2 = np.stack([_realify(f2 * tw[a][None, :]) for a in range(FFT_N1)])
    v2 = np.stack([_realify(np.conj(f2.T) * np.conj(tw[a])[:, None]) for a in range(FFT_N1)])
    pad = np.zeros((FFT_PITCH - 2 * FFT_N1, FFT_N1))
    w1_data = np.concatenate([w1_data, pad], axis=0)
    w1_real = np.concatenate([w1_real, pad], axis=0)
    return {k: v.astype(np.float32) for k, v in
            dict(w1_data=w1_data, w1_real=w1_real, v1=v1, w2=w2, v2=v2).items()}


N_BIAS_ROWS = 2 * WIN_ROWS - 1
N_BIAS_COLS = 2 * WIN_COLS - 1


def _attn_bias_row_index():
    rows = GRID_W
    groups = (0, 2, rows // ATTN_ROWS_PER_STEP - 1)
    dr = np.full((3, ATTN_ROWS_PER_STEP, ATTN_KEY_ROWS), -1, np.int32)
    for v, g in enumerate(groups):
        ws = min(max(ATTN_ROWS_PER_STEP * g - WIN_ROWS // 2, 0), rows - ATTN_KEY_ROWS)
        for i in range(ATTN_ROWS_PER_STEP):
            r = ATTN_ROWS_PER_STEP * g + i
            r_start = min(max(r - WIN_ROWS // 2, 0), rows - WIN_ROWS)
            for j in range(ATTN_KEY_ROWS):
                kr = ws + j
                if r_start <= kr < r_start + WIN_ROWS:
                    dr[v, i, j] = kr - r + (WIN_ROWS - 1)
    return dr


def _attn_bias_kernel(rpb_ref, o_ref, t_ref):
    head = pl.program_id(0)
    qc = lax.broadcasted_iota(jnp.int32, (GRID_W, LANES), 0)
    lane = lax.broadcasted_iota(jnp.int32, (GRID_W, LANES), 1)
    kc = lane % GRID_W
    c_start = jnp.clip(qc - WIN_COLS // 2, 0, GRID_W - WIN_COLS)
    col_in = (kc >= c_start) & (kc < c_start + WIN_COLS)
    dc = jnp.clip(kc - qc, 1 - WIN_COLS, WIN_COLS - 1) + (WIN_COLS - 1)
    base = head * (N_BIAS_ROWS * N_BIAS_COLS)
    for r in range(N_BIAS_ROWS):
        t = jnp.full((GRID_W, LANES), NEG_BIAS, F32)
        for cidx in range(N_BIAS_COLS):
            t = jnp.where(col_in & (dc == cidx), rpb_ref[base + r * N_BIAS_COLS + cidx], t)
        t_ref[r] = t
    dr = _attn_bias_row_index()
    low_half = lane < GRID_W
    masked = jnp.full((GRID_W, LANES), NEG_BIAS, F32)
    for v in range(3):
        for i in range(ATTN_ROWS_PER_STEP):
            for j in range(0, ATTN_KEY_ROWS, 2):
                lo = t_ref[int(dr[v, i, j])] if dr[v, i, j] >= 0 else masked
                hi = t_ref[int(dr[v, i, j + 1])] if dr[v, i, j + 1] >= 0 else masked
                o_ref[v, 0, i * GRID_W:(i + 1) * GRID_W, j * GRID_W:(j + 2) * GRID_W] = jnp.where(
                    low_half, lo, hi).astype(BF16)


def _mod_kernel(c_ref, w_ref, b_ref, o_ref):
    c = c_ref[...]
    s = c * jax.nn.sigmoid(c)
    o_ref[...] = jnp.dot(s, w_ref[...], precision=lax.Precision.HIGHEST,
                         preferred_element_type=F32) + b_ref[...]


def _norm_modulate(x, gain, shift, scale):
    ms = jnp.mean(x * x, axis=-1, keepdims=True)
    y = x * lax.rsqrt(ms + RMS_EPS) * gain
    return y * (1.0 + scale) + shift


def _rope(t, cos, sin_signed):
    n_pairs = HEAD_DIM // 4
    lane = lax.broadcasted_iota(jnp.int32, t.shape, 1)
    first = (lane % (2 * n_pairs)) < n_pairs
    partner = jnp.where(first, pltpu.roll(t, LANES - n_pairs, 1), pltpu.roll(t, n_pairs, 1))
    return t * cos + partner * sin_signed


def _in_proj_kernel(x_ref, prev_ref, next_ref, mod_ref, g_ref, w_ref, b_ref, cos_ref, sin_ref, cw_ref, cb_ref,
                    qr_ref, qp_ref, kr_ref, v_ref, gt_ref, u_ref, x0_ref, hy_ref):
    i = pl.program_id(1)
    n_tiles = pl.num_programs(1)
    xx = jnp.concatenate([prev_ref[0], x_ref[0], next_ref[0]], axis=0)
    h_ext = _norm_modulate(xx, g_ref[...], mod_ref[0, 0:1, :], mod_ref[0, 1:2, :]).astype(BF16)
    h = h_ext[HALO:HALO + TOKEN_TILE]
    cos = cos_ref[...]
    sin = sin_ref[...]

    def proj(lo, hi, rows=h):
        return jnp.dot(rows, w_ref[:, lo:hi], preferred_element_type=F32) + b_ref[:, lo:hi]

    q = proj(0, D_ATTN) * (HEAD_DIM ** -0.5)
    qp_ref[0] = q.astype(BF16)
    k = proj(D_ATTN, 2 * D_ATTN)
    for c in range(D_ATTN // LANES):
        lanes = slice(c * LANES, (c + 1) * LANES)
        qr_ref[0, :, lanes] = _rope(q[:, lanes], cos, sin).astype(BF16)
        kr_ref[0, :, lanes] = _rope(k[:, lanes], cos, sin).astype(BF16)
    v_ref[0] = proj(2 * D_ATTN, 3 * D_ATTN).astype(BF16)
    hy_lo = 3 * D_ATTN
    gl_lo = hy_lo + 3 * D_HYENA
    for c in range(4):
        w = D_MODEL // 2
        gt_ref[0, :, c * w:(c + 1) * w] = jax.nn.sigmoid(
            proj(gl_lo + c * w, gl_lo + (c + 1) * w)).astype(BF16)

    slabs_per_part = D_HYENA // LANES
    for c in range(3):
        hy = proj(hy_lo + c * D_HYENA, hy_lo + (c + 1) * D_HYENA, h_ext)
        for s in range(slabs_per_part):
            hy_ref[c * slabs_per_part + s] = hy[:, s * LANES:(s + 1) * LANES]
    zero_row = jnp.zeros((1, LANES), F32)

    @pl.when(i == 0)
    def _():
        for s in range(3 * slabs_per_part):
            hy_ref[s, HALO - 1:HALO, :] = zero_row

    @pl.when(i == n_tiles - 1)
    def _():
        for s in range(3 * slabs_per_part):
            hy_ref[s, HALO + TOKEN_TILE:HALO + TOKEN_TILE + 1, :] = zero_row

    def conv(s, r0, rows):
        lanes = slice(s * LANES, (s + 1) * LANES)
        tap = lambda j: hy_ref[s, pl.ds(HALO - 1 + j + r0, rows, stride=1), :]
        return (tap(0) * cw_ref[0:1, lanes] + tap(1) * cw_ref[1:2, lanes] + tap(2) * cw_ref[2:3, lanes]
                + cb_ref[:, lanes])

    for s in range(slabs_per_part):
        lanes = slice(s * LANES, (s + 1) * LANES)
        for j in range(TILE_GROUPS):
            r0 = j * FFT_N2
            x0_ref[0, r0:r0 + FFT_N2, lanes] = conv(s, r0, FFT_N2).astype(BF16)
            u_ref[0, j * SEQ_PITCH:j * SEQ_PITCH + FFT_N2, lanes] = (
                conv(slabs_per_part + s, r0, FFT_N2) * conv(2 * slabs_per_part + s, r0, FFT_N2))
            u_ref[0, j * SEQ_PITCH + FFT_N2:(j + 1) * SEQ_PITCH, lanes] = jnp.zeros((SEQ_PITCH - FFT_N2, LANES), F32)


def _ctx_kv_kernel(x_ref, mod_ref, g_ref, w_ref, b_ref, k_ref, v_ref):
    h = _norm_modulate(x_ref[0], g_ref[...], mod_ref[0:1, :], mod_ref[1:2, :]).astype(BF16)
    kv = jnp.dot(h, w_ref[...], preferred_element_type=F32) + b_ref[...]
    k_ref[0] = kv[:, :D_ATTN].astype(BF16)
    v_ref[0] = kv[:, D_ATTN:].astype(BF16)


def _attn_window_start(g):
    return jnp.clip(ATTN_ROWS_PER_STEP * g - WIN_ROWS // 2, 0, GRID_W - ATTN_KEY_ROWS)


def _attn_kernel(qr_ref, qp_ref, k_ref, v_ref, kc_ref, vc_ref, bias_ref, o_ref):
    g = pl.program_id(1)
    key0 = pl.multiple_of(_attn_window_start(g) * GRID_W, GRID_W)
    nt = (((1,), (1,)), ((), ()))
    quad_w = 4 * HEAD_DIM
    lane = lax.broadcasted_iota(jnp.int32, (1, quad_w), 1)
    for quad in range(N_HEADS // 4):
        ql = slice(quad * quad_w, (quad + 1) * quad_w)
        q_rot = qr_ref[0, :, ql]
        q_plain = qp_ref[0, :, ql]
        k_win = k_ref[0, pl.ds(key0, ATTN_KEYS), ql]
        v_win = v_ref[0, pl.ds(key0, ATTN_KEYS), ql]
        k_ctx = kc_ref[0, :, ql]
        v_ctx = vc_ref[0, :, ql]
        acc = jnp.zeros((ATTN_Q, quad_w), F32)
        for hh in range(4):
            head = quad * 4 + hh
            mine = (lane >= hh * HEAD_DIM) & (lane < (hh + 1) * HEAD_DIM)
            zero = jnp.zeros((), BF16)
            s_nb = lax.dot_general(jnp.where(mine, q_rot, zero), k_win, nt, preferred_element_type=F32)
            s_nb = s_nb + bias_ref[0, head].astype(F32)
            s_cx = lax.dot_general(jnp.where(mine, q_plain, zero), k_ctx, nt, preferred_element_type=F32)
            m = jnp.maximum(jnp.max(s_nb, axis=-1, keepdims=True), jnp.max(s_cx, axis=-1, keepdims=True))
            p_nb = jnp.exp(s_nb - m)
            p_cx = jnp.exp(s_cx - m)
            denom = jnp.sum(p_nb, axis=-1, keepdims=True) + jnp.sum(p_cx, axis=-1, keepdims=True)
            o = (jnp.dot(p_nb.astype(BF16), v_win, preferred_element_type=F32)
                 + jnp.dot(p_cx.astype(BF16), v_ctx, preferred_element_type=F32))
            acc = jnp.where(mine, o / denom, acc)
        o_ref[0, :, ql] = acc.astype(BF16)


def _filt_kernel(z_ref, aux_ref, w1_ref, b1_ref, w2_ref, b2_ref, w3_ref, b3_ref, freq_ref, decay_ref,
                 o_ref, asum_ref):
    i = pl.program_id(1)
    hp = lax.Precision.HIGHEST
    h = jnp.sin(freq_ref[0:1, :] * (jnp.dot(z_ref[0], w1_ref[...], precision=hp,
                                            preferred_element_type=F32) + b1_ref[...]))
    h = jnp.sin(freq_ref[1:2, :] * (jnp.dot(h, w2_ref[...], precision=hp,
                                            preferred_element_type=F32) + b2_ref[...]))
    taps = jnp.dot(h, w3_ref[...], precision=hp, preferred_element_type=F32) + b3_ref[...]
    t = aux_ref[0, :, 0:1]
    valid = aux_ref[0, :, 1:2] > 0.5
    k = jnp.where(valid, taps * jnp.exp(-t * jnp.abs(decay_ref[0])), 0.0)
    part = jnp.sum(jnp.abs(k), axis=0, keepdims=True)

    @pl.when(i == 0)
    def _():
        asum_ref[...] = jnp.zeros_like(asum_ref)

    asum_ref[0] += jnp.broadcast_to(part, asum_ref.shape[1:])
    o_ref[...] = k.reshape(o_ref.shape)


def _fft_stage1(load_group, w1_ref, a_ref):
    n_slabs = a_ref.shape[0]

    def body(n2, carry):
        x = load_group(n2)
        a = jnp.dot(w1_ref[...], x, preferred_element_type=F32)
        base = pl.multiple_of(n2 * FFT_PITCH, 8)
        for s in range(n_slabs):
            a_ref[s, pl.ds(base, FFT_PITCH), :] = a[:, s * LANES:(s + 1) * LANES]
        return carry

    lax.fori_loop(0, FFT_N2, body, 0, unroll=FFT_UNROLL)


def _fft_load_k1(a_ref, k1):
    parts = []
    for off in (0, FFT_N1):
        parts.append(jnp.concatenate(
            [a_ref[s, pl.ds(k1 + off, FFT_N2, stride=FFT_PITCH), :] for s in range(a_ref.shape[0])], axis=1))
    return jnp.concatenate(parts, axis=0)


def _filt_fft_kernel(k_ref, asum_ref, w1_ref, w2_ref, o_ref, a_ref):
    step = pl.program_id(1)

    @pl.when(step == 0)
    def _():
        norm = asum_ref[0, 0:1, :] + asum_ref[1, 0:1, :]

        def load_group(n2):
            return (k_ref[pl.ds(pl.multiple_of(n2 * FFT_N1, FFT_N1), FFT_N1), :] / norm).astype(BF16)
        _fft_stage1(load_group, w1_ref, a_ref)

    for j in range(FFT_K1_PER_STEP):
        b = _fft_load_k1(a_ref, step * FFT_K1_PER_STEP + j).astype(BF16)
        o_ref[j] = jnp.dot(w2_ref[j], b, preferred_element_type=F32)


def _hy_conv_kernel(u_ref, kf_ref, w1_ref, w2_ref, v2_ref, v1_ref, o_ref, a_ref):
    step = pl.program_id(1)
    n_slabs = a_ref.shape[0]

    def tokens_of(member, s, n2):
        return (2 * s + member, pl.ds(n2, FFT_HALF_N1, stride=SEQ_PITCH), slice(None))

    @pl.when(step == 0)
    def _():
        o_ref[...] = jnp.zeros_like(o_ref)

        def load_group(n2):
            return jnp.concatenate(
                [jnp.concatenate([u_ref[tokens_of(member, s, n2)] for s in range(n_slabs)], axis=1)
                 for member in range(2)], axis=0).astype(BF16)
        _fft_stage1(load_group, w1_ref, a_ref)

    for j in range(FFT_K1_PER_STEP):
        k1 = step * FFT_K1_PER_STEP + j
        b = _fft_load_k1(a_ref, k1).astype(BF16)
        x = jnp.dot(w2_ref[j], b, preferred_element_type=F32)
        kf = kf_ref[j]
        kf = jnp.concatenate([kf] * n_slabs, axis=1)
        xr, xi = x[:FFT_N2], x[FFT_N2:]
        kr, ki = kf[:FFT_N2], kf[FFT_N2:]
        y = jnp.concatenate([xr * kr - xi * ki, xr * ki + xi * kr], axis=0).astype(BF16)
        d = jnp.dot(v2_ref[j], y, preferred_element_type=F32)
        for s in range(n_slabs):
            lanes = slice(s * LANES, (s + 1) * LANES)
            a_ref[s, pl.ds(k1, FFT_N2, stride=FFT_PITCH), :] = d[:FFT_N2, lanes]
            a_ref[s, pl.ds(k1 + FFT_N1, FFT_N2, stride=FFT_PITCH), :] = d[FFT_N2:, lanes]

    @pl.when(step == FFT_STEPS - 1)
    def _():
        def body(n2, carry):
            base = pl.multiple_of(n2 * FFT_PITCH, 8)
            d = jnp.concatenate([a_ref[s, pl.ds(base, 2 * FFT_N1), :] for s in range(n_slabs)], axis=1)
            y = jnp.dot(v1_ref[...], d.astype(BF16), preferred_element_type=F32)
            for s in range(n_slabs):
                for member in range(2):
                    o_ref[tokens_of(member, s, n2)] = y[member * FFT_HALF_N1:(member + 1) * FFT_HALF_N1,
                                                        s * LANES:(s + 1) * LANES]
            return carry

        lax.fori_loop(0, FFT_N2, body, 0, unroll=FFT_UNROLL)


def _from_pitched(ref):
    return jnp.concatenate([ref[0, j * SEQ_PITCH:j * SEQ_PITCH + FFT_N2, :] for j in range(TILE_GROUPS)], axis=0)


def _merge_kernel(x_ref, yna_ref, x0_ref, u_ref, yc_ref, gt_ref, mod_ref, skip_ref, gpost_ref,
                  wna_ref, why_ref, wout_ref, o_ref):
    y_hy = (x0_ref[0].astype(F32) * (_from_pitched(yc_ref) + _from_pitched(u_ref) * skip_ref[...])).astype(BF16)
    a = jnp.dot(yna_ref[0], wna_ref[...], preferred_element_type=F32)
    b = jnp.dot(y_hy, why_ref[...], preferred_element_type=F32)
    g_na = gt_ref[0, :, :D_MODEL].astype(F32)
    g_hy = gt_ref[0, :, D_MODEL:].astype(F32)
    m = (g_na * a + g_hy * b).astype(BF16)
    o = jnp.dot(m, wout_ref[...], preferred_element_type=F32)
    ms = jnp.mean(o * o, axis=-1, keepdims=True)
    o = o * lax.rsqrt(ms + RMS_EPS) * gpost_ref[...]
    o_ref[0] = x_ref[0] + mod_ref[0, 2:3, :] * o


def _gelu_tanh(a):
    return 0.5 * a * (1.0 + jnp.tanh(math.sqrt(2.0 / math.pi) * (a + 0.044715 * (a * a * a))))


def _ffn_kernel(x_ref, prev_ref, next_ref, mod_ref, gpre_ref, gpost_ref, wup_ref, cw_ref, cb_ref, wdn_ref,
                o_ref, u_ref, act_ref):
    i = pl.program_id(1)
    n_tiles = pl.num_programs(1)
    x = x_ref[0]
    xx = jnp.concatenate([prev_ref[0], x, next_ref[0]], axis=0)
    h = _norm_modulate(xx, gpre_ref[...], mod_ref[0, 3:4, :], mod_ref[0, 4:5, :])
    row = lax.broadcasted_iota(jnp.int32, (xx.shape[0], 1), 0)
    inside = ((row >= HALO) | (i > 0)) & ((row < HALO + TOKEN_TILE) | (i < n_tiles - 1))
    h = jnp.where(inside, h, 0.0).astype(BF16)
    n_chunks = D_FF // FF_CHUNK
    n_slabs = FF_CHUNK // LANES

    def conv(buf, half, s, lo, r0):
        lanes = slice(half * D_FF + lo, half * D_FF + lo + LANES)
        tap = lambda j: u_ref[buf, half, s, pl.ds(HALO - 1 + j + r0, FFN_ROW_BLOCK, stride=1), :]
        return (tap(0) * cw_ref[0:1, lanes] + tap(1) * cw_ref[1:2, lanes] + tap(2) * cw_ref[2:3, lanes]
                + cb_ref[:, lanes])

    for c in range(n_chunks):
        buf = c % 2
        for half in range(2):
            lo = half * D_FF + c * FF_CHUNK
            u = jnp.dot(h, wup_ref[:, lo:lo + FF_CHUNK], preferred_element_type=F32)
            for s in range(n_slabs):
                u_ref[buf, half, s] = u[:, s * LANES:(s + 1) * LANES]
        for s in range(n_slabs):
            lo = c * FF_CHUNK + s * LANES
            for r0 in range(0, TOKEN_TILE, FFN_ROW_BLOCK):
                act_ref[r0:r0 + FFN_ROW_BLOCK, lo:lo + LANES] = (
                    _gelu_tanh(conv(buf, 0, s, lo, r0)) * conv(buf, 1, s, lo, r0)).astype(BF16)
    y = jnp.dot(act_ref[...], wdn_ref[...], preferred_element_type=F32)
    ms = jnp.mean(y * y, axis=-1, keepdims=True)
    y = y * lax.rsqrt(ms + RMS_EPS) * gpost_ref[...]
    o_ref[0] = x + mod_ref[0, 5:6, :] * y


def _const_spec(shape):
    nd = len(shape)
    return pl.BlockSpec(shape, lambda *_: (0,) * nd, pipeline_mode=pl.Buffered(1))


def kernel(x, c, ctx, c_ctx, w_mod, b_mod, norm_mix_pre, norm_mix_post, norm_ffn_pre, norm_ffn_post, w_in, b_in, na_rpb, hy_conv_w, hy_conv_b, hy_filt_w1, hy_filt_b1, hy_filt_w2, hy_filt_b2, hy_filt_w3, hy_filt_b3, hy_sin_freq, hy_decay, hy_skip, w_o_na, w_o_hy, w_out, ffn_w_up, ffn_conv_w, ffn_conv_b, ffn_w_down):
    batch, seq, d = x.shape
    n_ctx = ctx.shape[1]
    assert d == D_MODEL and 2 * seq == FFT_N and seq == GRID_W * GRID_W and batch % 2 == 0
    assert w_mod.shape[0] == 1, "single-layer block"
    n_tiles = seq // TOKEN_TILE
    d_in = w_in.shape[2]
    row2 = lambda a: a.reshape(1, -1)

    c_all = jnp.zeros((8, d), F32).at[:batch].set(c).at[batch].set(c_ctx)
    mod_n = 1024
    mod = pl.pallas_call(
        _mod_kernel,
        grid=(N_MOD * d // mod_n,),
        in_specs=[_const_spec((8, d)),
                  pl.BlockSpec((d, mod_n), lambda j: (0, j)),
                  pl.BlockSpec((1, mod_n), lambda j: (0, j))],
        out_specs=pl.BlockSpec((8, mod_n), lambda j: (0, j)),
        out_shape=jax.ShapeDtypeStruct((8, N_MOD * d), F32),
        compiler_params=_cparams(("arbitrary",)),
        name="mod",
    )(c_all, w_mod[0], row2(b_mod[0]))
    mod_lat = jnp.pad(mod[:batch].reshape(batch, N_MOD, d), ((0, 0), (0, 8 - N_MOD), (0, 0)))
    mod_ctx = jnp.pad(mod[batch].reshape(N_MOD, d), ((0, 8 - N_MOD), (0, 0)))

    w_in_b = w_in[0].astype(BF16)
    b_in_r = row2(b_in[0])
    g_mix_pre = row2(norm_mix_pre[0])

    k_ctx, v_ctx = pl.pallas_call(
        _ctx_kv_kernel,
        grid=(batch,),
        in_specs=[pl.BlockSpec((1, n_ctx, d), lambda b: (b, 0, 0)),
                  _const_spec((8, d)), _const_spec((1, d)),
                  _const_spec((d, 2 * D_ATTN)), _const_spec((1, 2 * D_ATTN))],
        out_specs=[pl.BlockSpec((1, n_ctx, D_ATTN), lambda b: (b, 0, 0))] * 2,
        out_shape=[jax.ShapeDtypeStruct((batch, n_ctx, D_ATTN), BF16)] * 2,
        compiler_params=_cparams(("arbitrary",)),
        name="ctx_kv",
    )(ctx, mod_ctx, g_mix_pre, w_in_b[:, D_ATTN:3 * D_ATTN], b_in_r[:, D_ATTN:3 * D_ATTN])

    cos_t, sin_t = _rope_tables(seq)
    tok = lambda w: pl.BlockSpec((1, TOKEN_TILE, w), lambda b, i: (b, i, 0))
    mod_spec = pl.BlockSpec((1, 8, d), lambda b, i: (b, 0, 0))
    rope_spec = pl.BlockSpec((TOKEN_TILE, LANES), lambda b, i: (i, 0))
    halo_blocks = TOKEN_TILE // HALO
    n_halo_blocks = seq // HALO
    prev_spec = lambda w: pl.BlockSpec((1, HALO, w), lambda b, i: (b, jnp.maximum(i * halo_blocks - 1, 0), 0))
    next_spec = lambda w: pl.BlockSpec(
        (1, HALO, w), lambda b, i: (b, jnp.minimum((i + 1) * halo_blocks, n_halo_blocks - 1), 0))
    pitched_spec = pl.BlockSpec((1, TILE_GROUPS * SEQ_PITCH, D_HYENA), lambda b, i: (b, i, 0))
    pitched_shape = jax.ShapeDtypeStruct((batch, PITCHED_ROWS, D_HYENA), F32)
    q_rot, q_plain, k_rot, v_lat, gates, u_p, x0 = pl.pallas_call(
        _in_proj_kernel,
        grid=(batch, n_tiles),
        in_specs=[tok(d), prev_spec(d), next_spec(d), mod_spec, _const_spec((1, d)), _const_spec((d, d_in)),
                  _const_spec((1, d_in)), rope_spec, rope_spec,
                  _const_spec((3, 3 * D_HYENA)), _const_spec((1, 3 * D_HYENA))],
        out_specs=[tok(D_ATTN)] * 4 + [tok(2 * d), pitched_spec, tok(D_HYENA)],
        out_shape=[jax.ShapeDtypeStruct((batch, seq, D_ATTN), BF16)] * 4
        + [jax.ShapeDtypeStruct((batch, seq, 2 * d), BF16), pitched_shape,
           jax.ShapeDtypeStruct((batch, seq, D_HYENA), BF16)],
        scratch_shapes=[pltpu.VMEM((3 * D_HYENA // LANES, TOKEN_TILE + 2 * HALO, LANES), F32)],
        compiler_params=_cparams(("arbitrary", "arbitrary")),
        name="in_proj",
    )(x, x, x, mod_lat, g_mix_pre, w_in_b, b_in_r, jnp.asarray(cos_t), jnp.asarray(sin_t),
      hy_conv_w[0], row2(hy_conv_b[0]))

    bias = pl.pallas_call(
        _attn_bias_kernel,
        grid=(N_HEADS,),
        in_specs=[pl.BlockSpec(memory_space=pltpu.SMEM)],
        out_specs=pl.BlockSpec((3, 1, ATTN_Q, ATTN_KEYS), lambda h: (0, h, 0, 0)),
        out_shape=jax.ShapeDtypeStruct((3, N_HEADS, ATTN_Q, ATTN_KEYS), BF16),
        scratch_shapes=[pltpu.VMEM((N_BIAS_ROWS, GRID_W, LANES), F32)],
        compiler_params=_cparams(("arbitrary",)),
        name="attn_bias",
    )(na_rpb[0].reshape(-1))
    n_groups = GRID_W // ATTN_ROWS_PER_STEP
    q_spec = pl.BlockSpec((1, ATTN_Q, D_ATTN), lambda b, g: (b, g, 0))
    full = lambda n: pl.BlockSpec((1, n, D_ATTN), lambda b, g: (b, 0, 0))
    bias_spec = pl.BlockSpec(
        (1, N_HEADS, ATTN_Q, ATTN_KEYS),
        lambda b, g: ((g > 0).astype(jnp.int32) + (g == n_groups - 1).astype(jnp.int32), 0, 0, 0))
    y_na = pl.pallas_call(
        _attn_kernel,
        grid=(batch, n_groups),
        in_specs=[q_spec, q_spec, full(seq), full(seq), full(n_ctx), full(n_ctx), bias_spec],
        out_specs=q_spec,
        out_shape=jax.ShapeDtypeStruct((batch, seq, D_ATTN), BF16),
        compiler_params=_cparams(("arbitrary", "arbitrary")),
        name="attn",
    )(q_rot, q_plain, k_rot, v_lat, k_ctx, v_ctx, bias)

    z_t, aux_t = _filter_tables(seq)
    filt_n2 = 32
    filt_rows = filt_n2 * FFT_HALF_N1
    w1_pad = jnp.pad(hy_filt_w1[0], ((0, z_t.shape[2] - POS_FEATS), (0, 0)))
    k_circ, k_asum = pl.pallas_call(
        _filt_kernel,
        grid=(2, seq // filt_rows),
        in_specs=[pl.BlockSpec((1, filt_rows, z_t.shape[2]), lambda hf, i: (hf, i, 0)),
                  pl.BlockSpec((1, filt_rows, aux_t.shape[2]), lambda hf, i: (hf, i, 0)),
                  _const_spec(w1_pad.shape), _const_spec((1, FILTER_HIDDEN)),
                  _const_spec((FILTER_HIDDEN, FILTER_HIDDEN)), _const_spec((1, FILTER_HIDDEN)),
                  pl.BlockSpec((FILTER_HIDDEN, D_HYENA), lambda hf, i: (0, hf)),
                  pl.BlockSpec((1, D_HYENA), lambda hf, i: (0, hf)),
                  _const_spec((2, FILTER_HIDDEN)),
                  pl.BlockSpec((1, 1, D_HYENA), lambda hf, i: (hf, 0, 0))],
        out_specs=[pl.BlockSpec((filt_n2, FFT_HALF_N1, D_HYENA), lambda hf, i: (i, hf, 0)),
                   pl.BlockSpec((1, 8, D_HYENA), lambda hf, i: (hf, 0, 0))],
        out_shape=[jax.ShapeDtypeStruct((FFT_N2, FFT_N1, D_HYENA), F32),
                   jax.ShapeDtypeStruct((2, 8, D_HYENA), F32)],
        compiler_params=_cparams(("arbitrary", "arbitrary")),
        name="filt",
    )(jnp.asarray(z_t), jnp.asarray(aux_t), w1_pad, row2(hy_filt_b1[0]), hy_filt_w2[0], row2(hy_filt_b2[0]),
      hy_filt_w3[0], row2(hy_filt_b3[0]), hy_sin_freq[0], hy_decay[0].reshape(2, 1, D_HYENA))
    k_circ = k_circ.reshape(FFT_N, D_HYENA)

    mats = _fft_matrices()
    mats = {k: jnp.asarray(v).astype(BF16) for k, v in mats.items()}
    w2_all = mats["w2"]
    v2_all = mats["v2"]
    step_mat_spec = pl.BlockSpec((FFT_K1_PER_STEP, 2 * FFT_N2, 2 * FFT_N2), lambda cb, s: (s, 0, 0))
    filt_slabs = 2
    kf = pl.pallas_call(
        _filt_fft_kernel,
        grid=(D_HYENA // (filt_slabs * LANES), FFT_STEPS),
        in_specs=[pl.BlockSpec((FFT_N, filt_slabs * LANES), lambda cb, s: (0, cb)),
                  pl.BlockSpec((2, 8, filt_slabs * LANES), lambda cb, s: (0, 0, cb)),
                  _const_spec((FFT_PITCH, FFT_N1)), step_mat_spec],
        out_specs=pl.BlockSpec((FFT_K1_PER_STEP, 2 * FFT_N2, filt_slabs * LANES), lambda cb, s: (s, 0, cb)),
        out_shape=jax.ShapeDtypeStruct((FFT_N1, 2 * FFT_N2, D_HYENA), F32),
        scratch_shapes=[pltpu.VMEM((filt_slabs, FFT_N2 * FFT_PITCH, LANES), F32)],
        compiler_params=_cparams(("arbitrary", "arbitrary")),
        name="filt_fft",
    )(k_circ, k_asum, mats["w1_real"], w2_all)

    n_pairs = batch // 2
    pair_spec = pl.BlockSpec((batch, PITCHED_ROWS, LANES), lambda cb, s: (0, 0, cb),
                             pipeline_mode=pl.Buffered(1))
    y_conv = pl.pallas_call(
        _hy_conv_kernel,
        grid=(D_HYENA // LANES, FFT_STEPS),
        in_specs=[pair_spec,
                  pl.BlockSpec((FFT_K1_PER_STEP, 2 * FFT_N2, LANES), lambda cb, s: (s, 0, cb)),
                  _const_spec((FFT_PITCH, FFT_N1)), step_mat_spec, step_mat_spec,
                  _const_spec((FFT_N1, 2 * FFT_N1))],
        out_specs=pair_spec,
        out_shape=pitched_shape,
        scratch_shapes=[pltpu.VMEM((n_pairs, FFT_N2 * FFT_PITCH, LANES), F32)],
        compiler_params=_cparams(("arbitrary", "arbitrary")),
        name="hy_conv",
    )(u_p, kf, mats["w1_data"], w2_all, v2_all, mats["v1"])

    x1 = pl.pallas_call(
        _merge_kernel,
        grid=(batch, n_tiles),
        in_specs=[tok(d), tok(D_ATTN), tok(D_HYENA), pitched_spec, pitched_spec, tok(2 * d), mod_spec,
                  _const_spec((1, D_HYENA)), _const_spec((1, d)),
                  _const_spec((D_ATTN, d)), _const_spec((D_HYENA, d)), _const_spec((d, d))],
        out_specs=tok(d),
        out_shape=jax.ShapeDtypeStruct((batch, seq, d), F32),
        compiler_params=_cparams(("arbitrary", "arbitrary")),
        name="merge",
    )(x, y_na, x0, u_p, y_conv, gates, mod_lat, row2(hy_skip[0]), row2(norm_mix_post[0]),
      w_o_na[0].astype(BF16), w_o_hy[0].astype(BF16), w_out[0].astype(BF16))

    w_up_c = ffn_w_up[0].astype(BF16)
    conv_w_c = ffn_conv_w[0]
    conv_b_c = row2(ffn_conv_b[0])
    w_dn_c = ffn_w_down[0].astype(BF16)
    out = pl.pallas_call(
        _ffn_kernel,
        grid=(batch, n_tiles),
        in_specs=[tok(d), prev_spec(d), next_spec(d), mod_spec, _const_spec((1, d)), _const_spec((1, d)),
                  _const_spec(w_up_c.shape), _const_spec(conv_w_c.shape), _const_spec(conv_b_c.shape),
                  _const_spec(w_dn_c.shape)],
        out_specs=tok(d),
        out_shape=jax.ShapeDtypeStruct((batch, seq, d), F32),
        scratch_shapes=[pltpu.VMEM((2, 2, FF_CHUNK // LANES, TOKEN_TILE + 2 * HALO, LANES), F32),
                        pltpu.VMEM((TOKEN_TILE, D_FF), BF16)],
        compiler_params=_cparams(("arbitrary", "arbitrary")),
        name="ffn",
    )(x1, x1, x1, mod_lat, row2(norm_ffn_pre[0]), row2(norm_ffn_post[0]), w_up_c, conv_w_c, conv_b_c, w_dn_c)
    return out
```

```python
import functools
import math

import jax
import jax.numpy as jnp
import numpy as np
from jax import lax
from jax.experimental import pallas as pl
from jax.experimental.pallas import tpu as pltpu

F32 = jnp.float32
BF16 = jnp.bfloat16

D_MODEL = 1024
N_HEADS = 8
HEAD_DIM = 64
D_ATTN = N_HEADS * HEAD_DIM
D_HYENA = 512
GRID_W = 64
WIN_ROWS = 8
WIN_COLS = 16
POS_BANDS = 16
POS_FEATS = 1 + 2 * POS_BANDS
FILTER_HIDDEN = 64
D_FF = 2816
N_MOD = 6
ROPE_BASE = 10000.0
RMS_EPS = 1e-6
NEG_BIAS = -1e30
LOG2_E = math.log2(math.e)

LANES = 128
VMEM_LIMIT_BYTES = 56 * 1024 * 1024

FFT_N1 = 64
FFT_N2 = 128
FFT_N = FFT_N1 * FFT_N2
FFT_HALF_N1 = FFT_N1 // 2
FFT_PITCH = 2 * FFT_N1 + 8
FFT_K1_PER_STEP = 8
FFT_UNROLL = 8
FFT_STEPS = FFT_N1 // FFT_K1_PER_STEP

TOKEN_TILE = 512
SEQ_PITCH = FFT_N2 + 8
PITCHED_ROWS = FFT_HALF_N1 * SEQ_PITCH
TILE_GROUPS = TOKEN_TILE // FFT_N2
ATTN_ROWS_PER_STEP = 4
ATTN_Q = ATTN_ROWS_PER_STEP * GRID_W
ATTN_KEY_ROWS = 12
ATTN_KEYS = ATTN_KEY_ROWS * GRID_W
ATTN_KEY_CHUNK = 256
FF_CHUNK = 256
HALO = 8
FFN_ROW_BLOCK = 64


def _cparams(sem):
    return pltpu.CompilerParams(dimension_semantics=sem, vmem_limit_bytes=VMEM_LIMIT_BYTES)


@functools.lru_cache(maxsize=None)
def _rope_tables(seq):
    pos = np.arange(seq)
    row = pos // GRID_W
    col = pos % GRID_W
    n_pairs = HEAD_DIM // 4
    inv = ROPE_BASE ** (-np.arange(n_pairs, dtype=np.float64) / n_pairs)
    lane = np.arange(LANES) % HEAD_DIM
    p = np.where(lane[None, :] < HEAD_DIM // 2, row[:, None], col[:, None]).astype(np.float64)
    ang = p * inv[lane % n_pairs][None, :]
    sign = np.where((lane % (2 * n_pairs)) < n_pairs, -1.0, 1.0)
    return np.cos(ang).astype(np.float32), (np.sin(ang) * sign[None, :]).astype(np.float32)


@functools.lru_cache(maxsize=None)
def _filter_tables(seq):
    assert 2 * seq == FFT_N
    half, n2, n1 = np.meshgrid(np.arange(2), np.arange(FFT_N2), np.arange(FFT_HALF_N1), indexing="ij")
    n = FFT_N2 * (half * FFT_HALF_N1 + n1) + n2
    fwd = n < seq
    m = n - seq
    valid = fwd | (m >= 1)
    pos = np.where(valid, np.where(fwd, n, seq - m), 0).astype(np.float64).reshape(2, seq)
    t = pos / max(seq - 1, 1)
    bands = np.linspace(1e-4, POS_BANDS - 1, POS_BANDS)
    ang = (2.0 * math.pi / seq) * pos[..., None] * bands
    z = np.zeros((2, seq, 64), np.float64)
    z[..., 0] = t
    z[..., 1:1 + POS_BANDS] = np.cos(ang)
    z[..., 1 + POS_BANDS:POS_FEATS] = -np.sin(ang)
    aux = np.zeros((2, seq, 8), np.float64)
    aux[..., 0] = t
    aux[..., 1] = valid.reshape(2, seq)
    return z.astype(np.float32), aux.astype(np.float32)


def _realify(m):
    return np.block([[m.real, -m.imag], [m.imag, m.real]])


@functools.lru_cache(maxsize=None)
def _fft_matrices():
    n1 = np.arange(FFT_N1)
    n2 = np.arange(FFT_N2)
    k1 = np.arange(FFT_N1)
    k2 = np.arange(FFT_N2)
    f1 = np.exp(-2j * np.pi * np.outer(k1, n1) / FFT_N1)
    w1_data = _realify(f1[:, :FFT_HALF_N1])
    w1_real = np.concatenate([f1.real, f1.imag], axis=0)
    v1 = _realify(np.conj(f1.T)[:FFT_HALF_N1, :] / FFT_N)
    f2 = np.exp(-2j * np.pi * np.outer(k2, n2) / FFT_N2)
    tw = np.exp(-2j * np.pi * np.outer(k1, n2) / FFT_N)
    w2 = np.stack([_realify(f2 * tw[a][None, :]) for a in range(FFT_N1)])
    v2 = np.stack([_realify(np.conj(f2.T) * np.conj(tw[a])[:, None]) for a in range(FFT_N1)])
    pad = np.zeros((FFT_PITCH - 2 * FFT_N1, FFT_N1))
    w1_data = np.concatenate([w1_data, pad], axis=0)
    w1_real = np.concatenate([w1_real, pad], axis=0)
    return {k: v.astype(np.float32) for k, v in
            dict(w1_data=w1_data, w1_real=w1_real, v1=v1, w2=w2, v2=v2).items()}


N_BIAS_ROWS = 2 * WIN_ROWS - 1
N_BIAS_COLS = 2 * WIN_COLS - 1


def _attn_bias_row_index():
    rows = GRID_W
    groups = (0, 2, rows // ATTN_ROWS_PER_STEP - 1)
    dr = np.full((3, ATTN_ROWS_PER_STEP, ATTN_KEY_ROWS), -1, np.int32)
    for v, g in enumerate(groups):
        ws = min(max(ATTN_ROWS_PER_STEP * g - WIN_ROWS // 2, 0), rows - ATTN_KEY_ROWS)
        for i in range(ATTN_ROWS_PER_STEP):
            r = ATTN_ROWS_PER_STEP * g + i
            r_start = min(max(r - WIN_ROWS // 2, 0), rows - WIN_ROWS)
            for j in range(ATTN_KEY_ROWS):
                kr = ws + j
                if r_start <= kr < r_start + WIN_ROWS:
                    dr[v, i, j] = kr - r + (WIN_ROWS - 1)
    return dr


def _attn_bias_kernel(rpb_ref, o_ref, t_ref):
    head = pl.program_id(0)
    kc = lax.broadcasted_iota(jnp.int32, (GRID_W, LANES), 0)
    lane = lax.broadcasted_iota(jnp.int32, (GRID_W, LANES), 1)
    qc = lane % GRID_W
    c_start = jnp.clip(qc - WIN_COLS // 2, 0, GRID_W - WIN_COLS)
    col_in = (kc >= c_start) & (kc < c_start + WIN_COLS)
    dc = jnp.clip(kc - qc, 1 - WIN_COLS, WIN_COLS - 1) + (WIN_COLS - 1)
    base = head * (N_BIAS_ROWS * N_BIAS_COLS)
    for r in range(N_BIAS_ROWS):
        t = jnp.full((GRID_W, LANES), NEG_BIAS, F32)
        for cidx in range(N_BIAS_COLS):
            t = jnp.where(col_in & (dc == cidx), rpb_ref[base + r * N_BIAS_COLS + cidx] * LOG2_E, t)
        t_ref[r] = t
    dr = _attn_bias_row_index()
    low_half = lane < GRID_W
    masked = jnp.full((GRID_W, LANES), NEG_BIAS, F32)
    for v in range(3):
        for i in range(0, ATTN_ROWS_PER_STEP, 2):
            for j in range(ATTN_KEY_ROWS):
                lo = t_ref[int(dr[v, i, j])] if dr[v, i, j] >= 0 else masked
                hi = t_ref[int(dr[v, i + 1, j])] if dr[v, i + 1, j] >= 0 else masked
                o_ref[v, 0, j * GRID_W:(j + 1) * GRID_W, i * GRID_W:(i + 2) * GRID_W] = jnp.where(
                    low_half, lo, hi)


def _mod_kernel(c_ref, w_ref, b_ref, o_ref):
    c = c_ref[...]
    s = c * jax.nn.sigmoid(c)
    o_ref[...] = jnp.dot(s, w_ref[...], precision=lax.Precision.HIGHEST,
                         preferred_element_type=F32) + b_ref[...]


def _norm_modulate(x, gain, shift, scale):
    ms = jnp.mean(x * x, axis=-1, keepdims=True)
    y = x * lax.rsqrt(ms + RMS_EPS) * gain
    return y * (1.0 + scale) + shift


def _rope(t, cos, sin_signed):
    n_pairs = HEAD_DIM // 4
    lane = lax.broadcasted_iota(jnp.int32, t.shape, 1)
    first = (lane % (2 * n_pairs)) < n_pairs
    partner = jnp.where(first, pltpu.roll(t, LANES - n_pairs, 1), pltpu.roll(t, n_pairs, 1))
    return t * cos + partner * sin_signed


def _in_proj_kernel(x_ref, prev_ref, next_ref, mod_ref, g_ref, w_ref, b_ref, cos_ref, sin_ref, cw_ref, cb_ref,
                    qr_ref, qp_ref, kr_ref, v_ref, gt_ref, u_ref, x0_ref, hy_ref):
    i = pl.program_id(1)
    n_tiles = pl.num_programs(1)
    xx = jnp.concatenate([prev_ref[0], x_ref[0], next_ref[0]], axis=0)
    h_ext = _norm_modulate(xx, g_ref[...], mod_ref[0, 0:1, :], mod_ref[0, 1:2, :]).astype(BF16)
    h = h_ext[HALO:HALO + TOKEN_TILE]
    cos = cos_ref[...]
    sin = sin_ref[...]

    def proj(lo, hi, rows=h):
        return jnp.dot(rows, w_ref[:, lo:hi], preferred_element_type=F32) + b_ref[:, lo:hi]

    q = proj(0, D_ATTN) * (HEAD_DIM ** -0.5 * LOG2_E)
    qp_ref[0] = q.astype(BF16)
    k = proj(D_ATTN, 2 * D_ATTN)
    for c in range(D_ATTN // LANES):
        lanes = slice(c * LANES, (c + 1) * LANES)
        qr_ref[0, :, lanes] = _rope(q[:, lanes], cos, sin).astype(BF16)
        kr_ref[0, :, lanes] = _rope(k[:, lanes], cos, sin).astype(BF16)
    v_t = proj(2 * D_ATTN, 3 * D_ATTN).T.astype(BF16)
    for c in range(TOKEN_TILE // ATTN_KEY_CHUNK):
        v_ref[0, c] = v_t[:, c * ATTN_KEY_CHUNK:(c + 1) * ATTN_KEY_CHUNK]
    hy_lo = 3 * D_ATTN
    gl_lo = hy_lo + 3 * D_HYENA
    for c in range(4):
        w = D_MODEL // 2
        gt_ref[0, :, c * w:(c + 1) * w] = jax.nn.sigmoid(
            proj(gl_lo + c * w, gl_lo + (c + 1) * w)).astype(BF16)

    slabs_per_part = D_HYENA // LANES
    for c in range(3):
        hy = proj(hy_lo + c * D_HYENA, hy_lo + (c + 1) * D_HYENA, h_ext)
        for s in range(slabs_per_part):
            hy_ref[c * slabs_per_part + s] = hy[:, s * LANES:(s + 1) * LANES]
    zero_row = jnp.zeros((1, LANES), F32)

    @pl.when(i == 0)
    def _():
        for s in range(3 * slabs_per_part):
            hy_ref[s, HALO - 1:HALO, :] = zero_row

    @pl.when(i == n_tiles - 1)
    def _():
        for s in range(3 * slabs_per_part):
            hy_ref[s, HALO + TOKEN_TILE:HALO + TOKEN_TILE + 1, :] = zero_row

    def conv(s, r0, rows):
        lanes = slice(s * LANES, (s + 1) * LANES)
        tap = lambda j: hy_ref[s, pl.ds(HALO - 1 + j + r0, rows, stride=1), :]
        return (tap(0) * cw_ref[0:1, lanes] + tap(1) * cw_ref[1:2, lanes] + tap(2) * cw_ref[2:3, lanes]
                + cb_ref[:, lanes])

    for s in range(slabs_per_part):
        lanes = slice(s * LANES, (s + 1) * LANES)
        for j in range(TILE_GROUPS):
            r0 = j * FFT_N2
            x0_ref[0, r0:r0 + FFT_N2, lanes] = conv(s, r0, FFT_N2).astype(BF16)
            u_ref[0, j * SEQ_PITCH:j * SEQ_PITCH + FFT_N2, lanes] = (
                conv(slabs_per_part + s, r0, FFT_N2) * conv(2 * slabs_per_part + s, r0, FFT_N2))
            u_ref[0, j * SEQ_PITCH + FFT_N2:(j + 1) * SEQ_PITCH, lanes] = jnp.zeros((SEQ_PITCH - FFT_N2, LANES), F32)


def _ctx_kv_kernel(x_ref, mod_ref, g_ref, w_ref, b_ref, k_ref, v_ref):
    h = _norm_modulate(x_ref[0], g_ref[...], mod_ref[0:1, :], mod_ref[1:2, :]).astype(BF16)
    kv = jnp.dot(h, w_ref[...], preferred_element_type=F32) + b_ref[...]
    k_ref[0] = kv[:, :D_ATTN].astype(BF16)
    v_ref[0] = kv[:, D_ATTN:].T.astype(BF16)


def _attn_window_start(g):
    return jnp.clip(ATTN_ROWS_PER_STEP * g - WIN_ROWS // 2, 0, GRID_W - ATTN_KEY_ROWS)


def _attn_kernel(qr_ref, qp_ref, k_ref, vt_ref, kc_ref, vct_ref, bias_ref, o_ref):
    g = pl.program_id(1)
    win = _attn_window_start(g)
    key0 = pl.multiple_of(win * GRID_W, ATTN_KEY_CHUNK)
    chunk0 = win // (ATTN_KEY_CHUNK // GRID_W)
    nt = (((1,), (1,)), ((), ()))
    quad_w = 4 * HEAD_DIM
    lane = lax.broadcasted_iota(jnp.int32, (1, quad_w), 1)
    zero = jnp.zeros((), BF16)
    def scores(head):
        quad, hh = divmod(head, 4)
        ql = slice(quad * quad_w, (quad + 1) * quad_w)
        mine = (lane >= hh * HEAD_DIM) & (lane < (hh + 1) * HEAD_DIM)
        s_nb = lax.dot_general(k_ref[0, pl.ds(key0, ATTN_KEYS), ql], jnp.where(mine, qr_ref[0, :, ql], zero), nt,
                               preferred_element_type=F32)
        s_cx = lax.dot_general(kc_ref[0, :, ql], jnp.where(mine, qp_ref[0, :, ql], zero), nt,
                               preferred_element_type=F32)
        return s_nb + bias_ref[0, head], s_cx

    def with_ones(v):
        return jnp.concatenate([v, jnp.ones((16, v.shape[1]), BF16)], axis=0)

    def attend(head, s_nb, s_cx):
        rows = slice(head * HEAD_DIM, (head + 1) * HEAD_DIM)
        m = jnp.maximum(jnp.max(s_nb, axis=0, keepdims=True), jnp.max(s_cx, axis=0, keepdims=True))
        p_nb = jnp.exp2(s_nb - m).astype(BF16)
        p_cx = jnp.exp2(s_cx - m).astype(BF16)
        v_win = jnp.concatenate([vt_ref[0, chunk0 + c, rows, :] for c in range(ATTN_KEYS // ATTN_KEY_CHUNK)],
                                axis=1)
        o = (jnp.dot(with_ones(v_win), p_nb, preferred_element_type=F32)
             + jnp.dot(with_ones(vct_ref[0, rows, :]), p_cx, preferred_element_type=F32))
        return o[:HEAD_DIM] / o[HEAD_DIM:HEAD_DIM + 1]

    outs = []
    nxt = scores(0)
    for head in range(N_HEADS):
        cur = nxt
        if head + 1 < N_HEADS:
            nxt = scores(head + 1)
        outs.append(attend(head, *cur))
    o_ref[0] = jnp.concatenate(outs, axis=0).T.astype(BF16)


def _filt_kernel(z_ref, aux_ref, w1_ref, b1_ref, w2_ref, b2_ref, w3_ref, b3_ref, freq_ref, decay_ref,
                 o_ref, asum_ref):
    i = pl.program_id(1)
    hp = lax.Precision.HIGHEST
    h = jnp.sin(freq_ref[0:1, :] * (jnp.dot(z_ref[0], w1_ref[...], precision=hp,
                                            preferred_element_type=F32) + b1_ref[...]))
    h = jnp.sin(freq_ref[1:2, :] * (jnp.dot(h, w2_ref[...], precision=hp,
                                            preferred_element_type=F32) + b2_ref[...]))
    taps = jnp.dot(h, w3_ref[...], precision=hp, preferred_element_type=F32) + b3_ref[...]
    t = aux_ref[0, :, 0:1]
    valid = aux_ref[0, :, 1:2] > 0.5
    k = jnp.where(valid, taps * jnp.exp(-t * jnp.abs(decay_ref[0])), 0.0)
    part = jnp.sum(jnp.abs(k), axis=0, keepdims=True)

    @pl.when(i == 0)
    def _():
        asum_ref[...] = jnp.zeros_like(asum_ref)

    asum_ref[0] += jnp.broadcast_to(part, asum_ref.shape[1:])
    o_ref[...] = k.reshape(o_ref.shape)


def _fft_stage1(load_group, w1_ref, a_ref):
    n_slabs = a_ref.shape[0]

    def body(n2, carry):
        x = load_group(n2)
        a = jnp.dot(w1_ref[...], x, preferred_element_type=F32)
        base = pl.multiple_of(n2 * FFT_PITCH, 8)
        for s in range(n_slabs):
            a_ref[s, pl.ds(base, FFT_PITCH), :] = a[:, s * LANES:(s + 1) * LANES]
        return carry

    lax.fori_loop(0, FFT_N2, body, 0, unroll=FFT_UNROLL)


def _fft_load_k1(a_ref, k1):
    parts = []
    for off in (0, FFT_N1):
        parts.append(jnp.concatenate(
            [a_ref[s, pl.ds(k1 + off, FFT_N2, stride=FFT_PITCH), :] for s in range(a_ref.shape[0])], axis=1))
    return jnp.concatenate(parts, axis=0)


def _filt_fft_kernel(k_ref, asum_ref, w1_ref, w2_ref, o_ref, a_ref):
    step = pl.program_id(1)

    @pl.when(step == 0)
    def _():
        norm = asum_ref[0, 0:1, :] + asum_ref[1, 0:1, :]

        def load_group(n2):
            return (k_ref[pl.ds(pl.multiple_of(n2 * FFT_N1, FFT_N1), FFT_N1), :] / norm).astype(BF16)
        _fft_stage1(load_group, w1_ref, a_ref)

    for j in range(FFT_K1_PER_STEP):
        b = _fft_load_k1(a_ref, step * FFT_K1_PER_STEP + j).astype(BF16)
        o_ref[j] = jnp.dot(w2_ref[j], b, preferred_element_type=F32)


def _hy_conv_kernel(u_ref, kf_ref, w1_ref, w2_ref, v2_ref, v1_ref, o_ref, a_ref):
    step = pl.program_id(1)
    n_slabs = a_ref.shape[0]

    def tokens_of(member, s, n2):
        return (2 * s + member, pl.ds(n2, FFT_HALF_N1, stride=SEQ_PITCH), slice(None))

    @pl.when(step == 0)
    def _():
        o_ref[...] = jnp.zeros_like(o_ref)

        def load_group(n2):
            return jnp.concatenate(
                [jnp.concatenate([u_ref[tokens_of(member, s, n2)] for s in range(n_slabs)], axis=1)
                 for member in range(2)], axis=0).astype(BF16)
        _fft_stage1(load_group, w1_ref, a_ref)

    for j in range(FFT_K1_PER_STEP):
        k1 = step * FFT_K1_PER_STEP + j
        b = _fft_load_k1(a_ref, k1).astype(BF16)
        x = jnp.dot(w2_ref[j], b, preferred_element_type=F32)
        kf = kf_ref[j]
        kf = jnp.concatenate([kf] * n_slabs, axis=1)
        xr, xi = x[:FFT_N2], x[FFT_N2:]
        kr, ki = kf[:FFT_N2], kf[FFT_N2:]
        y = jnp.concatenate([xr * kr - xi * ki, xr * ki + xi * kr], axis=0).astype(BF16)
        d = jnp.dot(v2_ref[j], y, preferred_element_type=F32)
        for s in range(n_slabs):
            lanes = slice(s * LANES, (s + 1) * LANES)
            a_ref[s, pl.ds(k1, FFT_N2, stride=FFT_PITCH), :] = d[:FFT_N2, lanes]
            a_ref[s, pl.ds(k1 + FFT_N1, FFT_N2, stride=FFT_PITCH), :] = d[FFT_N2:, lanes]

    @pl.when(step == FFT_STEPS - 1)
    def _():
        def body(n2, carry):
            base = pl.multiple_of(n2 * FFT_PITCH, 8)
            d = jnp.concatenate([a_ref[s, pl.ds(base, 2 * FFT_N1), :] for s in range(n_slabs)], axis=1)
            y = jnp.dot(v1_ref[...], d.astype(BF16), preferred_element_type=F32)
            for s in range(n_slabs):
                for member in range(2):
                    o_ref[tokens_of(member, s, n2)] = y[member * FFT_HALF_N1:(member + 1) * FFT_HALF_N1,
                                                        s * LANES:(s + 1) * LANES]
            return carry

        lax.fori_loop(0, FFT_N2, body, 0, unroll=FFT_UNROLL)


def _from_pitched(ref):
    return jnp.concatenate([ref[0, j * SEQ_PITCH:j * SEQ_PITCH + FFT_N2, :] for j in range(TILE_GROUPS)], axis=0)


def _merge_kernel(x_ref, yna_ref, x0_ref, u_ref, yc_ref, gt_ref, mod_ref, skip_ref, gpost_ref,
                  wna_ref, why_ref, wout_ref, o_ref):
    y_hy = (x0_ref[0].astype(F32) * (_from_pitched(yc_ref) + _from_pitched(u_ref) * skip_ref[...])).astype(BF16)
    a = jnp.dot(yna_ref[0], wna_ref[...], preferred_element_type=F32)
    b = jnp.dot(y_hy, why_ref[...], preferred_element_type=F32)
    g_na = gt_ref[0, :, :D_MODEL].astype(F32)
    g_hy = gt_ref[0, :, D_MODEL:].astype(F32)
    m = (g_na * a + g_hy * b).astype(BF16)
    o = jnp.dot(m, wout_ref[...], preferred_element_type=F32)
    ms = jnp.mean(o * o, axis=-1, keepdims=True)
    o = o * lax.rsqrt(ms + RMS_EPS) * gpost_ref[...]
    o_ref[0] = x_ref[0] + mod_ref[0, 2:3, :] * o


def _gelu_tanh(a):
    return 0.5 * a * (1.0 + jnp.tanh(math.sqrt(2.0 / math.pi) * (a + 0.044715 * (a * a * a))))


def _ffn_kernel(x_ref, prev_ref, next_ref, mod_ref, gpre_ref, gpost_ref, wup_ref, cw_ref, cb_ref, wdn_ref,
                o_ref, u_ref, act_ref):
    i = pl.program_id(1)
    n_tiles = pl.num_programs(1)
    x = x_ref[0]
    xx = jnp.concatenate([prev_ref[0], x, next_ref[0]], axis=0)
    h = _norm_modulate(xx, gpre_ref[...], mod_ref[0, 3:4, :], mod_ref[0, 4:5, :])
    row = lax.broadcasted_iota(jnp.int32, (xx.shape[0], 1), 0)
    inside = ((row >= HALO) | (i > 0)) & ((row < HALO + TOKEN_TILE) | (i < n_tiles - 1))
    h = jnp.where(inside, h, 0.0).astype(BF16)
    n_chunks = D_FF // FF_CHUNK
    n_slabs = FF_CHUNK // LANES

    def conv(buf, half, s, lo, r0):
        lanes = slice(half * D_FF + lo, half * D_FF + lo + LANES)
        tap = lambda j: u_ref[buf, half, s, pl.ds(HALO - 1 + j + r0, FFN_ROW_BLOCK, stride=1), :]
        return (tap(0) * cw_ref[0:1, lanes] + tap(1) * cw_ref[1:2, lanes] + tap(2) * cw_ref[2:3, lanes]
                + cb_ref[:, lanes])

    for c in range(n_chunks):
        buf = c % 2
        for half in range(2):
            lo = half * D_FF + c * FF_CHUNK
            u = jnp.dot(h, wup_ref[:, lo:lo + FF_CHUNK], preferred_element_type=F32)
            for s in range(n_slabs):
                u_ref[buf, half, s] = u[:, s * LANES:(s + 1) * LANES]
        for s in range(n_slabs):
            lo = c * FF_CHUNK + s * LANES
            for r0 in range(0, TOKEN_TILE, FFN_ROW_BLOCK):
                act_ref[r0:r0 + FFN_ROW_BLOCK, lo:lo + LANES] = (
                    _gelu_tanh(conv(buf, 0, s, lo, r0)) * conv(buf, 1, s, lo, r0)).astype(BF16)
    y = jnp.dot(act_ref[...], wdn_ref[...], preferred_element_type=F32)
    ms = jnp.mean(y * y, axis=-1, keepdims=True)
    y = y * lax.rsqrt(ms + RMS_EPS) * gpost_ref[...]
    o_ref[0] = x + mod_ref[0, 5:6, :] * y


def _const_spec(shape):
    nd = len(shape)
    return pl.BlockSpec(shape, lambda *_: (0,) * nd, pipeline_mode=pl.Buffered(1))


def kernel(x, c, ctx, c_ctx, w_mod, b_mod, norm_mix_pre, norm_mix_post, norm_ffn_pre, norm_ffn_post, w_in, b_in, na_rpb, hy_conv_w, hy_conv_b, hy_filt_w1, hy_filt_b1, hy_filt_w2, hy_filt_b2, hy_filt_w3, hy_filt_b3, hy_sin_freq, hy_decay, hy_skip, w_o_na, w_o_hy, w_out, ffn_w_up, ffn_conv_w, ffn_conv_b, ffn_w_down):
    batch, seq, d = x.shape
    n_ctx = ctx.shape[1]
    assert d == D_MODEL and 2 * seq == FFT_N and seq == GRID_W * GRID_W and batch % 2 == 0
    assert w_mod.shape[0] == 1, "single-layer block"
    n_tiles = seq // TOKEN_TILE
    d_in = w_in.shape[2]
    row2 = lambda a: a.reshape(1, -1)

    c_all = jnp.zeros((8, d), F32).at[:batch].set(c).at[batch].set(c_ctx)
    mod_n = 1024
    mod = pl.pallas_call(
        _mod_kernel,
        grid=(N_MOD * d // mod_n,),
        in_specs=[_const_spec((8, d)),
                  pl.BlockSpec((d, mod_n), lambda j: (0, j)),
                  pl.BlockSpec((1, mod_n), lambda j: (0, j))],
        out_specs=pl.BlockSpec((8, mod_n), lambda j: (0, j)),
        out_shape=jax.ShapeDtypeStruct((8, N_MOD * d), F32),
        compiler_params=_cparams(("arbitrary",)),
        name="mod",
    )(c_all, w_mod[0], row2(b_mod[0]))
    mod_lat = jnp.pad(mod[:batch].reshape(batch, N_MOD, d), ((0, 0), (0, 8 - N_MOD), (0, 0)))
    mod_ctx = jnp.pad(mod[batch].reshape(N_MOD, d), ((0, 8 - N_MOD), (0, 0)))

    w_in_b = w_in[0].astype(BF16)
    b_in_r = row2(b_in[0])
    g_mix_pre = row2(norm_mix_pre[0])

    k_ctx, v_ctx = pl.pallas_call(
        _ctx_kv_kernel,
        grid=(batch,),
        in_specs=[pl.BlockSpec((1, n_ctx, d), lambda b: (b, 0, 0)),
                  _const_spec((8, d)), _const_spec((1, d)),
                  _const_spec((d, 2 * D_ATTN)), _const_spec((1, 2 * D_ATTN))],
        out_specs=[pl.BlockSpec((1, n_ctx, D_ATTN), lambda b: (b, 0, 0)),
                   pl.BlockSpec((1, D_ATTN, n_ctx), lambda b: (b, 0, 0))],
        out_shape=[jax.ShapeDtypeStruct((batch, n_ctx, D_ATTN), BF16),
                   jax.ShapeDtypeStruct((batch, D_ATTN, n_ctx), BF16)],
        compiler_params=_cparams(("arbitrary",)),
        name="ctx_kv",
    )(ctx, mod_ctx, g_mix_pre, w_in_b[:, D_ATTN:3 * D_ATTN], b_in_r[:, D_ATTN:3 * D_ATTN])

    cos_t, sin_t = _rope_tables(seq)
    tok = lambda w: pl.BlockSpec((1, TOKEN_TILE, w), lambda b, i: (b, i, 0))
    mod_spec = pl.BlockSpec((1, 8, d), lambda b, i: (b, 0, 0))
    rope_spec = pl.BlockSpec((TOKEN_TILE, LANES), lambda b, i: (i, 0))
    halo_blocks = TOKEN_TILE // HALO
    n_halo_blocks = seq // HALO
    prev_spec = lambda w: pl.BlockSpec((1, HALO, w), lambda b, i: (b, jnp.maximum(i * halo_blocks - 1, 0), 0))
    next_spec = lambda w: pl.BlockSpec(
        (1, HALO, w), lambda b, i: (b, jnp.minimum((i + 1) * halo_blocks, n_halo_blocks - 1), 0))
    pitched_spec = pl.BlockSpec((1, TILE_GROUPS * SEQ_PITCH, D_HYENA), lambda b, i: (b, i, 0))
    pitched_shape = jax.ShapeDtypeStruct((batch, PITCHED_ROWS, D_HYENA), F32)
    q_rot, q_plain, k_rot, v_lat, gates, u_p, x0 = pl.pallas_call(
        _in_proj_kernel,
        grid=(batch, n_tiles),
        in_specs=[tok(d), prev_spec(d), next_spec(d), mod_spec, _const_spec((1, d)), _const_spec((d, d_in)),
                  _const_spec((1, d_in)), rope_spec, rope_spec,
                  _const_spec((3, 3 * D_HYENA)), _const_spec((1, 3 * D_HYENA))],
        out_specs=[tok(D_ATTN)] * 3
        + [pl.BlockSpec((1, TOKEN_TILE // ATTN_KEY_CHUNK, D_ATTN, ATTN_KEY_CHUNK), lambda b, i: (b, i, 0, 0)),
           tok(2 * d), pitched_spec, tok(D_HYENA)],
        out_shape=[jax.ShapeDtypeStruct((batch, seq, D_ATTN), BF16)] * 3
        + [jax.ShapeDtypeStruct((batch, seq // ATTN_KEY_CHUNK, D_ATTN, ATTN_KEY_CHUNK), BF16),
           jax.ShapeDtypeStruct((batch, seq, 2 * d), BF16), pitched_shape,
           jax.ShapeDtypeStruct((batch, seq, D_HYENA), BF16)],
        scratch_shapes=[pltpu.VMEM((3 * D_HYENA // LANES, TOKEN_TILE + 2 * HALO, LANES), F32)],
        compiler_params=_cparams(("arbitrary", "arbitrary")),
        name="in_proj",
    )(x, x, x, mod_lat, g_mix_pre, w_in_b, b_in_r, jnp.asarray(cos_t), jnp.asarray(sin_t),
      hy_conv_w[0], row2(hy_conv_b[0]))

    bias = pl.pallas_call(
        _attn_bias_kernel,
        grid=(N_HEADS,),
        in_specs=[pl.BlockSpec(memory_space=pltpu.SMEM)],
        out_specs=pl.BlockSpec((3, 1, ATTN_KEYS, ATTN_Q), lambda h: (0, h, 0, 0)),
        out_shape=jax.ShapeDtypeStruct((3, N_HEADS, ATTN_KEYS, ATTN_Q), F32),
        scratch_shapes=[pltpu.VMEM((N_BIAS_ROWS, GRID_W, LANES), F32)],
        compiler_params=_cparams(("arbitrary",)),
        name="attn_bias",
    )(na_rpb[0].reshape(-1))
    n_groups = GRID_W // ATTN_ROWS_PER_STEP
    q_spec = pl.BlockSpec((1, ATTN_Q, D_ATTN), lambda b, g: (b, g, 0))
    full = lambda n: pl.BlockSpec((1, n, D_ATTN), lambda b, g: (b, 0, 0))
    bias_spec = pl.BlockSpec(
        (1, N_HEADS, ATTN_KEYS, ATTN_Q),
        lambda b, g: ((g > 0).astype(jnp.int32) + (g == n_groups - 1).astype(jnp.int32), 0, 0, 0))
    vt_spec = pl.BlockSpec((1, seq // ATTN_KEY_CHUNK, D_ATTN, ATTN_KEY_CHUNK), lambda b, g: (b, 0, 0, 0))
    vct_spec = pl.BlockSpec((1, D_ATTN, n_ctx), lambda b, g: (b, 0, 0))
    y_na = pl.pallas_call(
        _attn_kernel,
        grid=(batch, n_groups),
        in_specs=[q_spec, q_spec, full(seq), vt_spec, full(n_ctx), vct_spec, bias_spec],
        out_specs=q_spec,
        out_shape=jax.ShapeDtypeStruct((batch, seq, D_ATTN), BF16),
        compiler_params=_cparams(("arbitrary", "arbitrary")),
        name="attn",
    )(q_rot, q_plain, k_rot, v_lat, k_ctx, v_ctx, bias)

    z_t, aux_t = _filter_tables(seq)
    filt_n2 = 32
    filt_rows = filt_n2 * FFT_HALF_N1
    w1_pad = jnp.pad(hy_filt_w1[0], ((0, z_t.shape[2] - POS_FEATS), (0, 0)))
    k_circ, k_asum = pl.pallas_call(
        _filt_kernel,
        grid=(2, seq // filt_rows),
        in_specs=[pl.BlockSpec((1, filt_rows, z_t.shape[2]), lambda hf, i: (hf, i, 0)),
                  pl.BlockSpec((1, filt_rows, aux_t.shape[2]), lambda hf, i: (hf, i, 0)),
                  _const_spec(w1_pad.shape), _const_spec((1, FILTER_HIDDEN)),
                  _const_spec((FILTER_HIDDEN, FILTER_HIDDEN)), _const_spec((1, FILTER_HIDDEN)),
                  pl.BlockSpec((FILTER_HIDDEN, D_HYENA), lambda hf, i: (0, hf)),
                  pl.BlockSpec((1, D_HYENA), lambda hf, i: (0, hf)),
                  _const_spec((2, FILTER_HIDDEN)),
                  pl.BlockSpec((1, 1, D_HYENA), lambda hf, i: (hf, 0, 0))],
        out_specs=[pl.BlockSpec((filt_n2, FFT_HALF_N1, D_HYENA), lambda hf, i: (i, hf, 0)),
                   pl.BlockSpec((1, 8, D_HYENA), lambda hf, i: (hf, 0, 0))],
        out_shape=[jax.ShapeDtypeStruct((FFT_N2, FFT_N1, D_HYENA), F32),
                   jax.ShapeDtypeStruct((2, 8, D_HYENA), F32)],
        compiler_params=_cparams(("arbitrary", "arbitrary")),
        name="filt",
    )(jnp.asarray(z_t), jnp.asarray(aux_t), w1_pad, row2(hy_filt_b1[0]), hy_filt_w2[0], row2(hy_filt_b2[0]),
      hy_filt_w3[0], row2(hy_filt_b3[0]), hy_sin_freq[0], hy_decay[0].reshape(2, 1, D_HYENA))
    k_circ = k_circ.reshape(FFT_N, D_HYENA)

    mats = _fft_matrices()
    mats = {k: jnp.asarray(v).astype(BF16) for k, v in mats.items()}
    w2_all = mats["w2"]
    v2_all = mats["v2"]
    step_mat_spec = pl.BlockSpec((FFT_K1_PER_STEP, 2 * FFT_N2, 2 * FFT_N2), lambda cb, s: (s, 0, 0))
    filt_slabs = 2
    kf = pl.pallas_call(
        _filt_fft_kernel,
        grid=(D_HYENA // (filt_slabs * LANES), FFT_STEPS),
        in_specs=[pl.BlockSpec((FFT_N, filt_slabs * LANES), lambda cb, s: (0, cb)),
                  pl.BlockSpec((2, 8, filt_slabs * LANES), lambda cb, s: (0, 0, cb)),
                  _const_spec((FFT_PITCH, FFT_N1)), step_mat_spec],
        out_specs=pl.BlockSpec((FFT_K1_PER_STEP, 2 * FFT_N2, filt_slabs * LANES), lambda cb, s: (s, 0, cb)),
        out_shape=jax.ShapeDtypeStruct((FFT_N1, 2 * FFT_N2, D_HYENA), F32),
        scratch_shapes=[pltpu.VMEM((filt_slabs, FFT_N2 * FFT_PITCH, LANES), F32)],
        compiler_params=_cparams(("arbitrary", "arbitrary")),
        name="filt_fft",
    )(k_circ, k_asum, mats["w1_real"], w2_all)

    n_pairs = batch // 2
    pair_spec = pl.BlockSpec((batch, PITCHED_ROWS, LANES), lambda cb, s: (0, 0, cb),
                             pipeline_mode=pl.Buffered(1))
    y_conv = pl.pallas_call(
        _hy_conv_kernel,
        grid=(D_HYENA // LANES, FFT_STEPS),
        in_specs=[pair_spec,
                  pl.BlockSpec((FFT_K1_PER_STEP, 2 * FFT_N2, LANES), lambda cb, s: (s, 0, cb)),
                  _const_spec((FFT_PITCH, FFT_N1)), step_mat_spec, step_mat_spec,
                  _const_spec((FFT_N1, 2 * FFT_N1))],
        out_specs=pair_spec,
        out_shape=pitched_shape,
        scratch_shapes=[pltpu.VMEM((n_pairs, FFT_N2 * FFT_PITCH, LANES), F32)],
        compiler_params=_cparams(("arbitrary", "arbitrary")),
        name="hy_conv",
    )(u_p, kf, mats["w1_data"], w2_all, v2_all, mats["v1"])

    x1 = pl.pallas_call(
        _merge_kernel,
        grid=(batch, n_tiles),
        in_specs=[tok(d), tok(D_ATTN), tok(D_HYENA), pitched_spec, pitched_spec, tok(2 * d), mod_spec,
                  _const_spec((1, D_HYENA)), _const_spec((1, d)),
                  _const_spec((D_ATTN, d)), _const_spec((D_HYENA, d)), _const_spec((d, d))],
        out_specs=tok(d),
        out_shape=jax.ShapeDtypeStruct((batch, seq, d), F32),
        compiler_params=_cparams(("arbitrary", "arbitrary")),
        name="merge",
    )(x, y_na, x0, u_p, y_conv, gates, mod_lat, row2(hy_skip[0]), row2(norm_mix_post[0]),
      w_o_na[0].astype(BF16), w_o_hy[0].astype(BF16), w_out[0].astype(BF16))

    w_up_c = ffn_w_up[0].astype(BF16)
    conv_w_c = ffn_conv_w[0]
    conv_b_c = row2(ffn_conv_b[0])
    w_dn_c = ffn_w_down[0].astype(BF16)
    out = pl.pallas_call(
        _ffn_kernel,
        grid=(batch, n_tiles),
        in_specs=[tok(d), prev_spec(d), next_spec(d), mod_spec, _const_spec((1, d)), _const_spec((1, d)),
                  _const_spec(w_up_c.shape), _const_spec(conv_w_c.shape), _const_spec(conv_b_c.shape),
                  _const_spec(w_dn_c.shape)],
        out_specs=tok(d),
        out_shape=jax.ShapeDtypeStruct((batch, seq, d), F32),
        scratch_shapes=[pltpu.VMEM((2, 2, FF_CHUNK // LANES, TOKEN_TILE + 2 * HALO, LANES), F32),
                        pltpu.VMEM((TOKEN_TILE, D_FF), BF16)],
        compiler_params=_cparams(("arbitrary", "arbitrary")),
        name="ffn",
    )(x1, x1, x1, mod_lat, row2(norm_ffn_pre[0]), row2(norm_ffn_post[0]), w_up_c, conv_w_c, conv_b_c, w_dn_c)
    return out
```

```python
import functools
import math

import jax
import jax.numpy as jnp
import numpy as np
from jax import lax
from jax.experimental import pallas as pl
from jax.experimental.pallas import tpu as pltpu

F32 = jnp.float32
BF16 = jnp.bfloat16

D_MODEL = 1024
N_HEADS = 8
HEAD_DIM = 64
D_ATTN = N_HEADS * HEAD_DIM
D_HYENA = 512
GRID_W = 64
WIN_ROWS = 8
WIN_COLS = 16
POS_BANDS = 16
POS_FEATS = 1 + 2 * POS_BANDS
FILTER_HIDDEN = 64
D_FF = 2816
N_MOD = 6
ROPE_BASE = 10000.0
RMS_EPS = 1e-6
NEG_BIAS = -1e30
LOG2_E = math.log2(math.e)

LANES = 128
VMEM_LIMIT_BYTES = 56 * 1024 * 1024

FFT_N1 = 64
FFT_N2 = 128
FFT_N = FFT_N1 * FFT_N2
FFT_HALF_N1 = FFT_N1 // 2
FFT_PITCH = 2 * FFT_N1 + 8
FFT_K1_PER_STEP = 8
FFT_UNROLL = 8
FFT_STEPS = FFT_N1 // FFT_K1_PER_STEP

TOKEN_TILE = 512
SEQ_PITCH = FFT_N2 + 8
PITCHED_ROWS = FFT_HALF_N1 * SEQ_PITCH
TILE_GROUPS = TOKEN_TILE // FFT_N2
ATTN_ROWS_PER_STEP = 4
ATTN_Q = ATTN_ROWS_PER_STEP * GRID_W
ATTN_KEY_ROWS = 12
ATTN_KEYS = ATTN_KEY_ROWS * GRID_W
ATTN_KEY_CHUNK = 256
FF_CHUNK = 256
HALO = 8
FFN_ROW_BLOCK = 64


def _cparams(sem):
    return pltpu.CompilerParams(dimension_semantics=sem, vmem_limit_bytes=VMEM_LIMIT_BYTES)


@functools.lru_cache(maxsize=None)
def _rope_tables(seq):
    pos = np.arange(seq)
    row = pos // GRID_W
    col = pos % GRID_W
    n_pairs = HEAD_DIM // 4
    inv = ROPE_BASE ** (-np.arange(n_pairs, dtype=np.float64) / n_pairs)
    lane = np.arange(LANES) % HEAD_DIM
    p = np.where(lane[None, :] < HEAD_DIM // 2, row[:, None], col[:, None]).astype(np.float64)
    ang = p * inv[lane % n_pairs][None, :]
    sign = np.where((lane % (2 * n_pairs)) < n_pairs, -1.0, 1.0)
    return np.cos(ang).astype(np.float32), (np.sin(ang) * sign[None, :]).astype(np.float32)


@functools.lru_cache(maxsize=None)
def _filter_tables(seq):
    assert 2 * seq == FFT_N
    half, n2, n1 = np.meshgrid(np.arange(2), np.arange(FFT_N2), np.arange(FFT_HALF_N1), indexing="ij")
    n = FFT_N2 * (half * FFT_HALF_N1 + n1) + n2
    fwd = n < seq
    m = n - seq
    valid = fwd | (m >= 1)
    pos = np.where(valid, np.where(fwd, n, seq - m), 0).astype(np.float64).reshape(2, seq)
    t = pos / max(seq - 1, 1)
    bands = np.linspace(1e-4, POS_BANDS - 1, POS_BANDS)
    ang = (2.0 * math.pi / seq) * pos[..., None] * bands
    z = np.zeros((2, seq, 64), np.float64)
    z[..., 0] = t
    z[..., 1:1 + POS_BANDS] = np.cos(ang)
    z[..., 1 + POS_BANDS:POS_FEATS] = -np.sin(ang)
    aux = np.zeros((2, seq, 8), np.float64)
    aux[..., 0] = t
    aux[..., 1] = valid.reshape(2, seq)
    return z.astype(np.float32), aux.astype(np.float32)


def _realify(m):
    return np.block([[m.real, -m.imag], [m.imag, m.real]])


@functools.lru_cache(maxsize=None)
def _fft_matrices():
    n1 = np.arange(FFT_N1)
    n2 = np.arange(FFT_N2)
    k1 = np.arange(FFT_N1)
    k2 = np.arange(FFT_N2)
    f1 = np.exp(-2j * np.pi * np.outer(k1, n1) / FFT_N1)
    w1_data = _realify(f1[:, :FFT_HALF_N1])
    w1_real = np.concatenate([f1.real, f1.imag], axis=0)
    v1 = _realify(np.conj(f1.T)[:FFT_HALF_N1, :] / FFT_N)
    f2 = np.exp(-2j * np.pi * np.outer(k2, n2) / FFT_N2)
    tw = np.exp(-2j * np.pi * np.outer(k1, n2) / FFT_N)
    w2 = np.stack([_realify(f2 * tw[a][None, :]) for a in range(FFT_N1)])
    v2 = np.stack([_realify(np.conj(f2.T) * np.conj(tw[a])[:, None]) for a in range(FFT_N1)])
    pad = np.zeros((FFT_PITCH - 2 * FFT_N1, FFT_N1))
    w1_data = np.concatenate([w1_data, pad], axis=0)
    w1_real = np.concatenate([w1_real, pad], axis=0)
    return {k: v.astype(np.float32) for k, v in
            dict(w1_data=w1_data, w1_real=w1_real, v1=v1, w2=w2, v2=v2).items()}


N_BIAS_ROWS = 2 * WIN_ROWS - 1
N_BIAS_COLS = 2 * WIN_COLS - 1


def _attn_bias_row_index():
    rows = GRID_W
    groups = (0, 2, rows // ATTN_ROWS_PER_STEP - 1)
    dr = np.full((3, ATTN_ROWS_PER_STEP, ATTN_KEY_ROWS), -1, np.int32)
    for v, g in enumerate(groups):
        ws = min(max(ATTN_ROWS_PER_STEP * g - WIN_ROWS // 2, 0), rows - ATTN_KEY_ROWS)
        for i in range(ATTN_ROWS_PER_STEP):
            r = ATTN_ROWS_PER_STEP * g + i
            r_start = min(max(r - WIN_ROWS // 2, 0), rows - WIN_ROWS)
            for j in range(ATTN_KEY_ROWS):
                kr = ws + j
                if r_start <= kr < r_start + WIN_ROWS:
                    dr[v, i, j] = kr - r + (WIN_ROWS - 1)
    return dr


def _attn_bias_kernel(rpb_ref, o_ref, t_ref):
    head = pl.program_id(0)
    kc = lax.broadcasted_iota(jnp.int32, (GRID_W, LANES), 0)
    lane = lax.broadcasted_iota(jnp.int32, (GRID_W, LANES), 1)
    qc = lane % GRID_W
    c_start = jnp.clip(qc - WIN_COLS // 2, 0, GRID_W - WIN_COLS)
    col_in = (kc >= c_start) & (kc < c_start + WIN_COLS)
    dc = jnp.clip(kc - qc, 1 - WIN_COLS, WIN_COLS - 1) + (WIN_COLS - 1)
    base = head * (N_BIAS_ROWS * N_BIAS_COLS)
    for r in range(N_BIAS_ROWS):
        t = jnp.full((GRID_W, LANES), NEG_BIAS, F32)
        for cidx in range(N_BIAS_COLS):
            t = jnp.where(col_in & (dc == cidx), rpb_ref[base + r * N_BIAS_COLS + cidx] * LOG2_E, t)
        t_ref[r] = t
    dr = _attn_bias_row_index()
    low_half = lane < GRID_W
    masked = jnp.full((GRID_W, LANES), NEG_BIAS, F32)
    for v in range(3):
        for i in range(0, ATTN_ROWS_PER_STEP, 2):
            for j in range(ATTN_KEY_ROWS):
                lo = t_ref[int(dr[v, i, j])] if dr[v, i, j] >= 0 else masked
                hi = t_ref[int(dr[v, i + 1, j])] if dr[v, i + 1, j] >= 0 else masked
                o_ref[v, 0, j * GRID_W:(j + 1) * GRID_W, i * GRID_W:(i + 2) * GRID_W] = jnp.where(
                    low_half, lo, hi)


def _mod_kernel(c_ref, w_ref, b_ref, o_ref):
    c = c_ref[...]
    s = c * jax.nn.sigmoid(c)
    o_ref[...] = jnp.dot(s, w_ref[...], precision=lax.Precision.HIGHEST,
                         preferred_element_type=F32) + b_ref[...]


def _norm_modulate(x, gain, shift, scale):
    ms = jnp.mean(x * x, axis=-1, keepdims=True)
    y = x * lax.rsqrt(ms + RMS_EPS) * gain
    return y * (1.0 + scale) + shift


def _rope(t, cos, sin_signed):
    n_pairs = HEAD_DIM // 4
    lane = lax.broadcasted_iota(jnp.int32, t.shape, 1)
    first = (lane % (2 * n_pairs)) < n_pairs
    partner = jnp.where(first, pltpu.roll(t, LANES - n_pairs, 1), pltpu.roll(t, n_pairs, 1))
    return t * cos + partner * sin_signed


def _in_proj_kernel(x_ref, prev_ref, next_ref, mod_ref, g_ref, w_ref, b_ref, cos_ref, sin_ref, cw_ref, cb_ref,
                    qr_ref, qp_ref, kr_ref, v_ref, gt_ref, u_ref, x0_ref, hy_ref):
    i = pl.program_id(1)
    n_tiles = pl.num_programs(1)
    xx = jnp.concatenate([prev_ref[0], x_ref[0], next_ref[0]], axis=0)
    h_ext = _norm_modulate(xx, g_ref[...], mod_ref[0, 0:1, :], mod_ref[0, 1:2, :]).astype(BF16)
    h = h_ext[HALO:HALO + TOKEN_TILE]
    cos = cos_ref[...]
    sin = sin_ref[...]

    def proj(lo, hi, rows=h):
        return jnp.dot(rows, w_ref[:, lo:hi], preferred_element_type=F32) + b_ref[:, lo:hi]

    q = proj(0, D_ATTN) * (HEAD_DIM ** -0.5 * LOG2_E)
    qp_ref[0] = q.astype(BF16)
    k = proj(D_ATTN, 2 * D_ATTN)
    for c in range(D_ATTN // LANES):
        lanes = slice(c * LANES, (c + 1) * LANES)
        qr_ref[0, :, lanes] = _rope(q[:, lanes], cos, sin).astype(BF16)
        kr_ref[0, :, lanes] = _rope(k[:, lanes], cos, sin).astype(BF16)
    v_t = proj(2 * D_ATTN, 3 * D_ATTN).T.astype(BF16)
    for c in range(TOKEN_TILE // ATTN_KEY_CHUNK):
        v_ref[0, c] = v_t[:, c * ATTN_KEY_CHUNK:(c + 1) * ATTN_KEY_CHUNK]
    hy_lo = 3 * D_ATTN
    gl_lo = hy_lo + 3 * D_HYENA
    for c in range(4):
        w = D_MODEL // 2
        gt_ref[0, :, c * w:(c + 1) * w] = jax.nn.sigmoid(
            proj(gl_lo + c * w, gl_lo + (c + 1) * w)).astype(BF16)

    slabs_per_part = D_HYENA // LANES
    for c in range(3):
        hy = proj(hy_lo + c * D_HYENA, hy_lo + (c + 1) * D_HYENA, h_ext)
        for s in range(slabs_per_part):
            hy_ref[c * slabs_per_part + s] = hy[:, s * LANES:(s + 1) * LANES]
    zero_row = jnp.zeros((1, LANES), F32)

    @pl.when(i == 0)
    def _():
        for s in range(3 * slabs_per_part):
            hy_ref[s, HALO - 1:HALO, :] = zero_row

    @pl.when(i == n_tiles - 1)
    def _():
        for s in range(3 * slabs_per_part):
            hy_ref[s, HALO + TOKEN_TILE:HALO + TOKEN_TILE + 1, :] = zero_row

    def conv(s, r0, rows):
        lanes = slice(s * LANES, (s + 1) * LANES)
        tap = lambda j: hy_ref[s, pl.ds(HALO - 1 + j + r0, rows, stride=1), :]
        return (tap(0) * cw_ref[0:1, lanes] + tap(1) * cw_ref[1:2, lanes] + tap(2) * cw_ref[2:3, lanes]
                + cb_ref[:, lanes])

    for s in range(slabs_per_part):
        lanes = slice(s * LANES, (s + 1) * LANES)
        for j in range(TILE_GROUPS):
            r0 = j * FFT_N2
            x0_ref[0, r0:r0 + FFT_N2, lanes] = conv(s, r0, FFT_N2).astype(BF16)
            u_ref[0, j * SEQ_PITCH:j * SEQ_PITCH + FFT_N2, lanes] = (
                conv(slabs_per_part + s, r0, FFT_N2) * conv(2 * slabs_per_part + s, r0, FFT_N2))
            u_ref[0, j * SEQ_PITCH + FFT_N2:(j + 1) * SEQ_PITCH, lanes] = jnp.zeros((SEQ_PITCH - FFT_N2, LANES), F32)


def _ctx_kv_kernel(x_ref, mod_ref, g_ref, w_ref, b_ref, k_ref, v_ref):
    h = _norm_modulate(x_ref[0], g_ref[...], mod_ref[0:1, :], mod_ref[1:2, :]).astype(BF16)
    kv = jnp.dot(h, w_ref[...], preferred_element_type=F32) + b_ref[...]
    k_ref[0] = kv[:, :D_ATTN].astype(BF16)
    v_ref[0] = kv[:, D_ATTN:].T.astype(BF16)


def _attn_window_start(g):
    return jnp.clip(ATTN_ROWS_PER_STEP * g - WIN_ROWS // 2, 0, GRID_W - ATTN_KEY_ROWS)


def _attn_kernel(qr_ref, qp_ref, k_ref, vt_ref, kc_ref, vct_ref, bias_ref, o_ref):
    g = pl.program_id(1)
    win = _attn_window_start(g)
    key0 = pl.multiple_of(win * GRID_W, ATTN_KEY_CHUNK)
    chunk0 = win // (ATTN_KEY_CHUNK // GRID_W)
    nt = (((1,), (1,)), ((), ()))
    quad_w = 4 * HEAD_DIM
    lane = lax.broadcasted_iota(jnp.int32, (1, quad_w), 1)
    zero = jnp.zeros((), BF16)
    def scores(head):
        quad, hh = divmod(head, 4)
        ql = slice(quad * quad_w, (quad + 1) * quad_w)
        mine = (lane >= hh * HEAD_DIM) & (lane < (hh + 1) * HEAD_DIM)
        s_nb = lax.dot_general(k_ref[0, pl.ds(key0, ATTN_KEYS), ql], jnp.where(mine, qr_ref[0, :, ql], zero), nt,
                               preferred_element_type=F32)
        s_cx = lax.dot_general(kc_ref[0, :, ql], jnp.where(mine, qp_ref[0, :, ql], zero), nt,
                               preferred_element_type=F32)
        return s_nb + bias_ref[0, head], s_cx

    def with_ones(v):
        return jnp.concatenate([v, jnp.ones((16, v.shape[1]), BF16)], axis=0)

    def probs(s_nb, s_cx):
        m = jnp.maximum(jnp.max(s_nb, axis=0, keepdims=True), jnp.max(s_cx, axis=0, keepdims=True))
        return jnp.exp2(s_nb - m).astype(BF16), jnp.exp2(s_cx - m).astype(BF16)

    def values(head, p_nb, p_cx):
        rows = slice(head * HEAD_DIM, (head + 1) * HEAD_DIM)
        v_win = jnp.concatenate([vt_ref[0, chunk0 + c, rows, :] for c in range(ATTN_KEYS // ATTN_KEY_CHUNK)],
                                axis=1)
        o = (jnp.dot(with_ones(v_win), p_nb, preferred_element_type=F32)
             + jnp.dot(with_ones(vct_ref[0, rows, :]), p_cx, preferred_element_type=F32))
        return o[:HEAD_DIM] / o[HEAD_DIM:HEAD_DIM + 1]

    outs = []
    s_q = {0: scores(0), 1: scores(1)}
    p_q = {0: probs(*s_q.pop(0))}
    for head in range(N_HEADS):
        if head + 2 < N_HEADS:
            s_q[head + 2] = scores(head + 2)
        if head + 1 < N_HEADS:
            p_q[head + 1] = probs(*s_q.pop(head + 1))
        outs.append(values(head, *p_q.pop(head)))
    o_ref[0] = jnp.concatenate(outs, axis=0).T.astype(BF16)


def _filt_kernel(z_ref, aux_ref, w1_ref, b1_ref, w2_ref, b2_ref, w3_ref, b3_ref, freq_ref, decay_ref,
                 o_ref, asum_ref):
    i = pl.program_id(1)
    hp = lax.Precision.HIGHEST
    h = jnp.sin(freq_ref[0:1, :] * (jnp.dot(z_ref[0], w1_ref[...], precision=hp,
                                            preferred_element_type=F32) + b1_ref[...]))
    h = jnp.sin(freq_ref[1:2, :] * (jnp.dot(h, w2_ref[...], precision=hp,
                                            preferred_element_type=F32) + b2_ref[...]))
    taps = jnp.dot(h, w3_ref[...], precision=hp, preferred_element_type=F32) + b3_ref[...]
    t = aux_ref[0, :, 0:1]
    valid = aux_ref[0, :, 1:2] > 0.5
    k = jnp.where(valid, taps * jnp.exp(-t * jnp.abs(decay_ref[0])), 0.0)
    part = jnp.sum(jnp.abs(k), axis=0, keepdims=True)

    @pl.when(i == 0)
    def _():
        asum_ref[...] = jnp.zeros_like(asum_ref)

    asum_ref[0] += jnp.broadcast_to(part, asum_ref.shape[1:])
    o_ref[...] = k.reshape(o_ref.shape)


def _fft_stage1(load_group, w1_ref, a_ref):
    n_slabs = a_ref.shape[0]

    def body(n2, carry):
        x = load_group(n2)
        a = jnp.dot(w1_ref[...], x, preferred_element_type=F32)
        base = pl.multiple_of(n2 * FFT_PITCH, 8)
        for s in range(n_slabs):
            a_ref[s, pl.ds(base, FFT_PITCH), :] = a[:, s * LANES:(s + 1) * LANES]
        return carry

    lax.fori_loop(0, FFT_N2, body, 0, unroll=FFT_UNROLL)


def _fft_load_k1(a_ref, k1):
    parts = []
    for off in (0, FFT_N1):
        parts.append(jnp.concatenate(
            [a_ref[s, pl.ds(k1 + off, FFT_N2, stride=FFT_PITCH), :] for s in range(a_ref.shape[0])], axis=1))
    return jnp.concatenate(parts, axis=0)


def _filt_fft_kernel(k_ref, asum_ref, w1_ref, w2_ref, o_ref, a_ref):
    step = pl.program_id(1)

    @pl.when(step == 0)
    def _():
        norm = asum_ref[0, 0:1, :] + asum_ref[1, 0:1, :]

        def load_group(n2):
            return (k_ref[pl.ds(pl.multiple_of(n2 * FFT_N1, FFT_N1), FFT_N1), :] / norm).astype(BF16)
        _fft_stage1(load_group, w1_ref, a_ref)

    for j in range(FFT_K1_PER_STEP):
        b = _fft_load_k1(a_ref, step * FFT_K1_PER_STEP + j).astype(BF16)
        o_ref[j] = jnp.dot(w2_ref[j], b, preferred_element_type=F32)


def _hy_conv_kernel(u_ref, kf_ref, w1_ref, w2_ref, v2_ref, v1_ref, o_ref, a_ref):
    step = pl.program_id(1)
    n_slabs = a_ref.shape[0]

    def tokens_of(member, s, n2):
        return (2 * s + member, pl.ds(n2, FFT_HALF_N1, stride=SEQ_PITCH), slice(None))

    @pl.when(step == 0)
    def _():
        o_ref[...] = jnp.zeros_like(o_ref)

        def load_group(n2):
            return jnp.concatenate(
                [jnp.concatenate([u_ref[tokens_of(member, s, n2)] for s in range(n_slabs)], axis=1)
                 for member in range(2)], axis=0).astype(BF16)
        _fft_stage1(load_group, w1_ref, a_ref)

    for j in range(FFT_K1_PER_STEP):
        k1 = step * FFT_K1_PER_STEP + j
        b = _fft_load_k1(a_ref, k1).astype(BF16)
        x = jnp.dot(w2_ref[j], b, preferred_element_type=F32)
        kf = kf_ref[j]
        kf = jnp.concatenate([kf] * n_slabs, axis=1)
        xr, xi = x[:FFT_N2], x[FFT_N2:]
        kr, ki = kf[:FFT_N2], kf[FFT_N2:]
        y = jnp.concatenate([xr * kr - xi * ki, xr * ki + xi * kr], axis=0).astype(BF16)
        d = jnp.dot(v2_ref[j], y, preferred_element_type=F32)
        for s in range(n_slabs):
            lanes = slice(s * LANES, (s + 1) * LANES)
            a_ref[s, pl.ds(k1, FFT_N2, stride=FFT_PITCH), :] = d[:FFT_N2, lanes]
            a_ref[s, pl.ds(k1 + FFT_N1, FFT_N2, stride=FFT_PITCH), :] = d[FFT_N2:, lanes]

    @pl.when(step == FFT_STEPS - 1)
    def _():
        def body(n2, carry):
            base = pl.multiple_of(n2 * FFT_PITCH, 8)
            d = jnp.concatenate([a_ref[s, pl.ds(base, 2 * FFT_N1), :] for s in range(n_slabs)], axis=1)
            y = jnp.dot(v1_ref[...], d.astype(BF16), preferred_element_type=F32)
            for s in range(n_slabs):
                for member in range(2):
                    o_ref[tokens_of(member, s, n2)] = y[member * FFT_HALF_N1:(member + 1) * FFT_HALF_N1,
                                                        s * LANES:(s + 1) * LANES]
            return carry

        lax.fori_loop(0, FFT_N2, body, 0, unroll=FFT_UNROLL)


def _from_pitched(ref):
    return jnp.concatenate([ref[0, j * SEQ_PITCH:j * SEQ_PITCH + FFT_N2, :] for j in range(TILE_GROUPS)], axis=0)


def _merge_kernel(x_ref, yna_ref, x0_ref, u_ref, yc_ref, gt_ref, mod_ref, skip_ref, gpost_ref,
                  wna_ref, why_ref, wout_ref, o_ref):
    y_hy = (x0_ref[0].astype(F32) * (_from_pitched(yc_ref) + _from_pitched(u_ref) * skip_ref[...])).astype(BF16)
    a = jnp.dot(yna_ref[0], wna_ref[...], preferred_element_type=F32)
    b = jnp.dot(y_hy, why_ref[...], preferred_element_type=F32)
    g_na = gt_ref[0, :, :D_MODEL].astype(F32)
    g_hy = gt_ref[0, :, D_MODEL:].astype(F32)
    m = (g_na * a + g_hy * b).astype(BF16)
    o = jnp.dot(m, wout_ref[...], preferred_element_type=F32)
    ms = jnp.mean(o * o, axis=-1, keepdims=True)
    o = o * lax.rsqrt(ms + RMS_EPS) * gpost_ref[...]
    o_ref[0] = x_ref[0] + mod_ref[0, 2:3, :] * o


def _gelu_tanh(a):
    return 0.5 * a * (1.0 + jnp.tanh(math.sqrt(2.0 / math.pi) * (a + 0.044715 * (a * a * a))))


def _ffn_kernel(x_ref, prev_ref, next_ref, mod_ref, gpre_ref, gpost_ref, wup_ref, cw_ref, cb_ref, wdn_ref,
                o_ref, u_ref, act_ref):
    i = pl.program_id(1)
    n_tiles = pl.num_programs(1)
    x = x_ref[0]
    xx = jnp.concatenate([prev_ref[0], x, next_ref[0]], axis=0)
    h = _norm_modulate(xx, gpre_ref[...], mod_ref[0, 3:4, :], mod_ref[0, 4:5, :])
    row = lax.broadcasted_iota(jnp.int32, (xx.shape[0], 1), 0)
    inside = ((row >= HALO) | (i > 0)) & ((row < HALO + TOKEN_TILE) | (i < n_tiles - 1))
    h = jnp.where(inside, h, 0.0).astype(BF16)
    n_chunks = D_FF // FF_CHUNK
    n_slabs = FF_CHUNK // LANES

    def conv(buf, half, s, lo, r0):
        lanes = slice(half * D_FF + lo, half * D_FF + lo + LANES)
        tap = lambda j: u_ref[buf, half, s, pl.ds(HALO - 1 + j + r0, FFN_ROW_BLOCK, stride=1), :]
        return (tap(0) * cw_ref[0:1, lanes] + tap(1) * cw_ref[1:2, lanes] + tap(2) * cw_ref[2:3, lanes]
                + cb_ref[:, lanes])

    def up_project(c):
        for half in range(2):
            lo = half * D_FF + c * FF_CHUNK
            u = jnp.dot(h, wup_ref[:, lo:lo + FF_CHUNK], preferred_element_type=F32)
            for s in range(n_slabs):
                u_ref[c % 2, half, s] = u[:, s * LANES:(s + 1) * LANES]

    up_project(0)
    for c in range(n_chunks):
        buf = c % 2
        if c + 1 < n_chunks:
            up_project(c + 1)
        for s in range(n_slabs):
            lo = c * FF_CHUNK + s * LANES
            for r0 in range(0, TOKEN_TILE, FFN_ROW_BLOCK):
                act_ref[r0:r0 + FFN_ROW_BLOCK, lo:lo + LANES] = (
                    _gelu_tanh(conv(buf, 0, s, lo, r0)) * conv(buf, 1, s, lo, r0)).astype(BF16)
    y = jnp.dot(act_ref[...], wdn_ref[...], preferred_element_type=F32)
    ms = jnp.mean(y * y, axis=-1, keepdims=True)
    y = y * lax.rsqrt(ms + RMS_EPS) * gpost_ref[...]
    o_ref[0] = x + mod_ref[0, 5:6, :] * y


def _const_spec(shape):
    nd = len(shape)
    return pl.BlockSpec(shape, lambda *_: (0,) * nd, pipeline_mode=pl.Buffered(1))


def kernel(x, c, ctx, c_ctx, w_mod, b_mod, norm_mix_pre, norm_mix_post, norm_ffn_pre, norm_ffn_post, w_in, b_in, na_rpb, hy_conv_w, hy_conv_b, hy_filt_w1, hy_filt_b1, hy_filt_w2, hy_filt_b2, hy_filt_w3, hy_filt_b3, hy_sin_freq, hy_decay, hy_skip, w_o_na, w_o_hy, w_out, ffn_w_up, ffn_conv_w, ffn_conv_b, ffn_w_down):
    batch, seq, d = x.shape
    n_ctx = ctx.shape[1]
    assert d == D_MODEL and 2 * seq == FFT_N and seq == GRID_W * GRID_W and batch % 2 == 0
    assert w_mod.shape[0] == 1, "single-layer block"
    n_tiles = seq // TOKEN_TILE
    d_in = w_in.shape[2]
    row2 = lambda a: a.reshape(1, -1)

    c_all = jnp.zeros((8, d), F32).at[:batch].set(c).at[batch].set(c_ctx)
    mod_n = 1024
    mod = pl.pallas_call(
        _mod_kernel,
        grid=(N_MOD * d // mod_n,),
        in_specs=[_const_spec((8, d)),
                  pl.BlockSpec((d, mod_n), lambda j: (0, j)),
                  pl.BlockSpec((1, mod_n), lambda j: (0, j))],
        out_specs=pl.BlockSpec((8, mod_n), lambda j: (0, j)),
        out_shape=jax.ShapeDtypeStruct((8, N_MOD * d), F32),
        compiler_params=_cparams(("arbitrary",)),
        name="mod",
    )(c_all, w_mod[0], row2(b_mod[0]))
    mod_lat = jnp.pad(mod[:batch].reshape(batch, N_MOD, d), ((0, 0), (0, 8 - N_MOD), (0, 0)))
    mod_ctx = jnp.pad(mod[batch].reshape(N_MOD, d), ((0, 8 - N_MOD), (0, 0)))

    w_in_b = w_in[0].astype(BF16)
    b_in_r = row2(b_in[0])
    g_mix_pre = row2(norm_mix_pre[0])

    k_ctx, v_ctx = pl.pallas_call(
        _ctx_kv_kernel,
        grid=(batch,),
        in_specs=[pl.BlockSpec((1, n_ctx, d), lambda b: (b, 0, 0)),
                  _const_spec((8, d)), _const_spec((1, d)),
                  _const_spec((d, 2 * D_ATTN)), _const_spec((1, 2 * D_ATTN))],
        out_specs=[pl.BlockSpec((1, n_ctx, D_ATTN), lambda b: (b, 0, 0)),
                   pl.BlockSpec((1, D_ATTN, n_ctx), lambda b: (b, 0, 0))],
        out_shape=[jax.ShapeDtypeStruct((batch, n_ctx, D_ATTN), BF16),
                   jax.ShapeDtypeStruct((batch, D_ATTN, n_ctx), BF16)],
        compiler_params=_cparams(("arbitrary",)),
        name="ctx_kv",
    )(ctx, mod_ctx, g_mix_pre, w_in_b[:, D_ATTN:3 * D_ATTN], b_in_r[:, D_ATTN:3 * D_ATTN])

    cos_t, sin_t = _rope_tables(seq)
    tok = lambda w: pl.BlockSpec((1, TOKEN_TILE, w), lambda b, i: (b, i, 0))
    mod_spec = pl.BlockSpec((1, 8, d), lambda b, i: (b, 0, 0))
    rope_spec = pl.BlockSpec((TOKEN_TILE, LANES), lambda b, i: (i, 0))
    halo_blocks = TOKEN_TILE // HALO
    n_halo_blocks = seq // HALO
    prev_spec = lambda w: pl.BlockSpec((1, HALO, w), lambda b, i: (b, jnp.maximum(i * halo_blocks - 1, 0), 0))
    next_spec = lambda w: pl.BlockSpec(
        (1, HALO, w), lambda b, i: (b, jnp.minimum((i + 1) * halo_blocks, n_halo_blocks - 1), 0))
    pitched_spec = pl.BlockSpec((1, TILE_GROUPS * SEQ_PITCH, D_HYENA), lambda b, i: (b, i, 0))
    pitched_shape = jax.ShapeDtypeStruct((batch, PITCHED_ROWS, D_HYENA), F32)
    q_rot, q_plain, k_rot, v_lat, gates, u_p, x0 = pl.pallas_call(
        _in_proj_kernel,
        grid=(batch, n_tiles),
        in_specs=[tok(d), prev_spec(d), next_spec(d), mod_spec, _const_spec((1, d)), _const_spec((d, d_in)),
                  _const_spec((1, d_in)), rope_spec, rope_spec,
                  _const_spec((3, 3 * D_HYENA)), _const_spec((1, 3 * D_HYENA))],
        out_specs=[tok(D_ATTN)] * 3
        + [pl.BlockSpec((1, TOKEN_TILE // ATTN_KEY_CHUNK, D_ATTN, ATTN_KEY_CHUNK), lambda b, i: (b, i, 0, 0)),
           tok(2 * d), pitched_spec, tok(D_HYENA)],
        out_shape=[jax.ShapeDtypeStruct((batch, seq, D_ATTN), BF16)] * 3
        + [jax.ShapeDtypeStruct((batch, seq // ATTN_KEY_CHUNK, D_ATTN, ATTN_KEY_CHUNK), BF16),
           jax.ShapeDtypeStruct((batch, seq, 2 * d), BF16), pitched_shape,
           jax.ShapeDtypeStruct((batch, seq, D_HYENA), BF16)],
        scratch_shapes=[pltpu.VMEM((3 * D_HYENA // LANES, TOKEN_TILE + 2 * HALO, LANES), F32)],
        compiler_params=_cparams(("arbitrary", "arbitrary")),
        name="in_proj",
    )(x, x, x, mod_lat, g_mix_pre, w_in_b, b_in_r, jnp.asarray(cos_t), jnp.asarray(sin_t),
      hy_conv_w[0], row2(hy_conv_b[0]))

    bias = pl.pallas_call(
        _attn_bias_kernel,
        grid=(N_HEADS,),
        in_specs=[pl.BlockSpec(memory_space=pltpu.SMEM)],
        out_specs=pl.BlockSpec((3, 1, ATTN_KEYS, ATTN_Q), lambda h: (0, h, 0, 0)),
        out_shape=jax.ShapeDtypeStruct((3, N_HEADS, ATTN_KEYS, ATTN_Q), F32),
        scratch_shapes=[pltpu.VMEM((N_BIAS_ROWS, GRID_W, LANES), F32)],
        compiler_params=_cparams(("arbitrary",)),
        name="attn_bias",
    )(na_rpb[0].reshape(-1))
    n_groups = GRID_W // ATTN_ROWS_PER_STEP
    q_spec = pl.BlockSpec((1, ATTN_Q, D_ATTN), lambda b, g: (b, g, 0))
    full = lambda n: pl.BlockSpec((1, n, D_ATTN), lambda b, g: (b, 0, 0))
    bias_spec = pl.BlockSpec(
        (1, N_HEADS, ATTN_KEYS, ATTN_Q),
        lambda b, g: ((g > 0).astype(jnp.int32) + (g == n_groups - 1).astype(jnp.int32), 0, 0, 0))
    vt_spec = pl.BlockSpec((1, seq // ATTN_KEY_CHUNK, D_ATTN, ATTN_KEY_CHUNK), lambda b, g: (b, 0, 0, 0))
    vct_spec = pl.BlockSpec((1, D_ATTN, n_ctx), lambda b, g: (b, 0, 0))
    y_na = pl.pallas_call(
        _attn_kernel,
        grid=(batch, n_groups),
        in_specs=[q_spec, q_spec, full(seq), vt_spec, full(n_ctx), vct_spec, bias_spec],
        out_specs=q_spec,
        out_shape=jax.ShapeDtypeStruct((batch, seq, D_ATTN), BF16),
        compiler_params=_cparams(("arbitrary", "arbitrary")),
        name="attn",
    )(q_rot, q_plain, k_rot, v_lat, k_ctx, v_ctx, bias)

    z_t, aux_t = _filter_tables(seq)
    filt_n2 = 32
    filt_rows = filt_n2 * FFT_HALF_N1
    w1_pad = jnp.pad(hy_filt_w1[0], ((0, z_t.shape[2] - POS_FEATS), (0, 0)))
    k_circ, k_asum = pl.pallas_call(
        _filt_kernel,
        grid=(2, seq // filt_rows),
        in_specs=[pl.BlockSpec((1, filt_rows, z_t.shape[2]), lambda hf, i: (hf, i, 0)),
                  pl.BlockSpec((1, filt_rows, aux_t.shape[2]), lambda hf, i: (hf, i, 0)),
                  _const_spec(w1_pad.shape), _const_spec((1, FILTER_HIDDEN)),
                  _const_spec((FILTER_HIDDEN, FILTER_HIDDEN)), _const_spec((1, FILTER_HIDDEN)),
                  pl.BlockSpec((FILTER_HIDDEN, D_HYENA), lambda hf, i: (0, hf)),
                  pl.BlockSpec((1, D_HYENA), lambda hf, i: (0, hf)),
                  _const_spec((2, FILTER_HIDDEN)),
                  pl.BlockSpec((1, 1, D_HYENA), lambda hf, i: (hf, 0, 0))],
        out_specs=[pl.BlockSpec((filt_n2, FFT_HALF_N1, D_HYENA), lambda hf, i: (i, hf, 0)),
                   pl.BlockSpec((1, 8, D_HYENA), lambda hf, i: (hf, 0, 0))],
        out_shape=[jax.ShapeDtypeStruct((FFT_N2, FFT_N1, D_HYENA), F32),
                   jax.ShapeDtypeStruct((2, 8, D_HYENA), F32)],
        compiler_params=_cparams(("arbitrary", "arbitrary")),
        name="filt",
    )(jnp.asarray(z_t), jnp.asarray(aux_t), w1_pad, row2(hy_filt_b1[0]), hy_filt_w2[0], row2(hy_filt_b2[0]),
      hy_filt_w3[0], row2(hy_filt_b3[0]), hy_sin_freq[0], hy_decay[0].reshape(2, 1, D_HYENA))
    k_circ = k_circ.reshape(FFT_N, D_HYENA)

    mats = _fft_matrices()
    mats = {k: jnp.asarray(v).astype(BF16) for k, v in mats.items()}
    w2_all = mats["w2"]
    v2_all = mats["v2"]
    step_mat_spec = pl.BlockSpec((FFT_K1_PER_STEP, 2 * FFT_N2, 2 * FFT_N2), lambda cb, s: (s, 0, 0))
    filt_slabs = 2
    kf = pl.pallas_call(
        _filt_fft_kernel,
        grid=(D_HYENA // (filt_slabs * LANES), FFT_STEPS),
        in_specs=[pl.BlockSpec((FFT_N, filt_slabs * LANES), lambda cb, s: (0, cb)),
                  pl.BlockSpec((2, 8, filt_slabs * LANES), lambda cb, s: (0, 0, cb)),
                  _const_spec((FFT_PITCH, FFT_N1)), step_mat_spec],
        out_specs=pl.BlockSpec((FFT_K1_PER_STEP, 2 * FFT_N2, filt_slabs * LANES), lambda cb, s: (s, 0, cb)),
        out_shape=jax.ShapeDtypeStruct((FFT_N1, 2 * FFT_N2, D_HYENA), F32),
        scratch_shapes=[pltpu.VMEM((filt_slabs, FFT_N2 * FFT_PITCH, LANES), F32)],
        compiler_params=_cparams(("arbitrary", "arbitrary")),
        name="filt_fft",
    )(k_circ, k_asum, mats["w1_real"], w2_all)

    n_pairs = batch // 2
    pair_block = ((batch, PITCHED_ROWS, LANES), lambda cb, s: (0, 0, cb))
    pair_spec = pl.BlockSpec(*pair_block, pipeline_mode=pl.Buffered(1))
    y_conv = pl.pallas_call(
        _hy_conv_kernel,
        grid=(D_HYENA // LANES, FFT_STEPS),
        in_specs=[pl.BlockSpec(*pair_block),
                  pl.BlockSpec((FFT_K1_PER_STEP, 2 * FFT_N2, LANES), lambda cb, s: (s, 0, cb)),
                  _const_spec((FFT_PITCH, FFT_N1)), step_mat_spec, step_mat_spec,
                  _const_spec((FFT_N1, 2 * FFT_N1))],
        out_specs=pair_spec,
        out_shape=pitched_shape,
        scratch_shapes=[pltpu.VMEM((n_pairs, FFT_N2 * FFT_PITCH, LANES), F32)],
        compiler_params=_cparams(("arbitrary", "arbitrary")),
        name="hy_conv",
    )(u_p, kf, mats["w1_data"], w2_all, v2_all, mats["v1"])

    x1 = pl.pallas_call(
        _merge_kernel,
        grid=(batch, n_tiles),
        in_specs=[tok(d), tok(D_ATTN), tok(D_HYENA), pitched_spec, pitched_spec, tok(2 * d), mod_spec,
                  _const_spec((1, D_HYENA)), _const_spec((1, d)),
                  _const_spec((D_ATTN, d)), _const_spec((D_HYENA, d)), _const_spec((d, d))],
        out_specs=tok(d),
        out_shape=jax.ShapeDtypeStruct((batch, seq, d), F32),
        compiler_params=_cparams(("arbitrary", "arbitrary")),
        name="merge",
    )(x, y_na, x0, u_p, y_conv, gates, mod_lat, row2(hy_skip[0]), row2(norm_mix_post[0]),
      w_o_na[0].astype(BF16), w_o_hy[0].astype(BF16), w_out[0].astype(BF16))

    w_up_c = ffn_w_up[0].astype(BF16)
    conv_w_c = ffn_conv_w[0]
    conv_b_c = row2(ffn_conv_b[0])
    w_dn_c = ffn_w_down[0].astype(BF16)
    out = pl.pallas_call(
        _ffn_kernel,
        grid=(batch, n_tiles),
        in_specs=[tok(d), prev_spec(d), next_spec(d), mod_spec, _const_spec((1, d)), _const_spec((1, d)),
                  _const_spec(w_up_c.shape), _const_spec(conv_w_c.shape), _const_spec(conv_b_c.shape),
                  _const_spec(w_dn_c.shape)],
        out_specs=tok(d),
        out_shape=jax.ShapeDtypeStruct((batch, seq, d), F32),
        scratch_shapes=[pltpu.VMEM((2, 2, FF_CHUNK // LANES, TOKEN_TILE + 2 * HALO, LANES), F32),
                        pltpu.VMEM((TOKEN_TILE, D_FF), BF16)],
        compiler_params=_cparams(("arbitrary", "arbitrary")),
        name="ffn",
    )(x1, x1, x1, mod_lat, row2(norm_ffn_pre[0]), row2(norm_ffn_post[0]), w_up_c, conv_w_c, conv_b_c, w_dn_c)
    return out
```

```python
import functools
import math

import jax
import jax.numpy as jnp
import numpy as np
from jax import lax
from jax.experimental import pallas as pl
from jax.experimental.pallas import tpu as pltpu

F32 = jnp.float32
BF16 = jnp.bfloat16

D_MODEL = 1024
N_HEADS = 8
HEAD_DIM = 64
D_ATTN = N_HEADS * HEAD_DIM
D_HYENA = 512
GRID_W = 64
WIN_ROWS = 8
WIN_COLS = 16
POS_BANDS = 16
POS_FEATS = 1 + 2 * POS_BANDS
FILTER_HIDDEN = 64
D_FF = 2816
N_MOD = 6
ROPE_BASE = 10000.0
RMS_EPS = 1e-6
NEG_BIAS = -1e30
LOG2_E = math.log2(math.e)

LANES = 128
VMEM_LIMIT_BYTES = 56 * 1024 * 1024

FFT_N1 = 64
FFT_N2 = 128
FFT_N = FFT_N1 * FFT_N2
FFT_HALF_N1 = FFT_N1 // 2
FFT_PITCH = 2 * FFT_N1 + 8
FFT_K1_PER_STEP = 8
FFT_UNROLL = 8
FFT_STEPS = FFT_N1 // FFT_K1_PER_STEP

TOKEN_TILE = 512
FILT_ROWS = 1024
SEQ_PITCH = FFT_N2 + 8
PITCHED_ROWS = FFT_HALF_N1 * SEQ_PITCH
TILE_GROUPS = TOKEN_TILE // FFT_N2
ATTN_ROWS_PER_STEP = 4
ATTN_Q = ATTN_ROWS_PER_STEP * GRID_W
ATTN_KEY_ROWS = 12
ATTN_KEYS = ATTN_KEY_ROWS * GRID_W
ATTN_KEY_CHUNK = 256
FF_CHUNK = 768
HALO = 8
FFN_ROW_BLOCK = 64


def _cparams(sem):
    return pltpu.CompilerParams(dimension_semantics=sem, vmem_limit_bytes=VMEM_LIMIT_BYTES)


@functools.lru_cache(maxsize=None)
def _rope_tables(seq):
    pos = np.arange(seq)
    row = pos // GRID_W
    col = pos % GRID_W
    n_pairs = HEAD_DIM // 4
    inv = ROPE_BASE ** (-np.arange(n_pairs, dtype=np.float64) / n_pairs)
    lane = np.arange(LANES) % HEAD_DIM
    p = np.where(lane[None, :] < HEAD_DIM // 2, row[:, None], col[:, None]).astype(np.float64)
    ang = p * inv[lane % n_pairs][None, :]
    sign = np.where((lane % (2 * n_pairs)) < n_pairs, -1.0, 1.0)
    return np.cos(ang).astype(np.float32), (np.sin(ang) * sign[None, :]).astype(np.float32)


@functools.lru_cache(maxsize=None)
def _filter_tables(seq):
    assert 2 * seq == FFT_N
    half, n2, n1 = np.meshgrid(np.arange(2), np.arange(FFT_N2), np.arange(FFT_HALF_N1), indexing="ij")
    n = FFT_N2 * (half * FFT_HALF_N1 + n1) + n2
    fwd = n < seq
    m = n - seq
    valid = fwd | (m >= 1)
    pos = np.where(valid, np.where(fwd, n, seq - m), 0).astype(np.float64).reshape(2, seq)
    t = pos / max(seq - 1, 1)
    bands = np.linspace(1e-4, POS_BANDS - 1, POS_BANDS)
    ang = (2.0 * math.pi / seq) * pos[..., None] * bands
    z = np.zeros((2, seq, 64), np.float64)
    z[..., 0] = t
    z[..., 1:1 + POS_BANDS] = np.cos(ang)
    z[..., 1 + POS_BANDS:POS_FEATS] = -np.sin(ang)
    aux = np.zeros((2, seq, 8), np.float64)
    aux[..., 0] = t
    aux[..., 1] = valid.reshape(2, seq)
    zp = z.reshape(2, seq // FILT_ROWS, 2, FILT_ROWS // 2, 64).transpose(0, 1, 3, 2, 4).reshape(2, seq // 2, 128)
    return zp.astype(np.float32), aux.astype(np.float32)


def _realify(m):
    return np.block([[m.real, -m.imag], [m.imag, m.real]])


@functools.lru_cache(maxsize=None)
def _fft_matrices():
    n1 = np.arange(FFT_N1)
    n2 = np.arange(FFT_N2)
    k1 = np.arange(FFT_N1)
    k2 = np.arange(FFT_N2)
    f1 = np.exp(-2j * np.pi * np.outer(k1, n1) / FFT_N1)
    w1_data = _realify(f1[:, :FFT_HALF_N1])
    w1_real = np.concatenate([f1.real, f1.imag], axis=0)
    v1 = _realify(np.conj(f1.T)[:FFT_HALF_N1, :] / FFT_N)
    f2 = np.exp(-2j * np.pi * np.outer(k2, n2) / FFT_N2)
    tw = np.exp(-2j * np.pi * np.outer(k1, n2) / FFT_N)
    w2 = np.stack([_realify(f2 * tw[a][None, :]) for a in range(FFT_N1)])
    v2 = np.stack([_realify(np.conj(f2.T) * np.conj(tw[a])[:, None]) for a in range(FFT_N1)])
    pad = np.zeros((FFT_PITCH - 2 * FFT_N1, FFT_N1))
    w1_data = np.concatenate([w1_data, pad], axis=0)
    w1_real = np.concatenate([w1_real, pad], axis=0)
    return {k: v.astype(np.float32) for k, v in
            dict(w1_data=w1_data, w1_real=w1_real, v1=v1, w2=w2, v2=v2).items()}


N_BIAS_ROWS = 2 * WIN_ROWS - 1
N_BIAS_COLS = 2 * WIN_COLS - 1


def _attn_bias_row_index():
    rows = GRID_W
    groups = (0, 2, rows // ATTN_ROWS_PER_STEP - 1)
    dr = np.full((3, ATTN_ROWS_PER_STEP, ATTN_KEY_ROWS), -1, np.int32)
    for v, g in enumerate(groups):
        ws = min(max(ATTN_ROWS_PER_STEP * g - WIN_ROWS // 2, 0), rows - ATTN_KEY_ROWS)
        for i in range(ATTN_ROWS_PER_STEP):
            r = ATTN_ROWS_PER_STEP * g + i
            r_start = min(max(r - WIN_ROWS // 2, 0), rows - WIN_ROWS)
            for j in range(ATTN_KEY_ROWS):
                kr = ws + j
                if r_start <= kr < r_start + WIN_ROWS:
                    dr[v, i, j] = kr - r + (WIN_ROWS - 1)
    return dr


def _attn_bias_kernel(rpb_ref, o_ref, t_ref):
    head = pl.program_id(0)
    kc = lax.broadcasted_iota(jnp.int32, (GRID_W, LANES), 0)
    lane = lax.broadcasted_iota(jnp.int32, (GRID_W, LANES), 1)
    qc = lane % GRID_W
    c_start = jnp.clip(qc - WIN_COLS // 2, 0, GRID_W - WIN_COLS)
    col_in = (kc >= c_start) & (kc < c_start + WIN_COLS)
    dc = jnp.clip(kc - qc, 1 - WIN_COLS, WIN_COLS - 1) + (WIN_COLS - 1)
    base = head * (N_BIAS_ROWS * N_BIAS_COLS)
    for r in range(N_BIAS_ROWS):
        t = jnp.full((GRID_W, LANES), NEG_BIAS, F32)
        for cidx in range(N_BIAS_COLS):
            t = jnp.where(col_in & (dc == cidx), rpb_ref[base + r * N_BIAS_COLS + cidx] * LOG2_E, t)
        t_ref[r] = t
    dr = _attn_bias_row_index()
    low_half = lane < GRID_W
    masked = jnp.full((GRID_W, LANES), NEG_BIAS, F32)
    for v in range(3):
        for i in range(0, ATTN_ROWS_PER_STEP, 2):
            for j in range(ATTN_KEY_ROWS):
                lo = t_ref[int(dr[v, i, j])] if dr[v, i, j] >= 0 else masked
                hi = t_ref[int(dr[v, i + 1, j])] if dr[v, i + 1, j] >= 0 else masked
                o_ref[v, 0, j * GRID_W:(j + 1) * GRID_W, i * GRID_W:(i + 2) * GRID_W] = jnp.where(
                    low_half, lo, hi)


def _mod_kernel(c_ref, w_ref, b_ref, o_ref):
    c = c_ref[...]
    s = c * jax.nn.sigmoid(c)
    o_ref[...] = jnp.dot(s, w_ref[...], precision=lax.Precision.HIGHEST,
                         preferred_element_type=F32) + b_ref[...]


def _norm_modulate(x, gain, shift, scale):
    ms = jnp.mean(x * x, axis=-1, keepdims=True)
    y = x * lax.rsqrt(ms + RMS_EPS) * gain
    return y * (1.0 + scale) + shift


def _rope(t, cos, sin_signed):
    n_pairs = HEAD_DIM // 4
    lane = lax.broadcasted_iota(jnp.int32, t.shape, 1)
    first = (lane % (2 * n_pairs)) < n_pairs
    partner = jnp.where(first, pltpu.roll(t, LANES - n_pairs, 1), pltpu.roll(t, n_pairs, 1))
    return t * cos + partner * sin_signed


def _in_proj_kernel(x_ref, prev_ref, next_ref, mod_ref, g_ref, w_ref, b_ref, cos_ref, sin_ref, cw_ref, cb_ref,
                    qr_ref, qp_ref, kr_ref, v_ref, gt_ref, u_ref, x0_ref, hy_ref):
    i = pl.program_id(1)
    n_tiles = pl.num_programs(1)
    xx = jnp.concatenate([prev_ref[0], x_ref[0], next_ref[0]], axis=0)
    h_ext = _norm_modulate(xx, g_ref[...], mod_ref[0, 0:1, :], mod_ref[0, 1:2, :]).astype(BF16)
    h = h_ext[HALO:HALO + TOKEN_TILE]
    cos = cos_ref[...]
    sin = sin_ref[...]

    def proj(lo, hi, rows=h):
        return jnp.dot(rows, w_ref[:, lo:hi], preferred_element_type=F32) + b_ref[:, lo:hi]

    q = proj(0, D_ATTN) * (HEAD_DIM ** -0.5 * LOG2_E)
    qp_ref[0] = q.astype(BF16)
    k = proj(D_ATTN, 2 * D_ATTN)
    for c in range(D_ATTN // LANES):
        lanes = slice(c * LANES, (c + 1) * LANES)
        qr_ref[0, :, lanes] = _rope(q[:, lanes], cos, sin).astype(BF16)
        kr_ref[0, :, lanes] = _rope(k[:, lanes], cos, sin).astype(BF16)
    v_t = proj(2 * D_ATTN, 3 * D_ATTN).T.astype(BF16)
    for c in range(TOKEN_TILE // ATTN_KEY_CHUNK):
        v_ref[0, c] = v_t[:, c * ATTN_KEY_CHUNK:(c + 1) * ATTN_KEY_CHUNK]
    hy_lo = 3 * D_ATTN
    gl_lo = hy_lo + 3 * D_HYENA
    for c in range(4):
        w = D_MODEL // 2
        gt_ref[0, :, c * w:(c + 1) * w] = jax.nn.sigmoid(
            proj(gl_lo + c * w, gl_lo + (c + 1) * w)).astype(BF16)

    slabs_per_part = D_HYENA // LANES
    for c in range(3):
        hy = proj(hy_lo + c * D_HYENA, hy_lo + (c + 1) * D_HYENA, h_ext)
        for s in range(slabs_per_part):
            hy_ref[c * slabs_per_part + s] = hy[:, s * LANES:(s + 1) * LANES]
    zero_row = jnp.zeros((1, LANES), F32)

    @pl.when(i == 0)
    def _():
        for s in range(3 * slabs_per_part):
            hy_ref[s, HALO - 1:HALO, :] = zero_row

    @pl.when(i == n_tiles - 1)
    def _():
        for s in range(3 * slabs_per_part):
            hy_ref[s, HALO + TOKEN_TILE:HALO + TOKEN_TILE + 1, :] = zero_row

    def conv(s, r0, rows):
        lanes = slice(s * LANES, (s + 1) * LANES)
        tap = lambda j: hy_ref[s, pl.ds(HALO - 1 + j + r0, rows, stride=1), :]
        return (tap(0) * cw_ref[0:1, lanes] + tap(1) * cw_ref[1:2, lanes] + tap(2) * cw_ref[2:3, lanes]
                + cb_ref[:, lanes])

    for s in range(slabs_per_part):
        lanes = slice(s * LANES, (s + 1) * LANES)
        for j in range(TILE_GROUPS):
            r0 = j * FFT_N2
            x0_ref[0, r0:r0 + FFT_N2, lanes] = conv(s, r0, FFT_N2).astype(BF16)
            u_ref[0, j * SEQ_PITCH:j * SEQ_PITCH + FFT_N2, lanes] = (
                conv(slabs_per_part + s, r0, FFT_N2) * conv(2 * slabs_per_part + s, r0, FFT_N2))
            u_ref[0, j * SEQ_PITCH + FFT_N2:(j + 1) * SEQ_PITCH, lanes] = jnp.zeros((SEQ_PITCH - FFT_N2, LANES), F32)


def _ctx_kv_kernel(x_ref, mod_ref, g_ref, w_ref, b_ref, k_ref, v_ref):
    h = _norm_modulate(x_ref[0], g_ref[...], mod_ref[0:1, :], mod_ref[1:2, :]).astype(BF16)
    kv = jnp.dot(h, w_ref[...], preferred_element_type=F32) + b_ref[...]
    k_ref[0] = kv[:, :D_ATTN].astype(BF16)
    v_ref[0] = kv[:, D_ATTN:].T.astype(BF16)


def _attn_window_start(g):
    return jnp.clip(ATTN_ROWS_PER_STEP * g - WIN_ROWS // 2, 0, GRID_W - ATTN_KEY_ROWS)


def _attn_kernel(qr_ref, qp_ref, k_ref, vt_ref, kc_ref, vct_ref, bias_ref, o_ref):
    g = pl.program_id(1)
    win = _attn_window_start(g)
    key0 = pl.multiple_of(win * GRID_W, ATTN_KEY_CHUNK)
    chunk0 = win // (ATTN_KEY_CHUNK // GRID_W)
    nt = (((1,), (1,)), ((), ()))
    quad_w = 4 * HEAD_DIM
    lane = lax.broadcasted_iota(jnp.int32, (1, quad_w), 1)
    zero = jnp.zeros((), BF16)
    def scores(head):
        quad, hh = divmod(head, 4)
        ql = slice(quad * quad_w, (quad + 1) * quad_w)
        mine = (lane >= hh * HEAD_DIM) & (lane < (hh + 1) * HEAD_DIM)
        s_nb = lax.dot_general(k_ref[0, pl.ds(key0, ATTN_KEYS), ql], jnp.where(mine, qr_ref[0, :, ql], zero), nt,
                               preferred_element_type=F32)
        s_cx = lax.dot_general(kc_ref[0, :, ql], jnp.where(mine, qp_ref[0, :, ql], zero), nt,
                               preferred_element_type=F32)
        return s_nb + bias_ref[0, head], s_cx

    def with_ones(v):
        return jnp.concatenate([v, jnp.ones((16, v.shape[1]), BF16)], axis=0)

    def probs(s_nb, s_cx):
        m = jnp.maximum(jnp.max(s_nb, axis=0, keepdims=True), jnp.max(s_cx, axis=0, keepdims=True))
        return jnp.exp2(s_nb - m).astype(BF16), jnp.exp2(s_cx - m).astype(BF16)

    def values(head, p_nb, p_cx):
        rows = slice(head * HEAD_DIM, (head + 1) * HEAD_DIM)
        v_win = jnp.concatenate([vt_ref[0, chunk0 + c, rows, :] for c in range(ATTN_KEYS // ATTN_KEY_CHUNK)],
                                axis=1)
        o = (jnp.dot(with_ones(v_win), p_nb, preferred_element_type=F32)
             + jnp.dot(with_ones(vct_ref[0, rows, :]), p_cx, preferred_element_type=F32))
        return o[:HEAD_DIM] / o[HEAD_DIM:HEAD_DIM + 1]

    outs = []
    s_q = {0: scores(0), 1: scores(1)}
    p_q = {0: probs(*s_q.pop(0))}
    for head in range(N_HEADS):
        if head + 2 < N_HEADS:
            s_q[head + 2] = scores(head + 2)
        if head + 1 < N_HEADS:
            p_q[head + 1] = probs(*s_q.pop(head + 1))
        outs.append(values(head, *p_q.pop(head)))
    o_ref[0] = jnp.concatenate(outs, axis=0).T.astype(BF16)


def _filt_kernel(z_ref, aux_ref, w1_ref, b1_ref, w2_ref, b2_ref, w3_ref, b3_ref, freq_ref, decay_ref,
                 o_ref, asum_ref):
    i = pl.program_id(1)
    hp = lax.Precision.HIGHEST
    h = jnp.sin(freq_ref[0:1, :] * (jnp.dot(z_ref[0], w1_ref[...], precision=hp,
                                            preferred_element_type=F32) + b1_ref[...]))
    h = jnp.sin(freq_ref[1:2, :] * (jnp.dot(h, w2_ref[...], precision=hp,
                                            preferred_element_type=F32) + b2_ref[...]))
    taps = jnp.concatenate([jnp.dot(h, w3_ref[part], precision=hp, preferred_element_type=F32)
                            for part in range(2)], axis=0) + b3_ref[...]
    t = aux_ref[0, :, 0:1]
    valid = aux_ref[0, :, 1:2] > 0.5
    k = jnp.where(valid, taps * jnp.exp(-t * jnp.abs(decay_ref[0])), 0.0)
    part = jnp.sum(jnp.abs(k), axis=0, keepdims=True)

    @pl.when(i == 0)
    def _():
        asum_ref[...] = jnp.zeros_like(asum_ref)

    asum_ref[0] += jnp.broadcast_to(part, asum_ref.shape[1:])
    o_ref[...] = k.reshape(o_ref.shape)


def _fft_stage1(load_group, w1_ref, a_ref):
    n_slabs = a_ref.shape[0]

    def body(n2, carry):
        x = load_group(n2)
        a = jnp.dot(w1_ref[...], x, preferred_element_type=F32)
        base = pl.multiple_of(n2 * FFT_PITCH, 8)
        for s in range(n_slabs):
            a_ref[s, pl.ds(base, FFT_PITCH), :] = a[:, s * LANES:(s + 1) * LANES]
        return carry

    lax.fori_loop(0, FFT_N2, body, 0, unroll=FFT_UNROLL)


def _fft_load_k1(a_ref, k1):
    parts = []
    for off in (0, FFT_N1):
        parts.append(jnp.concatenate(
            [a_ref[s, pl.ds(k1 + off, FFT_N2, stride=FFT_PITCH), :] for s in range(a_ref.shape[0])], axis=1))
    return jnp.concatenate(parts, axis=0)


def _filt_fft_kernel(k_ref, asum_ref, w1_ref, w2_ref, o_ref, a_ref):
    step = pl.program_id(1)

    @pl.when(step == 0)
    def _():
        norm = asum_ref[0, 0:1, :] + asum_ref[1, 0:1, :]

        def load_group(n2):
            return (k_ref[pl.ds(pl.multiple_of(n2 * FFT_N1, FFT_N1), FFT_N1), :] / norm).astype(BF16)
        _fft_stage1(load_group, w1_ref, a_ref)

    for j in range(FFT_K1_PER_STEP):
        b = _fft_load_k1(a_ref, step * FFT_K1_PER_STEP + j).astype(BF16)
        o_ref[j] = jnp.dot(w2_ref[j], b, preferred_element_type=F32)


def _hy_conv_kernel(u_ref, kf_ref, w1_ref, w2_ref, v2_ref, v1_ref, o_ref, a_ref):
    step = pl.program_id(1)
    n_slabs = a_ref.shape[0]

    def tokens_of(member, s, n2):
        return (2 * s + member, pl.ds(n2, FFT_HALF_N1, stride=SEQ_PITCH), slice(None))

    @pl.when(step == 0)
    def _():
        o_ref[...] = jnp.zeros_like(o_ref)

        def load_group(n2):
            return jnp.concatenate(
                [jnp.concatenate([u_ref[tokens_of(member, s, n2)] for s in range(n_slabs)], axis=1)
                 for member in range(2)], axis=0).astype(BF16)
        _fft_stage1(load_group, w1_ref, a_ref)

    for j in range(FFT_K1_PER_STEP):
        k1 = step * FFT_K1_PER_STEP + j
        b = _fft_load_k1(a_ref, k1).astype(BF16)
        x = jnp.dot(w2_ref[j], b, preferred_element_type=F32)
        kf = kf_ref[j]
        kf = jnp.concatenate([kf] * n_slabs, axis=1)
        xr, xi = x[:FFT_N2], x[FFT_N2:]
        kr, ki = kf[:FFT_N2], kf[FFT_N2:]
        y = jnp.concatenate([xr * kr - xi * ki, xr * ki + xi * kr], axis=0).astype(BF16)
        d = jnp.dot(v2_ref[j], y, preferred_element_type=F32)
        for s in range(n_slabs):
            lanes = slice(s * LANES, (s + 1) * LANES)
            a_ref[s, pl.ds(k1, FFT_N2, stride=FFT_PITCH), :] = d[:FFT_N2, lanes]
            a_ref[s, pl.ds(k1 + FFT_N1, FFT_N2, stride=FFT_PITCH), :] = d[FFT_N2:, lanes]

    @pl.when(step == FFT_STEPS - 1)
    def _():
        def body(n2, carry):
            base = pl.multiple_of(n2 * FFT_PITCH, 8)
            d = jnp.concatenate([a_ref[s, pl.ds(base, 2 * FFT_N1), :] for s in range(n_slabs)], axis=1)
            y = jnp.dot(v1_ref[...], d.astype(BF16), preferred_element_type=F32)
            for s in range(n_slabs):
                for member in range(2):
                    o_ref[tokens_of(member, s, n2)] = y[member * FFT_HALF_N1:(member + 1) * FFT_HALF_N1,
                                                        s * LANES:(s + 1) * LANES]
            return carry

        lax.fori_loop(0, FFT_N2, body, 0, unroll=FFT_UNROLL)


def _from_pitched(ref):
    return jnp.concatenate([ref[0, j * SEQ_PITCH:j * SEQ_PITCH + FFT_N2, :] for j in range(TILE_GROUPS)], axis=0)


def _merge_kernel(x_ref, yna_ref, x0_ref, u_ref, yc_ref, gt_ref, mod_ref, skip_ref, gpost_ref,
                  wna_ref, why_ref, wout_ref, o_ref):
    y_hy = (x0_ref[0].astype(F32) * (_from_pitched(yc_ref) + _from_pitched(u_ref) * skip_ref[...])).astype(BF16)
    a = jnp.dot(yna_ref[0], wna_ref[...], preferred_element_type=F32)
    b = jnp.dot(y_hy, why_ref[...], preferred_element_type=F32)
    g_na = gt_ref[0, :, :D_MODEL].astype(F32)
    g_hy = gt_ref[0, :, D_MODEL:].astype(F32)
    m = (g_na * a + g_hy * b).astype(BF16)
    o = jnp.dot(m, wout_ref[...], preferred_element_type=F32)
    ms = jnp.mean(o * o, axis=-1, keepdims=True)
    o = o * lax.rsqrt(ms + RMS_EPS) * gpost_ref[...]
    o_ref[0] = x_ref[0] + mod_ref[0, 2:3, :] * o


def _gelu_tanh(a):
    return 0.5 * a * (1.0 + jnp.tanh(math.sqrt(2.0 / math.pi) * (a + 0.044715 * (a * a * a))))


def _ffn_kernel(x_ref, prev_ref, next_ref, mod_ref, gpre_ref, gpost_ref, wup_ref, cw_ref, cb_ref, wdn_ref,
                o_ref, u_ref, act_ref):
    i = pl.program_id(1)
    n_tiles = pl.num_programs(1)
    x = x_ref[0]
    xx = jnp.concatenate([prev_ref[0], x, next_ref[0]], axis=0)
    h = _norm_modulate(xx, gpre_ref[...], mod_ref[0, 3:4, :], mod_ref[0, 4:5, :])
    row = lax.broadcasted_iota(jnp.int32, (xx.shape[0], 1), 0)
    inside = ((row >= HALO) | (i > 0)) & ((row < HALO + TOKEN_TILE) | (i < n_tiles - 1))
    h = jnp.where(inside, h, 0.0).astype(BF16)
    starts = list(range(0, D_FF, FF_CHUNK))
    widths = [min(FF_CHUNK, D_FF - lo) for lo in starts]

    def conv(buf, half, s, lo, r0):
        lanes = slice(half * D_FF + lo, half * D_FF + lo + LANES)
        tap = lambda j: u_ref[buf, half, s, pl.ds(HALO - 1 + j + r0, FFN_ROW_BLOCK, stride=1), :]
        return (tap(0) * cw_ref[0:1, lanes] + tap(1) * cw_ref[1:2, lanes] + tap(2) * cw_ref[2:3, lanes]
                + cb_ref[:, lanes])

    def up_project(c):
        for half in range(2):
            lo = half * D_FF + starts[c]
            u = jnp.dot(h, wup_ref[:, lo:lo + widths[c]], preferred_element_type=F32)
            for s in range(widths[c] // LANES):
                u_ref[c % 2, half, s] = u[:, s * LANES:(s + 1) * LANES]

    up_project(0)
    for c in range(len(starts)):
        buf = c % 2
        if c + 1 < len(starts):
            up_project(c + 1)
        for s in range(widths[c] // LANES):
            lo = starts[c] + s * LANES
            for r0 in range(0, TOKEN_TILE, FFN_ROW_BLOCK):
                act_ref[r0:r0 + FFN_ROW_BLOCK, lo:lo + LANES] = (
                    _gelu_tanh(conv(buf, 0, s, lo, r0)) * conv(buf, 1, s, lo, r0)).astype(BF16)
    y = jnp.dot(act_ref[...], wdn_ref[...], preferred_element_type=F32)
    ms = jnp.mean(y * y, axis=-1, keepdims=True)
    y = y * lax.rsqrt(ms + RMS_EPS) * gpost_ref[...]
    o_ref[0] = x + mod_ref[0, 5:6, :] * y


def _const_spec(shape):
    nd = len(shape)
    return pl.BlockSpec(shape, lambda *_: (0,) * nd, pipeline_mode=pl.Buffered(1))


def kernel(x, c, ctx, c_ctx, w_mod, b_mod, norm_mix_pre, norm_mix_post, norm_ffn_pre, norm_ffn_post, w_in, b_in, na_rpb, hy_conv_w, hy_conv_b, hy_filt_w1, hy_filt_b1, hy_filt_w2, hy_filt_b2, hy_filt_w3, hy_filt_b3, hy_sin_freq, hy_decay, hy_skip, w_o_na, w_o_hy, w_out, ffn_w_up, ffn_conv_w, ffn_conv_b, ffn_w_down):
    batch, seq, d = x.shape
    n_ctx = ctx.shape[1]
    assert d == D_MODEL and 2 * seq == FFT_N and seq == GRID_W * GRID_W and batch % 2 == 0
    assert w_mod.shape[0] == 1, "single-layer block"
    n_tiles = seq // TOKEN_TILE
    d_in = w_in.shape[2]
    row2 = lambda a: a.reshape(1, -1)

    c_all = jnp.zeros((8, d), F32).at[:batch].set(c).at[batch].set(c_ctx)
    mod_n = 1024
    mod = pl.pallas_call(
        _mod_kernel,
        grid=(N_MOD * d // mod_n,),
        in_specs=[_const_spec((8, d)),
                  pl.BlockSpec((d, mod_n), lambda j: (0, j)),
                  pl.BlockSpec((1, mod_n), lambda j: (0, j))],
        out_specs=pl.BlockSpec((8, mod_n), lambda j: (0, j)),
        out_shape=jax.ShapeDtypeStruct((8, N_MOD * d), F32),
        compiler_params=_cparams(("arbitrary",)),
        name="mod",
    )(c_all, w_mod[0], row2(b_mod[0]))
    mod_lat = jnp.pad(mod[:batch].reshape(batch, N_MOD, d), ((0, 0), (0, 8 - N_MOD), (0, 0)))
    mod_ctx = jnp.pad(mod[batch].reshape(N_MOD, d), ((0, 8 - N_MOD), (0, 0)))

    w_in_b = w_in[0].astype(BF16)
    b_in_r = row2(b_in[0])
    g_mix_pre = row2(norm_mix_pre[0])

    k_ctx, v_ctx = pl.pallas_call(
        _ctx_kv_kernel,
        grid=(batch,),
        in_specs=[pl.BlockSpec((1, n_ctx, d), lambda b: (b, 0, 0)),
                  _const_spec((8, d)), _const_spec((1, d)),
                  _const_spec((d, 2 * D_ATTN)), _const_spec((1, 2 * D_ATTN))],
        out_specs=[pl.BlockSpec((1, n_ctx, D_ATTN), lambda b: (b, 0, 0)),
                   pl.BlockSpec((1, D_ATTN, n_ctx), lambda b: (b, 0, 0))],
        out_shape=[jax.ShapeDtypeStruct((batch, n_ctx, D_ATTN), BF16),
                   jax.ShapeDtypeStruct((batch, D_ATTN, n_ctx), BF16)],
        compiler_params=_cparams(("arbitrary",)),
        name="ctx_kv",
    )(ctx, mod_ctx, g_mix_pre, w_in_b[:, D_ATTN:3 * D_ATTN], b_in_r[:, D_ATTN:3 * D_ATTN])

    cos_t, sin_t = _rope_tables(seq)
    tok = lambda w: pl.BlockSpec((1, TOKEN_TILE, w), lambda b, i: (b, i, 0))
    mod_spec = pl.BlockSpec((1, 8, d), lambda b, i: (b, 0, 0))
    rope_spec = pl.BlockSpec((TOKEN_TILE, LANES), lambda b, i: (i, 0))
    halo_blocks = TOKEN_TILE // HALO
    n_halo_blocks = seq // HALO
    prev_spec = lambda w: pl.BlockSpec((1, HALO, w), lambda b, i: (b, jnp.maximum(i * halo_blocks - 1, 0), 0))
    next_spec = lambda w: pl.BlockSpec(
        (1, HALO, w), lambda b, i: (b, jnp.minimum((i + 1) * halo_blocks, n_halo_blocks - 1), 0))
    pitched_spec = pl.BlockSpec((1, TILE_GROUPS * SEQ_PITCH, D_HYENA), lambda b, i: (b, i, 0))
    pitched_shape = jax.ShapeDtypeStruct((batch, PITCHED_ROWS, D_HYENA), F32)
    q_rot, q_plain, k_rot, v_lat, gates, u_p, x0 = pl.pallas_call(
        _in_proj_kernel,
        grid=(batch, n_tiles),
        in_specs=[tok(d), prev_spec(d), next_spec(d), mod_spec, _const_spec((1, d)), _const_spec((d, d_in)),
                  _const_spec((1, d_in)), rope_spec, rope_spec,
                  _const_spec((3, 3 * D_HYENA)), _const_spec((1, 3 * D_HYENA))],
        out_specs=[tok(D_ATTN)] * 3
        + [pl.BlockSpec((1, TOKEN_TILE // ATTN_KEY_CHUNK, D_ATTN, ATTN_KEY_CHUNK), lambda b, i: (b, i, 0, 0)),
           tok(2 * d), pitched_spec, tok(D_HYENA)],
        out_shape=[jax.ShapeDtypeStruct((batch, seq, D_ATTN), BF16)] * 3
        + [jax.ShapeDtypeStruct((batch, seq // ATTN_KEY_CHUNK, D_ATTN, ATTN_KEY_CHUNK), BF16),
           jax.ShapeDtypeStruct((batch, seq, 2 * d), BF16), pitched_shape,
           jax.ShapeDtypeStruct((batch, seq, D_HYENA), BF16)],
        scratch_shapes=[pltpu.VMEM((3 * D_HYENA // LANES, TOKEN_TILE + 2 * HALO, LANES), F32)],
        compiler_params=_cparams(("arbitrary", "arbitrary")),
        name="in_proj",
    )(x, x, x, mod_lat, g_mix_pre, w_in_b, b_in_r, jnp.asarray(cos_t), jnp.asarray(sin_t),
      hy_conv_w[0], row2(hy_conv_b[0]))

    bias = pl.pallas_call(
        _attn_bias_kernel,
        grid=(N_HEADS,),
        in_specs=[pl.BlockSpec(memory_space=pltpu.SMEM)],
        out_specs=pl.BlockSpec((3, 1, ATTN_KEYS, ATTN_Q), lambda h: (0, h, 0, 0)),
        out_shape=jax.ShapeDtypeStruct((3, N_HEADS, ATTN_KEYS, ATTN_Q), F32),
        scratch_shapes=[pltpu.VMEM((N_BIAS_ROWS, GRID_W, LANES), F32)],
        compiler_params=_cparams(("arbitrary",)),
        name="attn_bias",
    )(na_rpb[0].reshape(-1))
    n_groups = GRID_W // ATTN_ROWS_PER_STEP
    q_spec = pl.BlockSpec((1, ATTN_Q, D_ATTN), lambda b, g: (b, g, 0))
    full = lambda n: pl.BlockSpec((1, n, D_ATTN), lambda b, g: (b, 0, 0))
    bias_spec = pl.BlockSpec(
        (1, N_HEADS, ATTN_KEYS, ATTN_Q),
        lambda b, g: ((g > 0).astype(jnp.int32) + (g == n_groups - 1).astype(jnp.int32), 0, 0, 0))
    vt_spec = pl.BlockSpec((1, seq // ATTN_KEY_CHUNK, D_ATTN, ATTN_KEY_CHUNK), lambda b, g: (b, 0, 0, 0))
    vct_spec = pl.BlockSpec((1, D_ATTN, n_ctx), lambda b, g: (b, 0, 0))
    y_na = pl.pallas_call(
        _attn_kernel,
        grid=(batch, n_groups),
        in_specs=[q_spec, q_spec, full(seq), vt_spec, full(n_ctx), vct_spec, bias_spec],
        out_specs=q_spec,
        out_shape=jax.ShapeDtypeStruct((batch, seq, D_ATTN), BF16),
        compiler_params=_cparams(("arbitrary", "arbitrary")),
        name="attn",
    )(q_rot, q_plain, k_rot, v_lat, k_ctx, v_ctx, bias)

    z_t, aux_t = _filter_tables(seq)
    filt_n2 = FILT_ROWS // FFT_HALF_N1
    hid = FILTER_HIDDEN
    w1_pad = jnp.pad(hy_filt_w1[0], ((0, hid - POS_FEATS), (0, 0)))
    block_diag = lambda w: jnp.zeros((2 * hid, 2 * hid), F32).at[:hid, :hid].set(w).at[hid:, hid:].set(w)
    twice = lambda v: jnp.tile(v, (1, 2))
    w3 = hy_filt_w3[0]
    w3_parts = jnp.stack([jnp.concatenate([w3, jnp.zeros_like(w3)], axis=0),
                          jnp.concatenate([jnp.zeros_like(w3), w3], axis=0)])
    k_circ, k_asum = pl.pallas_call(
        _filt_kernel,
        grid=(2, seq // FILT_ROWS),
        in_specs=[pl.BlockSpec((1, FILT_ROWS // 2, 2 * hid), lambda hf, i: (hf, i, 0)),
                  pl.BlockSpec((1, FILT_ROWS, aux_t.shape[2]), lambda hf, i: (hf, i, 0)),
                  _const_spec((2 * hid, 2 * hid)), _const_spec((1, 2 * hid)),
                  _const_spec((2 * hid, 2 * hid)), _const_spec((1, 2 * hid)),
                  pl.BlockSpec((2, 2 * hid, D_HYENA), lambda hf, i: (0, 0, hf)),
                  pl.BlockSpec((1, D_HYENA), lambda hf, i: (0, hf)),
                  _const_spec((2, 2 * hid)),
                  pl.BlockSpec((1, 1, D_HYENA), lambda hf, i: (hf, 0, 0))],
        out_specs=[pl.BlockSpec((filt_n2, FFT_HALF_N1, D_HYENA), lambda hf, i: (i, hf, 0)),
                   pl.BlockSpec((1, 8, D_HYENA), lambda hf, i: (hf, 0, 0))],
        out_shape=[jax.ShapeDtypeStruct((FFT_N2, FFT_N1, D_HYENA), F32),
                   jax.ShapeDtypeStruct((2, 8, D_HYENA), F32)],
        compiler_params=_cparams(("arbitrary", "arbitrary")),
        name="filt",
    )(jnp.asarray(z_t), jnp.asarray(aux_t), block_diag(w1_pad), twice(row2(hy_filt_b1[0])),
      block_diag(hy_filt_w2[0]), twice(row2(hy_filt_b2[0])), w3_parts, row2(hy_filt_b3[0]),
      twice(hy_sin_freq[0]), hy_decay[0].reshape(2, 1, D_HYENA))
    k_circ = k_circ.reshape(FFT_N, D_HYENA)

    mats = _fft_matrices()
    mats = {k: jnp.asarray(v).astype(BF16) for k, v in mats.items()}
    w2_all = mats["w2"]
    v2_all = mats["v2"]
    step_mat_spec = pl.BlockSpec((FFT_K1_PER_STEP, 2 * FFT_N2, 2 * FFT_N2), lambda cb, s: (s, 0, 0))
    filt_slabs = 2
    kf = pl.pallas_call(
        _filt_fft_kernel,
        grid=(D_HYENA // (filt_slabs * LANES), FFT_STEPS),
        in_specs=[pl.BlockSpec((FFT_N, filt_slabs * LANES), lambda cb, s: (0, cb)),
                  pl.BlockSpec((2, 8, filt_slabs * LANES), lambda cb, s: (0, 0, cb)),
                  _const_spec((FFT_PITCH, FFT_N1)), step_mat_spec],
        out_specs=pl.BlockSpec((FFT_K1_PER_STEP, 2 * FFT_N2, filt_slabs * LANES), lambda cb, s: (s, 0, cb)),
        out_shape=jax.ShapeDtypeStruct((FFT_N1, 2 * FFT_N2, D_HYENA), F32),
        scratch_shapes=[pltpu.VMEM((filt_slabs, FFT_N2 * FFT_PITCH, LANES), F32)],
        compiler_params=_cparams(("arbitrary", "arbitrary")),
        name="filt_fft",
    )(k_circ, k_asum, mats["w1_real"], w2_all)

    n_pairs = batch // 2
    pair_block = ((batch, PITCHED_ROWS, LANES), lambda cb, s: (0, 0, cb))
    pair_spec = pl.BlockSpec(*pair_block, pipeline_mode=pl.Buffered(1))
    y_conv = pl.pallas_call(
        _hy_conv_kernel,
        grid=(D_HYENA // LANES, FFT_STEPS),
        in_specs=[pl.BlockSpec(*pair_block),
                  pl.BlockSpec((FFT_K1_PER_STEP, 2 * FFT_N2, LANES), lambda cb, s: (s, 0, cb)),
                  _const_spec((FFT_PITCH, FFT_N1)), step_mat_spec, step_mat_spec,
                  _const_spec((FFT_N1, 2 * FFT_N1))],
        out_specs=pair_spec,
        out_shape=pitched_shape,
        scratch_shapes=[pltpu.VMEM((n_pairs, FFT_N2 * FFT_PITCH, LANES), F32)],
        compiler_params=_cparams(("arbitrary", "arbitrary")),
        name="hy_conv",
    )(u_p, kf, mats["w1_data"], w2_all, v2_all, mats["v1"])

    x1 = pl.pallas_call(
        _merge_kernel,
        grid=(batch, n_tiles),
        in_specs=[tok(d), tok(D_ATTN), tok(D_HYENA), pitched_spec, pitched_spec, tok(2 * d), mod_spec,
                  _const_spec((1, D_HYENA)), _const_spec((1, d)),
                  _const_spec((D_ATTN, d)), _const_spec((D_HYENA, d)), _const_spec((d, d))],
        out_specs=tok(d),
        out_shape=jax.ShapeDtypeStruct((batch, seq, d), F32),
        compiler_params=_cparams(("arbitrary", "arbitrary")),
        name="merge",
    )(x, y_na, x0, u_p, y_conv, gates, mod_lat, row2(hy_skip[0]), row2(norm_mix_post[0]),
      w_o_na[0].astype(BF16), w_o_hy[0].astype(BF16), w_out[0].astype(BF16))

    w_up_c = ffn_w_up[0].astype(BF16)
    conv_w_c = ffn_conv_w[0]
    conv_b_c = row2(ffn_conv_b[0])
    w_dn_c = ffn_w_down[0].astype(BF16)
    out = pl.pallas_call(
        _ffn_kernel,
        grid=(batch, n_tiles),
        in_specs=[tok(d), prev_spec(d), next_spec(d), mod_spec, _const_spec((1, d)), _const_spec((1, d)),
                  _const_spec(w_up_c.shape), _const_spec(conv_w_c.shape), _const_spec(conv_b_c.shape),
                  _const_spec(w_dn_c.shape)],
        out_specs=tok(d),
        out_shape=jax.ShapeDtypeStruct((batch, seq, d), F32),
        scratch_shapes=[pltpu.VMEM((2, 2, FF_CHUNK // LANES, TOKEN_TILE + 2 * HALO, LANES), F32),
                        pltpu.VMEM((TOKEN_TILE, D_FF), BF16)],
        compiler_params=_cparams(("arbitrary", "arbitrary")),
        name="ffn",
    )(x1, x1, x1, mod_lat, row2(norm_ffn_pre[0]), row2(norm_ffn_post[0]), w_up_c, conv_w_c, conv_b_c, w_dn_c)
    return out
```

```python
import functools
import math

import jax
import jax.numpy as jnp
import numpy as np
from jax import lax
from jax.experimental import pallas as pl
from jax.experimental.pallas import tpu as pltpu

F32 = jnp.float32
BF16 = jnp.bfloat16

D_MODEL = 1024
N_HEADS = 8
HEAD_DIM = 64
D_ATTN = N_HEADS * HEAD_DIM
D_HYENA = 512
GRID_W = 64
WIN_ROWS = 8
WIN_COLS = 16
POS_BANDS = 16
POS_FEATS = 1 + 2 * POS_BANDS
FILTER_HIDDEN = 64
D_FF = 2816
N_MOD = 6
ROPE_BASE = 10000.0
RMS_EPS = 1e-6
NEG_BIAS = -1e30
LOG2_E = math.log2(math.e)

LANES = 128
VMEM_LIMIT_BYTES = 56 * 1024 * 1024

FFT_N1 = 64
FFT_N2 = 128
FFT_N = FFT_N1 * FFT_N2
FFT_HALF_N1 = FFT_N1 // 2
FFT_PITCH = 2 * FFT_N1 + 8
FFT_K1_PER_STEP = 8
FFT_UNROLL = 8
FFT_STEPS = FFT_N1 // FFT_K1_PER_STEP

TOKEN_TILE = 512
FILT_ROWS = 1024
SEQ_PITCH = FFT_N2 + 8
PITCHED_ROWS = FFT_HALF_N1 * SEQ_PITCH
TILE_GROUPS = TOKEN_TILE // FFT_N2
ATTN_ROWS_PER_STEP = 4
ATTN_Q = ATTN_ROWS_PER_STEP * GRID_W
ATTN_KEY_ROWS = 12
ATTN_KEYS = ATTN_KEY_ROWS * GRID_W
ATTN_KEY_CHUNK = 256
FF_CHUNK = 768
HALO = 8
FFN_ROW_BLOCK = 64


def _cparams(sem):
    return pltpu.CompilerParams(dimension_semantics=sem, vmem_limit_bytes=VMEM_LIMIT_BYTES)


@functools.lru_cache(maxsize=None)
def _rope_tables(seq):
    pos = np.arange(seq)
    row = pos // GRID_W
    col = pos % GRID_W
    n_pairs = HEAD_DIM // 4
    inv = ROPE_BASE ** (-np.arange(n_pairs, dtype=np.float64) / n_pairs)
    lane = np.arange(LANES) % HEAD_DIM
    p = np.where(lane[None, :] < HEAD_DIM // 2, row[:, None], col[:, None]).astype(np.float64)
    ang = p * inv[lane % n_pairs][None, :]
    sign = np.where((lane % (2 * n_pairs)) < n_pairs, -1.0, 1.0)
    return np.cos(ang).astype(np.float32), (np.sin(ang) * sign[None, :]).astype(np.float32)


@functools.lru_cache(maxsize=None)
def _filter_tables(seq):
    assert 2 * seq == FFT_N
    half, n2, n1 = np.meshgrid(np.arange(2), np.arange(FFT_N2), np.arange(FFT_HALF_N1), indexing="ij")
    n = FFT_N2 * (half * FFT_HALF_N1 + n1) + n2
    fwd = n < seq
    m = n - seq
    valid = fwd | (m >= 1)
    pos = np.where(valid, np.where(fwd, n, seq - m), 0).astype(np.float64).reshape(2, seq)
    t = pos / max(seq - 1, 1)
    bands = np.linspace(1e-4, POS_BANDS - 1, POS_BANDS)
    ang = (2.0 * math.pi / seq) * pos[..., None] * bands
    z = np.zeros((2, seq, 64), np.float64)
    z[..., 0] = t
    z[..., 1:1 + POS_BANDS] = np.cos(ang)
    z[..., 1 + POS_BANDS:POS_FEATS] = -np.sin(ang)
    aux = np.zeros((2, seq, 8), np.float64)
    aux[..., 0] = t
    aux[..., 1] = valid.reshape(2, seq)
    zp = z.reshape(2, seq // FILT_ROWS, 2, FILT_ROWS // 2, 64).transpose(0, 1, 3, 2, 4).reshape(2, seq // 2, 128)
    return zp.astype(np.float32), aux.astype(np.float32)


def _realify(m):
    return np.block([[m.real, -m.imag], [m.imag, m.real]])


@functools.lru_cache(maxsize=None)
def _fft_matrices():
    n1 = np.arange(FFT_N1)
    n2 = np.arange(FFT_N2)
    k1 = np.arange(FFT_N1)
    k2 = np.arange(FFT_N2)
    f1 = np.exp(-2j * np.pi * np.outer(k1, n1) / FFT_N1)
    w1_data = _realify(f1[:, :FFT_HALF_N1])
    w1_real = np.concatenate([f1.real, f1.imag], axis=0)
    v1 = _realify(np.conj(f1.T)[:FFT_HALF_N1, :] / FFT_N)
    f2 = np.exp(-2j * np.pi * np.outer(k2, n2) / FFT_N2)
    tw = np.exp(-2j * np.pi * np.outer(k1, n2) / FFT_N)
    w2 = np.stack([_realify(f2 * tw[a][None, :]) for a in range(FFT_N1)])
    v2 = np.stack([_realify(np.conj(f2.T) * np.conj(tw[a])[:, None]) for a in range(FFT_N1)])
    pad = np.zeros((FFT_PITCH - 2 * FFT_N1, FFT_N1))
    w1_data = np.concatenate([w1_data, pad], axis=0)
    w1_real = np.concatenate([w1_real, pad], axis=0)
    return {k: v.astype(np.float32) for k, v in
            dict(w1_data=w1_data, w1_real=w1_real, v1=v1, w2=w2, v2=v2).items()}


N_BIAS_ROWS = 2 * WIN_ROWS - 1
N_BIAS_COLS = 2 * WIN_COLS - 1


def _attn_bias_row_index():
    rows = GRID_W
    groups = (0, 2, rows // ATTN_ROWS_PER_STEP - 1)
    dr = np.full((3, ATTN_ROWS_PER_STEP, ATTN_KEY_ROWS), -1, np.int32)
    for v, g in enumerate(groups):
        ws = min(max(ATTN_ROWS_PER_STEP * g - WIN_ROWS // 2, 0), rows - ATTN_KEY_ROWS)
        for i in range(ATTN_ROWS_PER_STEP):
            r = ATTN_ROWS_PER_STEP * g + i
            r_start = min(max(r - WIN_ROWS // 2, 0), rows - WIN_ROWS)
            for j in range(ATTN_KEY_ROWS):
                kr = ws + j
                if r_start <= kr < r_start + WIN_ROWS:
                    dr[v, i, j] = kr - r + (WIN_ROWS - 1)
    return dr


def _attn_bias_kernel(rpb_ref, o_ref, t_ref):
    head = pl.program_id(0)
    kc = lax.broadcasted_iota(jnp.int32, (GRID_W, LANES), 0)
    lane = lax.broadcasted_iota(jnp.int32, (GRID_W, LANES), 1)
    qc = lane % GRID_W
    c_start = jnp.clip(qc - WIN_COLS // 2, 0, GRID_W - WIN_COLS)
    col_in = (kc >= c_start) & (kc < c_start + WIN_COLS)
    dc = jnp.clip(kc - qc, 1 - WIN_COLS, WIN_COLS - 1) + (WIN_COLS - 1)
    base = head * (N_BIAS_ROWS * N_BIAS_COLS)
    for r in range(N_BIAS_ROWS):
        t = jnp.full((GRID_W, LANES), NEG_BIAS, F32)
        for cidx in range(N_BIAS_COLS):
            t = jnp.where(col_in & (dc == cidx), rpb_ref[base + r * N_BIAS_COLS + cidx] * LOG2_E, t)
        t_ref[r] = t
    dr = _attn_bias_row_index()
    low_half = lane < GRID_W
    masked = jnp.full((GRID_W, LANES), NEG_BIAS, F32)
    for v in range(3):
        for i in range(0, ATTN_ROWS_PER_STEP, 2):
            for j in range(ATTN_KEY_ROWS):
                lo = t_ref[int(dr[v, i, j])] if dr[v, i, j] >= 0 else masked
                hi = t_ref[int(dr[v, i + 1, j])] if dr[v, i + 1, j] >= 0 else masked
                o_ref[v, 0, j * GRID_W:(j + 1) * GRID_W, i * GRID_W:(i + 2) * GRID_W] = jnp.where(
                    low_half, lo, hi)


def _mod_kernel(c_ref, w_ref, b_ref, o_ref):
    c = c_ref[...]
    s = c * jax.nn.sigmoid(c)
    o_ref[...] = jnp.dot(s, w_ref[...], precision=lax.Precision.HIGHEST,
                         preferred_element_type=F32) + b_ref[...]


def _norm_modulate(x, gain, shift, scale):
    ms = jnp.mean(x * x, axis=-1, keepdims=True)
    y = x * lax.rsqrt(ms + RMS_EPS) * gain
    return y * (1.0 + scale) + shift


def _rope(t, cos, sin_signed):
    n_pairs = HEAD_DIM // 4
    lane = lax.broadcasted_iota(jnp.int32, t.shape, 1)
    first = (lane % (2 * n_pairs)) < n_pairs
    partner = jnp.where(first, pltpu.roll(t, LANES - n_pairs, 1), pltpu.roll(t, n_pairs, 1))
    return t * cos + partner * sin_signed


def _in_proj_kernel(x_ref, prev_ref, next_ref, mod_ref, g_ref, w_ref, b_ref, cos_ref, sin_ref, cw_ref, cb_ref,
                    qr_ref, qp_ref, kr_ref, v_ref, gt_ref, u_ref, x0_ref, hy_ref):
    i = pl.program_id(1)
    n_tiles = pl.num_programs(1)
    xx = jnp.concatenate([prev_ref[0], x_ref[0], next_ref[0]], axis=0)
    h_ext = _norm_modulate(xx, g_ref[...], mod_ref[0, 0:1, :], mod_ref[0, 1:2, :]).astype(BF16)
    h = h_ext[HALO:HALO + TOKEN_TILE]
    cos = cos_ref[...]
    sin = sin_ref[...]

    def proj(lo, hi, rows=h):
        return jnp.dot(rows, w_ref[:, lo:hi], preferred_element_type=F32) + b_ref[:, lo:hi]

    q = proj(0, D_ATTN) * (HEAD_DIM ** -0.5 * LOG2_E)
    qp_ref[0] = q.astype(BF16)
    k = proj(D_ATTN, 2 * D_ATTN)
    for c in range(D_ATTN // LANES):
        lanes = slice(c * LANES, (c + 1) * LANES)
        qr_ref[0, :, lanes] = _rope(q[:, lanes], cos, sin).astype(BF16)
        kr_ref[0, :, lanes] = _rope(k[:, lanes], cos, sin).astype(BF16)
    v_t = proj(2 * D_ATTN, 3 * D_ATTN).T.astype(BF16)
    for c in range(TOKEN_TILE // ATTN_KEY_CHUNK):
        v_ref[0, c] = v_t[:, c * ATTN_KEY_CHUNK:(c + 1) * ATTN_KEY_CHUNK]
    hy_lo = 3 * D_ATTN
    gl_lo = hy_lo + 3 * D_HYENA
    for c in range(4):
        w = D_MODEL // 2
        gt_ref[0, :, c * w:(c + 1) * w] = jax.nn.sigmoid(
            proj(gl_lo + c * w, gl_lo + (c + 1) * w)).astype(BF16)

    slabs_per_part = D_HYENA // LANES
    for c in range(3):
        hy = proj(hy_lo + c * D_HYENA, hy_lo + (c + 1) * D_HYENA, h_ext)
        for s in range(slabs_per_part):
            hy_ref[c * slabs_per_part + s] = hy[:, s * LANES:(s + 1) * LANES]
    zero_row = jnp.zeros((1, LANES), F32)

    @pl.when(i == 0)
    def _():
        for s in range(3 * slabs_per_part):
            hy_ref[s, HALO - 1:HALO, :] = zero_row

    @pl.when(i == n_tiles - 1)
    def _():
        for s in range(3 * slabs_per_part):
            hy_ref[s, HALO + TOKEN_TILE:HALO + TOKEN_TILE + 1, :] = zero_row

    def conv(s, r0, rows):
        lanes = slice(s * LANES, (s + 1) * LANES)
        tap = lambda j: hy_ref[s, pl.ds(HALO - 1 + j + r0, rows, stride=1), :]
        return (tap(0) * cw_ref[0:1, lanes] + tap(1) * cw_ref[1:2, lanes] + tap(2) * cw_ref[2:3, lanes]
                + cb_ref[:, lanes])

    for s in range(slabs_per_part):
        lanes = slice(s * LANES, (s + 1) * LANES)
        for j in range(TILE_GROUPS):
            r0 = j * FFT_N2
            x0_ref[0, r0:r0 + FFT_N2, lanes] = conv(s, r0, FFT_N2).astype(BF16)
            u_ref[0, j * SEQ_PITCH:j * SEQ_PITCH + FFT_N2, lanes] = (
                conv(slabs_per_part + s, r0, FFT_N2) * conv(2 * slabs_per_part + s, r0, FFT_N2))
            u_ref[0, j * SEQ_PITCH + FFT_N2:(j + 1) * SEQ_PITCH, lanes] = jnp.zeros((SEQ_PITCH - FFT_N2, LANES), F32)


def _ctx_kv_kernel(x_ref, mod_ref, g_ref, w_ref, b_ref, k_ref, v_ref):
    h = _norm_modulate(x_ref[0], g_ref[...], mod_ref[0:1, :], mod_ref[1:2, :]).astype(BF16)
    kv = jnp.dot(h, w_ref[...], preferred_element_type=F32) + b_ref[...]
    k_ref[0] = kv[:, :D_ATTN].astype(BF16)
    v_ref[0] = kv[:, D_ATTN:].T.astype(BF16)


def _attn_window_start(g):
    return jnp.clip(ATTN_ROWS_PER_STEP * g - WIN_ROWS // 2, 0, GRID_W - ATTN_KEY_ROWS)


def _attn_kernel(qr_ref, qp_ref, k_ref, vt_ref, kc_ref, vct_ref, bias_ref, o_ref):
    g = pl.program_id(1)
    win = _attn_window_start(g)
    key0 = pl.multiple_of(win * GRID_W, ATTN_KEY_CHUNK)
    chunk0 = win // (ATTN_KEY_CHUNK // GRID_W)
    nt = (((1,), (1,)), ((), ()))
    quad_w = 4 * HEAD_DIM
    lane = lax.broadcasted_iota(jnp.int32, (1, quad_w), 1)
    zero = jnp.zeros((), BF16)
    def scores(head):
        quad, hh = divmod(head, 4)
        ql = slice(quad * quad_w, (quad + 1) * quad_w)
        mine = (lane >= hh * HEAD_DIM) & (lane < (hh + 1) * HEAD_DIM)
        s_nb = lax.dot_general(k_ref[0, pl.ds(key0, ATTN_KEYS), ql], jnp.where(mine, qr_ref[0, :, ql], zero), nt,
                               preferred_element_type=F32)
        s_cx = lax.dot_general(kc_ref[0, :, ql], jnp.where(mine, qp_ref[0, :, ql], zero), nt,
                               preferred_element_type=F32)
        return s_nb + bias_ref[0, head], s_cx

    def with_ones(v):
        return jnp.concatenate([v, jnp.ones((16, v.shape[1]), BF16)], axis=0)

    def probs(s_nb, s_cx):
        m = jnp.maximum(jnp.max(s_nb, axis=0, keepdims=True), jnp.max(s_cx, axis=0, keepdims=True))
        return jnp.exp2(s_nb - m).astype(BF16), jnp.exp2(s_cx - m).astype(BF16)

    def values(head, p_nb, p_cx):
        rows = slice(head * HEAD_DIM, (head + 1) * HEAD_DIM)
        v_win = jnp.concatenate([vt_ref[0, chunk0 + c, rows, :] for c in range(ATTN_KEYS // ATTN_KEY_CHUNK)],
                                axis=1)
        o = (jnp.dot(with_ones(v_win), p_nb, preferred_element_type=F32)
             + jnp.dot(with_ones(vct_ref[0, rows, :]), p_cx, preferred_element_type=F32))
        return o[:HEAD_DIM] / o[HEAD_DIM:HEAD_DIM + 1]

    outs = []
    s_q = {0: scores(0), 1: scores(1)}
    p_q = {0: probs(*s_q.pop(0))}
    for head in range(N_HEADS):
        if head + 2 < N_HEADS:
            s_q[head + 2] = scores(head + 2)
        if head + 1 < N_HEADS:
            p_q[head + 1] = probs(*s_q.pop(head + 1))
        outs.append(values(head, *p_q.pop(head)))
    o_ref[0] = jnp.concatenate(outs, axis=0).T.astype(BF16)


def _filt_kernel(z_ref, aux_ref, w1_ref, b1_ref, w2_ref, b2_ref, w3_ref, b3_ref, freq_ref, decay_ref,
                 o_ref, asum_ref):
    i = pl.program_id(1)
    hp = lax.Precision.HIGHEST
    h = jnp.sin(freq_ref[0:1, :] * (jnp.dot(z_ref[0], w1_ref[...], precision=hp,
                                            preferred_element_type=F32) + b1_ref[...]))
    h = jnp.sin(freq_ref[1:2, :] * (jnp.dot(h, w2_ref[...], precision=hp,
                                            preferred_element_type=F32) + b2_ref[...]))
    taps = jnp.concatenate([jnp.dot(h, w3_ref[part], precision=hp, preferred_element_type=F32)
                            for part in range(2)], axis=0) + b3_ref[...]
    t = aux_ref[0, :, 0:1]
    valid = aux_ref[0, :, 1:2] > 0.5
    k = jnp.where(valid, taps * jnp.exp(-t * jnp.abs(decay_ref[0])), 0.0)
    part = jnp.sum(jnp.abs(k), axis=0, keepdims=True)

    @pl.when(i == 0)
    def _():
        asum_ref[...] = jnp.zeros_like(asum_ref)

    asum_ref[0] += jnp.broadcast_to(part, asum_ref.shape[1:])
    o_ref[...] = k.reshape(o_ref.shape)


def _fft_stage1(load_group, w1_ref, a_ref):
    n_slabs = a_ref.shape[0]

    def body(n2, carry):
        x = load_group(n2)
        a = jnp.dot(w1_ref[...], x, preferred_element_type=F32)
        base = pl.multiple_of(n2 * FFT_PITCH, 8)
        for s in range(n_slabs):
            a_ref[s, pl.ds(base, FFT_PITCH), :] = a[:, s * LANES:(s + 1) * LANES]
        return carry

    lax.fori_loop(0, FFT_N2, body, 0, unroll=FFT_UNROLL)


def _fft_load_k1(a_ref, k1):
    parts = []
    for off in (0, FFT_N1):
        parts.append(jnp.concatenate(
            [a_ref[s, pl.ds(k1 + off, FFT_N2, stride=FFT_PITCH), :] for s in range(a_ref.shape[0])], axis=1))
    return jnp.concatenate(parts, axis=0)


def _filt_fft_kernel(k_ref, asum_ref, w1_ref, w2_ref, o_ref, a_ref):
    step = pl.program_id(1)

    @pl.when(step == 0)
    def _():
        norm = asum_ref[0, 0:1, :] + asum_ref[1, 0:1, :]

        def load_group(n2):
            return (k_ref[pl.ds(pl.multiple_of(n2 * FFT_N1, FFT_N1), FFT_N1), :] / norm).astype(BF16)
        _fft_stage1(load_group, w1_ref, a_ref)

    for j in range(FFT_K1_PER_STEP):
        b = _fft_load_k1(a_ref, step * FFT_K1_PER_STEP + j).astype(BF16)
        o_ref[j] = jnp.dot(w2_ref[j], b, preferred_element_type=F32)


def _hy_conv_kernel(u_ref, kf_ref, w1_ref, w2_ref, v2_ref, v1_ref, o_ref, a_ref):
    step = pl.program_id(1)
    n_slabs = a_ref.shape[0]

    def tokens_of(member, s, n2):
        return (2 * s + member, pl.ds(n2, FFT_HALF_N1, stride=SEQ_PITCH), slice(None))

    @pl.when(step == 0)
    def _():
        o_ref[...] = jnp.zeros_like(o_ref)

        def load_group(n2):
            return jnp.concatenate(
                [jnp.concatenate([u_ref[tokens_of(member, s, n2)] for s in range(n_slabs)], axis=1)
                 for member in range(2)], axis=0).astype(BF16)
        _fft_stage1(load_group, w1_ref, a_ref)

    for j in range(FFT_K1_PER_STEP):
        k1 = step * FFT_K1_PER_STEP + j
        b = _fft_load_k1(a_ref, k1).astype(BF16)
        x = jnp.dot(w2_ref[j], b, preferred_element_type=F32)
        kf = kf_ref[j]
        kf = jnp.concatenate([kf] * n_slabs, axis=1)
        xr, xi = x[:FFT_N2], x[FFT_N2:]
        kr, ki = kf[:FFT_N2], kf[FFT_N2:]
        y = jnp.concatenate([xr * kr - xi * ki, xr * ki + xi * kr], axis=0).astype(BF16)
        d = jnp.dot(v2_ref[j], y, preferred_element_type=F32)
        for s in range(n_slabs):
            lanes = slice(s * LANES, (s + 1) * LANES)
            a_ref[s, pl.ds(k1, FFT_N2, stride=FFT_PITCH), :] = d[:FFT_N2, lanes]
            a_ref[s, pl.ds(k1 + FFT_N1, FFT_N2, stride=FFT_PITCH), :] = d[FFT_N2:, lanes]

    @pl.when(step == FFT_STEPS - 1)
    def _():
        def body(n2, carry):
            base = pl.multiple_of(n2 * FFT_PITCH, 8)
            d = jnp.concatenate([a_ref[s, pl.ds(base, 2 * FFT_N1), :] for s in range(n_slabs)], axis=1)
            y = jnp.dot(v1_ref[...], d.astype(BF16), preferred_element_type=F32)
            for s in range(n_slabs):
                for member in range(2):
                    o_ref[tokens_of(member, s, n2)] = y[member * FFT_HALF_N1:(member + 1) * FFT_HALF_N1,
                                                        s * LANES:(s + 1) * LANES]
            return carry

        lax.fori_loop(0, FFT_N2, body, 0, unroll=FFT_UNROLL)


def _from_pitched(ref):
    return jnp.concatenate([ref[0, j * SEQ_PITCH:j * SEQ_PITCH + FFT_N2, :] for j in range(TILE_GROUPS)], axis=0)


def _merge_kernel(yna_ref, x0_ref, u_ref, yc_ref, gt_ref, skip_ref, gpost_ref, wna_ref, why_ref, wout_ref, o_ref):
    y_hy = (x0_ref[0].astype(F32) * (_from_pitched(yc_ref) + _from_pitched(u_ref) * skip_ref[...])).astype(BF16)
    a = jnp.dot(yna_ref[0], wna_ref[...], preferred_element_type=F32)
    b = jnp.dot(y_hy, why_ref[...], preferred_element_type=F32)
    g_na = gt_ref[0, :, :D_MODEL].astype(F32)
    g_hy = gt_ref[0, :, D_MODEL:].astype(F32)
    m = (g_na * a + g_hy * b).astype(BF16)
    o = jnp.dot(m, wout_ref[...], preferred_element_type=F32)
    ms = jnp.mean(o * o, axis=-1, keepdims=True)
    o_ref[0] = (o * lax.rsqrt(ms + RMS_EPS) * gpost_ref[...]).astype(BF16)


def _gelu_tanh(a):
    return 0.5 * a * (1.0 + jnp.tanh(math.sqrt(2.0 / math.pi) * (a + 0.044715 * (a * a * a))))


def _ffn_kernel(x_ref, prev_ref, next_ref, mix_ref, mix_prev_ref, mix_next_ref, mod_ref, gpre_ref, gpost_ref,
                wup_ref, cw_ref, cb_ref, wdn_ref, o_ref, u_ref, act_ref):
    i = pl.program_id(1)
    n_tiles = pl.num_programs(1)
    mix_halo = mix_prev_ref.shape[1]
    mix = jnp.concatenate([mix_prev_ref[0, mix_halo - HALO:, :], mix_ref[0], mix_next_ref[0, :HALO, :]], axis=0)
    xx = (jnp.concatenate([prev_ref[0], x_ref[0], next_ref[0]], axis=0)
          + mod_ref[0, 2:3, :] * mix.astype(F32))
    x = xx[HALO:HALO + TOKEN_TILE]
    h = _norm_modulate(xx, gpre_ref[...], mod_ref[0, 3:4, :], mod_ref[0, 4:5, :])
    row = lax.broadcasted_iota(jnp.int32, (xx.shape[0], 1), 0)
    inside = ((row >= HALO) | (i > 0)) & ((row < HALO + TOKEN_TILE) | (i < n_tiles - 1))
    h = jnp.where(inside, h, 0.0).astype(BF16)
    starts = list(range(0, D_FF, FF_CHUNK))
    widths = [min(FF_CHUNK, D_FF - lo) for lo in starts]

    def conv(buf, half, s, lo, r0):
        lanes = slice(half * D_FF + lo, half * D_FF + lo + LANES)
        tap = lambda j: u_ref[buf, half, s, pl.ds(HALO - 1 + j + r0, FFN_ROW_BLOCK, stride=1), :]
        return (tap(0) * cw_ref[0:1, lanes] + tap(1) * cw_ref[1:2, lanes] + tap(2) * cw_ref[2:3, lanes]
                + cb_ref[:, lanes])

    def up_project(c):
        for half in range(2):
            lo = half * D_FF + starts[c]
            u = jnp.dot(h, wup_ref[:, lo:lo + widths[c]], preferred_element_type=F32)
            for s in range(widths[c] // LANES):
                u_ref[c % 2, half, s] = u[:, s * LANES:(s + 1) * LANES]

    up_project(0)
    for c in range(len(starts)):
        buf = c % 2
        if c + 1 < len(starts):
            up_project(c + 1)
        for s in range(widths[c] // LANES):
            lo = starts[c] + s * LANES
            for r0 in range(0, TOKEN_TILE, FFN_ROW_BLOCK):
                act_ref[r0:r0 + FFN_ROW_BLOCK, lo:lo + LANES] = (
                    _gelu_tanh(conv(buf, 0, s, lo, r0)) * conv(buf, 1, s, lo, r0)).astype(BF16)
    y = jnp.dot(act_ref[...], wdn_ref[...], preferred_element_type=F32)
    ms = jnp.mean(y * y, axis=-1, keepdims=True)
    y = y * lax.rsqrt(ms + RMS_EPS) * gpost_ref[...]
    o_ref[0] = x + mod_ref[0, 5:6, :] * y


def _const_spec(shape):
    nd = len(shape)
    return pl.BlockSpec(shape, lambda *_: (0,) * nd, pipeline_mode=pl.Buffered(1))


def kernel(x, c, ctx, c_ctx, w_mod, b_mod, norm_mix_pre, norm_mix_post, norm_ffn_pre, norm_ffn_post, w_in, b_in, na_rpb, hy_conv_w, hy_conv_b, hy_filt_w1, hy_filt_b1, hy_filt_w2, hy_filt_b2, hy_filt_w3, hy_filt_b3, hy_sin_freq, hy_decay, hy_skip, w_o_na, w_o_hy, w_out, ffn_w_up, ffn_conv_w, ffn_conv_b, ffn_w_down):
    batch, seq, d = x.shape
    n_ctx = ctx.shape[1]
    assert d == D_MODEL and 2 * seq == FFT_N and seq == GRID_W * GRID_W and batch % 2 == 0
    assert w_mod.shape[0] == 1, "single-layer block"
    n_tiles = seq // TOKEN_TILE
    d_in = w_in.shape[2]
    row2 = lambda a: a.reshape(1, -1)

    c_all = jnp.zeros((8, d), F32).at[:batch].set(c).at[batch].set(c_ctx)
    mod_n = 1024
    mod = pl.pallas_call(
        _mod_kernel,
        grid=(N_MOD * d // mod_n,),
        in_specs=[_const_spec((8, d)),
                  pl.BlockSpec((d, mod_n), lambda j: (0, j)),
                  pl.BlockSpec((1, mod_n), lambda j: (0, j))],
        out_specs=pl.BlockSpec((8, mod_n), lambda j: (0, j)),
        out_shape=jax.ShapeDtypeStruct((8, N_MOD * d), F32),
        compiler_params=_cparams(("arbitrary",)),
        name="mod",
    )(c_all, w_mod[0], row2(b_mod[0]))
    mod_lat = jnp.pad(mod[:batch].reshape(batch, N_MOD, d), ((0, 0), (0, 8 - N_MOD), (0, 0)))
    mod_ctx = jnp.pad(mod[batch].reshape(N_MOD, d), ((0, 8 - N_MOD), (0, 0)))

    w_in_b = w_in[0].astype(BF16)
    b_in_r = row2(b_in[0])
    g_mix_pre = row2(norm_mix_pre[0])

    k_ctx, v_ctx = pl.pallas_call(
        _ctx_kv_kernel,
        grid=(batch,),
        in_specs=[pl.BlockSpec((1, n_ctx, d), lambda b: (b, 0, 0)),
                  _const_spec((8, d)), _const_spec((1, d)),
                  _const_spec((d, 2 * D_ATTN)), _const_spec((1, 2 * D_ATTN))],
        out_specs=[pl.BlockSpec((1, n_ctx, D_ATTN), lambda b: (b, 0, 0)),
                   pl.BlockSpec((1, D_ATTN, n_ctx), lambda b: (b, 0, 0))],
        out_shape=[jax.ShapeDtypeStruct((batch, n_ctx, D_ATTN), BF16),
                   jax.ShapeDtypeStruct((batch, D_ATTN, n_ctx), BF16)],
        compiler_params=_cparams(("arbitrary",)),
        name="ctx_kv",
    )(ctx, mod_ctx, g_mix_pre, w_in_b[:, D_ATTN:3 * D_ATTN], b_in_r[:, D_ATTN:3 * D_ATTN])

    cos_t, sin_t = _rope_tables(seq)
    tok = lambda w: pl.BlockSpec((1, TOKEN_TILE, w), lambda b, i: (b, i, 0))
    mod_spec = pl.BlockSpec((1, 8, d), lambda b, i: (b, 0, 0))
    rope_spec = pl.BlockSpec((TOKEN_TILE, LANES), lambda b, i: (i, 0))
    halo_blocks = TOKEN_TILE // HALO
    n_halo_blocks = seq // HALO
    prev_spec = lambda w: pl.BlockSpec((1, HALO, w), lambda b, i: (b, jnp.maximum(i * halo_blocks - 1, 0), 0))
    next_spec = lambda w: pl.BlockSpec(
        (1, HALO, w), lambda b, i: (b, jnp.minimum((i + 1) * halo_blocks, n_halo_blocks - 1), 0))
    pitched_spec = pl.BlockSpec((1, TILE_GROUPS * SEQ_PITCH, D_HYENA), lambda b, i: (b, i, 0))
    pitched_shape = jax.ShapeDtypeStruct((batch, PITCHED_ROWS, D_HYENA), F32)
    q_rot, q_plain, k_rot, v_lat, gates, u_p, x0 = pl.pallas_call(
        _in_proj_kernel,
        grid=(batch, n_tiles),
        in_specs=[tok(d), prev_spec(d), next_spec(d), mod_spec, _const_spec((1, d)), _const_spec((d, d_in)),
                  _const_spec((1, d_in)), rope_spec, rope_spec,
                  _const_spec((3, 3 * D_HYENA)), _const_spec((1, 3 * D_HYENA))],
        out_specs=[tok(D_ATTN)] * 3
        + [pl.BlockSpec((1, TOKEN_TILE // ATTN_KEY_CHUNK, D_ATTN, ATTN_KEY_CHUNK), lambda b, i: (b, i, 0, 0)),
           tok(2 * d), pitched_spec, tok(D_HYENA)],
        out_shape=[jax.ShapeDtypeStruct((batch, seq, D_ATTN), BF16)] * 3
        + [jax.ShapeDtypeStruct((batch, seq // ATTN_KEY_CHUNK, D_ATTN, ATTN_KEY_CHUNK), BF16),
           jax.ShapeDtypeStruct((batch, seq, 2 * d), BF16), pitched_shape,
           jax.ShapeDtypeStruct((batch, seq, D_HYENA), BF16)],
        scratch_shapes=[pltpu.VMEM((3 * D_HYENA // LANES, TOKEN_TILE + 2 * HALO, LANES), F32)],
        compiler_params=_cparams(("arbitrary", "arbitrary")),
        name="in_proj",
    )(x, x, x, mod_lat, g_mix_pre, w_in_b, b_in_r, jnp.asarray(cos_t), jnp.asarray(sin_t),
      hy_conv_w[0], row2(hy_conv_b[0]))

    bias = pl.pallas_call(
        _attn_bias_kernel,
        grid=(N_HEADS,),
        in_specs=[pl.BlockSpec(memory_space=pltpu.SMEM)],
        out_specs=pl.BlockSpec((3, 1, ATTN_KEYS, ATTN_Q), lambda h: (0, h, 0, 0)),
        out_shape=jax.ShapeDtypeStruct((3, N_HEADS, ATTN_KEYS, ATTN_Q), F32),
        scratch_shapes=[pltpu.VMEM((N_BIAS_ROWS, GRID_W, LANES), F32)],
        compiler_params=_cparams(("arbitrary",)),
        name="attn_bias",
    )(na_rpb[0].reshape(-1))
    n_groups = GRID_W // ATTN_ROWS_PER_STEP
    q_spec = pl.BlockSpec((1, ATTN_Q, D_ATTN), lambda b, g: (b, g, 0))
    full = lambda n: pl.BlockSpec((1, n, D_ATTN), lambda b, g: (b, 0, 0))
    bias_spec = pl.BlockSpec(
        (1, N_HEADS, ATTN_KEYS, ATTN_Q),
        lambda b, g: ((g > 0).astype(jnp.int32) + (g == n_groups - 1).astype(jnp.int32), 0, 0, 0))
    vt_spec = pl.BlockSpec((1, seq // ATTN_KEY_CHUNK, D_ATTN, ATTN_KEY_CHUNK), lambda b, g: (b, 0, 0, 0))
    vct_spec = pl.BlockSpec((1, D_ATTN, n_ctx), lambda b, g: (b, 0, 0))
    y_na = pl.pallas_call(
        _attn_kernel,
        grid=(batch, n_groups),
        in_specs=[q_spec, q_spec, full(seq), vt_spec, full(n_ctx), vct_spec, bias_spec],
        out_specs=q_spec,
        out_shape=jax.ShapeDtypeStruct((batch, seq, D_ATTN), BF16),
        compiler_params=_cparams(("arbitrary", "arbitrary")),
        name="attn",
    )(q_rot, q_plain, k_rot, v_lat, k_ctx, v_ctx, bias)

    z_t, aux_t = _filter_tables(seq)
    filt_n2 = FILT_ROWS // FFT_HALF_N1
    hid = FILTER_HIDDEN
    w1_pad = jnp.pad(hy_filt_w1[0], ((0, hid - POS_FEATS), (0, 0)))
    block_diag = lambda w: jnp.zeros((2 * hid, 2 * hid), F32).at[:hid, :hid].set(w).at[hid:, hid:].set(w)
    twice = lambda v: jnp.tile(v, (1, 2))
    w3 = hy_filt_w3[0]
    w3_parts = jnp.stack([jnp.concatenate([w3, jnp.zeros_like(w3)], axis=0),
                          jnp.concatenate([jnp.zeros_like(w3), w3], axis=0)])
    k_circ, k_asum = pl.pallas_call(
        _filt_kernel,
        grid=(2, seq // FILT_ROWS),
        in_specs=[pl.BlockSpec((1, FILT_ROWS // 2, 2 * hid), lambda hf, i: (hf, i, 0)),
                  pl.BlockSpec((1, FILT_ROWS, aux_t.shape[2]), lambda hf, i: (hf, i, 0)),
                  _const_spec((2 * hid, 2 * hid)), _const_spec((1, 2 * hid)),
                  _const_spec((2 * hid, 2 * hid)), _const_spec((1, 2 * hid)),
                  pl.BlockSpec((2, 2 * hid, D_HYENA), lambda hf, i: (0, 0, hf)),
                  pl.BlockSpec((1, D_HYENA), lambda hf, i: (0, hf)),
                  _const_spec((2, 2 * hid)),
                  pl.BlockSpec((1, 1, D_HYENA), lambda hf, i: (hf, 0, 0))],
        out_specs=[pl.BlockSpec((filt_n2, FFT_HALF_N1, D_HYENA), lambda hf, i: (i, hf, 0)),
                   pl.BlockSpec((1, 8, D_HYENA), lambda hf, i: (hf, 0, 0))],
        out_shape=[jax.ShapeDtypeStruct((FFT_N2, FFT_N1, D_HYENA), F32),
                   jax.ShapeDtypeStruct((2, 8, D_HYENA), F32)],
        compiler_params=_cparams(("arbitrary", "arbitrary")),
        name="filt",
    )(jnp.asarray(z_t), jnp.asarray(aux_t), block_diag(w1_pad), twice(row2(hy_filt_b1[0])),
      block_diag(hy_filt_w2[0]), twice(row2(hy_filt_b2[0])), w3_parts, row2(hy_filt_b3[0]),
      twice(hy_sin_freq[0]), hy_decay[0].reshape(2, 1, D_HYENA))
    k_circ = k_circ.reshape(FFT_N, D_HYENA)

    mats = _fft_matrices()
    mats = {k: jnp.asarray(v).astype(BF16) for k, v in mats.items()}
    w2_all = mats["w2"]
    v2_all = mats["v2"]
    step_mat_spec = pl.BlockSpec((FFT_K1_PER_STEP, 2 * FFT_N2, 2 * FFT_N2), lambda cb, s: (s, 0, 0))
    filt_slabs = 2
    kf = pl.pallas_call(
        _filt_fft_kernel,
        grid=(D_HYENA // (filt_slabs * LANES), FFT_STEPS),
        in_specs=[pl.BlockSpec((FFT_N, filt_slabs * LANES), lambda cb, s: (0, cb)),
                  pl.BlockSpec((2, 8, filt_slabs * LANES), lambda cb, s: (0, 0, cb)),
                  _const_spec((FFT_PITCH, FFT_N1)), step_mat_spec],
        out_specs=pl.BlockSpec((FFT_K1_PER_STEP, 2 * FFT_N2, filt_slabs * LANES), lambda cb, s: (s, 0, cb)),
        out_shape=jax.ShapeDtypeStruct((FFT_N1, 2 * FFT_N2, D_HYENA), F32),
        scratch_shapes=[pltpu.VMEM((filt_slabs, FFT_N2 * FFT_PITCH, LANES), F32)],
        compiler_params=_cparams(("arbitrary", "arbitrary")),
        name="filt_fft",
    )(k_circ, k_asum, mats["w1_real"], w2_all)

    n_pairs = batch // 2
    pair_block = ((batch, PITCHED_ROWS, LANES), lambda cb, s: (0, 0, cb))
    pair_spec = pl.BlockSpec(*pair_block, pipeline_mode=pl.Buffered(1))
    y_conv = pl.pallas_call(
        _hy_conv_kernel,
        grid=(D_HYENA // LANES, FFT_STEPS),
        in_specs=[pl.BlockSpec(*pair_block),
                  pl.BlockSpec((FFT_K1_PER_STEP, 2 * FFT_N2, LANES), lambda cb, s: (s, 0, cb)),
                  _const_spec((FFT_PITCH, FFT_N1)), step_mat_spec, step_mat_spec,
                  _const_spec((FFT_N1, 2 * FFT_N1))],
        out_specs=pair_spec,
        out_shape=pitched_shape,
        scratch_shapes=[pltpu.VMEM((n_pairs, FFT_N2 * FFT_PITCH, LANES), F32)],
        compiler_params=_cparams(("arbitrary", "arbitrary")),
        name="hy_conv",
    )(u_p, kf, mats["w1_data"], w2_all, v2_all, mats["v1"])

    mix = pl.pallas_call(
        _merge_kernel,
        grid=(batch, n_tiles),
        in_specs=[tok(D_ATTN), tok(D_HYENA), pitched_spec, pitched_spec, tok(2 * d),
                  _const_spec((1, D_HYENA)), _const_spec((1, d)),
                  _const_spec((D_ATTN, d)), _const_spec((D_HYENA, d)), _const_spec((d, d))],
        out_specs=tok(d),
        out_shape=jax.ShapeDtypeStruct((batch, seq, d), BF16),
        compiler_params=_cparams(("arbitrary", "arbitrary")),
        name="merge",
    )(y_na, x0, u_p, y_conv, gates, row2(hy_skip[0]), row2(norm_mix_post[0]),
      w_o_na[0].astype(BF16), w_o_hy[0].astype(BF16), w_out[0].astype(BF16))

    w_up_c = ffn_w_up[0].astype(BF16)
    conv_w_c = ffn_conv_w[0]
    conv_b_c = row2(ffn_conv_b[0])
    w_dn_c = ffn_w_down[0].astype(BF16)
    mix_halo = 2 * HALO
    mix_blocks = TOKEN_TILE // mix_halo
    mix_prev = pl.BlockSpec((1, mix_halo, d), lambda b, i: (b, jnp.maximum(i * mix_blocks - 1, 0), 0))
    mix_next = pl.BlockSpec((1, mix_halo, d),
                            lambda b, i: (b, jnp.minimum((i + 1) * mix_blocks, seq // mix_halo - 1), 0))
    out = pl.pallas_call(
        _ffn_kernel,
        grid=(batch, n_tiles),
        in_specs=[tok(d), prev_spec(d), next_spec(d), tok(d), mix_prev, mix_next,
                  mod_spec, _const_spec((1, d)), _const_spec((1, d)),
                  _const_spec(w_up_c.shape), _const_spec(conv_w_c.shape), _const_spec(conv_b_c.shape),
                  _const_spec(w_dn_c.shape)],
        out_specs=tok(d),
        out_shape=jax.ShapeDtypeStruct((batch, seq, d), F32),
        scratch_shapes=[pltpu.VMEM((2, 2, FF_CHUNK // LANES, TOKEN_TILE + 2 * HALO, LANES), F32),
                        pltpu.VMEM((TOKEN_TILE, D_FF), BF16)],
        compiler_params=_cparams(("arbitrary", "arbitrary")),
        name="ffn",
    )(x, x, x, mix, mix, mix, mod_lat, row2(norm_ffn_pre[0]), row2(norm_ffn_post[0]),
      w_up_c, conv_w_c, conv_b_c, w_dn_c)
    return out
```

```python
import functools
import math

import jax
import jax.numpy as jnp
import numpy as np
from jax import lax
from jax.experimental import pallas as pl
from jax.experimental.pallas import tpu as pltpu

F32 = jnp.float32
BF16 = jnp.bfloat16

D_MODEL = 1024
N_HEADS = 8
HEAD_DIM = 64
D_ATTN = N_HEADS * HEAD_DIM
D_HYENA = 512
GRID_W = 64
WIN_ROWS = 8
WIN_COLS = 16
POS_BANDS = 16
POS_FEATS = 1 + 2 * POS_BANDS
FILTER_HIDDEN = 64
D_FF = 2816
N_MOD = 6
ROPE_BASE = 10000.0
RMS_EPS = 1e-6
NEG_BIAS = -1e30
LOG2_E = math.log2(math.e)

LANES = 128
VMEM_LIMIT_BYTES = 56 * 1024 * 1024

FFT_N1 = 64
FFT_N2 = 128
FFT_N = FFT_N1 * FFT_N2
FFT_HALF_N1 = FFT_N1 // 2
FFT_PITCH = 2 * FFT_N1 + 8
FFT_K1_PER_STEP = 8
FFT_UNROLL = 8
FFT_STEPS = FFT_N1 // FFT_K1_PER_STEP

TOKEN_TILE = 512
FILT_ROWS = 1024
SEQ_PITCH = FFT_N2 + 8
PITCHED_ROWS = FFT_HALF_N1 * SEQ_PITCH
TILE_GROUPS = TOKEN_TILE // FFT_N2
ATTN_ROWS_PER_STEP = 4
ATTN_Q = ATTN_ROWS_PER_STEP * GRID_W
ATTN_KEY_ROWS = 12
ATTN_KEYS = ATTN_KEY_ROWS * GRID_W
ATTN_KEY_CHUNK = 256
FF_CHUNK = 768
LATER_WEIGHT_SLABS = (32, 16, 32, 32, 32)
HALO = 8
FFN_ROW_BLOCK = 64


def _cparams(sem):
    return pltpu.CompilerParams(dimension_semantics=sem, vmem_limit_bytes=VMEM_LIMIT_BYTES)


@functools.lru_cache(maxsize=None)
def _rope_tables(seq):
    pos = np.arange(seq)
    row = pos // GRID_W
    col = pos % GRID_W
    n_pairs = HEAD_DIM // 4
    inv = ROPE_BASE ** (-np.arange(n_pairs, dtype=np.float64) / n_pairs)
    lane = np.arange(LANES) % HEAD_DIM
    p = np.where(lane[None, :] < HEAD_DIM // 2, row[:, None], col[:, None]).astype(np.float64)
    ang = p * inv[lane % n_pairs][None, :]
    sign = np.where((lane % (2 * n_pairs)) < n_pairs, -1.0, 1.0)
    return np.cos(ang).astype(np.float32), (np.sin(ang) * sign[None, :]).astype(np.float32)


@functools.lru_cache(maxsize=None)
def _filter_tables(seq):
    assert 2 * seq == FFT_N
    half, n2, n1 = np.meshgrid(np.arange(2), np.arange(FFT_N2), np.arange(FFT_HALF_N1), indexing="ij")
    n = FFT_N2 * (half * FFT_HALF_N1 + n1) + n2
    fwd = n < seq
    m = n - seq
    valid = fwd | (m >= 1)
    pos = np.where(valid, np.where(fwd, n, seq - m), 0).astype(np.float64).reshape(2, seq)
    t = pos / max(seq - 1, 1)
    bands = np.linspace(1e-4, POS_BANDS - 1, POS_BANDS)
    ang = (2.0 * math.pi / seq) * pos[..., None] * bands
    z = np.zeros((2, seq, 64), np.float64)
    z[..., 0] = t
    z[..., 1:1 + POS_BANDS] = np.cos(ang)
    z[..., 1 + POS_BANDS:POS_FEATS] = -np.sin(ang)
    aux = np.zeros((2, seq, 8), np.float64)
    aux[..., 0] = t
    aux[..., 1] = valid.reshape(2, seq)
    zp = z.reshape(2, seq // FILT_ROWS, 2, FILT_ROWS // 2, 64).transpose(0, 1, 3, 2, 4).reshape(2, seq // 2, 128)
    return zp.astype(np.float32), aux.astype(np.float32)


def _realify(m):
    return np.block([[m.real, -m.imag], [m.imag, m.real]])


@functools.lru_cache(maxsize=None)
def _fft_matrices():
    n1 = np.arange(FFT_N1)
    n2 = np.arange(FFT_N2)
    k1 = np.arange(FFT_N1)
    k2 = np.arange(FFT_N2)
    f1 = np.exp(-2j * np.pi * np.outer(k1, n1) / FFT_N1)
    w1_data = _realify(f1[:, :FFT_HALF_N1])
    w1_real = np.concatenate([f1.real, f1.imag], axis=0)
    v1 = _realify(np.conj(f1.T)[:FFT_HALF_N1, :] / FFT_N)
    f2 = np.exp(-2j * np.pi * np.outer(k2, n2) / FFT_N2)
    tw = np.exp(-2j * np.pi * np.outer(k1, n2) / FFT_N)
    w2 = np.stack([_realify(f2 * tw[a][None, :]) for a in range(FFT_N1)])
    v2 = np.stack([_realify(np.conj(f2.T) * np.conj(tw[a])[:, None]) for a in range(FFT_N1)])
    pad = np.zeros((FFT_PITCH - 2 * FFT_N1, FFT_N1))
    w1_data = np.concatenate([w1_data, pad], axis=0)
    w1_real = np.concatenate([w1_real, pad], axis=0)
    return {k: v.astype(np.float32) for k, v in
            dict(w1_data=w1_data, w1_real=w1_real, v1=v1, w2=w2, v2=v2).items()}


N_BIAS_ROWS = 2 * WIN_ROWS - 1
N_BIAS_COLS = 2 * WIN_COLS - 1


def _attn_bias_row_index():
    rows = GRID_W
    groups = (0, 2, rows // ATTN_ROWS_PER_STEP - 1)
    dr = np.full((3, ATTN_ROWS_PER_STEP, ATTN_KEY_ROWS), -1, np.int32)
    for v, g in enumerate(groups):
        ws = min(max(ATTN_ROWS_PER_STEP * g - WIN_ROWS // 2, 0), rows - ATTN_KEY_ROWS)
        for i in range(ATTN_ROWS_PER_STEP):
            r = ATTN_ROWS_PER_STEP * g + i
            r_start = min(max(r - WIN_ROWS // 2, 0), rows - WIN_ROWS)
            for j in range(ATTN_KEY_ROWS):
                kr = ws + j
                if r_start <= kr < r_start + WIN_ROWS:
                    dr[v, i, j] = kr - r + (WIN_ROWS - 1)
    return dr


def _attn_bias_kernel(rpb_ref, w_ref, o_ref, wb_ref, t_ref):
    wb_ref[...] = w_ref[...].astype(BF16)
    head = pl.program_id(0)
    kc = lax.broadcasted_iota(jnp.int32, (GRID_W, LANES), 0)
    lane = lax.broadcasted_iota(jnp.int32, (GRID_W, LANES), 1)
    qc = lane % GRID_W
    c_start = jnp.clip(qc - WIN_COLS // 2, 0, GRID_W - WIN_COLS)
    col_in = (kc >= c_start) & (kc < c_start + WIN_COLS)
    dc = jnp.clip(kc - qc, 1 - WIN_COLS, WIN_COLS - 1) + (WIN_COLS - 1)
    base = head * (N_BIAS_ROWS * N_BIAS_COLS)
    for r in range(N_BIAS_ROWS):
        t = jnp.full((GRID_W, LANES), NEG_BIAS, F32)
        for cidx in range(N_BIAS_COLS):
            t = jnp.where(col_in & (dc == cidx), rpb_ref[base + r * N_BIAS_COLS + cidx] * LOG2_E, t)
        t_ref[r] = t
    dr = _attn_bias_row_index()
    low_half = lane < GRID_W
    masked = jnp.full((GRID_W, LANES), NEG_BIAS, F32)
    for v in range(3):
        for i in range(0, ATTN_ROWS_PER_STEP, 2):
            for j in range(ATTN_KEY_ROWS):
                lo = t_ref[int(dr[v, i, j])] if dr[v, i, j] >= 0 else masked
                hi = t_ref[int(dr[v, i + 1, j])] if dr[v, i + 1, j] >= 0 else masked
                o_ref[v, 0, j * GRID_W:(j + 1) * GRID_W, i * GRID_W:(i + 2) * GRID_W] = jnp.where(
                    low_half, lo, hi)


def _mod_kernel(c_ref, w_ref, b_ref, o_ref):
    c = c_ref[...]
    s = c * jax.nn.sigmoid(c)
    o_ref[...] = jnp.dot(s, w_ref[...], precision=lax.Precision.HIGHEST,
                         preferred_element_type=F32) + b_ref[...]


def _norm_modulate(x, gain, shift, scale):
    ms = jnp.mean(x * x, axis=-1, keepdims=True)
    y = x * lax.rsqrt(ms + RMS_EPS) * gain
    return y * (1.0 + scale) + shift


def _rope(t, cos, sin_signed):
    n_pairs = HEAD_DIM // 4
    lane = lax.broadcasted_iota(jnp.int32, t.shape, 1)
    first = (lane % (2 * n_pairs)) < n_pairs
    partner = jnp.where(first, pltpu.roll(t, LANES - n_pairs, 1), pltpu.roll(t, n_pairs, 1))
    return t * cos + partner * sin_signed


def _in_proj_kernel(x_ref, prev_ref, next_ref, mod_ref, g_ref, w_ref, b_ref, cos_ref, sin_ref, cw_ref, cb_ref,
                    *rest):
    n_cast = len(LATER_WEIGHT_SLABS)
    cast_in, rest = rest[:n_cast], rest[n_cast:]
    (qr_ref, qp_ref, kr_ref, v_ref, gt_ref, u_ref, x0_ref), rest = rest[:7], rest[7:]
    cast_out, (hy_ref,) = rest[:n_cast], rest[n_cast:]
    for src, dst in zip(cast_in, cast_out):
        dst[...] = src[...].astype(BF16)
    i = pl.program_id(1)
    n_tiles = pl.num_programs(1)
    xx = jnp.concatenate([prev_ref[0], x_ref[0], next_ref[0]], axis=0)
    h_ext = _norm_modulate(xx, g_ref[...], mod_ref[0, 0:1, :], mod_ref[0, 1:2, :]).astype(BF16)
    h = h_ext[HALO:HALO + TOKEN_TILE]
    cos = cos_ref[...]
    sin = sin_ref[...]

    def proj(lo, hi, rows=h):
        return jnp.dot(rows, w_ref[:, lo:hi], preferred_element_type=F32) + b_ref[:, lo:hi]

    q = proj(0, D_ATTN) * (HEAD_DIM ** -0.5 * LOG2_E)
    qp_ref[0] = q.astype(BF16)
    k = proj(D_ATTN, 2 * D_ATTN)
    for c in range(D_ATTN // LANES):
        lanes = slice(c * LANES, (c + 1) * LANES)
        qr_ref[0, :, lanes] = _rope(q[:, lanes], cos, sin).astype(BF16)
        kr_ref[0, :, lanes] = _rope(k[:, lanes], cos, sin).astype(BF16)
    v_t = proj(2 * D_ATTN, 3 * D_ATTN).T.astype(BF16)
    for c in range(TOKEN_TILE // ATTN_KEY_CHUNK):
        v_ref[0, c] = v_t[:, c * ATTN_KEY_CHUNK:(c + 1) * ATTN_KEY_CHUNK]
    hy_lo = 3 * D_ATTN
    gl_lo = hy_lo + 3 * D_HYENA
    for c in range(4):
        w = D_MODEL // 2
        gt_ref[0, :, c * w:(c + 1) * w] = jax.nn.sigmoid(
            proj(gl_lo + c * w, gl_lo + (c + 1) * w)).astype(BF16)

    slabs_per_part = D_HYENA // LANES
    for c in range(3):
        hy = proj(hy_lo + c * D_HYENA, hy_lo + (c + 1) * D_HYENA, h_ext)
        for s in range(slabs_per_part):
            hy_ref[c * slabs_per_part + s] = hy[:, s * LANES:(s + 1) * LANES]
    zero_row = jnp.zeros((1, LANES), F32)

    @pl.when(i == 0)
    def _():
        for s in range(3 * slabs_per_part):
            hy_ref[s, HALO - 1:HALO, :] = zero_row

    @pl.when(i == n_tiles - 1)
    def _():
        for s in range(3 * slabs_per_part):
            hy_ref[s, HALO + TOKEN_TILE:HALO + TOKEN_TILE + 1, :] = zero_row

    def conv(s, r0, rows):
        lanes = slice(s * LANES, (s + 1) * LANES)
        tap = lambda j: hy_ref[s, pl.ds(HALO - 1 + j + r0, rows, stride=1), :]
        return (tap(0) * cw_ref[0:1, lanes] + tap(1) * cw_ref[1:2, lanes] + tap(2) * cw_ref[2:3, lanes]
                + cb_ref[:, lanes])

    for s in range(slabs_per_part):
        lanes = slice(s * LANES, (s + 1) * LANES)
        for j in range(TILE_GROUPS):
            r0 = j * FFT_N2
            x0_ref[0, r0:r0 + FFT_N2, lanes] = conv(s, r0, FFT_N2).astype(BF16)
            u_ref[0, j * SEQ_PITCH:j * SEQ_PITCH + FFT_N2, lanes] = (
                conv(slabs_per_part + s, r0, FFT_N2) * conv(2 * slabs_per_part + s, r0, FFT_N2))
            u_ref[0, j * SEQ_PITCH + FFT_N2:(j + 1) * SEQ_PITCH, lanes] = jnp.zeros((SEQ_PITCH - FFT_N2, LANES), F32)


def _ctx_kv_kernel(x_ref, mod_ref, g_ref, w_ref, b_ref, k_ref, v_ref):
    h = _norm_modulate(x_ref[0], g_ref[...], mod_ref[0:1, :], mod_ref[1:2, :]).astype(BF16)
    kv = jnp.dot(h, w_ref[...], preferred_element_type=F32) + b_ref[...]
    k_ref[0] = kv[:, :D_ATTN].astype(BF16)
    v_ref[0] = kv[:, D_ATTN:].T.astype(BF16)


def _attn_window_start(g):
    return jnp.clip(ATTN_ROWS_PER_STEP * g - WIN_ROWS // 2, 0, GRID_W - ATTN_KEY_ROWS)


def _attn_kernel(qr_ref, qp_ref, k_ref, vt_ref, kc_ref, vct_ref, bias_ref, o_ref):
    g = pl.program_id(1)
    win = _attn_window_start(g)
    key0 = pl.multiple_of(win * GRID_W, ATTN_KEY_CHUNK)
    chunk0 = win // (ATTN_KEY_CHUNK // GRID_W)
    nt = (((1,), (1,)), ((), ()))
    quad_w = 4 * HEAD_DIM
    lane = lax.broadcasted_iota(jnp.int32, (1, quad_w), 1)
    zero = jnp.zeros((), BF16)
    def scores(head):
        quad, hh = divmod(head, 4)
        ql = slice(quad * quad_w, (quad + 1) * quad_w)
        mine = (lane >= hh * HEAD_DIM) & (lane < (hh + 1) * HEAD_DIM)
        s_nb = lax.dot_general(k_ref[0, pl.ds(key0, ATTN_KEYS), ql], jnp.where(mine, qr_ref[0, :, ql], zero), nt,
                               preferred_element_type=F32)
        s_cx = lax.dot_general(kc_ref[0, :, ql], jnp.where(mine, qp_ref[0, :, ql], zero), nt,
                               preferred_element_type=F32)
        return s_nb + bias_ref[0, head], s_cx

    def with_ones(v):
        return jnp.concatenate([v, jnp.ones((16, v.shape[1]), BF16)], axis=0)

    def probs(s_nb, s_cx):
        m = jnp.maximum(jnp.max(s_nb, axis=0, keepdims=True), jnp.max(s_cx, axis=0, keepdims=True))
        return jnp.exp2(s_nb - m).astype(BF16), jnp.exp2(s_cx - m).astype(BF16)

    def values(head, p_nb, p_cx):
        rows = slice(head * HEAD_DIM, (head + 1) * HEAD_DIM)
        v_win = jnp.concatenate([vt_ref[0, chunk0 + c, rows, :] for c in range(ATTN_KEYS // ATTN_KEY_CHUNK)],
                                axis=1)
        o = (jnp.dot(with_ones(v_win), p_nb, preferred_element_type=F32)
             + jnp.dot(with_ones(vct_ref[0, rows, :]), p_cx, preferred_element_type=F32))
        return o[:HEAD_DIM] / o[HEAD_DIM:HEAD_DIM + 1]

    outs = []
    s_q = {0: scores(0), 1: scores(1)}
    p_q = {0: probs(*s_q.pop(0))}
    for head in range(N_HEADS):
        if head + 2 < N_HEADS:
            s_q[head + 2] = scores(head + 2)
        if head + 1 < N_HEADS:
            p_q[head + 1] = probs(*s_q.pop(head + 1))
        outs.append(values(head, *p_q.pop(head)))
    o_ref[0] = jnp.concatenate(outs, axis=0).T.astype(BF16)


def _filt_kernel(z_ref, aux_ref, w1_ref, b1_ref, w2_ref, b2_ref, w3_ref, b3_ref, freq_ref, decay_ref,
                 o_ref, asum_ref):
    i = pl.program_id(1)
    hp = lax.Precision.HIGHEST
    h = jnp.sin(freq_ref[0:1, :] * (jnp.dot(z_ref[0], w1_ref[...], precision=hp,
                                            preferred_element_type=F32) + b1_ref[...]))
    h = jnp.sin(freq_ref[1:2, :] * (jnp.dot(h, w2_ref[...], precision=hp,
                                            preferred_element_type=F32) + b2_ref[...]))
    taps = jnp.concatenate([jnp.dot(h, w3_ref[part], precision=hp, preferred_element_type=F32)
                            for part in range(2)], axis=0) + b3_ref[...]
    t = aux_ref[0, :, 0:1]
    valid = aux_ref[0, :, 1:2] > 0.5
    k = jnp.where(valid, taps * jnp.exp(-t * jnp.abs(decay_ref[0])), 0.0)
    part = jnp.sum(jnp.abs(k), axis=0, keepdims=True)

    @pl.when(i == 0)
    def _():
        asum_ref[...] = jnp.zeros_like(asum_ref)

    asum_ref[0] += jnp.broadcast_to(part, asum_ref.shape[1:])
    o_ref[...] = k.reshape(o_ref.shape)


def _fft_stage1(load_group, w1_ref, a_ref):
    n_slabs = a_ref.shape[0]

    def body(n2, carry):
        x = load_group(n2)
        a = jnp.dot(w1_ref[...], x, preferred_element_type=F32)
        base = pl.multiple_of(n2 * FFT_PITCH, 8)
        for s in range(n_slabs):
            a_ref[s, pl.ds(base, FFT_PITCH), :] = a[:, s * LANES:(s + 1) * LANES]
        return carry

    lax.fori_loop(0, FFT_N2, body, 0, unroll=FFT_UNROLL)


def _fft_load_k1(a_ref, k1):
    parts = []
    for off in (0, FFT_N1):
        parts.append(jnp.concatenate(
            [a_ref[s, pl.ds(k1 + off, FFT_N2, stride=FFT_PITCH), :] for s in range(a_ref.shape[0])], axis=1))
    return jnp.concatenate(parts, axis=0)


def _filt_fft_kernel(k_ref, asum_ref, w1_ref, w2_ref, o_ref, a_ref):
    step = pl.program_id(1)

    @pl.when(step == 0)
    def _():
        norm = asum_ref[0, 0:1, :] + asum_ref[1, 0:1, :]

        def load_group(n2):
            return (k_ref[pl.ds(pl.multiple_of(n2 * FFT_N1, FFT_N1), FFT_N1), :] / norm).astype(BF16)
        _fft_stage1(load_group, w1_ref, a_ref)

    for j in range(FFT_K1_PER_STEP):
        b = _fft_load_k1(a_ref, step * FFT_K1_PER_STEP + j).astype(BF16)
        o_ref[j] = jnp.dot(w2_ref[j], b, preferred_element_type=F32)


def _hy_conv_kernel(u_ref, kf_ref, w1_ref, w2_ref, v2_ref, v1_ref, o_ref, a_ref):
    step = pl.program_id(1)
    n_slabs = a_ref.shape[0]

    def tokens_of(member, s, n2):
        return (2 * s + member, pl.ds(n2, FFT_HALF_N1, stride=SEQ_PITCH), slice(None))

    @pl.when(step == 0)
    def _():
        o_ref[...] = jnp.zeros_like(o_ref)

        def load_group(n2):
            return jnp.concatenate(
                [jnp.concatenate([u_ref[tokens_of(member, s, n2)] for s in range(n_slabs)], axis=1)
                 for member in range(2)], axis=0).astype(BF16)
        _fft_stage1(load_group, w1_ref, a_ref)

    for j in range(FFT_K1_PER_STEP):
        k1 = step * FFT_K1_PER_STEP + j
        b = _fft_load_k1(a_ref, k1).astype(BF16)
        x = jnp.dot(w2_ref[j], b, preferred_element_type=F32)
        kf = kf_ref[j]
        kf = jnp.concatenate([kf] * n_slabs, axis=1)
        xr, xi = x[:FFT_N2], x[FFT_N2:]
        kr, ki = kf[:FFT_N2], kf[FFT_N2:]
        y = jnp.concatenate([xr * kr - xi * ki, xr * ki + xi * kr], axis=0).astype(BF16)
        d = jnp.dot(v2_ref[j], y, preferred_element_type=F32)
        for s in range(n_slabs):
            lanes = slice(s * LANES, (s + 1) * LANES)
            a_ref[s, pl.ds(k1, FFT_N2, stride=FFT_PITCH), :] = d[:FFT_N2, lanes]
            a_ref[s, pl.ds(k1 + FFT_N1, FFT_N2, stride=FFT_PITCH), :] = d[FFT_N2:, lanes]

    @pl.when(step == FFT_STEPS - 1)
    def _():
        def body(n2, carry):
            base = pl.multiple_of(n2 * FFT_PITCH, 8)
            d = jnp.concatenate([a_ref[s, pl.ds(base, 2 * FFT_N1), :] for s in range(n_slabs)], axis=1)
            y = jnp.dot(v1_ref[...], d.astype(BF16), preferred_element_type=F32)
            for s in range(n_slabs):
                for member in range(2):
                    o_ref[tokens_of(member, s, n2)] = y[member * FFT_HALF_N1:(member + 1) * FFT_HALF_N1,
                                                        s * LANES:(s + 1) * LANES]
            return carry

        lax.fori_loop(0, FFT_N2, body, 0, unroll=FFT_UNROLL)


def _from_pitched(ref):
    return jnp.concatenate([ref[0, j * SEQ_PITCH:j * SEQ_PITCH + FFT_N2, :] for j in range(TILE_GROUPS)], axis=0)


def _merge_kernel(yna_ref, x0_ref, u_ref, yc_ref, gt_ref, skip_ref, gpost_ref, wna_ref, why_ref, wout_ref, o_ref):
    y_hy = (x0_ref[0].astype(F32) * (_from_pitched(yc_ref) + _from_pitched(u_ref) * skip_ref[...])).astype(BF16)
    a = jnp.dot(yna_ref[0], wna_ref[...], preferred_element_type=F32)
    b = jnp.dot(y_hy, why_ref[...], preferred_element_type=F32)
    g_na = gt_ref[0, :, :D_MODEL].astype(F32)
    g_hy = gt_ref[0, :, D_MODEL:].astype(F32)
    m = (g_na * a + g_hy * b).astype(BF16)
    o = jnp.dot(m, wout_ref[...], preferred_element_type=F32)
    ms = jnp.mean(o * o, axis=-1, keepdims=True)
    o_ref[0] = (o * lax.rsqrt(ms + RMS_EPS) * gpost_ref[...]).astype(BF16)


def _gelu_tanh(a):
    return 0.5 * a * (1.0 + jnp.tanh(math.sqrt(2.0 / math.pi) * (a + 0.044715 * (a * a * a))))


def _ffn_kernel(x_ref, prev_ref, next_ref, mix_ref, mix_prev_ref, mix_next_ref, mod_ref, gpre_ref, gpost_ref,
                wup_ref, cw_ref, cb_ref, wdn_ref, o_ref, u_ref, act_ref):
    i = pl.program_id(1)
    n_tiles = pl.num_programs(1)
    mix_halo = mix_prev_ref.shape[1]
    mix = jnp.concatenate([mix_prev_ref[0, mix_halo - HALO:, :], mix_ref[0], mix_next_ref[0, :HALO, :]], axis=0)
    xx = (jnp.concatenate([prev_ref[0], x_ref[0], next_ref[0]], axis=0)
          + mod_ref[0, 2:3, :] * mix.astype(F32))
    x = xx[HALO:HALO + TOKEN_TILE]
    h = _norm_modulate(xx, gpre_ref[...], mod_ref[0, 3:4, :], mod_ref[0, 4:5, :])
    row = lax.broadcasted_iota(jnp.int32, (xx.shape[0], 1), 0)
    inside = ((row >= HALO) | (i > 0)) & ((row < HALO + TOKEN_TILE) | (i < n_tiles - 1))
    h = jnp.where(inside, h, 0.0).astype(BF16)
    starts = list(range(0, D_FF, FF_CHUNK))
    widths = [min(FF_CHUNK, D_FF - lo) for lo in starts]

    def conv(buf, half, s, lo, r0):
        lanes = slice(half * D_FF + lo, half * D_FF + lo + LANES)
        tap = lambda j: u_ref[buf, half, s, pl.ds(HALO - 1 + j + r0, FFN_ROW_BLOCK, stride=1), :]
        return (tap(0) * cw_ref[0:1, lanes] + tap(1) * cw_ref[1:2, lanes] + tap(2) * cw_ref[2:3, lanes]
                + cb_ref[:, lanes])

    def up_project(c):
        for half in range(2):
            lo = half * D_FF + starts[c]
            u = jnp.dot(h, wup_ref[:, lo:lo + widths[c]], preferred_element_type=F32)
            for s in range(widths[c] // LANES):
                u_ref[c % 2, half, s] = u[:, s * LANES:(s + 1) * LANES]

    up_project(0)
    for c in range(len(starts)):
        buf = c % 2
        if c + 1 < len(starts):
            up_project(c + 1)
        for s in range(widths[c] // LANES):
            lo = starts[c] + s * LANES
            for r0 in range(0, TOKEN_TILE, FFN_ROW_BLOCK):
                act_ref[r0:r0 + FFN_ROW_BLOCK, lo:lo + LANES] = (
                    _gelu_tanh(conv(buf, 0, s, lo, r0)) * conv(buf, 1, s, lo, r0)).astype(BF16)
    y = jnp.dot(act_ref[...], wdn_ref[...], preferred_element_type=F32)
    ms = jnp.mean(y * y, axis=-1, keepdims=True)
    y = y * lax.rsqrt(ms + RMS_EPS) * gpost_ref[...]
    o_ref[0] = x + mod_ref[0, 5:6, :] * y


def _const_spec(shape):
    nd = len(shape)
    return pl.BlockSpec(shape, lambda *_: (0,) * nd, pipeline_mode=pl.Buffered(1))


def kernel(x, c, ctx, c_ctx, w_mod, b_mod, norm_mix_pre, norm_mix_post, norm_ffn_pre, norm_ffn_post, w_in, b_in, na_rpb, hy_conv_w, hy_conv_b, hy_filt_w1, hy_filt_b1, hy_filt_w2, hy_filt_b2, hy_filt_w3, hy_filt_b3, hy_sin_freq, hy_decay, hy_skip, w_o_na, w_o_hy, w_out, ffn_w_up, ffn_conv_w, ffn_conv_b, ffn_w_down):
    batch, seq, d = x.shape
    n_ctx = ctx.shape[1]
    assert d == D_MODEL and 2 * seq == FFT_N and seq == GRID_W * GRID_W and batch % 2 == 0
    assert w_mod.shape[0] == 1, "single-layer block"
    n_tiles = seq // TOKEN_TILE
    d_in = w_in.shape[2]
    row2 = lambda a: a.reshape(1, -1)

    c_all = jnp.zeros((8, d), F32).at[:batch].set(c).at[batch].set(c_ctx)
    mod_n = 1024
    mod = pl.pallas_call(
        _mod_kernel,
        grid=(N_MOD * d // mod_n,),
        in_specs=[_const_spec((8, d)),
                  pl.BlockSpec((d, mod_n), lambda j: (0, j)),
                  pl.BlockSpec((1, mod_n), lambda j: (0, j))],
        out_specs=pl.BlockSpec((8, mod_n), lambda j: (0, j)),
        out_shape=jax.ShapeDtypeStruct((8, N_MOD * d), F32),
        compiler_params=_cparams(("arbitrary",)),
        name="mod",
    )(c_all, w_mod[0], row2(b_mod[0]))
    mod_lat = jnp.pad(mod[:batch].reshape(batch, N_MOD, d), ((0, 0), (0, 8 - N_MOD), (0, 0)))
    mod_ctx = jnp.pad(mod[batch].reshape(N_MOD, d), ((0, 8 - N_MOD), (0, 0)))

    w_rows = d // N_HEADS
    bias, w_in_b = pl.pallas_call(
        _attn_bias_kernel,
        grid=(N_HEADS,),
        in_specs=[pl.BlockSpec(memory_space=pltpu.SMEM), pl.BlockSpec((w_rows, d_in), lambda h: (h, 0))],
        out_specs=[pl.BlockSpec((3, 1, ATTN_KEYS, ATTN_Q), lambda h: (0, h, 0, 0)),
                   pl.BlockSpec((w_rows, d_in), lambda h: (h, 0))],
        out_shape=[jax.ShapeDtypeStruct((3, N_HEADS, ATTN_KEYS, ATTN_Q), F32),
                   jax.ShapeDtypeStruct((d, d_in), BF16)],
        scratch_shapes=[pltpu.VMEM((N_BIAS_ROWS, GRID_W, LANES), F32)],
        compiler_params=_cparams(("arbitrary",)),
        name="attn_bias",
    )(na_rpb[0].reshape(-1), w_in[0])
    b_in_r = row2(b_in[0])
    g_mix_pre = row2(norm_mix_pre[0])

    k_ctx, v_ctx = pl.pallas_call(
        _ctx_kv_kernel,
        grid=(batch,),
        in_specs=[pl.BlockSpec((1, n_ctx, d), lambda b: (b, 0, 0)),
                  _const_spec((8, d)), _const_spec((1, d)),
                  _const_spec((d, 2 * D_ATTN)), _const_spec((1, 2 * D_ATTN))],
        out_specs=[pl.BlockSpec((1, n_ctx, D_ATTN), lambda b: (b, 0, 0)),
                   pl.BlockSpec((1, D_ATTN, n_ctx), lambda b: (b, 0, 0))],
        out_shape=[jax.ShapeDtypeStruct((batch, n_ctx, D_ATTN), BF16),
                   jax.ShapeDtypeStruct((batch, D_ATTN, n_ctx), BF16)],
        compiler_params=_cparams(("arbitrary",)),
        name="ctx_kv",
    )(ctx, mod_ctx, g_mix_pre, w_in_b[:, D_ATTN:3 * D_ATTN], b_in_r[:, D_ATTN:3 * D_ATTN])

    cos_t, sin_t = _rope_tables(seq)
    tok = lambda w: pl.BlockSpec((1, TOKEN_TILE, w), lambda b, i: (b, i, 0))
    mod_spec = pl.BlockSpec((1, 8, d), lambda b, i: (b, 0, 0))
    rope_spec = pl.BlockSpec((TOKEN_TILE, LANES), lambda b, i: (i, 0))
    halo_blocks = TOKEN_TILE // HALO
    n_halo_blocks = seq // HALO
    prev_spec = lambda w: pl.BlockSpec((1, HALO, w), lambda b, i: (b, jnp.maximum(i * halo_blocks - 1, 0), 0))
    next_spec = lambda w: pl.BlockSpec(
        (1, HALO, w), lambda b, i: (b, jnp.minimum((i + 1) * halo_blocks, n_halo_blocks - 1), 0))
    pitched_spec = pl.BlockSpec((1, TILE_GROUPS * SEQ_PITCH, D_HYENA), lambda b, i: (b, i, 0))
    pitched_shape = jax.ShapeDtypeStruct((batch, PITCHED_ROWS, D_HYENA), F32)
    later_weights = [ffn_w_up[0], ffn_w_down[0], w_out[0], w_o_na[0], w_o_hy[0]]
    slab_specs = [
        pl.BlockSpec((w.shape[0] // n, w.shape[1]), lambda b, i, n=n: (jnp.minimum(b * n_tiles + i, n - 1), 0))
        for w, n in zip(later_weights, LATER_WEIGHT_SLABS)]
    q_rot, q_plain, k_rot, v_lat, gates, u_p, x0, w_up_c, w_dn_c, w_out_b, w_o_na_b, w_o_hy_b = pl.pallas_call(
        _in_proj_kernel,
        grid=(batch, n_tiles),
        in_specs=[tok(d), prev_spec(d), next_spec(d), mod_spec, _const_spec((1, d)), _const_spec((d, d_in)),
                  _const_spec((1, d_in)), rope_spec, rope_spec,
                  _const_spec((3, 3 * D_HYENA)), _const_spec((1, 3 * D_HYENA))] + slab_specs,
        out_specs=[tok(D_ATTN)] * 3
        + [pl.BlockSpec((1, TOKEN_TILE // ATTN_KEY_CHUNK, D_ATTN, ATTN_KEY_CHUNK), lambda b, i: (b, i, 0, 0)),
           tok(2 * d), pitched_spec, tok(D_HYENA)] + slab_specs,
        out_shape=[jax.ShapeDtypeStruct((batch, seq, D_ATTN), BF16)] * 3
        + [jax.ShapeDtypeStruct((batch, seq // ATTN_KEY_CHUNK, D_ATTN, ATTN_KEY_CHUNK), BF16),
           jax.ShapeDtypeStruct((batch, seq, 2 * d), BF16), pitched_shape,
           jax.ShapeDtypeStruct((batch, seq, D_HYENA), BF16)]
        + [jax.ShapeDtypeStruct(w.shape, BF16) for w in later_weights],
        scratch_shapes=[pltpu.VMEM((3 * D_HYENA // LANES, TOKEN_TILE + 2 * HALO, LANES), F32)],
        compiler_params=_cparams(("arbitrary", "arbitrary")),
        name="in_proj",
    )(x, x, x, mod_lat, g_mix_pre, w_in_b, b_in_r, jnp.asarray(cos_t), jnp.asarray(sin_t),
      hy_conv_w[0], row2(hy_conv_b[0]), *later_weights)

    n_groups = GRID_W // ATTN_ROWS_PER_STEP
    q_spec = pl.BlockSpec((1, ATTN_Q, D_ATTN), lambda b, g: (b, g, 0))
    full = lambda n: pl.BlockSpec((1, n, D_ATTN), lambda b, g: (b, 0, 0))
    bias_spec = pl.BlockSpec(
        (1, N_HEADS, ATTN_KEYS, ATTN_Q),
        lambda b, g: ((g > 0).astype(jnp.int32) + (g == n_groups - 1).astype(jnp.int32), 0, 0, 0))
    vt_spec = pl.BlockSpec((1, seq // ATTN_KEY_CHUNK, D_ATTN, ATTN_KEY_CHUNK), lambda b, g: (b, 0, 0, 0))
    vct_spec = pl.BlockSpec((1, D_ATTN, n_ctx), lambda b, g: (b, 0, 0))
    y_na = pl.pallas_call(
        _attn_kernel,
        grid=(batch, n_groups),
        in_specs=[q_spec, q_spec, full(seq), vt_spec, full(n_ctx), vct_spec, bias_spec],
        out_specs=q_spec,
        out_shape=jax.ShapeDtypeStruct((batch, seq, D_ATTN), BF16),
        compiler_params=_cparams(("arbitrary", "arbitrary")),
        name="attn",
    )(q_rot, q_plain, k_rot, v_lat, k_ctx, v_ctx, bias)

    z_t, aux_t = _filter_tables(seq)
    filt_n2 = FILT_ROWS // FFT_HALF_N1
    hid = FILTER_HIDDEN
    w1_pad = jnp.pad(hy_filt_w1[0], ((0, hid - POS_FEATS), (0, 0)))
    block_diag = lambda w: jnp.zeros((2 * hid, 2 * hid), F32).at[:hid, :hid].set(w).at[hid:, hid:].set(w)
    twice = lambda v: jnp.tile(v, (1, 2))
    w3 = hy_filt_w3[0]
    w3_parts = jnp.stack([jnp.concatenate([w3, jnp.zeros_like(w3)], axis=0),
                          jnp.concatenate([jnp.zeros_like(w3), w3], axis=0)])
    k_circ, k_asum = pl.pallas_call(
        _filt_kernel,
        grid=(2, seq // FILT_ROWS),
        in_specs=[pl.BlockSpec((1, FILT_ROWS // 2, 2 * hid), lambda hf, i: (hf, i, 0)),
                  pl.BlockSpec((1, FILT_ROWS, aux_t.shape[2]), lambda hf, i: (hf, i, 0)),
                  _const_spec((2 * hid, 2 * hid)), _const_spec((1, 2 * hid)),
                  _const_spec((2 * hid, 2 * hid)), _const_spec((1, 2 * hid)),
                  pl.BlockSpec((2, 2 * hid, D_HYENA), lambda hf, i: (0, 0, hf)),
                  pl.BlockSpec((1, D_HYENA), lambda hf, i: (0, hf)),
                  _const_spec((2, 2 * hid)),
                  pl.BlockSpec((1, 1, D_HYENA), lambda hf, i: (hf, 0, 0))],
        out_specs=[pl.BlockSpec((filt_n2, FFT_HALF_N1, D_HYENA), lambda hf, i: (i, hf, 0)),
                   pl.BlockSpec((1, 8, D_HYENA), lambda hf, i: (hf, 0, 0))],
        out_shape=[jax.ShapeDtypeStruct((FFT_N2, FFT_N1, D_HYENA), F32),
                   jax.ShapeDtypeStruct((2, 8, D_HYENA), F32)],
        compiler_params=_cparams(("arbitrary", "arbitrary")),
        name="filt",
    )(jnp.asarray(z_t), jnp.asarray(aux_t), block_diag(w1_pad), twice(row2(hy_filt_b1[0])),
      block_diag(hy_filt_w2[0]), twice(row2(hy_filt_b2[0])), w3_parts, row2(hy_filt_b3[0]),
      twice(hy_sin_freq[0]), hy_decay[0].reshape(2, 1, D_HYENA))
    k_circ = k_circ.reshape(FFT_N, D_HYENA)

    mats = _fft_matrices()
    mats = {k: jnp.asarray(v).astype(BF16) for k, v in mats.items()}
    w2_all = mats["w2"]
    v2_all = mats["v2"]
    step_mat_spec = pl.BlockSpec((FFT_K1_PER_STEP, 2 * FFT_N2, 2 * FFT_N2), lambda cb, s: (s, 0, 0))
    filt_slabs = 2
    kf = pl.pallas_call(
        _filt_fft_kernel,
        grid=(D_HYENA // (filt_slabs * LANES), FFT_STEPS),
        in_specs=[pl.BlockSpec((FFT_N, filt_slabs * LANES), lambda cb, s: (0, cb)),
                  pl.BlockSpec((2, 8, filt_slabs * LANES), lambda cb, s: (0, 0, cb)),
                  _const_spec((FFT_PITCH, FFT_N1)), step_mat_spec],
        out_specs=pl.BlockSpec((FFT_K1_PER_STEP, 2 * FFT_N2, filt_slabs * LANES), lambda cb, s: (s, 0, cb)),
        out_shape=jax.ShapeDtypeStruct((FFT_N1, 2 * FFT_N2, D_HYENA), F32),
        scratch_shapes=[pltpu.VMEM((filt_slabs, FFT_N2 * FFT_PITCH, LANES), F32)],
        compiler_params=_cparams(("arbitrary", "arbitrary")),
        name="filt_fft",
    )(k_circ, k_asum, mats["w1_real"], w2_all)

    n_pairs = batch // 2
    pair_block = ((batch, PITCHED_ROWS, LANES), lambda cb, s: (0, 0, cb))
    pair_spec = pl.BlockSpec(*pair_block, pipeline_mode=pl.Buffered(1))
    y_conv = pl.pallas_call(
        _hy_conv_kernel,
        grid=(D_HYENA // LANES, FFT_STEPS),
        in_specs=[pl.BlockSpec(*pair_block),
                  pl.BlockSpec((FFT_K1_PER_STEP, 2 * FFT_N2, LANES), lambda cb, s: (s, 0, cb)),
                  _const_spec((FFT_PITCH, FFT_N1)), step_mat_spec, step_mat_spec,
                  _const_spec((FFT_N1, 2 * FFT_N1))],
        out_specs=pair_spec,
        out_shape=pitched_shape,
        scratch_shapes=[pltpu.VMEM((n_pairs, FFT_N2 * FFT_PITCH, LANES), F32)],
        compiler_params=_cparams(("arbitrary", "arbitrary")),
        name="hy_conv",
    )(u_p, kf, mats["w1_data"], w2_all, v2_all, mats["v1"])

    mix = pl.pallas_call(
        _merge_kernel,
        grid=(batch, n_tiles),
        in_specs=[tok(D_ATTN), tok(D_HYENA), pitched_spec, pitched_spec, tok(2 * d),
                  _const_spec((1, D_HYENA)), _const_spec((1, d)),
                  _const_spec((D_ATTN, d)), _const_spec((D_HYENA, d)), _const_spec((d, d))],
        out_specs=tok(d),
        out_shape=jax.ShapeDtypeStruct((batch, seq, d), BF16),
        compiler_params=_cparams(("arbitrary", "arbitrary")),
        name="merge",
    )(y_na, x0, u_p, y_conv, gates, row2(hy_skip[0]), row2(norm_mix_post[0]),
      w_o_na_b, w_o_hy_b, w_out_b)

    conv_w_c = ffn_conv_w[0]
    conv_b_c = row2(ffn_conv_b[0])
    mix_halo = 2 * HALO
    mix_blocks = TOKEN_TILE // mix_halo
    mix_prev = pl.BlockSpec((1, mix_halo, d), lambda b, i: (b, jnp.maximum(i * mix_blocks - 1, 0), 0))
    mix_next = pl.BlockSpec((1, mix_halo, d),
                            lambda b, i: (b, jnp.minimum((i + 1) * mix_blocks, seq // mix_halo - 1), 0))
    out = pl.pallas_call(
        _ffn_kernel,
        grid=(batch, n_tiles),
        in_specs=[tok(d), prev_spec(d), next_spec(d), tok(d), mix_prev, mix_next,
                  mod_spec, _const_spec((1, d)), _const_spec((1, d)),
                  _const_spec(w_up_c.shape), _const_spec(conv_w_c.shape), _const_spec(conv_b_c.shape),
                  _const_spec(w_dn_c.shape)],
        out_specs=tok(d),
        out_shape=jax.ShapeDtypeStruct((batch, seq, d), F32),
        scratch_shapes=[pltpu.VMEM((2, 2, FF_CHUNK // LANES, TOKEN_TILE + 2 * HALO, LANES), F32),
                        pltpu.VMEM((TOKEN_TILE, D_FF), BF16)],
        compiler_params=_cparams(("arbitrary", "arbitrary")),
        name="ffn",
    )(x, x, x, mix, mix, mix, mod_lat, row2(norm_ffn_pre[0]), row2(norm_ffn_post[0]),
      w_up_c, conv_w_c, conv_b_c, w_dn_c)
    return out
```

```python
import functools
import math

import jax
import jax.numpy as jnp
import numpy as np
from jax import lax
from jax.experimental import pallas as pl
from jax.experimental.pallas import tpu as pltpu

F32 = jnp.float32
BF16 = jnp.bfloat16

D_MODEL = 1024
N_HEADS = 8
HEAD_DIM = 64
D_ATTN = N_HEADS * HEAD_DIM
D_HYENA = 512
GRID_W = 64
WIN_ROWS = 8
WIN_COLS = 16
POS_BANDS = 16
POS_FEATS = 1 + 2 * POS_BANDS
FILTER_HIDDEN = 64
D_FF = 2816
N_MOD = 6
ROPE_BASE = 10000.0
RMS_EPS = 1e-6
NEG_BIAS = -1e30
LOG2_E = math.log2(math.e)

LANES = 128
VMEM_LIMIT_BYTES = 56 * 1024 * 1024

FFT_N1 = 64
FFT_N2 = 128
FFT_N = FFT_N1 * FFT_N2
FFT_HALF_N1 = FFT_N1 // 2
FFT_PITCH = 2 * FFT_N1 + 4
FFT_K1_PER_STEP = 8
FFT_UNROLL = 8
FFT_STEPS = FFT_N1 // FFT_K1_PER_STEP

TOKEN_TILE = 512
FILT_ROWS = 1024
SEQ_PITCH = FFT_N2 + 8
PITCHED_ROWS = FFT_HALF_N1 * SEQ_PITCH
TILE_GROUPS = TOKEN_TILE // FFT_N2
ATTN_ROWS_PER_STEP = 4
ATTN_Q = ATTN_ROWS_PER_STEP * GRID_W
ATTN_KEY_ROWS = 12
ATTN_KEYS = ATTN_KEY_ROWS * GRID_W
ATTN_KEY_CHUNK = 256
FF_CHUNK = 768
LATER_WEIGHT_SLABS = (32, 16, 32, 32, 32)
HALO = 8
FFN_ROW_BLOCK = 64


def _cparams(sem):
    return pltpu.CompilerParams(dimension_semantics=sem, vmem_limit_bytes=VMEM_LIMIT_BYTES)


@functools.lru_cache(maxsize=None)
def _rope_tables(seq):
    pos = np.arange(seq)
    row = pos // GRID_W
    col = pos % GRID_W
    n_pairs = HEAD_DIM // 4
    inv = ROPE_BASE ** (-np.arange(n_pairs, dtype=np.float64) / n_pairs)
    lane = np.arange(LANES) % HEAD_DIM
    p = np.where(lane[None, :] < HEAD_DIM // 2, row[:, None], col[:, None]).astype(np.float64)
    ang = p * inv[lane % n_pairs][None, :]
    sign = np.where((lane % (2 * n_pairs)) < n_pairs, -1.0, 1.0)
    return np.cos(ang).astype(np.float32), (np.sin(ang) * sign[None, :]).astype(np.float32)


@functools.lru_cache(maxsize=None)
def _filter_tables(seq):
    assert 2 * seq == FFT_N
    half, n2, n1 = np.meshgrid(np.arange(2), np.arange(FFT_N2), np.arange(FFT_HALF_N1), indexing="ij")
    n = FFT_N2 * (half * FFT_HALF_N1 + n1) + n2
    fwd = n < seq
    m = n - seq
    valid = fwd | (m >= 1)
    pos = np.where(valid, np.where(fwd, n, seq - m), 0).astype(np.float64).reshape(2, seq)
    t = pos / max(seq - 1, 1)
    bands = np.linspace(1e-4, POS_BANDS - 1, POS_BANDS)
    ang = (2.0 * math.pi / seq) * pos[..., None] * bands
    z = np.zeros((2, seq, 64), np.float64)
    z[..., 0] = t
    z[..., 1:1 + POS_BANDS] = np.cos(ang)
    z[..., 1 + POS_BANDS:POS_FEATS] = -np.sin(ang)
    aux = np.zeros((2, seq, 8), np.float64)
    aux[..., 0] = t
    aux[..., 1] = valid.reshape(2, seq)
    zp = z.reshape(2, seq // FILT_ROWS, 2, FILT_ROWS // 2, 64).transpose(0, 1, 3, 2, 4).reshape(2, seq // 2, 128)
    return zp.astype(np.float32), aux.astype(np.float32)


def _realify(m):
    return np.block([[m.real, -m.imag], [m.imag, m.real]])


@functools.lru_cache(maxsize=None)
def _fft_matrices():
    n1 = np.arange(FFT_N1)
    n2 = np.arange(FFT_N2)
    k1 = np.arange(FFT_N1)
    k2 = np.arange(FFT_N2)
    f1 = np.exp(-2j * np.pi * np.outer(k1, n1) / FFT_N1)
    w1_data = _realify(f1[:, :FFT_HALF_N1])
    w1_real = np.concatenate([f1.real, f1.imag], axis=0)
    v1 = _realify(np.conj(f1.T)[:FFT_HALF_N1, :] / FFT_N)
    f2 = np.exp(-2j * np.pi * np.outer(k2, n2) / FFT_N2)
    tw = np.exp(-2j * np.pi * np.outer(k1, n2) / FFT_N)
    w2 = np.stack([_realify(f2 * tw[a][None, :]) for a in range(FFT_N1)])
    v2 = np.stack([_realify(np.conj(f2.T) * np.conj(tw[a])[:, None]) for a in range(FFT_N1)])
    return {k: v.astype(np.float32) for k, v in
            dict(w1_data=w1_data, w1_real=w1_real, v1=v1, w2=w2, v2=v2).items()}


N_BIAS_ROWS = 2 * WIN_ROWS - 1
N_BIAS_COLS = 2 * WIN_COLS - 1


def _attn_bias_row_index():
    rows = GRID_W
    groups = (0, 2, rows // ATTN_ROWS_PER_STEP - 1)
    dr = np.full((3, ATTN_ROWS_PER_STEP, ATTN_KEY_ROWS), -1, np.int32)
    for v, g in enumerate(groups):
        ws = min(max(ATTN_ROWS_PER_STEP * g - WIN_ROWS // 2, 0), rows - ATTN_KEY_ROWS)
        for i in range(ATTN_ROWS_PER_STEP):
            r = ATTN_ROWS_PER_STEP * g + i
            r_start = min(max(r - WIN_ROWS // 2, 0), rows - WIN_ROWS)
            for j in range(ATTN_KEY_ROWS):
                kr = ws + j
                if r_start <= kr < r_start + WIN_ROWS:
                    dr[v, i, j] = kr - r + (WIN_ROWS - 1)
    return dr


def _attn_bias_kernel(rpb_ref, w_ref, o_ref, wb_ref, t_ref):
    wb_ref[...] = w_ref[...].astype(BF16)
    head = pl.program_id(0)
    kc = lax.broadcasted_iota(jnp.int32, (GRID_W, LANES), 0)
    lane = lax.broadcasted_iota(jnp.int32, (GRID_W, LANES), 1)
    qc = lane % GRID_W
    c_start = jnp.clip(qc - WIN_COLS // 2, 0, GRID_W - WIN_COLS)
    col_in = (kc >= c_start) & (kc < c_start + WIN_COLS)
    dc = jnp.clip(kc - qc, 1 - WIN_COLS, WIN_COLS - 1) + (WIN_COLS - 1)
    base = head * (N_BIAS_ROWS * N_BIAS_COLS)
    for r in range(N_BIAS_ROWS):
        t = jnp.full((GRID_W, LANES), NEG_BIAS, F32)
        for cidx in range(N_BIAS_COLS):
            t = jnp.where(col_in & (dc == cidx), rpb_ref[base + r * N_BIAS_COLS + cidx] * LOG2_E, t)
        t_ref[r] = t
    dr = _attn_bias_row_index()
    low_half = lane < GRID_W
    masked = jnp.full((GRID_W, LANES), NEG_BIAS, F32)
    for v in range(3):
        for i in range(0, ATTN_ROWS_PER_STEP, 2):
            for j in range(ATTN_KEY_ROWS):
                lo = t_ref[int(dr[v, i, j])] if dr[v, i, j] >= 0 else masked
                hi = t_ref[int(dr[v, i + 1, j])] if dr[v, i + 1, j] >= 0 else masked
                o_ref[v, 0, j * GRID_W:(j + 1) * GRID_W, i * GRID_W:(i + 2) * GRID_W] = jnp.where(
                    low_half, lo, hi)


def _mod_kernel(c_ref, w_ref, b_ref, o_ref):
    c = c_ref[...]
    s = c * jax.nn.sigmoid(c)
    o_ref[...] = jnp.dot(s, w_ref[...], precision=lax.Precision.HIGHEST,
                         preferred_element_type=F32) + b_ref[...]


def _norm_modulate(x, gain, shift, scale):
    ms = jnp.mean(x * x, axis=-1, keepdims=True)
    y = x * lax.rsqrt(ms + RMS_EPS) * gain
    return y * (1.0 + scale) + shift


def _rope(t, cos, sin_signed):
    n_pairs = HEAD_DIM // 4
    lane = lax.broadcasted_iota(jnp.int32, t.shape, 1)
    first = (lane % (2 * n_pairs)) < n_pairs
    partner = jnp.where(first, pltpu.roll(t, LANES - n_pairs, 1), pltpu.roll(t, n_pairs, 1))
    return t * cos + partner * sin_signed


def _in_proj_kernel(x_ref, prev_ref, next_ref, mod_ref, g_ref, w_ref, b_ref, cos_ref, sin_ref, cw_ref, cb_ref,
                    *rest):
    n_cast = len(LATER_WEIGHT_SLABS)
    cast_in, rest = rest[:n_cast], rest[n_cast:]
    (qr_ref, qp_ref, kr_ref, v_ref, gt_ref, u_ref, x0_ref), rest = rest[:7], rest[7:]
    cast_out, (hy_ref,) = rest[:n_cast], rest[n_cast:]
    for src, dst in zip(cast_in, cast_out):
        dst[...] = src[...].astype(BF16)
    i = pl.program_id(1)
    n_tiles = pl.num_programs(1)
    xx = jnp.concatenate([prev_ref[0], x_ref[0], next_ref[0]], axis=0)
    h_ext = _norm_modulate(xx, g_ref[...], mod_ref[0, 0:1, :], mod_ref[0, 1:2, :]).astype(BF16)
    h = h_ext[HALO:HALO + TOKEN_TILE]
    cos = cos_ref[...]
    sin = sin_ref[...]

    def proj(lo, hi, rows=h):
        return jnp.dot(rows, w_ref[:, lo:hi], preferred_element_type=F32) + b_ref[:, lo:hi]

    q = proj(0, D_ATTN) * (HEAD_DIM ** -0.5 * LOG2_E)
    qp_ref[0] = q.astype(BF16)
    k = proj(D_ATTN, 2 * D_ATTN)
    for c in range(D_ATTN // LANES):
        lanes = slice(c * LANES, (c + 1) * LANES)
        qr_ref[0, :, lanes] = _rope(q[:, lanes], cos, sin).astype(BF16)
        kr_ref[0, :, lanes] = _rope(k[:, lanes], cos, sin).astype(BF16)
    v_t = proj(2 * D_ATTN, 3 * D_ATTN).T.astype(BF16)
    for c in range(TOKEN_TILE // ATTN_KEY_CHUNK):
        v_ref[0, c] = v_t[:, c * ATTN_KEY_CHUNK:(c + 1) * ATTN_KEY_CHUNK]
    hy_lo = 3 * D_ATTN
    gl_lo = hy_lo + 3 * D_HYENA
    for c in range(4):
        w = D_MODEL // 2
        gt_ref[0, :, c * w:(c + 1) * w] = jax.nn.sigmoid(
            proj(gl_lo + c * w, gl_lo + (c + 1) * w)).astype(BF16)

    slabs_per_part = D_HYENA // LANES
    for c in range(3):
        hy = proj(hy_lo + c * D_HYENA, hy_lo + (c + 1) * D_HYENA, h_ext)
        for s in range(slabs_per_part):
            hy_ref[c * slabs_per_part + s] = hy[:, s * LANES:(s + 1) * LANES]
    zero_row = jnp.zeros((1, LANES), F32)

    @pl.when(i == 0)
    def _():
        for s in range(3 * slabs_per_part):
            hy_ref[s, HALO - 1:HALO, :] = zero_row

    @pl.when(i == n_tiles - 1)
    def _():
        for s in range(3 * slabs_per_part):
            hy_ref[s, HALO + TOKEN_TILE:HALO + TOKEN_TILE + 1, :] = zero_row

    def conv(s, r0, rows):
        lanes = slice(s * LANES, (s + 1) * LANES)
        tap = lambda j: hy_ref[s, pl.ds(HALO - 1 + j + r0, rows, stride=1), :]
        return (tap(0) * cw_ref[0:1, lanes] + tap(1) * cw_ref[1:2, lanes] + tap(2) * cw_ref[2:3, lanes]
                + cb_ref[:, lanes])

    for s in range(slabs_per_part):
        lanes = slice(s * LANES, (s + 1) * LANES)
        for j in range(TILE_GROUPS):
            r0 = j * FFT_N2
            x0_ref[0, r0:r0 + FFT_N2, lanes] = conv(s, r0, FFT_N2).astype(BF16)
            u_ref[0, j * SEQ_PITCH:j * SEQ_PITCH + FFT_N2, lanes] = (
                conv(slabs_per_part + s, r0, FFT_N2) * conv(2 * slabs_per_part + s, r0, FFT_N2))
            u_ref[0, j * SEQ_PITCH + FFT_N2:(j + 1) * SEQ_PITCH, lanes] = jnp.zeros((SEQ_PITCH - FFT_N2, LANES), F32)


def _ctx_kv_kernel(x_ref, mod_ref, g_ref, w_ref, b_ref, k_ref, v_ref):
    h = _norm_modulate(x_ref[0], g_ref[...], mod_ref[0:1, :], mod_ref[1:2, :]).astype(BF16)
    kv = jnp.dot(h, w_ref[...], preferred_element_type=F32) + b_ref[...]
    k_ref[0] = kv[:, :D_ATTN].astype(BF16)
    v_ref[0] = kv[:, D_ATTN:].T.astype(BF16)


def _attn_window_start(g):
    return jnp.clip(ATTN_ROWS_PER_STEP * g - WIN_ROWS // 2, 0, GRID_W - ATTN_KEY_ROWS)


def _attn_kernel(qr_ref, qp_ref, k_ref, vt_ref, kc_ref, vct_ref, bias_ref, o_ref):
    g = pl.program_id(1)
    win = _attn_window_start(g)
    key0 = pl.multiple_of(win * GRID_W, ATTN_KEY_CHUNK)
    chunk0 = win // (ATTN_KEY_CHUNK // GRID_W)
    nt = (((1,), (1,)), ((), ()))
    quad_w = 4 * HEAD_DIM
    lane = lax.broadcasted_iota(jnp.int32, (1, quad_w), 1)
    zero = jnp.zeros((), BF16)
    def scores(head):
        quad, hh = divmod(head, 4)
        ql = slice(quad * quad_w, (quad + 1) * quad_w)
        mine = (lane >= hh * HEAD_DIM) & (lane < (hh + 1) * HEAD_DIM)
        s_nb = lax.dot_general(k_ref[0, pl.ds(key0, ATTN_KEYS), ql], jnp.where(mine, qr_ref[0, :, ql], zero), nt,
                               preferred_element_type=F32)
        s_cx = lax.dot_general(kc_ref[0, :, ql], jnp.where(mine, qp_ref[0, :, ql], zero), nt,
                               preferred_element_type=F32)
        return s_nb + bias_ref[0, head], s_cx

    def with_ones(v):
        return jnp.concatenate([v, jnp.ones((16, v.shape[1]), BF16)], axis=0)

    def probs(s_nb, s_cx):
        m = jnp.maximum(jnp.max(s_nb, axis=0, keepdims=True), jnp.max(s_cx, axis=0, keepdims=True))
        return jnp.exp2(s_nb - m).astype(BF16), jnp.exp2(s_cx - m).astype(BF16)

    def values(head, p_nb, p_cx):
        rows = slice(head * HEAD_DIM, (head + 1) * HEAD_DIM)
        v_win = jnp.concatenate([vt_ref[0, chunk0 + c, rows, :] for c in range(ATTN_KEYS // ATTN_KEY_CHUNK)],
                                axis=1)
        o = (jnp.dot(with_ones(v_win), p_nb, preferred_element_type=F32)
             + jnp.dot(with_ones(vct_ref[0, rows, :]), p_cx, preferred_element_type=F32))
        return o[:HEAD_DIM] / o[HEAD_DIM:HEAD_DIM + 1]

    outs = []
    s_q = {0: scores(0), 1: scores(1)}
    p_q = {0: probs(*s_q.pop(0))}
    for head in range(N_HEADS):
        if head + 2 < N_HEADS:
            s_q[head + 2] = scores(head + 2)
        if head + 1 < N_HEADS:
            p_q[head + 1] = probs(*s_q.pop(head + 1))
        outs.append(values(head, *p_q.pop(head)))
    o_ref[0] = jnp.concatenate(outs, axis=0).T.astype(BF16)


def _filt_kernel(z_ref, aux_ref, w1_ref, b1_ref, w2_ref, b2_ref, w3_ref, b3_ref, freq_ref, decay_ref,
                 o_ref, asum_ref):
    i = pl.program_id(1)
    hp = lax.Precision.HIGHEST
    h = jnp.sin(freq_ref[0:1, :] * (jnp.dot(z_ref[0], w1_ref[...], precision=hp,
                                            preferred_element_type=F32) + b1_ref[...]))
    h = jnp.sin(freq_ref[1:2, :] * (jnp.dot(h, w2_ref[...], precision=hp,
                                            preferred_element_type=F32) + b2_ref[...]))
    taps = jnp.concatenate([jnp.dot(h, w3_ref[part], precision=hp, preferred_element_type=F32)
                            for part in range(2)], axis=0) + b3_ref[...]
    t = aux_ref[0, :, 0:1]
    valid = aux_ref[0, :, 1:2] > 0.5
    k = jnp.where(valid, taps * jnp.exp(-t * jnp.abs(decay_ref[0])), 0.0)
    part = jnp.sum(jnp.abs(k), axis=0, keepdims=True)

    @pl.when(i == 0)
    def _():
        asum_ref[...] = jnp.zeros_like(asum_ref)

    asum_ref[0] += jnp.broadcast_to(part, asum_ref.shape[1:])
    o_ref[...] = k.reshape(o_ref.shape)


def _fft_stage1(load_group, w1_ref, a_ref):
    n_slabs = a_ref.shape[0]

    def body(n2, carry):
        x = load_group(n2)
        a = jnp.dot(w1_ref[...], x, preferred_element_type=F32)
        for s in range(n_slabs):
            a_ref[s, pl.ds(n2 * FFT_PITCH, 2 * FFT_N1, stride=1), :] = a[:, s * LANES:(s + 1) * LANES]
        return carry

    lax.fori_loop(0, FFT_N2, body, 0, unroll=FFT_UNROLL)


def _fft_load_k1(a_ref, k1):
    parts = []
    for off in (0, FFT_N1):
        parts.append(jnp.concatenate(
            [a_ref[s, pl.ds(k1 + off, FFT_N2, stride=FFT_PITCH), :] for s in range(a_ref.shape[0])], axis=1))
    return jnp.concatenate(parts, axis=0)


def _filt_fft_kernel(k_ref, asum_ref, w1_ref, w2_ref, o_ref, a_ref):
    step = pl.program_id(1)

    @pl.when(step == 0)
    def _():
        norm = asum_ref[0, 0:1, :] + asum_ref[1, 0:1, :]

        def load_group(n2):
            return (k_ref[pl.ds(pl.multiple_of(n2 * FFT_N1, FFT_N1), FFT_N1), :] / norm).astype(BF16)
        _fft_stage1(load_group, w1_ref, a_ref)

    for j in range(FFT_K1_PER_STEP):
        b = _fft_load_k1(a_ref, step * FFT_K1_PER_STEP + j).astype(BF16)
        o_ref[j] = jnp.dot(w2_ref[j], b, preferred_element_type=F32)


def _hy_conv_kernel(u_ref, kf_ref, w1_ref, w2_ref, v2_ref, v1_ref, o_ref, a_ref):
    step = pl.program_id(1)
    n_slabs = a_ref.shape[0]

    def tokens_of(member, s, n2):
        return (2 * s + member, pl.ds(n2, FFT_HALF_N1, stride=SEQ_PITCH), slice(None))

    @pl.when(step == 0)
    def _():
        for sample in range(o_ref.shape[0]):
            for grp in range(FFT_HALF_N1):
                o_ref[sample, grp * SEQ_PITCH + FFT_N2:(grp + 1) * SEQ_PITCH, :] = jnp.zeros(
                    (SEQ_PITCH - FFT_N2, LANES), F32)

        def load_group(n2):
            return jnp.concatenate(
                [jnp.concatenate([u_ref[tokens_of(member, s, n2)] for s in range(n_slabs)], axis=1)
                 for member in range(2)], axis=0).astype(BF16)
        _fft_stage1(load_group, w1_ref, a_ref)

    for j in range(FFT_K1_PER_STEP):
        k1 = step * FFT_K1_PER_STEP + j
        b = _fft_load_k1(a_ref, k1).astype(BF16)
        x = jnp.dot(w2_ref[j], b, preferred_element_type=F32)
        kf = kf_ref[j]
        kf = jnp.concatenate([kf] * n_slabs, axis=1)
        xr, xi = x[:FFT_N2], x[FFT_N2:]
        kr, ki = kf[:FFT_N2], kf[FFT_N2:]
        y = jnp.concatenate([xr * kr - xi * ki, xr * ki + xi * kr], axis=0).astype(BF16)
        d = jnp.dot(v2_ref[j], y, preferred_element_type=F32)
        for s in range(n_slabs):
            lanes = slice(s * LANES, (s + 1) * LANES)
            a_ref[s, pl.ds(k1, FFT_N2, stride=FFT_PITCH), :] = d[:FFT_N2, lanes]
            a_ref[s, pl.ds(k1 + FFT_N1, FFT_N2, stride=FFT_PITCH), :] = d[FFT_N2:, lanes]

    @pl.when(step == FFT_STEPS - 1)
    def _():
        def body(n2, carry):
            d = jnp.concatenate([a_ref[s, pl.ds(n2 * FFT_PITCH, 2 * FFT_N1, stride=1), :] for s in range(n_slabs)],
                                axis=1)
            y = jnp.dot(v1_ref[...], d.astype(BF16), preferred_element_type=F32)
            for s in range(n_slabs):
                for member in range(2):
                    o_ref[tokens_of(member, s, n2)] = y[member * FFT_HALF_N1:(member + 1) * FFT_HALF_N1,
                                                        s * LANES:(s + 1) * LANES]
            return carry

        lax.fori_loop(0, FFT_N2, body, 0, unroll=FFT_UNROLL)


def _from_pitched(ref):
    return jnp.concatenate([ref[0, j * SEQ_PITCH:j * SEQ_PITCH + FFT_N2, :] for j in range(TILE_GROUPS)], axis=0)


def _merge_kernel(yna_ref, x0_ref, u_ref, yc_ref, gt_ref, skip_ref, gpost_ref, wna_ref, why_ref, wout_ref, o_ref):
    y_hy = (x0_ref[0].astype(F32) * (_from_pitched(yc_ref) + _from_pitched(u_ref) * skip_ref[...])).astype(BF16)
    a = jnp.dot(yna_ref[0], wna_ref[...], preferred_element_type=F32)
    b = jnp.dot(y_hy, why_ref[...], preferred_element_type=F32)
    g_na = gt_ref[0, :, :D_MODEL].astype(F32)
    g_hy = gt_ref[0, :, D_MODEL:].astype(F32)
    m = (g_na * a + g_hy * b).astype(BF16)
    o = jnp.dot(m, wout_ref[...], preferred_element_type=F32)
    ms = jnp.mean(o * o, axis=-1, keepdims=True)
    o_ref[0] = (o * lax.rsqrt(ms + RMS_EPS) * gpost_ref[...]).astype(BF16)


def _gelu_tanh(a):
    return 0.5 * a * (1.0 + jnp.tanh(math.sqrt(2.0 / math.pi) * (a + 0.044715 * (a * a * a))))


def _ffn_kernel(x_ref, prev_ref, next_ref, mix_ref, mix_prev_ref, mix_next_ref, mod_ref, gpre_ref, gpost_ref,
                wup_ref, cw_ref, cb_ref, wdn_ref, o_ref, u_ref, act_ref):
    i = pl.program_id(1)
    n_tiles = pl.num_programs(1)
    mix_halo = mix_prev_ref.shape[1]
    mix = jnp.concatenate([mix_prev_ref[0, mix_halo - HALO:, :], mix_ref[0], mix_next_ref[0, :HALO, :]], axis=0)
    xx = (jnp.concatenate([prev_ref[0], x_ref[0], next_ref[0]], axis=0)
          + mod_ref[0, 2:3, :] * mix.astype(F32))
    x = xx[HALO:HALO + TOKEN_TILE]
    h = _norm_modulate(xx, gpre_ref[...], mod_ref[0, 3:4, :], mod_ref[0, 4:5, :])
    row = lax.broadcasted_iota(jnp.int32, (xx.shape[0], 1), 0)
    inside = ((row >= HALO) | (i > 0)) & ((row < HALO + TOKEN_TILE) | (i < n_tiles - 1))
    h = jnp.where(inside, h, 0.0).astype(BF16)
    starts = list(range(0, D_FF, FF_CHUNK))
    widths = [min(FF_CHUNK, D_FF - lo) for lo in starts]

    def conv(buf, half, s, lo, r0):
        lanes = slice(half * D_FF + lo, half * D_FF + lo + LANES)
        tap = lambda j: u_ref[buf, half, s, pl.ds(HALO - 1 + j + r0, FFN_ROW_BLOCK, stride=1), :]
        return (tap(0) * cw_ref[0:1, lanes] + tap(1) * cw_ref[1:2, lanes] + tap(2) * cw_ref[2:3, lanes]
                + cb_ref[:, lanes])

    def up_project(c):
        for half in range(2):
            lo = half * D_FF + starts[c]
            u = jnp.dot(h, wup_ref[:, lo:lo + widths[c]], preferred_element_type=F32)
            for s in range(widths[c] // LANES):
                u_ref[c % 2, half, s] = u[:, s * LANES:(s + 1) * LANES]

    up_project(0)
    for c in range(len(starts)):
        buf = c % 2
        if c + 1 < len(starts):
            up_project(c + 1)
        for s in range(widths[c] // LANES):
            lo = starts[c] + s * LANES
            for r0 in range(0, TOKEN_TILE, FFN_ROW_BLOCK):
                act_ref[r0:r0 + FFN_ROW_BLOCK, lo:lo + LANES] = (
                    _gelu_tanh(conv(buf, 0, s, lo, r0)) * conv(buf, 1, s, lo, r0)).astype(BF16)
    y = jnp.dot(act_ref[...], wdn_ref[...], preferred_element_type=F32)
    ms = jnp.mean(y * y, axis=-1, keepdims=True)
    y = y * lax.rsqrt(ms + RMS_EPS) * gpost_ref[...]
    o_ref[0] = x + mod_ref[0, 5:6, :] * y


def _const_spec(shape):
    nd = len(shape)
    return pl.BlockSpec(shape, lambda *_: (0,) * nd, pipeline_mode=pl.Buffered(1))


def kernel(x, c, ctx, c_ctx, w_mod, b_mod, norm_mix_pre, norm_mix_post, norm_ffn_pre, norm_ffn_post, w_in, b_in, na_rpb, hy_conv_w, hy_conv_b, hy_filt_w1, hy_filt_b1, hy_filt_w2, hy_filt_b2, hy_filt_w3, hy_filt_b3, hy_sin_freq, hy_decay, hy_skip, w_o_na, w_o_hy, w_out, ffn_w_up, ffn_conv_w, ffn_conv_b, ffn_w_down):
    batch, seq, d = x.shape
    n_ctx = ctx.shape[1]
    assert d == D_MODEL and 2 * seq == FFT_N and seq == GRID_W * GRID_W and batch % 2 == 0
    assert w_mod.shape[0] == 1, "single-layer block"
    n_tiles = seq // TOKEN_TILE
    d_in = w_in.shape[2]
    row2 = lambda a: a.reshape(1, -1)

    c_all = jnp.zeros((8, d), F32).at[:batch].set(c).at[batch].set(c_ctx)
    mod_n = 1024
    mod = pl.pallas_call(
        _mod_kernel,
        grid=(N_MOD * d // mod_n,),
        in_specs=[_const_spec((8, d)),
                  pl.BlockSpec((d, mod_n), lambda j: (0, j)),
                  pl.BlockSpec((1, mod_n), lambda j: (0, j))],
        out_specs=pl.BlockSpec((8, mod_n), lambda j: (0, j)),
        out_shape=jax.ShapeDtypeStruct((8, N_MOD * d), F32),
        compiler_params=_cparams(("arbitrary",)),
        name="mod",
    )(c_all, w_mod[0], row2(b_mod[0]))
    mod_lat = jnp.pad(mod[:batch].reshape(batch, N_MOD, d), ((0, 0), (0, 8 - N_MOD), (0, 0)))
    mod_ctx = jnp.pad(mod[batch].reshape(N_MOD, d), ((0, 8 - N_MOD), (0, 0)))

    w_rows = d // N_HEADS
    bias, w_in_b = pl.pallas_call(
        _attn_bias_kernel,
        grid=(N_HEADS,),
        in_specs=[pl.BlockSpec(memory_space=pltpu.SMEM), pl.BlockSpec((w_rows, d_in), lambda h: (h, 0))],
        out_specs=[pl.BlockSpec((3, 1, ATTN_KEYS, ATTN_Q), lambda h: (0, h, 0, 0)),
                   pl.BlockSpec((w_rows, d_in), lambda h: (h, 0))],
        out_shape=[jax.ShapeDtypeStruct((3, N_HEADS, ATTN_KEYS, ATTN_Q), F32),
                   jax.ShapeDtypeStruct((d, d_in), BF16)],
        scratch_shapes=[pltpu.VMEM((N_BIAS_ROWS, GRID_W, LANES), F32)],
        compiler_params=_cparams(("arbitrary",)),
        name="attn_bias",
    )(na_rpb[0].reshape(-1), w_in[0])
    b_in_r = row2(b_in[0])
    g_mix_pre = row2(norm_mix_pre[0])

    k_ctx, v_ctx = pl.pallas_call(
        _ctx_kv_kernel,
        grid=(batch,),
        in_specs=[pl.BlockSpec((1, n_ctx, d), lambda b: (b, 0, 0)),
                  _const_spec((8, d)), _const_spec((1, d)),
                  _const_spec((d, 2 * D_ATTN)), _const_spec((1, 2 * D_ATTN))],
        out_specs=[pl.BlockSpec((1, n_ctx, D_ATTN), lambda b: (b, 0, 0)),
                   pl.BlockSpec((1, D_ATTN, n_ctx), lambda b: (b, 0, 0))],
        out_shape=[jax.ShapeDtypeStruct((batch, n_ctx, D_ATTN), BF16),
                   jax.ShapeDtypeStruct((batch, D_ATTN, n_ctx), BF16)],
        compiler_params=_cparams(("arbitrary",)),
        name="ctx_kv",
    )(ctx, mod_ctx, g_mix_pre, w_in_b[:, D_ATTN:3 * D_ATTN], b_in_r[:, D_ATTN:3 * D_ATTN])

    cos_t, sin_t = _rope_tables(seq)
    tok = lambda w: pl.BlockSpec((1, TOKEN_TILE, w), lambda b, i: (b, i, 0))
    mod_spec = pl.BlockSpec((1, 8, d), lambda b, i: (b, 0, 0))
    rope_spec = pl.BlockSpec((TOKEN_TILE, LANES), lambda b, i: (i, 0))
    halo_blocks = TOKEN_TILE // HALO
    n_halo_blocks = seq // HALO
    prev_spec = lambda w: pl.BlockSpec((1, HALO, w), lambda b, i: (b, jnp.maximum(i * halo_blocks - 1, 0), 0))
    next_spec = lambda w: pl.BlockSpec(
        (1, HALO, w), lambda b, i: (b, jnp.minimum((i + 1) * halo_blocks, n_halo_blocks - 1), 0))
    pitched_spec = pl.BlockSpec((1, TILE_GROUPS * SEQ_PITCH, D_HYENA), lambda b, i: (b, i, 0))
    pitched_shape = jax.ShapeDtypeStruct((batch, PITCHED_ROWS, D_HYENA), F32)
    later_weights = [ffn_w_up[0], ffn_w_down[0], w_out[0], w_o_na[0], w_o_hy[0]]
    slab_specs = [
        pl.BlockSpec((w.shape[0] // n, w.shape[1]), lambda b, i, n=n: (jnp.minimum(b * n_tiles + i, n - 1), 0))
        for w, n in zip(later_weights, LATER_WEIGHT_SLABS)]
    q_rot, q_plain, k_rot, v_lat, gates, u_p, x0, w_up_c, w_dn_c, w_out_b, w_o_na_b, w_o_hy_b = pl.pallas_call(
        _in_proj_kernel,
        grid=(batch, n_tiles),
        in_specs=[tok(d), prev_spec(d), next_spec(d), mod_spec, _const_spec((1, d)), _const_spec((d, d_in)),
                  _const_spec((1, d_in)), rope_spec, rope_spec,
                  _const_spec((3, 3 * D_HYENA)), _const_spec((1, 3 * D_HYENA))] + slab_specs,
        out_specs=[tok(D_ATTN)] * 3
        + [pl.BlockSpec((1, TOKEN_TILE // ATTN_KEY_CHUNK, D_ATTN, ATTN_KEY_CHUNK), lambda b, i: (b, i, 0, 0)),
           tok(2 * d), pitched_spec, tok(D_HYENA)] + slab_specs,
        out_shape=[jax.ShapeDtypeStruct((batch, seq, D_ATTN), BF16)] * 3
        + [jax.ShapeDtypeStruct((batch, seq // ATTN_KEY_CHUNK, D_ATTN, ATTN_KEY_CHUNK), BF16),
           jax.ShapeDtypeStruct((batch, seq, 2 * d), BF16), pitched_shape,
           jax.ShapeDtypeStruct((batch, seq, D_HYENA), BF16)]
        + [jax.ShapeDtypeStruct(w.shape, BF16) for w in later_weights],
        scratch_shapes=[pltpu.VMEM((3 * D_HYENA // LANES, TOKEN_TILE + 2 * HALO, LANES), F32)],
        compiler_params=_cparams(("arbitrary", "arbitrary")),
        name="in_proj",
    )(x, x, x, mod_lat, g_mix_pre, w_in_b, b_in_r, jnp.asarray(cos_t), jnp.asarray(sin_t),
      hy_conv_w[0], row2(hy_conv_b[0]), *later_weights)

    n_groups = GRID_W // ATTN_ROWS_PER_STEP
    q_spec = pl.BlockSpec((1, ATTN_Q, D_ATTN), lambda b, g: (b, g, 0))
    full = lambda n: pl.BlockSpec((1, n, D_ATTN), lambda b, g: (b, 0, 0))
    bias_spec = pl.BlockSpec(
        (1, N_HEADS, ATTN_KEYS, ATTN_Q),
        lambda b, g: ((g > 0).astype(jnp.int32) + (g == n_groups - 1).astype(jnp.int32), 0, 0, 0))
    vt_spec = pl.BlockSpec((1, seq // ATTN_KEY_CHUNK, D_ATTN, ATTN_KEY_CHUNK), lambda b, g: (b, 0, 0, 0))
    vct_spec = pl.BlockSpec((1, D_ATTN, n_ctx), lambda b, g: (b, 0, 0))
    y_na = pl.pallas_call(
        _attn_kernel,
        grid=(batch, n_groups),
        in_specs=[q_spec, q_spec, full(seq), vt_spec, full(n_ctx), vct_spec, bias_spec],
        out_specs=q_spec,
        out_shape=jax.ShapeDtypeStruct((batch, seq, D_ATTN), BF16),
        compiler_params=_cparams(("arbitrary", "arbitrary")),
        name="attn",
    )(q_rot, q_plain, k_rot, v_lat, k_ctx, v_ctx, bias)

    z_t, aux_t = _filter_tables(seq)
    filt_n2 = FILT_ROWS // FFT_HALF_N1
    hid = FILTER_HIDDEN
    w1_pad = jnp.pad(hy_filt_w1[0], ((0, hid - POS_FEATS), (0, 0)))
    block_diag = lambda w: jnp.zeros((2 * hid, 2 * hid), F32).at[:hid, :hid].set(w).at[hid:, hid:].set(w)
    twice = lambda v: jnp.tile(v, (1, 2))
    w3 = hy_filt_w3[0]
    w3_parts = jnp.stack([jnp.concatenate([w3, jnp.zeros_like(w3)], axis=0),
                          jnp.concatenate([jnp.zeros_like(w3), w3], axis=0)])
    k_circ, k_asum = pl.pallas_call(
        _filt_kernel,
        grid=(2, seq // FILT_ROWS),
        in_specs=[pl.BlockSpec((1, FILT_ROWS // 2, 2 * hid), lambda hf, i: (hf, i, 0)),
                  pl.BlockSpec((1, FILT_ROWS, aux_t.shape[2]), lambda hf, i: (hf, i, 0)),
                  _const_spec((2 * hid, 2 * hid)), _const_spec((1, 2 * hid)),
                  _const_spec((2 * hid, 2 * hid)), _const_spec((1, 2 * hid)),
                  pl.BlockSpec((2, 2 * hid, D_HYENA), lambda hf, i: (0, 0, hf)),
                  pl.BlockSpec((1, D_HYENA), lambda hf, i: (0, hf)),
                  _const_spec((2, 2 * hid)),
                  pl.BlockSpec((1, 1, D_HYENA), lambda hf, i: (hf, 0, 0))],
        out_specs=[pl.BlockSpec((filt_n2, FFT_HALF_N1, D_HYENA), lambda hf, i: (i, hf, 0)),
                   pl.BlockSpec((1, 8, D_HYENA), lambda hf, i: (hf, 0, 0))],
        out_shape=[jax.ShapeDtypeStruct((FFT_N2, FFT_N1, D_HYENA), F32),
                   jax.ShapeDtypeStruct((2, 8, D_HYENA), F32)],
        compiler_params=_cparams(("arbitrary", "arbitrary")),
        name="filt",
    )(jnp.asarray(z_t), jnp.asarray(aux_t), block_diag(w1_pad), twice(row2(hy_filt_b1[0])),
      block_diag(hy_filt_w2[0]), twice(row2(hy_filt_b2[0])), w3_parts, row2(hy_filt_b3[0]),
      twice(hy_sin_freq[0]), hy_decay[0].reshape(2, 1, D_HYENA))
    k_circ = k_circ.reshape(FFT_N, D_HYENA)

    mats = _fft_matrices()
    mats = {k: jnp.asarray(v).astype(BF16) for k, v in mats.items()}
    w2_all = mats["w2"]
    v2_all = mats["v2"]
    step_mat_spec = pl.BlockSpec((FFT_K1_PER_STEP, 2 * FFT_N2, 2 * FFT_N2), lambda cb, s: (s, 0, 0))
    filt_slabs = 2
    kf = pl.pallas_call(
        _filt_fft_kernel,
        grid=(D_HYENA // (filt_slabs * LANES), FFT_STEPS),
        in_specs=[pl.BlockSpec((FFT_N, filt_slabs * LANES), lambda cb, s: (0, cb)),
                  pl.BlockSpec((2, 8, filt_slabs * LANES), lambda cb, s: (0, 0, cb)),
                  _const_spec((2 * FFT_N1, FFT_N1)), step_mat_spec],
        out_specs=pl.BlockSpec((FFT_K1_PER_STEP, 2 * FFT_N2, filt_slabs * LANES), lambda cb, s: (s, 0, cb)),
        out_shape=jax.ShapeDtypeStruct((FFT_N1, 2 * FFT_N2, D_HYENA), F32),
        scratch_shapes=[pltpu.VMEM((filt_slabs, FFT_N2 * FFT_PITCH, LANES), F32)],
        compiler_params=_cparams(("arbitrary", "arbitrary")),
        name="filt_fft",
    )(k_circ, k_asum, mats["w1_real"], w2_all)

    n_pairs = batch // 2
    pair_block = ((batch, PITCHED_ROWS, LANES), lambda cb, s: (0, 0, cb))
    pair_spec = pl.BlockSpec(*pair_block, pipeline_mode=pl.Buffered(1))
    y_conv = pl.pallas_call(
        _hy_conv_kernel,
        grid=(D_HYENA // LANES, FFT_STEPS),
        in_specs=[pl.BlockSpec(*pair_block),
                  pl.BlockSpec((FFT_K1_PER_STEP, 2 * FFT_N2, LANES), lambda cb, s: (s, 0, cb)),
                  _const_spec((2 * FFT_N1, FFT_N1)), step_mat_spec, step_mat_spec,
                  _const_spec((FFT_N1, 2 * FFT_N1))],
        out_specs=pair_spec,
        out_shape=pitched_shape,
        scratch_shapes=[pltpu.VMEM((n_pairs, FFT_N2 * FFT_PITCH, LANES), F32)],
        compiler_params=_cparams(("arbitrary", "arbitrary")),
        name="hy_conv",
    )(u_p, kf, mats["w1_data"], w2_all, v2_all, mats["v1"])

    mix = pl.pallas_call(
        _merge_kernel,
        grid=(batch, n_tiles),
        in_specs=[tok(D_ATTN), tok(D_HYENA), pitched_spec, pitched_spec, tok(2 * d),
                  _const_spec((1, D_HYENA)), _const_spec((1, d)),
                  _const_spec((D_ATTN, d)), _const_spec((D_HYENA, d)), _const_spec((d, d))],
        out_specs=tok(d),
        out_shape=jax.ShapeDtypeStruct((batch, seq, d), BF16),
        compiler_params=_cparams(("arbitrary", "arbitrary")),
        name="merge",
    )(y_na, x0, u_p, y_conv, gates, row2(hy_skip[0]), row2(norm_mix_post[0]),
      w_o_na_b, w_o_hy_b, w_out_b)

    conv_w_c = ffn_conv_w[0]
    conv_b_c = row2(ffn_conv_b[0])
    mix_halo = 2 * HALO
    mix_blocks = TOKEN_TILE // mix_halo
    mix_prev = pl.BlockSpec((1, mix_halo, d), lambda b, i: (b, jnp.maximum(i * mix_blocks - 1, 0), 0))
    mix_next = pl.BlockSpec((1, mix_halo, d),
                            lambda b, i: (b, jnp.minimum((i + 1) * mix_blocks, seq // mix_halo - 1), 0))
    out = pl.pallas_call(
        _ffn_kernel,
        grid=(batch, n_tiles),
        in_specs=[tok(d), prev_spec(d), next_spec(d), tok(d), mix_prev, mix_next,
                  mod_spec, _const_spec((1, d)), _const_spec((1, d)),
                  _const_spec(w_up_c.shape), _const_spec(conv_w_c.shape), _const_spec(conv_b_c.shape),
                  _const_spec(w_dn_c.shape)],
        out_specs=tok(d),
        out_shape=jax.ShapeDtypeStruct((batch, seq, d), F32),
        scratch_shapes=[pltpu.VMEM((2, 2, FF_CHUNK // LANES, TOKEN_TILE + 2 * HALO, LANES), F32),
                        pltpu.VMEM((TOKEN_TILE, D_FF), BF16)],
        compiler_params=_cparams(("arbitrary", "arbitrary")),
        name="ffn",
    )(x, x, x, mix, mix, mix, mod_lat, row2(norm_ffn_pre[0]), row2(norm_ffn_post[0]),
      w_up_c, conv_w_c, conv_b_c, w_dn_c)
    return out
```

```python
import functools
import math

import jax
import jax.numpy as jnp
import numpy as np
from jax import lax
from jax.experimental import pallas as pl
from jax.experimental.pallas import tpu as pltpu

F32 = jnp.float32
BF16 = jnp.bfloat16

D_MODEL = 1024
N_HEADS = 8
HEAD_DIM = 64
D_ATTN = N_HEADS * HEAD_DIM
D_HYENA = 512
GRID_W = 64
WIN_ROWS = 8
WIN_COLS = 16
POS_BANDS = 16
POS_FEATS = 1 + 2 * POS_BANDS
FILTER_HIDDEN = 64
D_FF = 2816
N_MOD = 6
ROPE_BASE = 10000.0
RMS_EPS = 1e-6
NEG_BIAS = -1e30
LOG2_E = math.log2(math.e)

LANES = 128
VMEM_LIMIT_BYTES = 56 * 1024 * 1024

FFT_N1 = 64
FFT_N2 = 128
FFT_N = FFT_N1 * FFT_N2
FFT_HALF_N1 = FFT_N1 // 2
FFT_PITCH = 2 * FFT_N1 + 4
FFT_K1_PER_STEP = 8
FFT_UNROLL = 8
FFT_STEPS = FFT_N1 // FFT_K1_PER_STEP

TOKEN_TILE = 512
FILT_ROWS = 1024
SEQ_PITCH = FFT_N2 + 8
PITCHED_ROWS = FFT_HALF_N1 * SEQ_PITCH
TILE_GROUPS = TOKEN_TILE // FFT_N2
ATTN_ROWS_PER_STEP = 4
ATTN_Q = ATTN_ROWS_PER_STEP * GRID_W
ATTN_KEY_ROWS = 12
ATTN_KEYS = ATTN_KEY_ROWS * GRID_W
ATTN_KEY_CHUNK = 256
FF_CHUNK = 768
LATER_WEIGHT_SLABS = (32, 16, 32, 32, 32)
HALO = 8
FFN_ROW_BLOCK = 64


def _cparams(sem):
    return pltpu.CompilerParams(dimension_semantics=sem, vmem_limit_bytes=VMEM_LIMIT_BYTES)


@functools.lru_cache(maxsize=None)
def _rope_tables(seq):
    pos = np.arange(seq)
    row = pos // GRID_W
    col = pos % GRID_W
    n_pairs = HEAD_DIM // 4
    inv = ROPE_BASE ** (-np.arange(n_pairs, dtype=np.float64) / n_pairs)
    lane = np.arange(LANES) % HEAD_DIM
    p = np.where(lane[None, :] < HEAD_DIM // 2, row[:, None], col[:, None]).astype(np.float64)
    ang = p * inv[lane % n_pairs][None, :]
    sign = np.where((lane % (2 * n_pairs)) < n_pairs, -1.0, 1.0)
    return np.cos(ang).astype(np.float32), (np.sin(ang) * sign[None, :]).astype(np.float32)


@functools.lru_cache(maxsize=None)
def _filter_tables(seq):
    assert 2 * seq == FFT_N
    half, n2, n1 = np.meshgrid(np.arange(2), np.arange(FFT_N2), np.arange(FFT_HALF_N1), indexing="ij")
    n = FFT_N2 * (half * FFT_HALF_N1 + n1) + n2
    fwd = n < seq
    m = n - seq
    valid = fwd | (m >= 1)
    pos = np.where(valid, np.where(fwd, n, seq - m), 0).astype(np.float64).reshape(2, seq)
    t = pos / max(seq - 1, 1)
    bands = np.linspace(1e-4, POS_BANDS - 1, POS_BANDS)
    ang = (2.0 * math.pi / seq) * pos[..., None] * bands
    z = np.zeros((2, seq, 64), np.float64)
    z[..., 0] = t
    z[..., 1:1 + POS_BANDS] = np.cos(ang)
    z[..., 1 + POS_BANDS:POS_FEATS] = -np.sin(ang)
    aux = np.zeros((2, seq, 8), np.float64)
    aux[..., 0] = t
    aux[..., 1] = valid.reshape(2, seq)
    zp = z.reshape(2, seq // FILT_ROWS, 2, FILT_ROWS // 2, 64).transpose(0, 1, 3, 2, 4).reshape(2, seq // 2, 128)
    return zp.astype(np.float32), aux.astype(np.float32)


def _realify(m):
    return np.block([[m.real, -m.imag], [m.imag, m.real]])


@functools.lru_cache(maxsize=None)
def _fft_matrices():
    n1 = np.arange(FFT_N1)
    n2 = np.arange(FFT_N2)
    k1 = np.arange(FFT_N1)
    k2 = np.arange(FFT_N2)
    f1 = np.exp(-2j * np.pi * np.outer(k1, n1) / FFT_N1)
    w1_data = _realify(f1[:, :FFT_HALF_N1])
    w1_real = np.concatenate([f1.real, f1.imag], axis=0)
    v1 = _realify(np.conj(f1.T)[:FFT_HALF_N1, :] / FFT_N)
    f2 = np.exp(-2j * np.pi * np.outer(k2, n2) / FFT_N2)
    tw = np.exp(-2j * np.pi * np.outer(k1, n2) / FFT_N)
    w2 = np.stack([_realify(f2 * tw[a][None, :]) for a in range(FFT_N1)])
    return {k: v.astype(np.float32) for k, v in
            dict(w1_data=w1_data, w1_real=w1_real, v1=v1, w2=w2).items()}


N_BIAS_ROWS = 2 * WIN_ROWS - 1
N_BIAS_COLS = 2 * WIN_COLS - 1


def _attn_bias_row_index():
    rows = GRID_W
    groups = (0, 2, rows // ATTN_ROWS_PER_STEP - 1)
    dr = np.full((3, ATTN_ROWS_PER_STEP, ATTN_KEY_ROWS), -1, np.int32)
    for v, g in enumerate(groups):
        ws = min(max(ATTN_ROWS_PER_STEP * g - WIN_ROWS // 2, 0), rows - ATTN_KEY_ROWS)
        for i in range(ATTN_ROWS_PER_STEP):
            r = ATTN_ROWS_PER_STEP * g + i
            r_start = min(max(r - WIN_ROWS // 2, 0), rows - WIN_ROWS)
            for j in range(ATTN_KEY_ROWS):
                kr = ws + j
                if r_start <= kr < r_start + WIN_ROWS:
                    dr[v, i, j] = kr - r + (WIN_ROWS - 1)
    return dr


def _attn_bias_kernel(rpb_ref, w_ref, o_ref, wb_ref, t_ref):
    wb_ref[...] = w_ref[...].astype(BF16)
    head = pl.program_id(0)
    kc = lax.broadcasted_iota(jnp.int32, (GRID_W, LANES), 0)
    lane = lax.broadcasted_iota(jnp.int32, (GRID_W, LANES), 1)
    qc = lane % GRID_W
    c_start = jnp.clip(qc - WIN_COLS // 2, 0, GRID_W - WIN_COLS)
    col_in = (kc >= c_start) & (kc < c_start + WIN_COLS)
    dc = jnp.clip(kc - qc, 1 - WIN_COLS, WIN_COLS - 1) + (WIN_COLS - 1)
    base = head * (N_BIAS_ROWS * N_BIAS_COLS)
    for r in range(N_BIAS_ROWS):
        t = jnp.full((GRID_W, LANES), NEG_BIAS, F32)
        for cidx in range(N_BIAS_COLS):
            t = jnp.where(col_in & (dc == cidx), rpb_ref[base + r * N_BIAS_COLS + cidx] * LOG2_E, t)
        t_ref[r] = t
    dr = _attn_bias_row_index()
    low_half = lane < GRID_W
    masked = jnp.full((GRID_W, LANES), NEG_BIAS, F32)
    for v in range(3):
        for i in range(0, ATTN_ROWS_PER_STEP, 2):
            for j in range(ATTN_KEY_ROWS):
                lo = t_ref[int(dr[v, i, j])] if dr[v, i, j] >= 0 else masked
                hi = t_ref[int(dr[v, i + 1, j])] if dr[v, i + 1, j] >= 0 else masked
                o_ref[v, 0, j * GRID_W:(j + 1) * GRID_W, i * GRID_W:(i + 2) * GRID_W] = jnp.where(
                    low_half, lo, hi)


def _mod_kernel(c_ref, w_ref, b_ref, o_ref):
    c = c_ref[...]
    s = c * jax.nn.sigmoid(c)
    o_ref[...] = jnp.dot(s, w_ref[...], precision=lax.Precision.HIGHEST,
                         preferred_element_type=F32) + b_ref[...]


def _norm_modulate(x, gain, shift, scale):
    ms = jnp.mean(x * x, axis=-1, keepdims=True)
    y = x * lax.rsqrt(ms + RMS_EPS) * gain
    return y * (1.0 + scale) + shift


def _rope(t, cos, sin_signed):
    n_pairs = HEAD_DIM // 4
    lane = lax.broadcasted_iota(jnp.int32, t.shape, 1)
    first = (lane % (2 * n_pairs)) < n_pairs
    partner = jnp.where(first, pltpu.roll(t, LANES - n_pairs, 1), pltpu.roll(t, n_pairs, 1))
    return t * cos + partner * sin_signed


def _in_proj_kernel(x_ref, prev_ref, next_ref, mod_ref, g_ref, w_ref, b_ref, cos_ref, sin_ref, cw_ref, cb_ref,
                    *rest):
    n_cast = len(LATER_WEIGHT_SLABS)
    cast_in, rest = rest[:n_cast], rest[n_cast:]
    (qr_ref, qp_ref, kr_ref, v_ref, gt_ref, u_ref, x0_ref), rest = rest[:7], rest[7:]
    cast_out, (hy_ref,) = rest[:n_cast], rest[n_cast:]
    for src, dst in zip(cast_in, cast_out):
        dst[...] = src[...].astype(BF16)
    i = pl.program_id(1)
    n_tiles = pl.num_programs(1)
    xx = jnp.concatenate([prev_ref[0], x_ref[0], next_ref[0]], axis=0)
    h_ext = _norm_modulate(xx, g_ref[...], mod_ref[0, 0:1, :], mod_ref[0, 1:2, :]).astype(BF16)
    h = h_ext[HALO:HALO + TOKEN_TILE]
    cos = cos_ref[...]
    sin = sin_ref[...]

    def proj(lo, hi, rows=h):
        return jnp.dot(rows, w_ref[:, lo:hi], preferred_element_type=F32) + b_ref[:, lo:hi]

    q = proj(0, D_ATTN) * (HEAD_DIM ** -0.5 * LOG2_E)
    qp_ref[0] = q.astype(BF16)
    k = proj(D_ATTN, 2 * D_ATTN)
    for c in range(D_ATTN // LANES):
        lanes = slice(c * LANES, (c + 1) * LANES)
        qr_ref[0, :, lanes] = _rope(q[:, lanes], cos, sin).astype(BF16)
        kr_ref[0, :, lanes] = _rope(k[:, lanes], cos, sin).astype(BF16)
    v_t = proj(2 * D_ATTN, 3 * D_ATTN).T.astype(BF16)
    for c in range(TOKEN_TILE // ATTN_KEY_CHUNK):
        v_ref[0, c] = v_t[:, c * ATTN_KEY_CHUNK:(c + 1) * ATTN_KEY_CHUNK]
    hy_lo = 3 * D_ATTN
    gl_lo = hy_lo + 3 * D_HYENA
    for c in range(4):
        w = D_MODEL // 2
        gt_ref[0, :, c * w:(c + 1) * w] = jax.nn.sigmoid(
            proj(gl_lo + c * w, gl_lo + (c + 1) * w)).astype(BF16)

    slabs_per_part = D_HYENA // LANES
    for c in range(3):
        hy = proj(hy_lo + c * D_HYENA, hy_lo + (c + 1) * D_HYENA, h_ext)
        for s in range(slabs_per_part):
            hy_ref[c * slabs_per_part + s] = hy[:, s * LANES:(s + 1) * LANES]
    zero_row = jnp.zeros((1, LANES), F32)

    @pl.when(i == 0)
    def _():
        for s in range(3 * slabs_per_part):
            hy_ref[s, HALO - 1:HALO, :] = zero_row

    @pl.when(i == n_tiles - 1)
    def _():
        for s in range(3 * slabs_per_part):
            hy_ref[s, HALO + TOKEN_TILE:HALO + TOKEN_TILE + 1, :] = zero_row

    def conv(s, r0, rows):
        lanes = slice(s * LANES, (s + 1) * LANES)
        tap = lambda j: hy_ref[s, pl.ds(HALO - 1 + j + r0, rows, stride=1), :]
        return (tap(0) * cw_ref[0:1, lanes] + tap(1) * cw_ref[1:2, lanes] + tap(2) * cw_ref[2:3, lanes]
                + cb_ref[:, lanes])

    for s in range(slabs_per_part):
        lanes = slice(s * LANES, (s + 1) * LANES)
        for j in range(TILE_GROUPS):
            r0 = j * FFT_N2
            x0_ref[0, r0:r0 + FFT_N2, lanes] = conv(s, r0, FFT_N2).astype(BF16)
            u_ref[0, j * SEQ_PITCH:j * SEQ_PITCH + FFT_N2, lanes] = (
                conv(slabs_per_part + s, r0, FFT_N2) * conv(2 * slabs_per_part + s, r0, FFT_N2))
            u_ref[0, j * SEQ_PITCH + FFT_N2:(j + 1) * SEQ_PITCH, lanes] = jnp.zeros((SEQ_PITCH - FFT_N2, LANES), F32)


def _ctx_kv_kernel(x_ref, mod_ref, g_ref, w_ref, b_ref, k_ref, v_ref):
    h = _norm_modulate(x_ref[0], g_ref[...], mod_ref[0:1, :], mod_ref[1:2, :]).astype(BF16)
    kv = jnp.dot(h, w_ref[...], preferred_element_type=F32) + b_ref[...]
    k_ref[0] = kv[:, :D_ATTN].astype(BF16)
    v_ref[0] = kv[:, D_ATTN:].T.astype(BF16)


def _attn_window_start(g):
    return jnp.clip(ATTN_ROWS_PER_STEP * g - WIN_ROWS // 2, 0, GRID_W - ATTN_KEY_ROWS)


def _attn_kernel(qr_ref, qp_ref, k_ref, vt_ref, kc_ref, vct_ref, bias_ref, o_ref):
    g = pl.program_id(1)
    win = _attn_window_start(g)
    key0 = pl.multiple_of(win * GRID_W, ATTN_KEY_CHUNK)
    chunk0 = win // (ATTN_KEY_CHUNK // GRID_W)
    nt = (((1,), (1,)), ((), ()))
    quad_w = 4 * HEAD_DIM
    lane = lax.broadcasted_iota(jnp.int32, (1, quad_w), 1)
    zero = jnp.zeros((), BF16)
    def scores(head):
        quad, hh = divmod(head, 4)
        ql = slice(quad * quad_w, (quad + 1) * quad_w)
        mine = (lane >= hh * HEAD_DIM) & (lane < (hh + 1) * HEAD_DIM)
        s_nb = lax.dot_general(k_ref[0, pl.ds(key0, ATTN_KEYS), ql], jnp.where(mine, qr_ref[0, :, ql], zero), nt,
                               preferred_element_type=F32)
        s_cx = lax.dot_general(kc_ref[0, :, ql], jnp.where(mine, qp_ref[0, :, ql], zero), nt,
                               preferred_element_type=F32)
        return s_nb + bias_ref[0, head], s_cx

    def with_ones(v):
        return jnp.concatenate([v, jnp.ones((16, v.shape[1]), BF16)], axis=0)

    def probs(s_nb, s_cx):
        m = jnp.maximum(jnp.max(s_nb, axis=0, keepdims=True), jnp.max(s_cx, axis=0, keepdims=True))
        return jnp.exp2(s_nb - m).astype(BF16), jnp.exp2(s_cx - m).astype(BF16)

    def values(head, p_nb, p_cx):
        rows = slice(head * HEAD_DIM, (head + 1) * HEAD_DIM)
        v_win = jnp.concatenate([vt_ref[0, chunk0 + c, rows, :] for c in range(ATTN_KEYS // ATTN_KEY_CHUNK)],
                                axis=1)
        o = (jnp.dot(with_ones(v_win), p_nb, preferred_element_type=F32)
             + jnp.dot(with_ones(vct_ref[0, rows, :]), p_cx, preferred_element_type=F32))
        return o[:HEAD_DIM] / o[HEAD_DIM:HEAD_DIM + 1]

    outs = []
    s_q = {0: scores(0), 1: scores(1)}
    p_q = {0: probs(*s_q.pop(0))}
    for head in range(N_HEADS):
        if head + 2 < N_HEADS:
            s_q[head + 2] = scores(head + 2)
        if head + 1 < N_HEADS:
            p_q[head + 1] = probs(*s_q.pop(head + 1))
        outs.append(values(head, *p_q.pop(head)))
    o_ref[0] = jnp.concatenate(outs, axis=0).T.astype(BF16)


def _filt_kernel(z_ref, aux_ref, w1_ref, b1_ref, w2_ref, b2_ref, w3_ref, b3_ref, freq_ref, decay_ref,
                 o_ref, asum_ref):
    i = pl.program_id(1)
    hp = lax.Precision.HIGHEST
    h = jnp.sin(freq_ref[0:1, :] * (jnp.dot(z_ref[0], w1_ref[...], precision=hp,
                                            preferred_element_type=F32) + b1_ref[...]))
    h = jnp.sin(freq_ref[1:2, :] * (jnp.dot(h, w2_ref[...], precision=hp,
                                            preferred_element_type=F32) + b2_ref[...]))
    taps = jnp.concatenate([jnp.dot(h, w3_ref[part], precision=hp, preferred_element_type=F32)
                            for part in range(2)], axis=0) + b3_ref[...]
    t = aux_ref[0, :, 0:1]
    valid = aux_ref[0, :, 1:2] > 0.5
    k = jnp.where(valid, taps * jnp.exp(-t * jnp.abs(decay_ref[0])), 0.0)
    part = jnp.sum(jnp.abs(k), axis=0, keepdims=True)

    @pl.when(i == 0)
    def _():
        asum_ref[...] = jnp.zeros_like(asum_ref)

    asum_ref[0] += jnp.broadcast_to(part, asum_ref.shape[1:])
    o_ref[...] = k.reshape(o_ref.shape)


def _fft_stage1(load_group, w1_ref, a_ref):
    n_slabs = a_ref.shape[0]

    def body(n2, carry):
        x = load_group(n2)
        a = jnp.dot(w1_ref[...], x, preferred_element_type=F32)
        for s in range(n_slabs):
            a_ref[s, pl.ds(n2 * FFT_PITCH, 2 * FFT_N1, stride=1), :] = a[:, s * LANES:(s + 1) * LANES]
        return carry

    lax.fori_loop(0, FFT_N2, body, 0, unroll=FFT_UNROLL)


def _fft_load_k1(a_ref, k1):
    parts = []
    for off in (0, FFT_N1):
        parts.append(jnp.concatenate(
            [a_ref[s, pl.ds(k1 + off, FFT_N2, stride=FFT_PITCH), :] for s in range(a_ref.shape[0])], axis=1))
    return jnp.concatenate(parts, axis=0)


def _filt_fft_kernel(k_ref, asum_ref, w1_ref, w2_ref, o_ref, a_ref):
    step = pl.program_id(1)

    @pl.when(step == 0)
    def _():
        norm = asum_ref[0, 0:1, :] + asum_ref[1, 0:1, :]

        def load_group(n2):
            return (k_ref[pl.ds(pl.multiple_of(n2 * FFT_N1, FFT_N1), FFT_N1), :] / norm).astype(BF16)
        _fft_stage1(load_group, w1_ref, a_ref)

    for j in range(FFT_K1_PER_STEP):
        b = _fft_load_k1(a_ref, step * FFT_K1_PER_STEP + j).astype(BF16)
        o_ref[j] = jnp.dot(w2_ref[j], b, preferred_element_type=F32).astype(BF16)


def _hy_conv_kernel(u_ref, kf_ref, w1_ref, w2_ref, v1_ref, o_ref, a_ref):
    step = pl.program_id(1)
    n_slabs = a_ref.shape[0]

    def tokens_of(member, s, n2):
        return (2 * s + member, pl.ds(n2, FFT_HALF_N1, stride=SEQ_PITCH), slice(None))

    @pl.when(step == 0)
    def _():
        for sample in range(o_ref.shape[0]):
            for grp in range(FFT_HALF_N1):
                o_ref[sample, grp * SEQ_PITCH + FFT_N2:(grp + 1) * SEQ_PITCH, :] = jnp.zeros(
                    (SEQ_PITCH - FFT_N2, LANES), F32)

        def load_group(n2):
            return jnp.concatenate(
                [jnp.concatenate([u_ref[tokens_of(member, s, n2)] for s in range(n_slabs)], axis=1)
                 for member in range(2)], axis=0).astype(BF16)
        _fft_stage1(load_group, w1_ref, a_ref)

    for j in range(FFT_K1_PER_STEP):
        k1 = step * FFT_K1_PER_STEP + j
        b = _fft_load_k1(a_ref, k1).astype(BF16)
        x = jnp.dot(w2_ref[j], b, preferred_element_type=F32)
        kf = kf_ref[j].astype(F32)
        kf = jnp.concatenate([kf] * n_slabs, axis=1)
        xr, xi = x[:FFT_N2], x[FFT_N2:]
        kr, ki = kf[:FFT_N2], kf[FFT_N2:]
        y = jnp.concatenate([xr * kr - xi * ki, xr * ki + xi * kr], axis=0).astype(BF16)
        d = lax.dot_general(w2_ref[j], y, (((0,), (0,)), ((), ())),
                            preferred_element_type=F32)
        for s in range(n_slabs):
            lanes = slice(s * LANES, (s + 1) * LANES)
            a_ref[s, pl.ds(k1, FFT_N2, stride=FFT_PITCH), :] = d[:FFT_N2, lanes]
            a_ref[s, pl.ds(k1 + FFT_N1, FFT_N2, stride=FFT_PITCH), :] = d[FFT_N2:, lanes]

    @pl.when(step == FFT_STEPS - 1)
    def _():
        def body(n2, carry):
            d = jnp.concatenate([a_ref[s, pl.ds(n2 * FFT_PITCH, 2 * FFT_N1, stride=1), :] for s in range(n_slabs)],
                                axis=1)
            y = jnp.dot(v1_ref[...], d.astype(BF16), preferred_element_type=F32)
            for s in range(n_slabs):
                for member in range(2):
                    o_ref[tokens_of(member, s, n2)] = y[member * FFT_HALF_N1:(member + 1) * FFT_HALF_N1,
                                                        s * LANES:(s + 1) * LANES]
            return carry

        lax.fori_loop(0, FFT_N2, body, 0, unroll=FFT_UNROLL)


def _from_pitched(ref):
    return jnp.concatenate([ref[0, j * SEQ_PITCH:j * SEQ_PITCH + FFT_N2, :] for j in range(TILE_GROUPS)], axis=0)


def _merge_kernel(yna_ref, x0_ref, u_ref, yc_ref, gt_ref, skip_ref, gpost_ref, wna_ref, why_ref, wout_ref, o_ref):
    y_hy = (x0_ref[0].astype(F32) * (_from_pitched(yc_ref) + _from_pitched(u_ref) * skip_ref[...])).astype(BF16)
    a = jnp.dot(yna_ref[0], wna_ref[...], preferred_element_type=F32)
    b = jnp.dot(y_hy, why_ref[...], preferred_element_type=F32)
    g_na = gt_ref[0, :, :D_MODEL].astype(F32)
    g_hy = gt_ref[0, :, D_MODEL:].astype(F32)
    m = (g_na * a + g_hy * b).astype(BF16)
    o = jnp.dot(m, wout_ref[...], preferred_element_type=F32)
    ms = jnp.mean(o * o, axis=-1, keepdims=True)
    o_ref[0] = (o * lax.rsqrt(ms + RMS_EPS) * gpost_ref[...]).astype(BF16)


def _gelu_tanh(a):
    return 0.5 * a * (1.0 + jnp.tanh(math.sqrt(2.0 / math.pi) * (a + 0.044715 * (a * a * a))))


def _ffn_kernel(x_ref, prev_ref, next_ref, mix_ref, mix_prev_ref, mix_next_ref, mod_ref, gpre_ref, gpost_ref,
                wup_ref, cw_ref, cb_ref, wdn_ref, o_ref, u_ref, act_ref):
    i = pl.program_id(1)
    n_tiles = pl.num_programs(1)
    mix_halo = mix_prev_ref.shape[1]
    mix = jnp.concatenate([mix_prev_ref[0, mix_halo - HALO:, :], mix_ref[0], mix_next_ref[0, :HALO, :]], axis=0)
    xx = (jnp.concatenate([prev_ref[0], x_ref[0], next_ref[0]], axis=0)
          + mod_ref[0, 2:3, :] * mix.astype(F32))
    x = xx[HALO:HALO + TOKEN_TILE]
    h = _norm_modulate(xx, gpre_ref[...], mod_ref[0, 3:4, :], mod_ref[0, 4:5, :])
    row = lax.broadcasted_iota(jnp.int32, (xx.shape[0], 1), 0)
    inside = ((row >= HALO) | (i > 0)) & ((row < HALO + TOKEN_TILE) | (i < n_tiles - 1))
    h = jnp.where(inside, h, 0.0).astype(BF16)
    starts = list(range(0, D_FF, FF_CHUNK))
    widths = [min(FF_CHUNK, D_FF - lo) for lo in starts]

    def conv(buf, half, s, lo, r0):
        lanes = slice(half * D_FF + lo, half * D_FF + lo + LANES)
        tap = lambda j: u_ref[buf, half, s, pl.ds(HALO - 1 + j + r0, FFN_ROW_BLOCK, stride=1), :]
        return (tap(0) * cw_ref[0:1, lanes] + tap(1) * cw_ref[1:2, lanes] + tap(2) * cw_ref[2:3, lanes]
                + cb_ref[:, lanes])

    def up_project(c):
        for half in range(2):
            lo = half * D_FF + starts[c]
            u = jnp.dot(h, wup_ref[:, lo:lo + widths[c]], preferred_element_type=F32)
            for s in range(widths[c] // LANES):
                u_ref[c % 2, half, s] = u[:, s * LANES:(s + 1) * LANES]

    up_project(0)
    for c in range(len(starts)):
        buf = c % 2
        if c + 1 < len(starts):
            up_project(c + 1)
        for s in range(widths[c] // LANES):
            lo = starts[c] + s * LANES
            for r0 in range(0, TOKEN_TILE, FFN_ROW_BLOCK):
                act_ref[r0:r0 + FFN_ROW_BLOCK, lo:lo + LANES] = (
                    _gelu_tanh(conv(buf, 0, s, lo, r0)) * conv(buf, 1, s, lo, r0)).astype(BF16)
    y = jnp.dot(act_ref[...], wdn_ref[...], preferred_element_type=F32)
    ms = jnp.mean(y * y, axis=-1, keepdims=True)
    y = y * lax.rsqrt(ms + RMS_EPS) * gpost_ref[...]
    o_ref[0] = x + mod_ref[0, 5:6, :] * y


def _const_spec(shape):
    nd = len(shape)
    return pl.BlockSpec(shape, lambda *_: (0,) * nd, pipeline_mode=pl.Buffered(1))


def kernel(x, c, ctx, c_ctx, w_mod, b_mod, norm_mix_pre, norm_mix_post, norm_ffn_pre, norm_ffn_post, w_in, b_in, na_rpb, hy_conv_w, hy_conv_b, hy_filt_w1, hy_filt_b1, hy_filt_w2, hy_filt_b2, hy_filt_w3, hy_filt_b3, hy_sin_freq, hy_decay, hy_skip, w_o_na, w_o_hy, w_out, ffn_w_up, ffn_conv_w, ffn_conv_b, ffn_w_down):
    batch, seq, d = x.shape
    n_ctx = ctx.shape[1]
    assert d == D_MODEL and 2 * seq == FFT_N and seq == GRID_W * GRID_W and batch % 2 == 0
    assert w_mod.shape[0] == 1, "single-layer block"
    n_tiles = seq // TOKEN_TILE
    d_in = w_in.shape[2]
    row2 = lambda a: a.reshape(1, -1)

    c_all = jnp.zeros((8, d), F32).at[:batch].set(c).at[batch].set(c_ctx)
    mod_n = 1024
    mod = pl.pallas_call(
        _mod_kernel,
        grid=(N_MOD * d // mod_n,),
        in_specs=[_const_spec((8, d)),
                  pl.BlockSpec((d, mod_n), lambda j: (0, j)),
                  pl.BlockSpec((1, mod_n), lambda j: (0, j))],
        out_specs=pl.BlockSpec((8, mod_n), lambda j: (0, j)),
        out_shape=jax.ShapeDtypeStruct((8, N_MOD * d), F32),
        compiler_params=_cparams(("arbitrary",)),
        name="mod",
    )(c_all, w_mod[0], row2(b_mod[0]))
    mod_lat = jnp.pad(mod[:batch].reshape(batch, N_MOD, d), ((0, 0), (0, 8 - N_MOD), (0, 0)))
    mod_ctx = jnp.pad(mod[batch].reshape(N_MOD, d), ((0, 8 - N_MOD), (0, 0)))

    w_rows = d // N_HEADS
    bias, w_in_b = pl.pallas_call(
        _attn_bias_kernel,
        grid=(N_HEADS,),
        in_specs=[pl.BlockSpec(memory_space=pltpu.SMEM), pl.BlockSpec((w_rows, d_in), lambda h: (h, 0))],
        out_specs=[pl.BlockSpec((3, 1, ATTN_KEYS, ATTN_Q), lambda h: (0, h, 0, 0)),
                   pl.BlockSpec((w_rows, d_in), lambda h: (h, 0))],
        out_shape=[jax.ShapeDtypeStruct((3, N_HEADS, ATTN_KEYS, ATTN_Q), F32),
                   jax.ShapeDtypeStruct((d, d_in), BF16)],
        scratch_shapes=[pltpu.VMEM((N_BIAS_ROWS, GRID_W, LANES), F32)],
        compiler_params=_cparams(("arbitrary",)),
        name="attn_bias",
    )(na_rpb[0].reshape(-1), w_in[0])
    b_in_r = row2(b_in[0])
    g_mix_pre = row2(norm_mix_pre[0])

    k_ctx, v_ctx = pl.pallas_call(
        _ctx_kv_kernel,
        grid=(batch,),
        in_specs=[pl.BlockSpec((1, n_ctx, d), lambda b: (b, 0, 0)),
                  _const_spec((8, d)), _const_spec((1, d)),
                  _const_spec((d, 2 * D_ATTN)), _const_spec((1, 2 * D_ATTN))],
        out_specs=[pl.BlockSpec((1, n_ctx, D_ATTN), lambda b: (b, 0, 0)),
                   pl.BlockSpec((1, D_ATTN, n_ctx), lambda b: (b, 0, 0))],
        out_shape=[jax.ShapeDtypeStruct((batch, n_ctx, D_ATTN), BF16),
                   jax.ShapeDtypeStruct((batch, D_ATTN, n_ctx), BF16)],
        compiler_params=_cparams(("arbitrary",)),
        name="ctx_kv",
    )(ctx, mod_ctx, g_mix_pre, w_in_b[:, D_ATTN:3 * D_ATTN], b_in_r[:, D_ATTN:3 * D_ATTN])

    cos_t, sin_t = _rope_tables(seq)
    tok = lambda w: pl.BlockSpec((1, TOKEN_TILE, w), lambda b, i: (b, i, 0))
    mod_spec = pl.BlockSpec((1, 8, d), lambda b, i: (b, 0, 0))
    rope_spec = pl.BlockSpec((TOKEN_TILE, LANES), lambda b, i: (i, 0))
    halo_blocks = TOKEN_TILE // HALO
    n_halo_blocks = seq // HALO
    prev_spec = lambda w: pl.BlockSpec((1, HALO, w), lambda b, i: (b, jnp.maximum(i * halo_blocks - 1, 0), 0))
    next_spec = lambda w: pl.BlockSpec(
        (1, HALO, w), lambda b, i: (b, jnp.minimum((i + 1) * halo_blocks, n_halo_blocks - 1), 0))
    pitched_spec = pl.BlockSpec((1, TILE_GROUPS * SEQ_PITCH, D_HYENA), lambda b, i: (b, i, 0))
    pitched_shape = jax.ShapeDtypeStruct((batch, PITCHED_ROWS, D_HYENA), F32)
    later_weights = [ffn_w_up[0], ffn_w_down[0], w_out[0], w_o_na[0], w_o_hy[0]]
    slab_specs = [
        pl.BlockSpec((w.shape[0] // n, w.shape[1]), lambda b, i, n=n: (jnp.minimum(b * n_tiles + i, n - 1), 0))
        for w, n in zip(later_weights, LATER_WEIGHT_SLABS)]
    q_rot, q_plain, k_rot, v_lat, gates, u_p, x0, w_up_c, w_dn_c, w_out_b, w_o_na_b, w_o_hy_b = pl.pallas_call(
        _in_proj_kernel,
        grid=(batch, n_tiles),
        in_specs=[tok(d), prev_spec(d), next_spec(d), mod_spec, _const_spec((1, d)), _const_spec((d, d_in)),
                  _const_spec((1, d_in)), rope_spec, rope_spec,
                  _const_spec((3, 3 * D_HYENA)), _const_spec((1, 3 * D_HYENA))] + slab_specs,
        out_specs=[tok(D_ATTN)] * 3
        + [pl.BlockSpec((1, TOKEN_TILE // ATTN_KEY_CHUNK, D_ATTN, ATTN_KEY_CHUNK), lambda b, i: (b, i, 0, 0)),
           tok(2 * d), pitched_spec, tok(D_HYENA)] + slab_specs,
        out_shape=[jax.ShapeDtypeStruct((batch, seq, D_ATTN), BF16)] * 3
        + [jax.ShapeDtypeStruct((batch, seq // ATTN_KEY_CHUNK, D_ATTN, ATTN_KEY_CHUNK), BF16),
           jax.ShapeDtypeStruct((batch, seq, 2 * d), BF16), pitched_shape,
           jax.ShapeDtypeStruct((batch, seq, D_HYENA), BF16)]
        + [jax.ShapeDtypeStruct(w.shape, BF16) for w in later_weights],
        scratch_shapes=[pltpu.VMEM((3 * D_HYENA // LANES, TOKEN_TILE + 2 * HALO, LANES), F32)],
        compiler_params=_cparams(("arbitrary", "arbitrary")),
        name="in_proj",
    )(x, x, x, mod_lat, g_mix_pre, w_in_b, b_in_r, jnp.asarray(cos_t), jnp.asarray(sin_t),
      hy_conv_w[0], row2(hy_conv_b[0]), *later_weights)

    n_groups = GRID_W // ATTN_ROWS_PER_STEP
    q_spec = pl.BlockSpec((1, ATTN_Q, D_ATTN), lambda b, g: (b, g, 0))
    full = lambda n: pl.BlockSpec((1, n, D_ATTN), lambda b, g: (b, 0, 0))
    bias_spec = pl.BlockSpec(
        (1, N_HEADS, ATTN_KEYS, ATTN_Q),
        lambda b, g: ((g > 0).astype(jnp.int32) + (g == n_groups - 1).astype(jnp.int32), 0, 0, 0))
    vt_spec = pl.BlockSpec((1, seq // ATTN_KEY_CHUNK, D_ATTN, ATTN_KEY_CHUNK), lambda b, g: (b, 0, 0, 0))
    vct_spec = pl.BlockSpec((1, D_ATTN, n_ctx), lambda b, g: (b, 0, 0))
    y_na = pl.pallas_call(
        _attn_kernel,
        grid=(batch, n_groups),
        in_specs=[q_spec, q_spec, full(seq), vt_spec, full(n_ctx), vct_spec, bias_spec],
        out_specs=q_spec,
        out_shape=jax.ShapeDtypeStruct((batch, seq, D_ATTN), BF16),
        compiler_params=_cparams(("arbitrary", "arbitrary")),
        name="attn",
    )(q_rot, q_plain, k_rot, v_lat, k_ctx, v_ctx, bias)

    z_t, aux_t = _filter_tables(seq)
    filt_n2 = FILT_ROWS // FFT_HALF_N1
    hid = FILTER_HIDDEN
    w1_pad = jnp.pad(hy_filt_w1[0], ((0, hid - POS_FEATS), (0, 0)))
    block_diag = lambda w: jnp.zeros((2 * hid, 2 * hid), F32).at[:hid, :hid].set(w).at[hid:, hid:].set(w)
    twice = lambda v: jnp.tile(v, (1, 2))
    w3 = hy_filt_w3[0]
    w3_parts = jnp.stack([jnp.concatenate([w3, jnp.zeros_like(w3)], axis=0),
                          jnp.concatenate([jnp.zeros_like(w3), w3], axis=0)])
    k_circ, k_asum = pl.pallas_call(
        _filt_kernel,
        grid=(2, seq // FILT_ROWS),
        in_specs=[pl.BlockSpec((1, FILT_ROWS // 2, 2 * hid), lambda hf, i: (hf, i, 0)),
                  pl.BlockSpec((1, FILT_ROWS, aux_t.shape[2]), lambda hf, i: (hf, i, 0)),
                  _const_spec((2 * hid, 2 * hid)), _const_spec((1, 2 * hid)),
                  _const_spec((2 * hid, 2 * hid)), _const_spec((1, 2 * hid)),
                  pl.BlockSpec((2, 2 * hid, D_HYENA), lambda hf, i: (0, 0, hf)),
                  pl.BlockSpec((1, D_HYENA), lambda hf, i: (0, hf)),
                  _const_spec((2, 2 * hid)),
                  pl.BlockSpec((1, 1, D_HYENA), lambda hf, i: (hf, 0, 0))],
        out_specs=[pl.BlockSpec((filt_n2, FFT_HALF_N1, D_HYENA), lambda hf, i: (i, hf, 0)),
                   pl.BlockSpec((1, 8, D_HYENA), lambda hf, i: (hf, 0, 0))],
        out_shape=[jax.ShapeDtypeStruct((FFT_N2, FFT_N1, D_HYENA), F32),
                   jax.ShapeDtypeStruct((2, 8, D_HYENA), F32)],
        compiler_params=_cparams(("arbitrary", "arbitrary")),
        name="filt",
    )(jnp.asarray(z_t), jnp.asarray(aux_t), block_diag(w1_pad), twice(row2(hy_filt_b1[0])),
      block_diag(hy_filt_w2[0]), twice(row2(hy_filt_b2[0])), w3_parts, row2(hy_filt_b3[0]),
      twice(hy_sin_freq[0]), hy_decay[0].reshape(2, 1, D_HYENA))
    k_circ = k_circ.reshape(FFT_N, D_HYENA)

    mats = _fft_matrices()
    mats = {k: jnp.asarray(v).astype(BF16) for k, v in mats.items()}
    w2_all = mats["w2"]
    step_mat_spec = pl.BlockSpec((FFT_K1_PER_STEP, 2 * FFT_N2, 2 * FFT_N2), lambda cb, s: (s, 0, 0))
    filt_slabs = 2
    kf = pl.pallas_call(
        _filt_fft_kernel,
        grid=(D_HYENA // (filt_slabs * LANES), FFT_STEPS),
        in_specs=[pl.BlockSpec((FFT_N, filt_slabs * LANES), lambda cb, s: (0, cb)),
                  pl.BlockSpec((2, 8, filt_slabs * LANES), lambda cb, s: (0, 0, cb)),
                  _const_spec((2 * FFT_N1, FFT_N1)), step_mat_spec],
        out_specs=pl.BlockSpec((FFT_K1_PER_STEP, 2 * FFT_N2, filt_slabs * LANES), lambda cb, s: (s, 0, cb)),
        out_shape=jax.ShapeDtypeStruct((FFT_N1, 2 * FFT_N2, D_HYENA), BF16),
        scratch_shapes=[pltpu.VMEM((filt_slabs, FFT_N2 * FFT_PITCH, LANES), F32)],
        compiler_params=_cparams(("arbitrary", "arbitrary")),
        name="filt_fft",
    )(k_circ, k_asum, mats["w1_real"], w2_all)

    n_pairs = batch // 2
    pair_block = ((batch, PITCHED_ROWS, LANES), lambda cb, s: (0, 0, cb))
    pair_spec = pl.BlockSpec(*pair_block, pipeline_mode=pl.Buffered(1))
    y_conv = pl.pallas_call(
        _hy_conv_kernel,
        grid=(D_HYENA // LANES, FFT_STEPS),
        in_specs=[pl.BlockSpec(*pair_block),
                  pl.BlockSpec((FFT_K1_PER_STEP, 2 * FFT_N2, LANES), lambda cb, s: (s, 0, cb)),
                  _const_spec((2 * FFT_N1, FFT_N1)), step_mat_spec,
                  _const_spec((FFT_N1, 2 * FFT_N1))],
        out_specs=pair_spec,
        out_shape=pitched_shape,
        scratch_shapes=[pltpu.VMEM((n_pairs, FFT_N2 * FFT_PITCH, LANES), F32)],
        compiler_params=_cparams(("arbitrary", "arbitrary")),
        name="hy_conv",
    )(u_p, kf, mats["w1_data"], w2_all, mats["v1"])

    mix = pl.pallas_call(
        _merge_kernel,
        grid=(batch, n_tiles),
        in_specs=[tok(D_ATTN), tok(D_HYENA), pitched_spec, pitched_spec, tok(2 * d),
                  _const_spec((1, D_HYENA)), _const_spec((1, d)),
                  _const_spec((D_ATTN, d)), _const_spec((D_HYENA, d)), _const_spec((d, d))],
        out_specs=tok(d),
        out_shape=jax.ShapeDtypeStruct((batch, seq, d), BF16),
        compiler_params=_cparams(("arbitrary", "arbitrary")),
        name="merge",
    )(y_na, x0, u_p, y_conv, gates, row2(hy_skip[0]), row2(norm_mix_post[0]),
      w_o_na_b, w_o_hy_b, w_out_b)

    conv_w_c = ffn_conv_w[0]
    conv_b_c = row2(ffn_conv_b[0])
    mix_halo = 2 * HALO
    mix_blocks = TOKEN_TILE // mix_halo
    mix_prev = pl.BlockSpec((1, mix_halo, d), lambda b, i: (b, jnp.maximum(i * mix_blocks - 1, 0), 0))
    mix_next = pl.BlockSpec((1, mix_halo, d),
                            lambda b, i: (b, jnp.minimum((i + 1) * mix_blocks, seq // mix_halo - 1), 0))
    out = pl.pallas_call(
        _ffn_kernel,
        grid=(batch, n_tiles),
        in_specs=[tok(d), prev_spec(d), next_spec(d), tok(d), mix_prev, mix_next,
                  mod_spec, _const_spec((1, d)), _const_spec((1, d)),
                  _const_spec(w_up_c.shape), _const_spec(conv_w_c.shape), _const_spec(conv_b_c.shape),
                  _const_spec(w_dn_c.shape)],
        out_specs=tok(d),
        out_shape=jax.ShapeDtypeStruct((batch, seq, d), F32),
        scratch_shapes=[pltpu.VMEM((2, 2, FF_CHUNK // LANES, TOKEN_TILE + 2 * HALO, LANES), F32),
                        pltpu.VMEM((TOKEN_TILE, D_FF), BF16)],
        compiler_params=_cparams(("arbitrary", "arbitrary")),
        name="ffn",
    )(x, x, x, mix, mix, mix, mod_lat, row2(norm_ffn_pre[0]), row2(norm_ffn_post[0]),
      w_up_c, conv_w_c, conv_b_c, w_dn_c)
    return out
```

```python
import functools
import math

import jax
import jax.numpy as jnp
import numpy as np
from jax import lax
from jax.experimental import pallas as pl
from jax.experimental.pallas import tpu as pltpu

F32 = jnp.float32
BF16 = jnp.bfloat16

D_MODEL = 1024
N_HEADS = 8
HEAD_DIM = 64
D_ATTN = N_HEADS * HEAD_DIM
D_HYENA = 512
GRID_W = 64
WIN_ROWS = 8
WIN_COLS = 16
POS_BANDS = 16
POS_FEATS = 1 + 2 * POS_BANDS
FILTER_HIDDEN = 64
D_FF = 2816
N_MOD = 6
ROPE_BASE = 10000.0
RMS_EPS = 1e-6
NEG_BIAS = -1e30
LOG2_E = math.log2(math.e)

LANES = 128
VMEM_LIMIT_BYTES = 58 * 1024 * 1024

FFT_N1 = 64
FFT_N2 = 128
FFT_N = FFT_N1 * FFT_N2
FFT_HALF_N1 = FFT_N1 // 2
FFT_PITCH = 2 * FFT_N1 + 4
FFT_K1_PER_STEP = 16
FFT_UNROLL = 8
FFT_STEPS = FFT_N1 // FFT_K1_PER_STEP

TOKEN_TILE = 512
FILT_ROWS = 1024
SEQ_PITCH = FFT_N2 + 8
PITCHED_ROWS = FFT_HALF_N1 * SEQ_PITCH
TILE_GROUPS = TOKEN_TILE // FFT_N2
ATTN_ROWS_PER_STEP = 4
ATTN_Q = ATTN_ROWS_PER_STEP * GRID_W
ATTN_KEY_ROWS = 12
ATTN_KEYS = ATTN_KEY_ROWS * GRID_W
ATTN_KEY_CHUNK = 256
FF_CHUNK = 768
LATER_WEIGHT_SLABS = (32, 16, 32, 32, 32)
HALO = 8
FFN_ROW_BLOCK = 64


def _cparams(sem):
    return pltpu.CompilerParams(dimension_semantics=sem, vmem_limit_bytes=VMEM_LIMIT_BYTES)


@functools.lru_cache(maxsize=None)
def _rope_tables(seq):
    pos = np.arange(seq)
    row = pos // GRID_W
    col = pos % GRID_W
    n_pairs = HEAD_DIM // 4
    inv = ROPE_BASE ** (-np.arange(n_pairs, dtype=np.float64) / n_pairs)
    lane = np.arange(LANES) % HEAD_DIM
    p = np.where(lane[None, :] < HEAD_DIM // 2, row[:, None], col[:, None]).astype(np.float64)
    ang = p * inv[lane % n_pairs][None, :]
    sign = np.where((lane % (2 * n_pairs)) < n_pairs, -1.0, 1.0)
    return np.cos(ang).astype(np.float32), (np.sin(ang) * sign[None, :]).astype(np.float32)


@functools.lru_cache(maxsize=None)
def _filter_tables(seq):
    assert 2 * seq == FFT_N
    half, n2, n1 = np.meshgrid(np.arange(2), np.arange(FFT_N2), np.arange(FFT_HALF_N1), indexing="ij")
    n = FFT_N2 * (half * FFT_HALF_N1 + n1) + n2
    fwd = n < seq
    m = n - seq
    valid = fwd | (m >= 1)
    pos = np.where(valid, np.where(fwd, n, seq - m), 0).astype(np.float64).reshape(2, seq)
    t = pos / max(seq - 1, 1)
    bands = np.linspace(1e-4, POS_BANDS - 1, POS_BANDS)
    ang = (2.0 * math.pi / seq) * pos[..., None] * bands
    z = np.zeros((2, seq, 64), np.float64)
    z[..., 0] = t
    z[..., 1:1 + POS_BANDS] = np.cos(ang)
    z[..., 1 + POS_BANDS:POS_FEATS] = -np.sin(ang)
    aux = np.zeros((2, seq, 8), np.float64)
    aux[..., 0] = t
    aux[..., 1] = valid.reshape(2, seq)
    zp = z.reshape(2, seq // FILT_ROWS, 2, FILT_ROWS // 2, 64).transpose(0, 1, 3, 2, 4).reshape(2, seq // 2, 128)
    return zp.astype(np.float32), aux.astype(np.float32)


def _realify(m):
    return np.block([[m.real, -m.imag], [m.imag, m.real]])


@functools.lru_cache(maxsize=None)
def _fft_matrices():
    n1 = np.arange(FFT_N1)
    n2 = np.arange(FFT_N2)
    k1 = np.arange(FFT_N1)
    k2 = np.arange(FFT_N2)
    f1 = np.exp(-2j * np.pi * np.outer(k1, n1) / FFT_N1)
    w1_data = _realify(f1[:, :FFT_HALF_N1])
    w1_real = np.concatenate([f1.real, f1.imag], axis=0)
    v1 = _realify(np.conj(f1.T)[:FFT_HALF_N1, :] / FFT_N)
    f2 = np.exp(-2j * np.pi * np.outer(k2, n2) / FFT_N2)
    tw = np.exp(-2j * np.pi * np.outer(k1, n2) / FFT_N)
    w2 = np.stack([_realify(f2 * tw[a][None, :]) for a in range(FFT_N1)])
    return {k: v.astype(np.float32) for k, v in
            dict(w1_data=w1_data, w1_real=w1_real, v1=v1, w2=w2).items()}


N_BIAS_ROWS = 2 * WIN_ROWS - 1
N_BIAS_COLS = 2 * WIN_COLS - 1


def _attn_bias_row_index():
    rows = GRID_W
    groups = (0, 2, rows // ATTN_ROWS_PER_STEP - 1)
    dr = np.full((3, ATTN_ROWS_PER_STEP, ATTN_KEY_ROWS), -1, np.int32)
    for v, g in enumerate(groups):
        ws = min(max(ATTN_ROWS_PER_STEP * g - WIN_ROWS // 2, 0), rows - ATTN_KEY_ROWS)
        for i in range(ATTN_ROWS_PER_STEP):
            r = ATTN_ROWS_PER_STEP * g + i
            r_start = min(max(r - WIN_ROWS // 2, 0), rows - WIN_ROWS)
            for j in range(ATTN_KEY_ROWS):
                kr = ws + j
                if r_start <= kr < r_start + WIN_ROWS:
                    dr[v, i, j] = kr - r + (WIN_ROWS - 1)
    return dr


def _attn_bias_kernel(rpb_ref, w_ref, o_ref, wb_ref, t_ref):
    wb_ref[...] = w_ref[...].astype(BF16)
    head = pl.program_id(0)
    kc = lax.broadcasted_iota(jnp.int32, (GRID_W, LANES), 0)
    lane = lax.broadcasted_iota(jnp.int32, (GRID_W, LANES), 1)
    qc = lane % GRID_W
    c_start = jnp.clip(qc - WIN_COLS // 2, 0, GRID_W - WIN_COLS)
    col_in = (kc >= c_start) & (kc < c_start + WIN_COLS)
    dc = jnp.clip(kc - qc, 1 - WIN_COLS, WIN_COLS - 1) + (WIN_COLS - 1)
    base = head * (N_BIAS_ROWS * N_BIAS_COLS)
    for r in range(N_BIAS_ROWS):
        t = jnp.full((GRID_W, LANES), NEG_BIAS, F32)
        for cidx in range(N_BIAS_COLS):
            t = jnp.where(col_in & (dc == cidx), rpb_ref[base + r * N_BIAS_COLS + cidx] * LOG2_E, t)
        t_ref[r] = t
    dr = _attn_bias_row_index()
    low_half = lane < GRID_W
    masked = jnp.full((GRID_W, LANES), NEG_BIAS, F32)
    for v in range(3):
        for i in range(0, ATTN_ROWS_PER_STEP, 2):
            for j in range(ATTN_KEY_ROWS):
                lo = t_ref[int(dr[v, i, j])] if dr[v, i, j] >= 0 else masked
                hi = t_ref[int(dr[v, i + 1, j])] if dr[v, i + 1, j] >= 0 else masked
                o_ref[v, 0, j * GRID_W:(j + 1) * GRID_W, i * GRID_W:(i + 2) * GRID_W] = jnp.where(
                    low_half, lo, hi)


def _mod_kernel(c_ref, w_ref, b_ref, o_ref):
    c = c_ref[...]
    s = c * jax.nn.sigmoid(c)
    o_ref[...] = jnp.dot(s, w_ref[...], precision=lax.Precision.HIGHEST,
                         preferred_element_type=F32) + b_ref[...]


def _norm_modulate(x, gain, shift, scale):
    ms = jnp.mean(x * x, axis=-1, keepdims=True)
    y = x * lax.rsqrt(ms + RMS_EPS) * gain
    return y * (1.0 + scale) + shift


def _rope(t, cos, sin_signed):
    n_pairs = HEAD_DIM // 4
    lane = lax.broadcasted_iota(jnp.int32, t.shape, 1)
    first = (lane % (2 * n_pairs)) < n_pairs
    partner = jnp.where(first, pltpu.roll(t, LANES - n_pairs, 1), pltpu.roll(t, n_pairs, 1))
    return t * cos + partner * sin_signed


def _in_proj_kernel(x_ref, prev_ref, next_ref, mod_ref, g_ref, w_ref, b_ref, cos_ref, sin_ref, cw_ref, cb_ref,
                    *rest):
    n_cast = len(LATER_WEIGHT_SLABS)
    cast_in, rest = rest[:n_cast], rest[n_cast:]
    (qr_ref, qp_ref, kr_ref, v_ref, gt_ref, u_ref, x0_ref), rest = rest[:7], rest[7:]
    cast_out, (hy_ref,) = rest[:n_cast], rest[n_cast:]
    for src, dst in zip(cast_in, cast_out):
        dst[...] = src[...].astype(BF16)
    i = pl.program_id(1)
    n_tiles = pl.num_programs(1)
    xx = jnp.concatenate([prev_ref[0], x_ref[0], next_ref[0]], axis=0)
    h_ext = _norm_modulate(xx, g_ref[...], mod_ref[0, 0:1, :], mod_ref[0, 1:2, :]).astype(BF16)
    h = h_ext[HALO:HALO + TOKEN_TILE]
    cos = cos_ref[...]
    sin = sin_ref[...]

    def proj(lo, hi, rows=h):
        return jnp.dot(rows, w_ref[:, lo:hi], preferred_element_type=F32) + b_ref[:, lo:hi]

    q = proj(0, D_ATTN) * (HEAD_DIM ** -0.5 * LOG2_E)
    qp_ref[0] = q.astype(BF16)
    k = proj(D_ATTN, 2 * D_ATTN)
    for c in range(D_ATTN // LANES):
        lanes = slice(c * LANES, (c + 1) * LANES)
        qr_ref[0, :, lanes] = _rope(q[:, lanes], cos, sin).astype(BF16)
        kr_ref[0, :, lanes] = _rope(k[:, lanes], cos, sin).astype(BF16)
    v_t = proj(2 * D_ATTN, 3 * D_ATTN).T.astype(BF16)
    for c in range(TOKEN_TILE // ATTN_KEY_CHUNK):
        v_ref[0, c] = v_t[:, c * ATTN_KEY_CHUNK:(c + 1) * ATTN_KEY_CHUNK]
    hy_lo = 3 * D_ATTN
    gl_lo = hy_lo + 3 * D_HYENA
    for c in range(4):
        w = D_MODEL // 2
        gt_ref[0, :, c * w:(c + 1) * w] = jax.nn.sigmoid(
            proj(gl_lo + c * w, gl_lo + (c + 1) * w)).astype(BF16)

    slabs_per_part = D_HYENA // LANES
    for c in range(3):
        hy = proj(hy_lo + c * D_HYENA, hy_lo + (c + 1) * D_HYENA, h_ext)
        for s in range(slabs_per_part):
            hy_ref[c * slabs_per_part + s] = hy[:, s * LANES:(s + 1) * LANES]
    zero_row = jnp.zeros((1, LANES), F32)

    @pl.when(i == 0)
    def _():
        for s in range(3 * slabs_per_part):
            hy_ref[s, HALO - 1:HALO, :] = zero_row

    @pl.when(i == n_tiles - 1)
    def _():
        for s in range(3 * slabs_per_part):
            hy_ref[s, HALO + TOKEN_TILE:HALO + TOKEN_TILE + 1, :] = zero_row

    def conv(s, r0, rows):
        lanes = slice(s * LANES, (s + 1) * LANES)
        tap = lambda j: hy_ref[s, pl.ds(HALO - 1 + j + r0, rows, stride=1), :]
        return (tap(0) * cw_ref[0:1, lanes] + tap(1) * cw_ref[1:2, lanes] + tap(2) * cw_ref[2:3, lanes]
                + cb_ref[:, lanes])

    for s in range(slabs_per_part):
        lanes = slice(s * LANES, (s + 1) * LANES)
        for j in range(TILE_GROUPS):
            r0 = j * FFT_N2
            x0_ref[0, r0:r0 + FFT_N2, lanes] = conv(s, r0, FFT_N2).astype(BF16)
            u_ref[0, j * SEQ_PITCH:j * SEQ_PITCH + FFT_N2, lanes] = (
                conv(slabs_per_part + s, r0, FFT_N2) * conv(2 * slabs_per_part + s, r0, FFT_N2))
            u_ref[0, j * SEQ_PITCH + FFT_N2:(j + 1) * SEQ_PITCH, lanes] = jnp.zeros((SEQ_PITCH - FFT_N2, LANES), F32)


def _ctx_kv_kernel(x_ref, mod_ref, g_ref, w_ref, b_ref, k_ref, v_ref):
    h = _norm_modulate(x_ref[0], g_ref[...], mod_ref[0:1, :], mod_ref[1:2, :]).astype(BF16)
    kv = jnp.dot(h, w_ref[...], preferred_element_type=F32) + b_ref[...]
    k_ref[0] = kv[:, :D_ATTN].astype(BF16)
    v_ref[0] = kv[:, D_ATTN:].T.astype(BF16)


def _attn_window_start(g):
    return jnp.clip(ATTN_ROWS_PER_STEP * g - WIN_ROWS // 2, 0, GRID_W - ATTN_KEY_ROWS)


def _attn_kernel(qr_ref, qp_ref, k_ref, vt_ref, kc_ref, vct_ref, bias_ref, o_ref):
    g = pl.program_id(1)
    win = _attn_window_start(g)
    key0 = pl.multiple_of(win * GRID_W, ATTN_KEY_CHUNK)
    chunk0 = win // (ATTN_KEY_CHUNK // GRID_W)
    nt = (((1,), (1,)), ((), ()))
    quad_w = 4 * HEAD_DIM
    lane = lax.broadcasted_iota(jnp.int32, (1, quad_w), 1)
    zero = jnp.zeros((), BF16)
    def scores(head):
        quad, hh = divmod(head, 4)
        ql = slice(quad * quad_w, (quad + 1) * quad_w)
        mine = (lane >= hh * HEAD_DIM) & (lane < (hh + 1) * HEAD_DIM)
        s_nb = lax.dot_general(k_ref[0, pl.ds(key0, ATTN_KEYS), ql], jnp.where(mine, qr_ref[0, :, ql], zero), nt,
                               preferred_element_type=F32)
        s_cx = lax.dot_general(kc_ref[0, :, ql], jnp.where(mine, qp_ref[0, :, ql], zero), nt,
                               preferred_element_type=F32)
        return s_nb + bias_ref[0, head], s_cx

    def with_ones(v):
        return jnp.concatenate([v, jnp.ones((16, v.shape[1]), BF16)], axis=0)

    def probs(s_nb, s_cx):
        m = jnp.maximum(jnp.max(s_nb, axis=0, keepdims=True), jnp.max(s_cx, axis=0, keepdims=True))
        return jnp.exp2(s_nb - m).astype(BF16), jnp.exp2(s_cx - m).astype(BF16)

    def values(head, p_nb, p_cx):
        rows = slice(head * HEAD_DIM, (head + 1) * HEAD_DIM)
        v_win = jnp.concatenate([vt_ref[0, chunk0 + c, rows, :] for c in range(ATTN_KEYS // ATTN_KEY_CHUNK)],
                                axis=1)
        o = (jnp.dot(with_ones(v_win), p_nb, preferred_element_type=F32)
             + jnp.dot(with_ones(vct_ref[0, rows, :]), p_cx, preferred_element_type=F32))
        return o[:HEAD_DIM] / o[HEAD_DIM:HEAD_DIM + 1]

    outs = []
    s_q = {0: scores(0), 1: scores(1)}
    p_q = {0: probs(*s_q.pop(0))}
    for head in range(N_HEADS):
        if head + 2 < N_HEADS:
            s_q[head + 2] = scores(head + 2)
        if head + 1 < N_HEADS:
            p_q[head + 1] = probs(*s_q.pop(head + 1))
        outs.append(values(head, *p_q.pop(head)))
    o_ref[0] = jnp.concatenate(outs, axis=0).T.astype(BF16)


def _filt_kernel(z_ref, aux_ref, w1_ref, b1_ref, w2_ref, b2_ref, w3_ref, b3_ref, freq_ref, decay_ref,
                 o_ref, asum_ref):
    i = pl.program_id(1)
    hp = lax.Precision.HIGHEST
    h = jnp.sin(freq_ref[0:1, :] * (jnp.dot(z_ref[0], w1_ref[...], precision=hp,
                                            preferred_element_type=F32) + b1_ref[...]))
    h = jnp.sin(freq_ref[1:2, :] * (jnp.dot(h, w2_ref[...], precision=hp,
                                            preferred_element_type=F32) + b2_ref[...]))
    taps = jnp.concatenate([jnp.dot(h, w3_ref[part], precision=hp, preferred_element_type=F32)
                            for part in range(2)], axis=0) + b3_ref[...]
    t = aux_ref[0, :, 0:1]
    valid = aux_ref[0, :, 1:2] > 0.5
    k = jnp.where(valid, taps * jnp.exp(-t * jnp.abs(decay_ref[0])), 0.0)
    part = jnp.sum(jnp.abs(k), axis=0, keepdims=True)

    @pl.when(i == 0)
    def _():
        asum_ref[...] = jnp.zeros_like(asum_ref)

    asum_ref[0] += jnp.broadcast_to(part, asum_ref.shape[1:])
    o_ref[...] = k.reshape(o_ref.shape)


def _fft_stage1(load_group, w1_ref, a_ref):
    n_slabs = a_ref.shape[0]

    def body(n2, carry):
        x = load_group(n2)
        a = jnp.dot(w1_ref[...], x, preferred_element_type=F32)
        for s in range(n_slabs):
            a_ref[s, pl.ds(n2 * FFT_PITCH, 2 * FFT_N1, stride=1), :] = a[:, s * LANES:(s + 1) * LANES]
        return carry

    lax.fori_loop(0, FFT_N2, body, 0, unroll=FFT_UNROLL)


def _fft_load_k1(a_ref, k1):
    parts = []
    for off in (0, FFT_N1):
        parts.append(jnp.concatenate(
            [a_ref[s, pl.ds(k1 + off, FFT_N2, stride=FFT_PITCH), :] for s in range(a_ref.shape[0])], axis=1))
    return jnp.concatenate(parts, axis=0)


def _filt_fft_kernel(k_ref, asum_ref, w1_ref, w2_ref, o_ref, a_ref):
    step = pl.program_id(1)

    @pl.when(step == 0)
    def _():
        norm = asum_ref[0, 0:1, :] + asum_ref[1, 0:1, :]

        def load_group(n2):
            return (k_ref[pl.ds(pl.multiple_of(n2 * FFT_N1, FFT_N1), FFT_N1), :] / norm).astype(BF16)
        _fft_stage1(load_group, w1_ref, a_ref)

    for j in range(FFT_K1_PER_STEP):
        b = _fft_load_k1(a_ref, step * FFT_K1_PER_STEP + j).astype(BF16)
        o_ref[j] = jnp.dot(w2_ref[j], b, preferred_element_type=F32).astype(BF16)


def _hy_conv_kernel(u_ref, kf_ref, w1_ref, w2_ref, v1_ref, o_ref, a_ref):
    step = pl.program_id(1)
    n_slabs = a_ref.shape[0]

    def tokens_of(member, s, n2):
        return (2 * s + member, pl.ds(n2, FFT_HALF_N1, stride=SEQ_PITCH), slice(None))

    @pl.when(step == 0)
    def _():
        for sample in range(o_ref.shape[0]):
            for grp in range(FFT_HALF_N1):
                o_ref[sample, grp * SEQ_PITCH + FFT_N2:(grp + 1) * SEQ_PITCH, :] = jnp.zeros(
                    (SEQ_PITCH - FFT_N2, LANES), F32)

        def load_group(n2):
            return jnp.concatenate(
                [jnp.concatenate([u_ref[tokens_of(member, s, n2)] for s in range(n_slabs)], axis=1)
                 for member in range(2)], axis=0).astype(BF16)
        _fft_stage1(load_group, w1_ref, a_ref)

    for j in range(FFT_K1_PER_STEP):
        k1 = step * FFT_K1_PER_STEP + j
        b = _fft_load_k1(a_ref, k1).astype(BF16)
        x = jnp.dot(w2_ref[j], b, preferred_element_type=F32)
        kf = kf_ref[j].astype(F32)
        kf = jnp.concatenate([kf] * n_slabs, axis=1)
        xr, xi = x[:FFT_N2], x[FFT_N2:]
        kr, ki = kf[:FFT_N2], kf[FFT_N2:]
        y = jnp.concatenate([xr * kr - xi * ki, xr * ki + xi * kr], axis=0).astype(BF16)
        d = lax.dot_general(w2_ref[j], y, (((0,), (0,)), ((), ())),
                            preferred_element_type=F32)
        for s in range(n_slabs):
            lanes = slice(s * LANES, (s + 1) * LANES)
            a_ref[s, pl.ds(k1, FFT_N2, stride=FFT_PITCH), :] = d[:FFT_N2, lanes]
            a_ref[s, pl.ds(k1 + FFT_N1, FFT_N2, stride=FFT_PITCH), :] = d[FFT_N2:, lanes]

    @pl.when(step == FFT_STEPS - 1)
    def _():
        def body(n2, carry):
            d = jnp.concatenate([a_ref[s, pl.ds(n2 * FFT_PITCH, 2 * FFT_N1, stride=1), :] for s in range(n_slabs)],
                                axis=1)
            y = jnp.dot(v1_ref[...], d.astype(BF16), preferred_element_type=F32)
            for s in range(n_slabs):
                for member in range(2):
                    o_ref[tokens_of(member, s, n2)] = y[member * FFT_HALF_N1:(member + 1) * FFT_HALF_N1,
                                                        s * LANES:(s + 1) * LANES]
            return carry

        lax.fori_loop(0, FFT_N2, body, 0, unroll=FFT_UNROLL)


def _from_pitched(ref):
    return jnp.concatenate([ref[0, j * SEQ_PITCH:j * SEQ_PITCH + FFT_N2, :] for j in range(TILE_GROUPS)], axis=0)


def _merge_kernel(yna_ref, x0_ref, u_ref, yc_ref, gt_ref, skip_ref, gpost_ref, wna_ref, why_ref, wout_ref, o_ref):
    y_hy = (x0_ref[0].astype(F32) * (_from_pitched(yc_ref) + _from_pitched(u_ref) * skip_ref[...])).astype(BF16)
    a = jnp.dot(yna_ref[0], wna_ref[...], preferred_element_type=F32)
    b = jnp.dot(y_hy, why_ref[...], preferred_element_type=F32)
    g_na = gt_ref[0, :, :D_MODEL].astype(F32)
    g_hy = gt_ref[0, :, D_MODEL:].astype(F32)
    m = (g_na * a + g_hy * b).astype(BF16)
    o = jnp.dot(m, wout_ref[...], preferred_element_type=F32)
    ms = jnp.mean(o * o, axis=-1, keepdims=True)
    o_ref[0] = (o * lax.rsqrt(ms + RMS_EPS) * gpost_ref[...]).astype(BF16)


def _gelu_tanh(a):
    return 0.5 * a * (1.0 + jnp.tanh(math.sqrt(2.0 / math.pi) * (a + 0.044715 * (a * a * a))))


def _ffn_kernel(x_ref, prev_ref, next_ref, mix_ref, mix_prev_ref, mix_next_ref, mod_ref, gpre_ref, gpost_ref,
                wup_ref, cw_ref, cb_ref, wdn_ref, o_ref, u_ref, act_ref):
    i = pl.program_id(1)
    n_tiles = pl.num_programs(1)
    mix_halo = mix_prev_ref.shape[1]
    mix = jnp.concatenate([mix_prev_ref[0, mix_halo - HALO:, :], mix_ref[0], mix_next_ref[0, :HALO, :]], axis=0)
    xx = (jnp.concatenate([prev_ref[0], x_ref[0], next_ref[0]], axis=0)
          + mod_ref[0, 2:3, :] * mix.astype(F32))
    x = xx[HALO:HALO + TOKEN_TILE]
    h = _norm_modulate(xx, gpre_ref[...], mod_ref[0, 3:4, :], mod_ref[0, 4:5, :])
    row = lax.broadcasted_iota(jnp.int32, (xx.shape[0], 1), 0)
    inside = ((row >= HALO) | (i > 0)) & ((row < HALO + TOKEN_TILE) | (i < n_tiles - 1))
    h = jnp.where(inside, h, 0.0).astype(BF16)
    starts = list(range(0, D_FF, FF_CHUNK))
    widths = [min(FF_CHUNK, D_FF - lo) for lo in starts]

    def conv(buf, half, s, lo, r0):
        lanes = slice(half * D_FF + lo, half * D_FF + lo + LANES)
        tap = lambda j: u_ref[buf, half, s, pl.ds(HALO - 1 + j + r0, FFN_ROW_BLOCK, stride=1), :]
        return (tap(0) * cw_ref[0:1, lanes] + tap(1) * cw_ref[1:2, lanes] + tap(2) * cw_ref[2:3, lanes]
                + cb_ref[:, lanes])

    def up_project(c):
        for half in range(2):
            lo = half * D_FF + starts[c]
            u = jnp.dot(h, wup_ref[:, lo:lo + widths[c]], preferred_element_type=F32)
            for s in range(widths[c] // LANES):
                u_ref[c % 2, half, s] = u[:, s * LANES:(s + 1) * LANES]

    up_project(0)
    for c in range(len(starts)):
        buf = c % 2
        if c + 1 < len(starts):
            up_project(c + 1)
        for s in range(widths[c] // LANES):
            lo = starts[c] + s * LANES
            for r0 in range(0, TOKEN_TILE, FFN_ROW_BLOCK):
                act_ref[r0:r0 + FFN_ROW_BLOCK, lo:lo + LANES] = (
                    _gelu_tanh(conv(buf, 0, s, lo, r0)) * conv(buf, 1, s, lo, r0)).astype(BF16)
    y = jnp.dot(act_ref[...], wdn_ref[...], preferred_element_type=F32)
    ms = jnp.mean(y * y, axis=-1, keepdims=True)
    y = y * lax.rsqrt(ms + RMS_EPS) * gpost_ref[...]
    o_ref[0] = x + mod_ref[0, 5:6, :] * y


def _const_spec(shape):
    nd = len(shape)
    return pl.BlockSpec(shape, lambda *_: (0,) * nd, pipeline_mode=pl.Buffered(1))


def kernel(x, c, ctx, c_ctx, w_mod, b_mod, norm_mix_pre, norm_mix_post, norm_ffn_pre, norm_ffn_post, w_in, b_in, na_rpb, hy_conv_w, hy_conv_b, hy_filt_w1, hy_filt_b1, hy_filt_w2, hy_filt_b2, hy_filt_w3, hy_filt_b3, hy_sin_freq, hy_decay, hy_skip, w_o_na, w_o_hy, w_out, ffn_w_up, ffn_conv_w, ffn_conv_b, ffn_w_down):
    batch, seq, d = x.shape
    n_ctx = ctx.shape[1]
    assert d == D_MODEL and 2 * seq == FFT_N and seq == GRID_W * GRID_W and batch % 2 == 0
    assert w_mod.shape[0] == 1, "single-layer block"
    n_tiles = seq // TOKEN_TILE
    d_in = w_in.shape[2]
    row2 = lambda a: a.reshape(1, -1)

    c_all = jnp.zeros((8, d), F32).at[:batch].set(c).at[batch].set(c_ctx)
    mod_n = 1024
    mod = pl.pallas_call(
        _mod_kernel,
        grid=(N_MOD * d // mod_n,),
        in_specs=[_const_spec((8, d)),
                  pl.BlockSpec((d, mod_n), lambda j: (0, j)),
                  pl.BlockSpec((1, mod_n), lambda j: (0, j))],
        out_specs=pl.BlockSpec((8, mod_n), lambda j: (0, j)),
        out_shape=jax.ShapeDtypeStruct((8, N_MOD * d), F32),
        compiler_params=_cparams(("arbitrary",)),
        name="mod",
    )(c_all, w_mod[0], row2(b_mod[0]))
    mod_lat = jnp.pad(mod[:batch].reshape(batch, N_MOD, d), ((0, 0), (0, 8 - N_MOD), (0, 0)))
    mod_ctx = jnp.pad(mod[batch].reshape(N_MOD, d), ((0, 8 - N_MOD), (0, 0)))

    w_rows = d // N_HEADS
    bias, w_in_b = pl.pallas_call(
        _attn_bias_kernel,
        grid=(N_HEADS,),
        in_specs=[pl.BlockSpec(memory_space=pltpu.SMEM), pl.BlockSpec((w_rows, d_in), lambda h: (h, 0))],
        out_specs=[pl.BlockSpec((3, 1, ATTN_KEYS, ATTN_Q), lambda h: (0, h, 0, 0)),
                   pl.BlockSpec((w_rows, d_in), lambda h: (h, 0))],
        out_shape=[jax.ShapeDtypeStruct((3, N_HEADS, ATTN_KEYS, ATTN_Q), F32),
                   jax.ShapeDtypeStruct((d, d_in), BF16)],
        scratch_shapes=[pltpu.VMEM((N_BIAS_ROWS, GRID_W, LANES), F32)],
        compiler_params=_cparams(("arbitrary",)),
        name="attn_bias",
    )(na_rpb[0].reshape(-1), w_in[0])
    b_in_r = row2(b_in[0])
    g_mix_pre = row2(norm_mix_pre[0])

    k_ctx, v_ctx = pl.pallas_call(
        _ctx_kv_kernel,
        grid=(batch,),
        in_specs=[pl.BlockSpec((1, n_ctx, d), lambda b: (b, 0, 0)),
                  _const_spec((8, d)), _const_spec((1, d)),
                  _const_spec((d, 2 * D_ATTN)), _const_spec((1, 2 * D_ATTN))],
        out_specs=[pl.BlockSpec((1, n_ctx, D_ATTN), lambda b: (b, 0, 0)),
                   pl.BlockSpec((1, D_ATTN, n_ctx), lambda b: (b, 0, 0))],
        out_shape=[jax.ShapeDtypeStruct((batch, n_ctx, D_ATTN), BF16),
                   jax.ShapeDtypeStruct((batch, D_ATTN, n_ctx), BF16)],
        compiler_params=_cparams(("arbitrary",)),
        name="ctx_kv",
    )(ctx, mod_ctx, g_mix_pre, w_in_b[:, D_ATTN:3 * D_ATTN], b_in_r[:, D_ATTN:3 * D_ATTN])

    cos_t, sin_t = _rope_tables(seq)
    tok = lambda w: pl.BlockSpec((1, TOKEN_TILE, w), lambda b, i: (b, i, 0))
    mod_spec = pl.BlockSpec((1, 8, d), lambda b, i: (b, 0, 0))
    rope_spec = pl.BlockSpec((TOKEN_TILE, LANES), lambda b, i: (i, 0))
    halo_blocks = TOKEN_TILE // HALO
    n_halo_blocks = seq // HALO
    prev_spec = lambda w: pl.BlockSpec((1, HALO, w), lambda b, i: (b, jnp.maximum(i * halo_blocks - 1, 0), 0))
    next_spec = lambda w: pl.BlockSpec(
        (1, HALO, w), lambda b, i: (b, jnp.minimum((i + 1) * halo_blocks, n_halo_blocks - 1), 0))
    pitched_spec = pl.BlockSpec((1, TILE_GROUPS * SEQ_PITCH, D_HYENA), lambda b, i: (b, i, 0))
    pitched_shape = jax.ShapeDtypeStruct((batch, PITCHED_ROWS, D_HYENA), F32)
    later_weights = [ffn_w_up[0], ffn_w_down[0], w_out[0], w_o_na[0], w_o_hy[0]]
    slab_specs = [
        pl.BlockSpec((w.shape[0] // n, w.shape[1]), lambda b, i, n=n: (jnp.minimum(b * n_tiles + i, n - 1), 0))
        for w, n in zip(later_weights, LATER_WEIGHT_SLABS)]
    q_rot, q_plain, k_rot, v_lat, gates, u_p, x0, w_up_c, w_dn_c, w_out_b, w_o_na_b, w_o_hy_b = pl.pallas_call(
        _in_proj_kernel,
        grid=(batch, n_tiles),
        in_specs=[tok(d), prev_spec(d), next_spec(d), mod_spec, _const_spec((1, d)), _const_spec((d, d_in)),
                  _const_spec((1, d_in)), rope_spec, rope_spec,
                  _const_spec((3, 3 * D_HYENA)), _const_spec((1, 3 * D_HYENA))] + slab_specs,
        out_specs=[tok(D_ATTN)] * 3
        + [pl.BlockSpec((1, TOKEN_TILE // ATTN_KEY_CHUNK, D_ATTN, ATTN_KEY_CHUNK), lambda b, i: (b, i, 0, 0)),
           tok(2 * d), pitched_spec, tok(D_HYENA)] + slab_specs,
        out_shape=[jax.ShapeDtypeStruct((batch, seq, D_ATTN), BF16)] * 3
        + [jax.ShapeDtypeStruct((batch, seq // ATTN_KEY_CHUNK, D_ATTN, ATTN_KEY_CHUNK), BF16),
           jax.ShapeDtypeStruct((batch, seq, 2 * d), BF16), pitched_shape,
           jax.ShapeDtypeStruct((batch, seq, D_HYENA), BF16)]
        + [jax.ShapeDtypeStruct(w.shape, BF16) for w in later_weights],
        scratch_shapes=[pltpu.VMEM((3 * D_HYENA // LANES, TOKEN_TILE + 2 * HALO, LANES), F32)],
        compiler_params=_cparams(("arbitrary", "arbitrary")),
        name="in_proj",
    )(x, x, x, mod_lat, g_mix_pre, w_in_b, b_in_r, jnp.asarray(cos_t), jnp.asarray(sin_t),
      hy_conv_w[0], row2(hy_conv_b[0]), *later_weights)

    n_groups = GRID_W // ATTN_ROWS_PER_STEP
    q_spec = pl.BlockSpec((1, ATTN_Q, D_ATTN), lambda b, g: (b, g, 0))
    full = lambda n: pl.BlockSpec((1, n, D_ATTN), lambda b, g: (b, 0, 0))
    bias_spec = pl.BlockSpec(
        (1, N_HEADS, ATTN_KEYS, ATTN_Q),
        lambda b, g: ((g > 0).astype(jnp.int32) + (g == n_groups - 1).astype(jnp.int32), 0, 0, 0))
    vt_spec = pl.BlockSpec((1, seq // ATTN_KEY_CHUNK, D_ATTN, ATTN_KEY_CHUNK), lambda b, g: (b, 0, 0, 0))
    vct_spec = pl.BlockSpec((1, D_ATTN, n_ctx), lambda b, g: (b, 0, 0))
    y_na = pl.pallas_call(
        _attn_kernel,
        grid=(batch, n_groups),
        in_specs=[q_spec, q_spec, full(seq), vt_spec, full(n_ctx), vct_spec, bias_spec],
        out_specs=q_spec,
        out_shape=jax.ShapeDtypeStruct((batch, seq, D_ATTN), BF16),
        compiler_params=_cparams(("arbitrary", "arbitrary")),
        name="attn",
    )(q_rot, q_plain, k_rot, v_lat, k_ctx, v_ctx, bias)

    z_t, aux_t = _filter_tables(seq)
    filt_n2 = FILT_ROWS // FFT_HALF_N1
    hid = FILTER_HIDDEN
    w1_pad = jnp.pad(hy_filt_w1[0], ((0, hid - POS_FEATS), (0, 0)))
    block_diag = lambda w: jnp.zeros((2 * hid, 2 * hid), F32).at[:hid, :hid].set(w).at[hid:, hid:].set(w)
    twice = lambda v: jnp.tile(v, (1, 2))
    w3 = hy_filt_w3[0]
    w3_parts = jnp.stack([jnp.concatenate([w3, jnp.zeros_like(w3)], axis=0),
                          jnp.concatenate([jnp.zeros_like(w3), w3], axis=0)])
    k_circ, k_asum = pl.pallas_call(
        _filt_kernel,
        grid=(2, seq // FILT_ROWS),
        in_specs=[pl.BlockSpec((1, FILT_ROWS // 2, 2 * hid), lambda hf, i: (hf, i, 0)),
                  pl.BlockSpec((1, FILT_ROWS, aux_t.shape[2]), lambda hf, i: (hf, i, 0)),
                  _const_spec((2 * hid, 2 * hid)), _const_spec((1, 2 * hid)),
                  _const_spec((2 * hid, 2 * hid)), _const_spec((1, 2 * hid)),
                  pl.BlockSpec((2, 2 * hid, D_HYENA), lambda hf, i: (0, 0, hf)),
                  pl.BlockSpec((1, D_HYENA), lambda hf, i: (0, hf)),
                  _const_spec((2, 2 * hid)),
                  pl.BlockSpec((1, 1, D_HYENA), lambda hf, i: (hf, 0, 0))],
        out_specs=[pl.BlockSpec((filt_n2, FFT_HALF_N1, D_HYENA), lambda hf, i: (i, hf, 0)),
                   pl.BlockSpec((1, 8, D_HYENA), lambda hf, i: (hf, 0, 0))],
        out_shape=[jax.ShapeDtypeStruct((FFT_N2, FFT_N1, D_HYENA), F32),
                   jax.ShapeDtypeStruct((2, 8, D_HYENA), F32)],
        compiler_params=_cparams(("arbitrary", "arbitrary")),
        name="filt",
    )(jnp.asarray(z_t), jnp.asarray(aux_t), block_diag(w1_pad), twice(row2(hy_filt_b1[0])),
      block_diag(hy_filt_w2[0]), twice(row2(hy_filt_b2[0])), w3_parts, row2(hy_filt_b3[0]),
      twice(hy_sin_freq[0]), hy_decay[0].reshape(2, 1, D_HYENA))
    k_circ = k_circ.reshape(FFT_N, D_HYENA)

    mats = _fft_matrices()
    mats = {k: jnp.asarray(v).astype(BF16) for k, v in mats.items()}
    w2_all = mats["w2"]
    step_mat_spec = pl.BlockSpec((FFT_K1_PER_STEP, 2 * FFT_N2, 2 * FFT_N2), lambda cb, s: (s, 0, 0))
    filt_slabs = 2
    kf = pl.pallas_call(
        _filt_fft_kernel,
        grid=(D_HYENA // (filt_slabs * LANES), FFT_STEPS),
        in_specs=[pl.BlockSpec((FFT_N, filt_slabs * LANES), lambda cb, s: (0, cb)),
                  pl.BlockSpec((2, 8, filt_slabs * LANES), lambda cb, s: (0, 0, cb)),
                  _const_spec((2 * FFT_N1, FFT_N1)), step_mat_spec],
        out_specs=pl.BlockSpec((FFT_K1_PER_STEP, 2 * FFT_N2, filt_slabs * LANES), lambda cb, s: (s, 0, cb)),
        out_shape=jax.ShapeDtypeStruct((FFT_N1, 2 * FFT_N2, D_HYENA), BF16),
        scratch_shapes=[pltpu.VMEM((filt_slabs, FFT_N2 * FFT_PITCH, LANES), F32)],
        compiler_params=_cparams(("arbitrary", "arbitrary")),
        name="filt_fft",
    )(k_circ, k_asum, mats["w1_real"], w2_all)

    n_pairs = batch // 2
    pair_block = ((batch, PITCHED_ROWS, LANES), lambda cb, s: (0, 0, cb))
    pair_spec = pl.BlockSpec(*pair_block, pipeline_mode=pl.Buffered(1))
    y_conv = pl.pallas_call(
        _hy_conv_kernel,
        grid=(D_HYENA // LANES, FFT_STEPS),
        in_specs=[pl.BlockSpec(*pair_block),
                  pl.BlockSpec((FFT_K1_PER_STEP, 2 * FFT_N2, LANES), lambda cb, s: (s, 0, cb)),
                  _const_spec((2 * FFT_N1, FFT_N1)), step_mat_spec,
                  _const_spec((FFT_N1, 2 * FFT_N1))],
        out_specs=pair_spec,
        out_shape=pitched_shape,
        scratch_shapes=[pltpu.VMEM((n_pairs, FFT_N2 * FFT_PITCH, LANES), F32)],
        compiler_params=_cparams(("arbitrary", "arbitrary")),
        name="hy_conv",
    )(u_p, kf, mats["w1_data"], w2_all, mats["v1"])

    mix = pl.pallas_call(
        _merge_kernel,
        grid=(batch, n_tiles),
        in_specs=[tok(D_ATTN), tok(D_HYENA), pitched_spec, pitched_spec, tok(2 * d),
                  _const_spec((1, D_HYENA)), _const_spec((1, d)),
                  _const_spec((D_ATTN, d)), _const_spec((D_HYENA, d)), _const_spec((d, d))],
        out_specs=tok(d),
        out_shape=jax.ShapeDtypeStruct((batch, seq, d), BF16),
        compiler_params=_cparams(("arbitrary", "arbitrary")),
        name="merge",
    )(y_na, x0, u_p, y_conv, gates, row2(hy_skip[0]), row2(norm_mix_post[0]),
      w_o_na_b, w_o_hy_b, w_out_b)

    conv_w_c = ffn_conv_w[0]
    conv_b_c = row2(ffn_conv_b[0])
    mix_halo = 2 * HALO
    mix_blocks = TOKEN_TILE // mix_halo
    mix_prev = pl.BlockSpec((1, mix_halo, d), lambda b, i: (b, jnp.maximum(i * mix_blocks - 1, 0), 0))
    mix_next = pl.BlockSpec((1, mix_halo, d),
                            lambda b, i: (b, jnp.minimum((i + 1) * mix_blocks, seq // mix_halo - 1), 0))
    out = pl.pallas_call(
        _ffn_kernel,
        grid=(batch, n_tiles),
        in_specs=[tok(d), prev_spec(d), next_spec(d), tok(d), mix_prev, mix_next,
                  mod_spec, _const_spec((1, d)), _const_spec((1, d)),
                  _const_spec(w_up_c.shape), _const_spec(conv_w_c.shape), _const_spec(conv_b_c.shape),
                  _const_spec(w_dn_c.shape)],
        out_specs=tok(d),
        out_shape=jax.ShapeDtypeStruct((batch, seq, d), F32),
        scratch_shapes=[pltpu.VMEM((2, 2, FF_CHUNK // LANES, TOKEN_TILE + 2 * HALO, LANES), F32),
                        pltpu.VMEM((TOKEN_TILE, D_FF), BF16)],
        compiler_params=_cparams(("arbitrary", "arbitrary")),
        name="ffn",
    )(x, x, x, mix, mix, mix, mod_lat, row2(norm_ffn_pre[0]), row2(norm_ffn_post[0]),
      w_up_c, conv_w_c, conv_b_c, w_dn_c)
    return out
```

```python
import functools
import math

import jax
import jax.numpy as jnp
import numpy as np
from jax import lax
from jax.experimental import pallas as pl
from jax.experimental.pallas import tpu as pltpu

F32 = jnp.float32
BF16 = jnp.bfloat16

D_MODEL = 1024
N_HEADS = 8
HEAD_DIM = 64
D_ATTN = N_HEADS * HEAD_DIM
D_HYENA = 512
GRID_W = 64
WIN_ROWS = 8
WIN_COLS = 16
POS_BANDS = 16
POS_FEATS = 1 + 2 * POS_BANDS
FILTER_HIDDEN = 64
D_FF = 2816
N_MOD = 6
ROPE_BASE = 10000.0
RMS_EPS = 1e-6
NEG_BIAS = -1e30
LOG2_E = math.log2(math.e)

LANES = 128
VMEM_LIMIT_BYTES = 58 * 1024 * 1024

FFT_N1 = 64
FFT_N2 = 128
FFT_N = FFT_N1 * FFT_N2
FFT_HALF_N1 = FFT_N1 // 2
FFT_PITCH = 2 * FFT_N1 + 4
FFT_K1_PER_STEP = 32
FFT_UNROLL = 8
FFT_STEPS = FFT_N1 // FFT_K1_PER_STEP

TOKEN_TILE = 512
FILT_ROWS = 1024
SEQ_PITCH = FFT_N2 + 8
PITCHED_ROWS = FFT_HALF_N1 * SEQ_PITCH
TILE_GROUPS = TOKEN_TILE // FFT_N2
ATTN_ROWS_PER_STEP = 4
ATTN_Q = ATTN_ROWS_PER_STEP * GRID_W
ATTN_KEY_ROWS = 12
ATTN_KEYS = ATTN_KEY_ROWS * GRID_W
ATTN_KEY_CHUNK = 256
FF_CHUNK = 768
LATER_WEIGHT_SLABS = (32, 16, 32, 32, 32)
HALO = 8
FFN_ROW_BLOCK = 64


def _cparams(sem):
    return pltpu.CompilerParams(dimension_semantics=sem, vmem_limit_bytes=VMEM_LIMIT_BYTES)


@functools.lru_cache(maxsize=None)
def _rope_tables(seq):
    pos = np.arange(seq)
    row = pos // GRID_W
    col = pos % GRID_W
    n_pairs = HEAD_DIM // 4
    inv = ROPE_BASE ** (-np.arange(n_pairs, dtype=np.float64) / n_pairs)
    lane = np.arange(LANES) % HEAD_DIM
    p = np.where(lane[None, :] < HEAD_DIM // 2, row[:, None], col[:, None]).astype(np.float64)
    ang = p * inv[lane % n_pairs][None, :]
    sign = np.where((lane % (2 * n_pairs)) < n_pairs, -1.0, 1.0)
    return np.cos(ang).astype(np.float32), (np.sin(ang) * sign[None, :]).astype(np.float32)


@functools.lru_cache(maxsize=None)
def _filter_tables(seq):
    assert 2 * seq == FFT_N
    half, n2, n1 = np.meshgrid(np.arange(2), np.arange(FFT_N2), np.arange(FFT_HALF_N1), indexing="ij")
    n = FFT_N2 * (half * FFT_HALF_N1 + n1) + n2
    fwd = n < seq
    m = n - seq
    valid = fwd | (m >= 1)
    pos = np.where(valid, np.where(fwd, n, seq - m), 0).astype(np.float64).reshape(2, seq)
    t = pos / max(seq - 1, 1)
    bands = np.linspace(1e-4, POS_BANDS - 1, POS_BANDS)
    ang = (2.0 * math.pi / seq) * pos[..., None] * bands
    z = np.zeros((2, seq, 64), np.float64)
    z[..., 0] = t
    z[..., 1:1 + POS_BANDS] = np.cos(ang)
    z[..., 1 + POS_BANDS:POS_FEATS] = -np.sin(ang)
    aux = np.zeros((2, seq, 8), np.float64)
    aux[..., 0] = t
    aux[..., 1] = valid.reshape(2, seq)
    zp = z.reshape(2, seq // FILT_ROWS, 2, FILT_ROWS // 2, 64).transpose(0, 1, 3, 2, 4).reshape(2, seq // 2, 128)
    return zp.astype(np.float32), aux.astype(np.float32)


def _realify(m):
    return np.block([[m.real, -m.imag], [m.imag, m.real]])


@functools.lru_cache(maxsize=None)
def _fft_matrices():
    n1 = np.arange(FFT_N1)
    n2 = np.arange(FFT_N2)
    k1 = np.arange(FFT_N1)
    k2 = np.arange(FFT_N2)
    f1 = np.exp(-2j * np.pi * np.outer(k1, n1) / FFT_N1)
    w1_data = _realify(f1[:, :FFT_HALF_N1])
    w1_real = np.concatenate([f1.real, f1.imag], axis=0)
    v1 = _realify(np.conj(f1.T)[:FFT_HALF_N1, :] / FFT_N)
    f2 = np.exp(-2j * np.pi * np.outer(k2, n2) / FFT_N2)
    tw = np.exp(-2j * np.pi * np.outer(k1, n2) / FFT_N)
    w2 = np.stack([_realify(f2 * tw[a][None, :]) for a in range(FFT_N1)])
    return {k: v.astype(np.float32) for k, v in
            dict(w1_data=w1_data, w1_real=w1_real, v1=v1, w2=w2).items()}


N_BIAS_ROWS = 2 * WIN_ROWS - 1
N_BIAS_COLS = 2 * WIN_COLS - 1


def _attn_bias_row_index():
    rows = GRID_W
    groups = (0, 2, rows // ATTN_ROWS_PER_STEP - 1)
    dr = np.full((3, ATTN_ROWS_PER_STEP, ATTN_KEY_ROWS), -1, np.int32)
    for v, g in enumerate(groups):
        ws = min(max(ATTN_ROWS_PER_STEP * g - WIN_ROWS // 2, 0), rows - ATTN_KEY_ROWS)
        for i in range(ATTN_ROWS_PER_STEP):
            r = ATTN_ROWS_PER_STEP * g + i
            r_start = min(max(r - WIN_ROWS // 2, 0), rows - WIN_ROWS)
            for j in range(ATTN_KEY_ROWS):
                kr = ws + j
                if r_start <= kr < r_start + WIN_ROWS:
                    dr[v, i, j] = kr - r + (WIN_ROWS - 1)
    return dr


def _attn_bias_kernel(rpb_ref, w_ref, o_ref, wb_ref, t_ref):
    wb_ref[...] = w_ref[...].astype(BF16)
    head = pl.program_id(0)
    kc = lax.broadcasted_iota(jnp.int32, (GRID_W, LANES), 0)
    lane = lax.broadcasted_iota(jnp.int32, (GRID_W, LANES), 1)
    qc = lane % GRID_W
    c_start = jnp.clip(qc - WIN_COLS // 2, 0, GRID_W - WIN_COLS)
    col_in = (kc >= c_start) & (kc < c_start + WIN_COLS)
    dc = jnp.clip(kc - qc, 1 - WIN_COLS, WIN_COLS - 1) + (WIN_COLS - 1)
    base = head * (N_BIAS_ROWS * N_BIAS_COLS)
    for r in range(N_BIAS_ROWS):
        t = jnp.full((GRID_W, LANES), NEG_BIAS, F32)
        for cidx in range(N_BIAS_COLS):
            t = jnp.where(col_in & (dc == cidx), rpb_ref[base + r * N_BIAS_COLS + cidx] * LOG2_E, t)
        t_ref[r] = t
    dr = _attn_bias_row_index()
    low_half = lane < GRID_W
    masked = jnp.full((GRID_W, LANES), NEG_BIAS, F32)
    for v in range(3):
        for i in range(0, ATTN_ROWS_PER_STEP, 2):
            for j in range(ATTN_KEY_ROWS):
                lo = t_ref[int(dr[v, i, j])] if dr[v, i, j] >= 0 else masked
                hi = t_ref[int(dr[v, i + 1, j])] if dr[v, i + 1, j] >= 0 else masked
                o_ref[v, 0, j * GRID_W:(j + 1) * GRID_W, i * GRID_W:(i + 2) * GRID_W] = jnp.where(
                    low_half, lo, hi)


def _mod_kernel(c_ref, w_ref, b_ref, o_ref):
    c = c_ref[...]
    s = c * jax.nn.sigmoid(c)
    o_ref[...] = jnp.dot(s, w_ref[...], precision=lax.Precision.HIGHEST,
                         preferred_element_type=F32) + b_ref[...]


def _norm_modulate(x, gain, shift, scale):
    ms = jnp.mean(x * x, axis=-1, keepdims=True)
    y = x * lax.rsqrt(ms + RMS_EPS) * gain
    return y * (1.0 + scale) + shift


def _rope(t, cos, sin_signed):
    n_pairs = HEAD_DIM // 4
    lane = lax.broadcasted_iota(jnp.int32, t.shape, 1)
    first = (lane % (2 * n_pairs)) < n_pairs
    partner = jnp.where(first, pltpu.roll(t, LANES - n_pairs, 1), pltpu.roll(t, n_pairs, 1))
    return t * cos + partner * sin_signed


def _in_proj_kernel(x_ref, prev_ref, next_ref, mod_ref, g_ref, w_ref, b_ref, cos_ref, sin_ref, cw_ref, cb_ref,
                    *rest):
    n_cast = len(LATER_WEIGHT_SLABS)
    cast_in, rest = rest[:n_cast], rest[n_cast:]
    (qr_ref, qp_ref, kr_ref, v_ref, gt_ref, u_ref, x0_ref), rest = rest[:7], rest[7:]
    cast_out, (hy_ref,) = rest[:n_cast], rest[n_cast:]
    for src, dst in zip(cast_in, cast_out):
        dst[...] = src[...].astype(BF16)
    i = pl.program_id(1)
    n_tiles = pl.num_programs(1)
    xx = jnp.concatenate([prev_ref[0], x_ref[0], next_ref[0]], axis=0)
    h_ext = _norm_modulate(xx, g_ref[...], mod_ref[0, 0:1, :], mod_ref[0, 1:2, :]).astype(BF16)
    h = h_ext[HALO:HALO + TOKEN_TILE]
    cos = cos_ref[...]
    sin = sin_ref[...]

    def proj(lo, hi, rows=h):
        return jnp.dot(rows, w_ref[:, lo:hi], preferred_element_type=F32) + b_ref[:, lo:hi]

    q = proj(0, D_ATTN) * (HEAD_DIM ** -0.5 * LOG2_E)
    qp_ref[0] = q.astype(BF16)
    k = proj(D_ATTN, 2 * D_ATTN)
    for c in range(D_ATTN // LANES):
        lanes = slice(c * LANES, (c + 1) * LANES)
        qr_ref[0, :, lanes] = _rope(q[:, lanes], cos, sin).astype(BF16)
        kr_ref[0, :, lanes] = _rope(k[:, lanes], cos, sin).astype(BF16)
    v_t = proj(2 * D_ATTN, 3 * D_ATTN).T.astype(BF16)
    for c in range(TOKEN_TILE // ATTN_KEY_CHUNK):
        v_ref[0, c] = v_t[:, c * ATTN_KEY_CHUNK:(c + 1) * ATTN_KEY_CHUNK]
    hy_lo = 3 * D_ATTN
    gl_lo = hy_lo + 3 * D_HYENA
    for c in range(4):
        w = D_MODEL // 2
        gt_ref[0, :, c * w:(c + 1) * w] = jax.nn.sigmoid(
            proj(gl_lo + c * w, gl_lo + (c + 1) * w)).astype(BF16)

    slabs_per_part = D_HYENA // LANES
    for c in range(3):
        hy = proj(hy_lo + c * D_HYENA, hy_lo + (c + 1) * D_HYENA, h_ext)
        for s in range(slabs_per_part):
            hy_ref[c * slabs_per_part + s] = hy[:, s * LANES:(s + 1) * LANES]
    zero_row = jnp.zeros((1, LANES), F32)

    @pl.when(i == 0)
    def _():
        for s in range(3 * slabs_per_part):
            hy_ref[s, HALO - 1:HALO, :] = zero_row

    @pl.when(i == n_tiles - 1)
    def _():
        for s in range(3 * slabs_per_part):
            hy_ref[s, HALO + TOKEN_TILE:HALO + TOKEN_TILE + 1, :] = zero_row

    def conv(s, r0, rows):
        lanes = slice(s * LANES, (s + 1) * LANES)
        tap = lambda j: hy_ref[s, pl.ds(HALO - 1 + j + r0, rows, stride=1), :]
        return (tap(0) * cw_ref[0:1, lanes] + tap(1) * cw_ref[1:2, lanes] + tap(2) * cw_ref[2:3, lanes]
                + cb_ref[:, lanes])

    for s in range(slabs_per_part):
        lanes = slice(s * LANES, (s + 1) * LANES)
        for j in range(TILE_GROUPS):
            r0 = j * FFT_N2
            x0_ref[0, r0:r0 + FFT_N2, lanes] = conv(s, r0, FFT_N2).astype(BF16)
            u_ref[0, j * SEQ_PITCH:j * SEQ_PITCH + FFT_N2, lanes] = (
                conv(slabs_per_part + s, r0, FFT_N2) * conv(2 * slabs_per_part + s, r0, FFT_N2))
            u_ref[0, j * SEQ_PITCH + FFT_N2:(j + 1) * SEQ_PITCH, lanes] = jnp.zeros((SEQ_PITCH - FFT_N2, LANES), F32)


def _ctx_kv_kernel(x_ref, mod_ref, g_ref, w_ref, b_ref, k_ref, v_ref):
    h = _norm_modulate(x_ref[0], g_ref[...], mod_ref[0:1, :], mod_ref[1:2, :]).astype(BF16)
    kv = jnp.dot(h, w_ref[...], preferred_element_type=F32) + b_ref[...]
    k_ref[0] = kv[:, :D_ATTN].astype(BF16)
    v_ref[0] = kv[:, D_ATTN:].T.astype(BF16)


def _attn_window_start(g):
    return jnp.clip(ATTN_ROWS_PER_STEP * g - WIN_ROWS // 2, 0, GRID_W - ATTN_KEY_ROWS)


def _attn_kernel(qr_ref, qp_ref, k_ref, vt_ref, kc_ref, vct_ref, bias_ref, o_ref):
    g = pl.program_id(1)
    win = _attn_window_start(g)
    key0 = pl.multiple_of(win * GRID_W, ATTN_KEY_CHUNK)
    chunk0 = win // (ATTN_KEY_CHUNK // GRID_W)
    nt = (((1,), (1,)), ((), ()))
    quad_w = 4 * HEAD_DIM
    lane = lax.broadcasted_iota(jnp.int32, (1, quad_w), 1)
    zero = jnp.zeros((), BF16)
    def scores(head):
        quad, hh = divmod(head, 4)
        ql = slice(quad * quad_w, (quad + 1) * quad_w)
        mine = (lane >= hh * HEAD_DIM) & (lane < (hh + 1) * HEAD_DIM)
        s_nb = lax.dot_general(k_ref[0, pl.ds(key0, ATTN_KEYS), ql], jnp.where(mine, qr_ref[0, :, ql], zero), nt,
                               preferred_element_type=F32)
        s_cx = lax.dot_general(kc_ref[0, :, ql], jnp.where(mine, qp_ref[0, :, ql], zero), nt,
                               preferred_element_type=F32)
        return s_nb + bias_ref[0, head], s_cx

    def with_ones(v):
        return jnp.concatenate([v, jnp.ones((16, v.shape[1]), BF16)], axis=0)

    def probs(s_nb, s_cx):
        m = jnp.maximum(jnp.max(s_nb, axis=0, keepdims=True), jnp.max(s_cx, axis=0, keepdims=True))
        return jnp.exp2(s_nb - m).astype(BF16), jnp.exp2(s_cx - m).astype(BF16)

    def values(head, p_nb, p_cx):
        rows = slice(head * HEAD_DIM, (head + 1) * HEAD_DIM)
        v_win = jnp.concatenate([vt_ref[0, chunk0 + c, rows, :] for c in range(ATTN_KEYS // ATTN_KEY_CHUNK)],
                                axis=1)
        o = (jnp.dot(with_ones(v_win), p_nb, preferred_element_type=F32)
             + jnp.dot(with_ones(vct_ref[0, rows, :]), p_cx, preferred_element_type=F32))
        return o[:HEAD_DIM] / o[HEAD_DIM:HEAD_DIM + 1]

    outs = []
    s_q = {0: scores(0), 1: scores(1)}
    p_q = {0: probs(*s_q.pop(0))}
    for head in range(N_HEADS):
        if head + 2 < N_HEADS:
            s_q[head + 2] = scores(head + 2)
        if head + 1 < N_HEADS:
            p_q[head + 1] = probs(*s_q.pop(head + 1))
        outs.append(values(head, *p_q.pop(head)))
    o_ref[0] = jnp.concatenate(outs, axis=0).T.astype(BF16)


def _filt_kernel(z_ref, aux_ref, w1_ref, b1_ref, w2_ref, b2_ref, w3_ref, b3_ref, freq_ref, decay_ref,
                 o_ref, asum_ref):
    i = pl.program_id(1)
    hp = lax.Precision.HIGHEST
    h = jnp.sin(freq_ref[0:1, :] * (jnp.dot(z_ref[0], w1_ref[...], precision=hp,
                                            preferred_element_type=F32) + b1_ref[...]))
    h = jnp.sin(freq_ref[1:2, :] * (jnp.dot(h, w2_ref[...], precision=hp,
                                            preferred_element_type=F32) + b2_ref[...]))
    taps = jnp.concatenate([jnp.dot(h, w3_ref[part], precision=hp, preferred_element_type=F32)
                            for part in range(2)], axis=0) + b3_ref[...]
    t = aux_ref[0, :, 0:1]
    valid = aux_ref[0, :, 1:2] > 0.5
    k = jnp.where(valid, taps * jnp.exp(-t * jnp.abs(decay_ref[0])), 0.0)
    part = jnp.sum(jnp.abs(k), axis=0, keepdims=True)

    @pl.when(i == 0)
    def _():
        asum_ref[...] = jnp.zeros_like(asum_ref)

    asum_ref[0] += jnp.broadcast_to(part, asum_ref.shape[1:])
    o_ref[...] = k.reshape(o_ref.shape)


def _fft_stage1(load_group, w1_ref, a_ref):
    n_slabs = a_ref.shape[0]

    def body(n2, carry):
        x = load_group(n2)
        a = jnp.dot(w1_ref[...], x, preferred_element_type=F32)
        for s in range(n_slabs):
            a_ref[s, pl.ds(n2 * FFT_PITCH, 2 * FFT_N1, stride=1), :] = a[:, s * LANES:(s + 1) * LANES]
        return carry

    lax.fori_loop(0, FFT_N2, body, 0, unroll=FFT_UNROLL)


def _fft_load_k1(a_ref, k1):
    parts = []
    for off in (0, FFT_N1):
        parts.append(jnp.concatenate(
            [a_ref[s, pl.ds(k1 + off, FFT_N2, stride=FFT_PITCH), :] for s in range(a_ref.shape[0])], axis=1))
    return jnp.concatenate(parts, axis=0)


def _filt_fft_kernel(k_ref, asum_ref, w1_ref, w2_ref, o_ref, a_ref):
    step = pl.program_id(1)

    @pl.when(step == 0)
    def _():
        norm = asum_ref[0, 0:1, :] + asum_ref[1, 0:1, :]

        def load_group(n2):
            return (k_ref[pl.ds(pl.multiple_of(n2 * FFT_N1, FFT_N1), FFT_N1), :] / norm).astype(BF16)
        _fft_stage1(load_group, w1_ref, a_ref)

    for j in range(FFT_K1_PER_STEP):
        b = _fft_load_k1(a_ref, step * FFT_K1_PER_STEP + j).astype(BF16)
        o_ref[j] = jnp.dot(w2_ref[j], b, preferred_element_type=F32).astype(BF16)


def _hy_conv_kernel(u_ref, kf_ref, w1_ref, w2_ref, v1_ref, o_ref, a_ref):
    step = pl.program_id(1)
    n_slabs = a_ref.shape[0]

    def tokens_of(member, s, n2):
        return (2 * s + member, pl.ds(n2, FFT_HALF_N1, stride=SEQ_PITCH), slice(None))

    @pl.when(step == 0)
    def _():
        for sample in range(o_ref.shape[0]):
            for grp in range(FFT_HALF_N1):
                o_ref[sample, grp * SEQ_PITCH + FFT_N2:(grp + 1) * SEQ_PITCH, :] = jnp.zeros(
                    (SEQ_PITCH - FFT_N2, LANES), F32)

        def load_group(n2):
            return jnp.concatenate(
                [jnp.concatenate([u_ref[tokens_of(member, s, n2)] for s in range(n_slabs)], axis=1)
                 for member in range(2)], axis=0).astype(BF16)
        _fft_stage1(load_group, w1_ref, a_ref)

    for j in range(FFT_K1_PER_STEP):
        k1 = step * FFT_K1_PER_STEP + j
        b = _fft_load_k1(a_ref, k1).astype(BF16)
        x = jnp.dot(w2_ref[j], b, preferred_element_type=F32)
        kf = kf_ref[j].astype(F32)
        kf = jnp.concatenate([kf] * n_slabs, axis=1)
        xr, xi = x[:FFT_N2], x[FFT_N2:]
        kr, ki = kf[:FFT_N2], kf[FFT_N2:]
        y = jnp.concatenate([xr * kr - xi * ki, xr * ki + xi * kr], axis=0).astype(BF16)
        d = lax.dot_general(w2_ref[j], y, (((0,), (0,)), ((), ())),
                            preferred_element_type=F32)
        for s in range(n_slabs):
            lanes = slice(s * LANES, (s + 1) * LANES)
            a_ref[s, pl.ds(k1, FFT_N2, stride=FFT_PITCH), :] = d[:FFT_N2, lanes]
            a_ref[s, pl.ds(k1 + FFT_N1, FFT_N2, stride=FFT_PITCH), :] = d[FFT_N2:, lanes]

    @pl.when(step == FFT_STEPS - 1)
    def _():
        def body(n2, carry):
            d = jnp.concatenate([a_ref[s, pl.ds(n2 * FFT_PITCH, 2 * FFT_N1, stride=1), :] for s in range(n_slabs)],
                                axis=1)
            y = jnp.dot(v1_ref[...], d.astype(BF16), preferred_element_type=F32)
            for s in range(n_slabs):
                for member in range(2):
                    o_ref[tokens_of(member, s, n2)] = y[member * FFT_HALF_N1:(member + 1) * FFT_HALF_N1,
                                                        s * LANES:(s + 1) * LANES]
            return carry

        lax.fori_loop(0, FFT_N2, body, 0, unroll=FFT_UNROLL)


def _from_pitched(ref):
    return jnp.concatenate([ref[0, j * SEQ_PITCH:j * SEQ_PITCH + FFT_N2, :] for j in range(TILE_GROUPS)], axis=0)


def _merge_kernel(yna_ref, x0_ref, u_ref, yc_ref, gt_ref, skip_ref, gpost_ref, wna_ref, why_ref, wout_ref, o_ref):
    y_hy = (x0_ref[0].astype(F32) * (_from_pitched(yc_ref) + _from_pitched(u_ref) * skip_ref[...])).astype(BF16)
    a = jnp.dot(yna_ref[0], wna_ref[...], preferred_element_type=F32)
    b = jnp.dot(y_hy, why_ref[...], preferred_element_type=F32)
    g_na = gt_ref[0, :, :D_MODEL].astype(F32)
    g_hy = gt_ref[0, :, D_MODEL:].astype(F32)
    m = (g_na * a + g_hy * b).astype(BF16)
    o = jnp.dot(m, wout_ref[...], preferred_element_type=F32)
    ms = jnp.mean(o * o, axis=-1, keepdims=True)
    o_ref[0] = (o * lax.rsqrt(ms + RMS_EPS) * gpost_ref[...]).astype(BF16)


def _gelu_tanh(a):
    return 0.5 * a * (1.0 + jnp.tanh(math.sqrt(2.0 / math.pi) * (a + 0.044715 * (a * a * a))))


def _ffn_kernel(x_ref, prev_ref, next_ref, mix_ref, mix_prev_ref, mix_next_ref, mod_ref, gpre_ref, gpost_ref,
                wup_ref, cw_ref, cb_ref, wdn_ref, o_ref, u_ref, act_ref):
    i = pl.program_id(1)
    n_tiles = pl.num_programs(1)
    mix_halo = mix_prev_ref.shape[1]
    mix = jnp.concatenate([mix_prev_ref[0, mix_halo - HALO:, :], mix_ref[0], mix_next_ref[0, :HALO, :]], axis=0)
    xx = (jnp.concatenate([prev_ref[0], x_ref[0], next_ref[0]], axis=0)
          + mod_ref[0, 2:3, :] * mix.astype(F32))
    x = xx[HALO:HALO + TOKEN_TILE]
    h = _norm_modulate(xx, gpre_ref[...], mod_ref[0, 3:4, :], mod_ref[0, 4:5, :])
    row = lax.broadcasted_iota(jnp.int32, (xx.shape[0], 1), 0)
    inside = ((row >= HALO) | (i > 0)) & ((row < HALO + TOKEN_TILE) | (i < n_tiles - 1))
    h = jnp.where(inside, h, 0.0).astype(BF16)
    starts = list(range(0, D_FF, FF_CHUNK))
    widths = [min(FF_CHUNK, D_FF - lo) for lo in starts]

    def conv(buf, half, s, lo, r0):
        lanes = slice(half * D_FF + lo, half * D_FF + lo + LANES)
        tap = lambda j: u_ref[buf, half, s, pl.ds(HALO - 1 + j + r0, FFN_ROW_BLOCK, stride=1), :]
        return (tap(0) * cw_ref[0:1, lanes] + tap(1) * cw_ref[1:2, lanes] + tap(2) * cw_ref[2:3, lanes]
                + cb_ref[:, lanes])

    def up_project(c):
        for half in range(2):
            lo = half * D_FF + starts[c]
            u = jnp.dot(h, wup_ref[:, lo:lo + widths[c]], preferred_element_type=F32)
            for s in range(widths[c] // LANES):
                u_ref[c % 2, half, s] = u[:, s * LANES:(s + 1) * LANES]

    up_project(0)
    for c in range(len(starts)):
        buf = c % 2
        if c + 1 < len(starts):
            up_project(c + 1)
        for s in range(widths[c] // LANES):
            lo = starts[c] + s * LANES
            for r0 in range(0, TOKEN_TILE, FFN_ROW_BLOCK):
                act_ref[r0:r0 + FFN_ROW_BLOCK, lo:lo + LANES] = (
                    _gelu_tanh(conv(buf, 0, s, lo, r0)) * conv(buf, 1, s, lo, r0)).astype(BF16)
    y = jnp.dot(act_ref[...], wdn_ref[...], preferred_element_type=F32)
    ms = jnp.mean(y * y, axis=-1, keepdims=True)
    y = y * lax.rsqrt(ms + RMS_EPS) * gpost_ref[...]
    o_ref[0] = x + mod_ref[0, 5:6, :] * y


def _const_spec(shape):
    nd = len(shape)
    return pl.BlockSpec(shape, lambda *_: (0,) * nd, pipeline_mode=pl.Buffered(1))


def kernel(x, c, ctx, c_ctx, w_mod, b_mod, norm_mix_pre, norm_mix_post, norm_ffn_pre, norm_ffn_post, w_in, b_in, na_rpb, hy_conv_w, hy_conv_b, hy_filt_w1, hy_filt_b1, hy_filt_w2, hy_filt_b2, hy_filt_w3, hy_filt_b3, hy_sin_freq, hy_decay, hy_skip, w_o_na, w_o_hy, w_out, ffn_w_up, ffn_conv_w, ffn_conv_b, ffn_w_down):
    batch, seq, d = x.shape
    n_ctx = ctx.shape[1]
    assert d == D_MODEL and 2 * seq == FFT_N and seq == GRID_W * GRID_W and batch % 2 == 0
    assert w_mod.shape[0] == 1, "single-layer block"
    n_tiles = seq // TOKEN_TILE
    d_in = w_in.shape[2]
    row2 = lambda a: a.reshape(1, -1)

    c_all = jnp.zeros((8, d), F32).at[:batch].set(c).at[batch].set(c_ctx)
    mod_n = 1024
    mod = pl.pallas_call(
        _mod_kernel,
        grid=(N_MOD * d // mod_n,),
        in_specs=[_const_spec((8, d)),
                  pl.BlockSpec((d, mod_n), lambda j: (0, j)),
                  pl.BlockSpec((1, mod_n), lambda j: (0, j))],
        out_specs=pl.BlockSpec((8, mod_n), lambda j: (0, j)),
        out_shape=jax.ShapeDtypeStruct((8, N_MOD * d), F32),
        compiler_params=_cparams(("arbitrary",)),
        name="mod",
    )(c_all, w_mod[0], row2(b_mod[0]))
    mod_lat = jnp.pad(mod[:batch].reshape(batch, N_MOD, d), ((0, 0), (0, 8 - N_MOD), (0, 0)))
    mod_ctx = jnp.pad(mod[batch].reshape(N_MOD, d), ((0, 8 - N_MOD), (0, 0)))

    w_rows = d // N_HEADS
    bias, w_in_b = pl.pallas_call(
        _attn_bias_kernel,
        grid=(N_HEADS,),
        in_specs=[pl.BlockSpec(memory_space=pltpu.SMEM), pl.BlockSpec((w_rows, d_in), lambda h: (h, 0))],
        out_specs=[pl.BlockSpec((3, 1, ATTN_KEYS, ATTN_Q), lambda h: (0, h, 0, 0)),
                   pl.BlockSpec((w_rows, d_in), lambda h: (h, 0))],
        out_shape=[jax.ShapeDtypeStruct((3, N_HEADS, ATTN_KEYS, ATTN_Q), F32),
                   jax.ShapeDtypeStruct((d, d_in), BF16)],
        scratch_shapes=[pltpu.VMEM((N_BIAS_ROWS, GRID_W, LANES), F32)],
        compiler_params=_cparams(("arbitrary",)),
        name="attn_bias",
    )(na_rpb[0].reshape(-1), w_in[0])
    b_in_r = row2(b_in[0])
    g_mix_pre = row2(norm_mix_pre[0])

    k_ctx, v_ctx = pl.pallas_call(
        _ctx_kv_kernel,
        grid=(batch,),
        in_specs=[pl.BlockSpec((1, n_ctx, d), lambda b: (b, 0, 0)),
                  _const_spec((8, d)), _const_spec((1, d)),
                  _const_spec((d, 2 * D_ATTN)), _const_spec((1, 2 * D_ATTN))],
        out_specs=[pl.BlockSpec((1, n_ctx, D_ATTN), lambda b: (b, 0, 0)),
                   pl.BlockSpec((1, D_ATTN, n_ctx), lambda b: (b, 0, 0))],
        out_shape=[jax.ShapeDtypeStruct((batch, n_ctx, D_ATTN), BF16),
                   jax.ShapeDtypeStruct((batch, D_ATTN, n_ctx), BF16)],
        compiler_params=_cparams(("arbitrary",)),
        name="ctx_kv",
    )(ctx, mod_ctx, g_mix_pre, w_in_b[:, D_ATTN:3 * D_ATTN], b_in_r[:, D_ATTN:3 * D_ATTN])

    cos_t, sin_t = _rope_tables(seq)
    tok = lambda w: pl.BlockSpec((1, TOKEN_TILE, w), lambda b, i: (b, i, 0))
    mod_spec = pl.BlockSpec((1, 8, d), lambda b, i: (b, 0, 0))
    rope_spec = pl.BlockSpec((TOKEN_TILE, LANES), lambda b, i: (i, 0))
    halo_blocks = TOKEN_TILE // HALO
    n_halo_blocks = seq // HALO
    prev_spec = lambda w: pl.BlockSpec((1, HALO, w), lambda b, i: (b, jnp.maximum(i * halo_blocks - 1, 0), 0))
    next_spec = lambda w: pl.BlockSpec(
        (1, HALO, w), lambda b, i: (b, jnp.minimum((i + 1) * halo_blocks, n_halo_blocks - 1), 0))
    pitched_spec = pl.BlockSpec((1, TILE_GROUPS * SEQ_PITCH, D_HYENA), lambda b, i: (b, i, 0))
    pitched_shape = jax.ShapeDtypeStruct((batch, PITCHED_ROWS, D_HYENA), F32)
    later_weights = [ffn_w_up[0], ffn_w_down[0], w_out[0], w_o_na[0], w_o_hy[0]]
    slab_specs = [
        pl.BlockSpec((w.shape[0] // n, w.shape[1]), lambda b, i, n=n: (jnp.minimum(b * n_tiles + i, n - 1), 0))
        for w, n in zip(later_weights, LATER_WEIGHT_SLABS)]
    q_rot, q_plain, k_rot, v_lat, gates, u_p, x0, w_up_c, w_dn_c, w_out_b, w_o_na_b, w_o_hy_b = pl.pallas_call(
        _in_proj_kernel,
        grid=(batch, n_tiles),
        in_specs=[tok(d), prev_spec(d), next_spec(d), mod_spec, _const_spec((1, d)), _const_spec((d, d_in)),
                  _const_spec((1, d_in)), rope_spec, rope_spec,
                  _const_spec((3, 3 * D_HYENA)), _const_spec((1, 3 * D_HYENA))] + slab_specs,
        out_specs=[tok(D_ATTN)] * 3
        + [pl.BlockSpec((1, TOKEN_TILE // ATTN_KEY_CHUNK, D_ATTN, ATTN_KEY_CHUNK), lambda b, i: (b, i, 0, 0)),
           tok(2 * d), pitched_spec, tok(D_HYENA)] + slab_specs,
        out_shape=[jax.ShapeDtypeStruct((batch, seq, D_ATTN), BF16)] * 3
        + [jax.ShapeDtypeStruct((batch, seq // ATTN_KEY_CHUNK, D_ATTN, ATTN_KEY_CHUNK), BF16),
           jax.ShapeDtypeStruct((batch, seq, 2 * d), BF16), pitched_shape,
           jax.ShapeDtypeStruct((batch, seq, D_HYENA), BF16)]
        + [jax.ShapeDtypeStruct(w.shape, BF16) for w in later_weights],
        scratch_shapes=[pltpu.VMEM((3 * D_HYENA // LANES, TOKEN_TILE + 2 * HALO, LANES), F32)],
        compiler_params=_cparams(("arbitrary", "arbitrary")),
        name="in_proj",
    )(x, x, x, mod_lat, g_mix_pre, w_in_b, b_in_r, jnp.asarray(cos_t), jnp.asarray(sin_t),
      hy_conv_w[0], row2(hy_conv_b[0]), *later_weights)

    n_groups = GRID_W // ATTN_ROWS_PER_STEP
    q_spec = pl.BlockSpec((1, ATTN_Q, D_ATTN), lambda b, g: (b, g, 0))
    full = lambda n: pl.BlockSpec((1, n, D_ATTN), lambda b, g: (b, 0, 0))
    bias_spec = pl.BlockSpec(
        (1, N_HEADS, ATTN_KEYS, ATTN_Q),
        lambda b, g: ((g > 0).astype(jnp.int32) + (g == n_groups - 1).astype(jnp.int32), 0, 0, 0))
    vt_spec = pl.BlockSpec((1, seq // ATTN_KEY_CHUNK, D_ATTN, ATTN_KEY_CHUNK), lambda b, g: (b, 0, 0, 0))
    vct_spec = pl.BlockSpec((1, D_ATTN, n_ctx), lambda b, g: (b, 0, 0))
    y_na = pl.pallas_call(
        _attn_kernel,
        grid=(batch, n_groups),
        in_specs=[q_spec, q_spec, full(seq), vt_spec, full(n_ctx), vct_spec, bias_spec],
        out_specs=q_spec,
        out_shape=jax.ShapeDtypeStruct((batch, seq, D_ATTN), BF16),
        compiler_params=_cparams(("arbitrary", "arbitrary")),
        name="attn",
    )(q_rot, q_plain, k_rot, v_lat, k_ctx, v_ctx, bias)

    z_t, aux_t = _filter_tables(seq)
    filt_n2 = FILT_ROWS // FFT_HALF_N1
    hid = FILTER_HIDDEN
    w1_pad = jnp.pad(hy_filt_w1[0], ((0, hid - POS_FEATS), (0, 0)))
    block_diag = lambda w: jnp.zeros((2 * hid, 2 * hid), F32).at[:hid, :hid].set(w).at[hid:, hid:].set(w)
    twice = lambda v: jnp.tile(v, (1, 2))
    w3 = hy_filt_w3[0]
    w3_parts = jnp.stack([jnp.concatenate([w3, jnp.zeros_like(w3)], axis=0),
                          jnp.concatenate([jnp.zeros_like(w3), w3], axis=0)])
    k_circ, k_asum = pl.pallas_call(
        _filt_kernel,
        grid=(2, seq // FILT_ROWS),
        in_specs=[pl.BlockSpec((1, FILT_ROWS // 2, 2 * hid), lambda hf, i: (hf, i, 0)),
                  pl.BlockSpec((1, FILT_ROWS, aux_t.shape[2]), lambda hf, i: (hf, i, 0)),
                  _const_spec((2 * hid, 2 * hid)), _const_spec((1, 2 * hid)),
                  _const_spec((2 * hid, 2 * hid)), _const_spec((1, 2 * hid)),
                  pl.BlockSpec((2, 2 * hid, D_HYENA), lambda hf, i: (0, 0, hf)),
                  pl.BlockSpec((1, D_HYENA), lambda hf, i: (0, hf)),
                  _const_spec((2, 2 * hid)),
                  pl.BlockSpec((1, 1, D_HYENA), lambda hf, i: (hf, 0, 0))],
        out_specs=[pl.BlockSpec((filt_n2, FFT_HALF_N1, D_HYENA), lambda hf, i: (i, hf, 0)),
                   pl.BlockSpec((1, 8, D_HYENA), lambda hf, i: (hf, 0, 0))],
        out_shape=[jax.ShapeDtypeStruct((FFT_N2, FFT_N1, D_HYENA), F32),
                   jax.ShapeDtypeStruct((2, 8, D_HYENA), F32)],
        compiler_params=_cparams(("arbitrary", "arbitrary")),
        name="filt",
    )(jnp.asarray(z_t), jnp.asarray(aux_t), block_diag(w1_pad), twice(row2(hy_filt_b1[0])),
      block_diag(hy_filt_w2[0]), twice(row2(hy_filt_b2[0])), w3_parts, row2(hy_filt_b3[0]),
      twice(hy_sin_freq[0]), hy_decay[0].reshape(2, 1, D_HYENA))
    k_circ = k_circ.reshape(FFT_N, D_HYENA)

    mats = _fft_matrices()
    mats = {k: jnp.asarray(v).astype(BF16) for k, v in mats.items()}
    w2_all = mats["w2"]
    step_mat_spec = pl.BlockSpec((FFT_K1_PER_STEP, 2 * FFT_N2, 2 * FFT_N2), lambda cb, s: (s, 0, 0))
    filt_slabs = 2
    kf = pl.pallas_call(
        _filt_fft_kernel,
        grid=(D_HYENA // (filt_slabs * LANES), FFT_STEPS),
        in_specs=[pl.BlockSpec((FFT_N, filt_slabs * LANES), lambda cb, s: (0, cb)),
                  pl.BlockSpec((2, 8, filt_slabs * LANES), lambda cb, s: (0, 0, cb)),
                  _const_spec((2 * FFT_N1, FFT_N1)), step_mat_spec],
        out_specs=pl.BlockSpec((FFT_K1_PER_STEP, 2 * FFT_N2, filt_slabs * LANES), lambda cb, s: (s, 0, cb)),
        out_shape=jax.ShapeDtypeStruct((FFT_N1, 2 * FFT_N2, D_HYENA), BF16),
        scratch_shapes=[pltpu.VMEM((filt_slabs, FFT_N2 * FFT_PITCH, LANES), F32)],
        compiler_params=_cparams(("arbitrary", "arbitrary")),
        name="filt_fft",
    )(k_circ, k_asum, mats["w1_real"], w2_all)

    n_pairs = batch // 2
    pair_block = ((batch, PITCHED_ROWS, LANES), lambda cb, s: (0, 0, cb))
    pair_spec = pl.BlockSpec(*pair_block, pipeline_mode=pl.Buffered(1))
    y_conv = pl.pallas_call(
        _hy_conv_kernel,
        grid=(D_HYENA // LANES, FFT_STEPS),
        in_specs=[pl.BlockSpec(*pair_block),
                  pl.BlockSpec((FFT_K1_PER_STEP, 2 * FFT_N2, LANES), lambda cb, s: (s, 0, cb)),
                  _const_spec((2 * FFT_N1, FFT_N1)), step_mat_spec,
                  _const_spec((FFT_N1, 2 * FFT_N1))],
        out_specs=pair_spec,
        out_shape=pitched_shape,
        scratch_shapes=[pltpu.VMEM((n_pairs, FFT_N2 * FFT_PITCH, LANES), F32)],
        compiler_params=_cparams(("arbitrary", "arbitrary")),
        name="hy_conv",
    )(u_p, kf, mats["w1_data"], w2_all, mats["v1"])

    mix = pl.pallas_call(
        _merge_kernel,
        grid=(batch, n_tiles),
        in_specs=[tok(D_ATTN), tok(D_HYENA), pitched_spec, pitched_spec, tok(2 * d),
                  _const_spec((1, D_HYENA)), _const_spec((1, d)),
                  _const_spec((D_ATTN, d)), _const_spec((D_HYENA, d)), _const_spec((d, d))],
        out_specs=tok(d),
        out_shape=jax.ShapeDtypeStruct((batch, seq, d), BF16),
        compiler_params=_cparams(("arbitrary", "arbitrary")),
        name="merge",
    )(y_na, x0, u_p, y_conv, gates, row2(hy_skip[0]), row2(norm_mix_post[0]),
      w_o_na_b, w_o_hy_b, w_out_b)

    conv_w_c = ffn_conv_w[0]
    conv_b_c = row2(ffn_conv_b[0])
    mix_halo = 2 * HALO
    mix_blocks = TOKEN_TILE // mix_halo
    mix_prev = pl.BlockSpec((1, mix_halo, d), lambda b, i: (b, jnp.maximum(i * mix_blocks - 1, 0), 0))
    mix_next = pl.BlockSpec((1, mix_halo, d),
                            lambda b, i: (b, jnp.minimum((i + 1) * mix_blocks, seq // mix_halo - 1), 0))
    out = pl.pallas_call(
        _ffn_kernel,
        grid=(batch, n_tiles),
        in_specs=[tok(d), prev_spec(d), next_spec(d), tok(d), mix_prev, mix_next,
                  mod_spec, _const_spec((1, d)), _const_spec((1, d)),
                  _const_spec(w_up_c.shape), _const_spec(conv_w_c.shape), _const_spec(conv_b_c.shape),
                  _const_spec(w_dn_c.shape)],
        out_specs=tok(d),
        out_shape=jax.ShapeDtypeStruct((batch, seq, d), F32),
        scratch_shapes=[pltpu.VMEM((2, 2, FF_CHUNK // LANES, TOKEN_TILE + 2 * HALO, LANES), F32),
                        pltpu.VMEM((TOKEN_TILE, D_FF), BF16)],
        compiler_params=_cparams(("arbitrary", "arbitrary")),
        name="ffn",
    )(x, x, x, mix, mix, mix, mod_lat, row2(norm_ffn_pre[0]), row2(norm_ffn_post[0]),
      w_up_c, conv_w_c, conv_b_c, w_dn_c)
    return out
```

```python
import functools
import math

import jax
import jax.numpy as jnp
import numpy as np
from jax import lax
from jax.experimental import pallas as pl
from jax.experimental.pallas import tpu as pltpu

F32 = jnp.float32
BF16 = jnp.bfloat16

D_MODEL = 1024
N_HEADS = 8
HEAD_DIM = 64
D_ATTN = N_HEADS * HEAD_DIM
D_HYENA = 512
GRID_W = 64
WIN_ROWS = 8
WIN_COLS = 16
POS_BANDS = 16
POS_FEATS = 1 + 2 * POS_BANDS
FILTER_HIDDEN = 64
D_FF = 2816
N_MOD = 6
ROPE_BASE = 10000.0
RMS_EPS = 1e-6
NEG_BIAS = -1e30
LOG2_E = math.log2(math.e)

LANES = 128
VMEM_LIMIT_BYTES = 58 * 1024 * 1024

FFT_N1 = 64
FFT_N2 = 128
FFT_N = FFT_N1 * FFT_N2
FFT_HALF_N1 = FFT_N1 // 2
FFT_PITCH = 2 * FFT_N1 + 4
FFT_K1_PER_STEP = 32
FFT_UNROLL = 8
FFT_STEPS = FFT_N1 // FFT_K1_PER_STEP

TOKEN_TILE = 512
FILT_ROWS = 1024
SEQ_PITCH = FFT_N2 + 8
PITCHED_ROWS = FFT_HALF_N1 * SEQ_PITCH
TILE_GROUPS = TOKEN_TILE // FFT_N2
ATTN_ROWS_PER_STEP = 4
ATTN_Q = ATTN_ROWS_PER_STEP * GRID_W
ATTN_KEY_ROWS = 12
ATTN_KEYS = ATTN_KEY_ROWS * GRID_W
ATTN_KEY_CHUNK = 256
ATTN_GROUPS_PER_STEP = 4
FF_CHUNK = 768
LATER_WEIGHT_SLABS = (32, 16, 32, 32, 32)
HALO = 8
FFN_ROW_BLOCK = 64


def _cparams(sem):
    return pltpu.CompilerParams(dimension_semantics=sem, vmem_limit_bytes=VMEM_LIMIT_BYTES)


@functools.lru_cache(maxsize=None)
def _rope_tables(seq):
    pos = np.arange(seq)
    row = pos // GRID_W
    col = pos % GRID_W
    n_pairs = HEAD_DIM // 4
    inv = ROPE_BASE ** (-np.arange(n_pairs, dtype=np.float64) / n_pairs)
    lane = np.arange(LANES) % HEAD_DIM
    p = np.where(lane[None, :] < HEAD_DIM // 2, row[:, None], col[:, None]).astype(np.float64)
    ang = p * inv[lane % n_pairs][None, :]
    sign = np.where((lane % (2 * n_pairs)) < n_pairs, -1.0, 1.0)
    return np.cos(ang).astype(np.float32), (np.sin(ang) * sign[None, :]).astype(np.float32)


@functools.lru_cache(maxsize=None)
def _filter_tables(seq):
    assert 2 * seq == FFT_N
    half, n2, n1 = np.meshgrid(np.arange(2), np.arange(FFT_N2), np.arange(FFT_HALF_N1), indexing="ij")
    n = FFT_N2 * (half * FFT_HALF_N1 + n1) + n2
    fwd = n < seq
    m = n - seq
    valid = fwd | (m >= 1)
    pos = np.where(valid, np.where(fwd, n, seq - m), 0).astype(np.float64).reshape(2, seq)
    t = pos / max(seq - 1, 1)
    bands = np.linspace(1e-4, POS_BANDS - 1, POS_BANDS)
    ang = (2.0 * math.pi / seq) * pos[..., None] * bands
    z = np.zeros((2, seq, 64), np.float64)
    z[..., 0] = t
    z[..., 1:1 + POS_BANDS] = np.cos(ang)
    z[..., 1 + POS_BANDS:POS_FEATS] = -np.sin(ang)
    aux = np.zeros((2, seq, 8), np.float64)
    aux[..., 0] = t
    aux[..., 1] = valid.reshape(2, seq)
    zp = z.reshape(2, seq // FILT_ROWS, 2, FILT_ROWS // 2, 64).transpose(0, 1, 3, 2, 4).reshape(2, seq // 2, 128)
    return zp.astype(np.float32), aux.astype(np.float32)


def _realify(m):
    return np.block([[m.real, -m.imag], [m.imag, m.real]])


@functools.lru_cache(maxsize=None)
def _fft_matrices():
    n1 = np.arange(FFT_N1)
    n2 = np.arange(FFT_N2)
    k1 = np.arange(FFT_N1)
    k2 = np.arange(FFT_N2)
    f1 = np.exp(-2j * np.pi * np.outer(k1, n1) / FFT_N1)
    w1_data = _realify(f1[:, :FFT_HALF_N1])
    w1_real = np.concatenate([f1.real, f1.imag], axis=0)
    v1 = _realify(np.conj(f1.T)[:FFT_HALF_N1, :] / FFT_N)
    f2 = np.exp(-2j * np.pi * np.outer(k2, n2) / FFT_N2)
    tw = np.exp(-2j * np.pi * np.outer(k1, n2) / FFT_N)
    w2 = np.stack([_realify(f2 * tw[a][None, :]) for a in range(FFT_N1)])
    return {k: v.astype(np.float32) for k, v in
            dict(w1_data=w1_data, w1_real=w1_real, v1=v1, w2=w2).items()}


N_BIAS_ROWS = 2 * WIN_ROWS - 1
N_BIAS_COLS = 2 * WIN_COLS - 1


def _attn_bias_row_index():
    rows = GRID_W
    groups = (0, 2, rows // ATTN_ROWS_PER_STEP - 1)
    dr = np.full((3, ATTN_ROWS_PER_STEP, ATTN_KEY_ROWS), -1, np.int32)
    for v, g in enumerate(groups):
        ws = min(max(ATTN_ROWS_PER_STEP * g - WIN_ROWS // 2, 0), rows - ATTN_KEY_ROWS)
        for i in range(ATTN_ROWS_PER_STEP):
            r = ATTN_ROWS_PER_STEP * g + i
            r_start = min(max(r - WIN_ROWS // 2, 0), rows - WIN_ROWS)
            for j in range(ATTN_KEY_ROWS):
                kr = ws + j
                if r_start <= kr < r_start + WIN_ROWS:
                    dr[v, i, j] = kr - r + (WIN_ROWS - 1)
    return dr


def _attn_bias_kernel(rpb_ref, w_ref, o_ref, wb_ref, t_ref):
    wb_ref[...] = w_ref[...].astype(BF16)
    head = pl.program_id(0)
    kc = lax.broadcasted_iota(jnp.int32, (GRID_W, LANES), 0)
    lane = lax.broadcasted_iota(jnp.int32, (GRID_W, LANES), 1)
    qc = lane % GRID_W
    c_start = jnp.clip(qc - WIN_COLS // 2, 0, GRID_W - WIN_COLS)
    col_in = (kc >= c_start) & (kc < c_start + WIN_COLS)
    dc = jnp.clip(kc - qc, 1 - WIN_COLS, WIN_COLS - 1) + (WIN_COLS - 1)
    base = head * (N_BIAS_ROWS * N_BIAS_COLS)
    for r in range(N_BIAS_ROWS):
        t = jnp.full((GRID_W, LANES), NEG_BIAS, F32)
        for cidx in range(N_BIAS_COLS):
            t = jnp.where(col_in & (dc == cidx), rpb_ref[base + r * N_BIAS_COLS + cidx] * LOG2_E, t)
        t_ref[r] = t
    dr = _attn_bias_row_index()
    low_half = lane < GRID_W
    masked = jnp.full((GRID_W, LANES), NEG_BIAS, F32)
    for v in range(3):
        for i in range(0, ATTN_ROWS_PER_STEP, 2):
            for j in range(ATTN_KEY_ROWS):
                lo = t_ref[int(dr[v, i, j])] if dr[v, i, j] >= 0 else masked
                hi = t_ref[int(dr[v, i + 1, j])] if dr[v, i + 1, j] >= 0 else masked
                o_ref[v, 0, j * GRID_W:(j + 1) * GRID_W, i * GRID_W:(i + 2) * GRID_W] = jnp.where(
                    low_half, lo, hi)


def _mod_kernel(c_ref, w_ref, b_ref, o_ref):
    c = c_ref[...]
    s = c * jax.nn.sigmoid(c)
    o_ref[...] = jnp.dot(s, w_ref[...], precision=lax.Precision.HIGHEST,
                         preferred_element_type=F32) + b_ref[...]


def _norm_modulate(x, gain, shift, scale):
    ms = jnp.mean(x * x, axis=-1, keepdims=True)
    y = x * lax.rsqrt(ms + RMS_EPS) * gain
    return y * (1.0 + scale) + shift


def _rope(t, cos, sin_signed):
    n_pairs = HEAD_DIM // 4
    lane = lax.broadcasted_iota(jnp.int32, t.shape, 1)
    first = (lane % (2 * n_pairs)) < n_pairs
    partner = jnp.where(first, pltpu.roll(t, LANES - n_pairs, 1), pltpu.roll(t, n_pairs, 1))
    return t * cos + partner * sin_signed


def _in_proj_kernel(x_ref, prev_ref, next_ref, mod_ref, g_ref, w_ref, b_ref, cos_ref, sin_ref, cw_ref, cb_ref,
                    *rest):
    n_cast = len(LATER_WEIGHT_SLABS)
    cast_in, rest = rest[:n_cast], rest[n_cast:]
    (qr_ref, qp_ref, kr_ref, v_ref, gt_ref, u_ref, x0_ref), rest = rest[:7], rest[7:]
    cast_out, (hy_ref,) = rest[:n_cast], rest[n_cast:]
    for src, dst in zip(cast_in, cast_out):
        dst[...] = src[...].astype(BF16)
    i = pl.program_id(1)
    n_tiles = pl.num_programs(1)
    xx = jnp.concatenate([prev_ref[0], x_ref[0], next_ref[0]], axis=0)
    h_ext = _norm_modulate(xx, g_ref[...], mod_ref[0, 0:1, :], mod_ref[0, 1:2, :]).astype(BF16)
    h = h_ext[HALO:HALO + TOKEN_TILE]
    cos = cos_ref[...]
    sin = sin_ref[...]

    def proj(lo, hi, rows=h):
        return jnp.dot(rows, w_ref[:, lo:hi], preferred_element_type=F32) + b_ref[:, lo:hi]

    q = proj(0, D_ATTN) * (HEAD_DIM ** -0.5 * LOG2_E)
    qp_ref[0] = q.astype(BF16)
    k = proj(D_ATTN, 2 * D_ATTN)
    for c in range(D_ATTN // LANES):
        lanes = slice(c * LANES, (c + 1) * LANES)
        qr_ref[0, :, lanes] = _rope(q[:, lanes], cos, sin).astype(BF16)
        kr_ref[0, :, lanes] = _rope(k[:, lanes], cos, sin).astype(BF16)
    v_t = proj(2 * D_ATTN, 3 * D_ATTN).T.astype(BF16)
    for c in range(TOKEN_TILE // ATTN_KEY_CHUNK):
        v_ref[0, c] = v_t[:, c * ATTN_KEY_CHUNK:(c + 1) * ATTN_KEY_CHUNK]
    hy_lo = 3 * D_ATTN
    gl_lo = hy_lo + 3 * D_HYENA
    for c in range(4):
        w = D_MODEL // 2
        gt_ref[0, :, c * w:(c + 1) * w] = jax.nn.sigmoid(
            proj(gl_lo + c * w, gl_lo + (c + 1) * w)).astype(BF16)

    slabs_per_part = D_HYENA // LANES
    for c in range(3):
        hy = proj(hy_lo + c * D_HYENA, hy_lo + (c + 1) * D_HYENA, h_ext)
        for s in range(slabs_per_part):
            hy_ref[c * slabs_per_part + s] = hy[:, s * LANES:(s + 1) * LANES]
    zero_row = jnp.zeros((1, LANES), F32)

    @pl.when(i == 0)
    def _():
        for s in range(3 * slabs_per_part):
            hy_ref[s, HALO - 1:HALO, :] = zero_row

    @pl.when(i == n_tiles - 1)
    def _():
        for s in range(3 * slabs_per_part):
            hy_ref[s, HALO + TOKEN_TILE:HALO + TOKEN_TILE + 1, :] = zero_row

    def conv(s, r0, rows):
        lanes = slice(s * LANES, (s + 1) * LANES)
        tap = lambda j: hy_ref[s, pl.ds(HALO - 1 + j + r0, rows, stride=1), :]
        return (tap(0) * cw_ref[0:1, lanes] + tap(1) * cw_ref[1:2, lanes] + tap(2) * cw_ref[2:3, lanes]
                + cb_ref[:, lanes])

    for s in range(slabs_per_part):
        lanes = slice(s * LANES, (s + 1) * LANES)
        for j in range(TILE_GROUPS):
            r0 = j * FFT_N2
            x0_ref[0, r0:r0 + FFT_N2, lanes] = conv(s, r0, FFT_N2).astype(BF16)
            u_ref[0, j * SEQ_PITCH:j * SEQ_PITCH + FFT_N2, lanes] = (
                conv(slabs_per_part + s, r0, FFT_N2) * conv(2 * slabs_per_part + s, r0, FFT_N2))
            u_ref[0, j * SEQ_PITCH + FFT_N2:(j + 1) * SEQ_PITCH, lanes] = jnp.zeros((SEQ_PITCH - FFT_N2, LANES), F32)


def _ctx_kv_kernel(x_ref, mod_ref, g_ref, w_ref, b_ref, k_ref, v_ref):
    h = _norm_modulate(x_ref[0], g_ref[...], mod_ref[0:1, :], mod_ref[1:2, :]).astype(BF16)
    kv = jnp.dot(h, w_ref[...], preferred_element_type=F32) + b_ref[...]
    k_ref[0] = kv[:, :D_ATTN].astype(BF16)
    v_ref[0] = kv[:, D_ATTN:].T.astype(BF16)


def _attn_window_start(g):
    return jnp.clip(ATTN_ROWS_PER_STEP * g - WIN_ROWS // 2, 0, GRID_W - ATTN_KEY_ROWS)


def _attn_kernel(qr_ref, qp_ref, k_ref, vt_ref, kc_ref, vct_ref, bias_ref, o_ref):
    step = pl.program_id(1)
    n_groups = GRID_W // ATTN_ROWS_PER_STEP
    nt = (((1,), (1,)), ((), ()))
    quad_w = 4 * HEAD_DIM
    lane = lax.broadcasted_iota(jnp.int32, (1, quad_w), 1)
    zero = jnp.zeros((), BF16)

    def group_params(sub):
        g = step * ATTN_GROUPS_PER_STEP + sub
        win = _attn_window_start(g)
        key0 = pl.multiple_of(win * GRID_W, ATTN_KEY_CHUNK)
        chunk0 = win // (ATTN_KEY_CHUNK // GRID_W)
        variant = (g > 0).astype(jnp.int32) + (g == n_groups - 1).astype(jnp.int32)
        return key0, chunk0, variant

    params = [group_params(sub) for sub in range(ATTN_GROUPS_PER_STEP)]

    def scores(item):
        sub, head = item
        key0, _, variant = params[sub]
        qrows = slice(sub * ATTN_Q, (sub + 1) * ATTN_Q)
        quad, hh = divmod(head, 4)
        ql = slice(quad * quad_w, (quad + 1) * quad_w)
        mine = (lane >= hh * HEAD_DIM) & (lane < (hh + 1) * HEAD_DIM)
        s_nb = lax.dot_general(k_ref[0, pl.ds(key0, ATTN_KEYS), ql], jnp.where(mine, qr_ref[0, qrows, ql], zero), nt,
                               preferred_element_type=F32)
        s_cx = lax.dot_general(kc_ref[0, :, ql], jnp.where(mine, qp_ref[0, qrows, ql], zero), nt,
                               preferred_element_type=F32)
        return s_nb + bias_ref[variant, head], s_cx

    def with_ones(v):
        return jnp.concatenate([v, jnp.ones((16, v.shape[1]), BF16)], axis=0)

    def probs(s_nb, s_cx):
        m = jnp.maximum(jnp.max(s_nb, axis=0, keepdims=True), jnp.max(s_cx, axis=0, keepdims=True))
        return jnp.exp2(s_nb - m).astype(BF16), jnp.exp2(s_cx - m).astype(BF16)

    def values(item, p_nb, p_cx):
        sub, head = item
        chunk0 = params[sub][1]
        rows = slice(head * HEAD_DIM, (head + 1) * HEAD_DIM)
        v_win = jnp.concatenate([vt_ref[0, chunk0 + c, rows, :] for c in range(ATTN_KEYS // ATTN_KEY_CHUNK)],
                                axis=1)
        o = (jnp.dot(with_ones(v_win), p_nb, preferred_element_type=F32)
             + jnp.dot(with_ones(vct_ref[0, rows, :]), p_cx, preferred_element_type=F32))
        return o[:HEAD_DIM] / o[HEAD_DIM:HEAD_DIM + 1]

    items = [(sub, head) for sub in range(ATTN_GROUPS_PER_STEP) for head in range(N_HEADS)]
    outs = []
    s_q = {0: scores(items[0]), 1: scores(items[1])}
    p_q = {0: probs(*s_q.pop(0))}
    for n, item in enumerate(items):
        if n + 2 < len(items):
            s_q[n + 2] = scores(items[n + 2])
        if n + 1 < len(items):
            p_q[n + 1] = probs(*s_q.pop(n + 1))
        outs.append(values(item, *p_q.pop(n)))
    for sub in range(ATTN_GROUPS_PER_STEP):
        o_ref[0, sub * ATTN_Q:(sub + 1) * ATTN_Q, :] = jnp.concatenate(
            outs[sub * N_HEADS:(sub + 1) * N_HEADS], axis=0).T.astype(BF16)


def _filt_kernel(z_ref, aux_ref, w1_ref, b1_ref, w2_ref, b2_ref, w3_ref, b3_ref, freq_ref, decay_ref,
                 o_ref, asum_ref):
    i = pl.program_id(1)
    hp = lax.Precision.HIGHEST
    h = jnp.sin(freq_ref[0:1, :] * (jnp.dot(z_ref[0], w1_ref[...], precision=hp,
                                            preferred_element_type=F32) + b1_ref[...]))
    h = jnp.sin(freq_ref[1:2, :] * (jnp.dot(h, w2_ref[...], precision=hp,
                                            preferred_element_type=F32) + b2_ref[...]))
    taps = jnp.concatenate([jnp.dot(h, w3_ref[part], precision=hp, preferred_element_type=F32)
                            for part in range(2)], axis=0) + b3_ref[...]
    t = aux_ref[0, :, 0:1]
    valid = aux_ref[0, :, 1:2] > 0.5
    k = jnp.where(valid, taps * jnp.exp(-t * jnp.abs(decay_ref[0])), 0.0)
    part = jnp.sum(jnp.abs(k), axis=0, keepdims=True)

    @pl.when(i == 0)
    def _():
        asum_ref[...] = jnp.zeros_like(asum_ref)

    asum_ref[0] += jnp.broadcast_to(part, asum_ref.shape[1:])
    o_ref[...] = k.reshape(o_ref.shape)


def _fft_stage1(load_group, w1_ref, a_ref):
    n_slabs = a_ref.shape[0]

    def body(n2, carry):
        x = load_group(n2)
        a = jnp.dot(w1_ref[...], x, preferred_element_type=F32)
        for s in range(n_slabs):
            a_ref[s, pl.ds(n2 * FFT_PITCH, 2 * FFT_N1, stride=1), :] = a[:, s * LANES:(s + 1) * LANES]
        return carry

    lax.fori_loop(0, FFT_N2, body, 0, unroll=FFT_UNROLL)


def _fft_load_k1(a_ref, k1):
    parts = []
    for off in (0, FFT_N1):
        parts.append(jnp.concatenate(
            [a_ref[s, pl.ds(k1 + off, FFT_N2, stride=FFT_PITCH), :] for s in range(a_ref.shape[0])], axis=1))
    return jnp.concatenate(parts, axis=0)


def _filt_fft_kernel(k_ref, asum_ref, w1_ref, w2_ref, o_ref, a_ref):
    step = pl.program_id(1)

    @pl.when(step == 0)
    def _():
        norm = asum_ref[0, 0:1, :] + asum_ref[1, 0:1, :]

        def load_group(n2):
            return (k_ref[pl.ds(pl.multiple_of(n2 * FFT_N1, FFT_N1), FFT_N1), :] / norm).astype(BF16)
        _fft_stage1(load_group, w1_ref, a_ref)

    for j in range(FFT_K1_PER_STEP):
        b = _fft_load_k1(a_ref, step * FFT_K1_PER_STEP + j).astype(BF16)
        o_ref[j] = jnp.dot(w2_ref[j], b, preferred_element_type=F32).astype(BF16)


def _hy_conv_kernel(u_ref, kf_ref, w1_ref, w2_ref, v1_ref, o_ref, a_ref):
    step = pl.program_id(1)
    n_slabs = a_ref.shape[0]

    def tokens_of(member, s, n2):
        return (2 * s + member, pl.ds(n2, FFT_HALF_N1, stride=SEQ_PITCH), slice(None))

    @pl.when(step == 0)
    def _():
        for sample in range(o_ref.shape[0]):
            for grp in range(FFT_HALF_N1):
                o_ref[sample, grp * SEQ_PITCH + FFT_N2:(grp + 1) * SEQ_PITCH, :] = jnp.zeros(
                    (SEQ_PITCH - FFT_N2, LANES), F32)

        def load_group(n2):
            return jnp.concatenate(
                [jnp.concatenate([u_ref[tokens_of(member, s, n2)] for s in range(n_slabs)], axis=1)
                 for member in range(2)], axis=0).astype(BF16)
        _fft_stage1(load_group, w1_ref, a_ref)

    for j in range(FFT_K1_PER_STEP):
        k1 = step * FFT_K1_PER_STEP + j
        b = _fft_load_k1(a_ref, k1).astype(BF16)
        x = jnp.dot(w2_ref[j], b, preferred_element_type=F32)
        kf = kf_ref[j].astype(F32)
        kf = jnp.concatenate([kf] * n_slabs, axis=1)
        xr, xi = x[:FFT_N2], x[FFT_N2:]
        kr, ki = kf[:FFT_N2], kf[FFT_N2:]
        y = jnp.concatenate([xr * kr - xi * ki, xr * ki + xi * kr], axis=0).astype(BF16)
        d = lax.dot_general(w2_ref[j], y, (((0,), (0,)), ((), ())),
                            preferred_element_type=F32)
        for s in range(n_slabs):
            lanes = slice(s * LANES, (s + 1) * LANES)
            a_ref[s, pl.ds(k1, FFT_N2, stride=FFT_PITCH), :] = d[:FFT_N2, lanes]
            a_ref[s, pl.ds(k1 + FFT_N1, FFT_N2, stride=FFT_PITCH), :] = d[FFT_N2:, lanes]

    @pl.when(step == FFT_STEPS - 1)
    def _():
        def body(n2, carry):
            d = jnp.concatenate([a_ref[s, pl.ds(n2 * FFT_PITCH, 2 * FFT_N1, stride=1), :] for s in range(n_slabs)],
                                axis=1)
            y = jnp.dot(v1_ref[...], d.astype(BF16), preferred_element_type=F32)
            for s in range(n_slabs):
                for member in range(2):
                    o_ref[tokens_of(member, s, n2)] = y[member * FFT_HALF_N1:(member + 1) * FFT_HALF_N1,
                                                        s * LANES:(s + 1) * LANES]
            return carry

        lax.fori_loop(0, FFT_N2, body, 0, unroll=FFT_UNROLL)


def _from_pitched(ref):
    return jnp.concatenate([ref[0, j * SEQ_PITCH:j * SEQ_PITCH + FFT_N2, :] for j in range(TILE_GROUPS)], axis=0)


def _merge_kernel(yna_ref, x0_ref, u_ref, yc_ref, gt_ref, skip_ref, gpost_ref, wna_ref, why_ref, wout_ref, o_ref):
    y_hy = (x0_ref[0].astype(F32) * (_from_pitched(yc_ref) + _from_pitched(u_ref) * skip_ref[...])).astype(BF16)
    a = jnp.dot(yna_ref[0], wna_ref[...], preferred_element_type=F32)
    b = jnp.dot(y_hy, why_ref[...], preferred_element_type=F32)
    g_na = gt_ref[0, :, :D_MODEL].astype(F32)
    g_hy = gt_ref[0, :, D_MODEL:].astype(F32)
    m = (g_na * a + g_hy * b).astype(BF16)
    o = jnp.dot(m, wout_ref[...], preferred_element_type=F32)
    ms = jnp.mean(o * o, axis=-1, keepdims=True)
    o_ref[0] = (o * lax.rsqrt(ms + RMS_EPS) * gpost_ref[...]).astype(BF16)


def _gelu_tanh(a):
    return 0.5 * a * (1.0 + jnp.tanh(math.sqrt(2.0 / math.pi) * (a + 0.044715 * (a * a * a))))


def _ffn_kernel(x_ref, prev_ref, next_ref, mix_ref, mix_prev_ref, mix_next_ref, mod_ref, gpre_ref, gpost_ref,
                wup_ref, cw_ref, cb_ref, wdn_ref, o_ref, u_ref, act_ref):
    i = pl.program_id(1)
    n_tiles = pl.num_programs(1)
    mix_halo = mix_prev_ref.shape[1]
    mix = jnp.concatenate([mix_prev_ref[0, mix_halo - HALO:, :], mix_ref[0], mix_next_ref[0, :HALO, :]], axis=0)
    xx = (jnp.concatenate([prev_ref[0], x_ref[0], next_ref[0]], axis=0)
          + mod_ref[0, 2:3, :] * mix.astype(F32))
    x = xx[HALO:HALO + TOKEN_TILE]
    h = _norm_modulate(xx, gpre_ref[...], mod_ref[0, 3:4, :], mod_ref[0, 4:5, :])
    row = lax.broadcasted_iota(jnp.int32, (xx.shape[0], 1), 0)
    inside = ((row >= HALO) | (i > 0)) & ((row < HALO + TOKEN_TILE) | (i < n_tiles - 1))
    h = jnp.where(inside, h, 0.0).astype(BF16)
    starts = list(range(0, D_FF, FF_CHUNK))
    widths = [min(FF_CHUNK, D_FF - lo) for lo in starts]

    def conv(buf, half, s, lo, r0):
        lanes = slice(half * D_FF + lo, half * D_FF + lo + LANES)
        tap = lambda j: u_ref[buf, half, s, pl.ds(HALO - 1 + j + r0, FFN_ROW_BLOCK, stride=1), :]
        return (tap(0) * cw_ref[0:1, lanes] + tap(1) * cw_ref[1:2, lanes] + tap(2) * cw_ref[2:3, lanes]
                + cb_ref[:, lanes])

    def up_project(c):
        for half in range(2):
            lo = half * D_FF + starts[c]
            u = jnp.dot(h, wup_ref[:, lo:lo + widths[c]], preferred_element_type=F32)
            for s in range(widths[c] // LANES):
                u_ref[c % 2, half, s] = u[:, s * LANES:(s + 1) * LANES]

    up_project(0)
    for c in range(len(starts)):
        buf = c % 2
        if c + 1 < len(starts):
            up_project(c + 1)
        for s in range(widths[c] // LANES):
            lo = starts[c] + s * LANES
            for r0 in range(0, TOKEN_TILE, FFN_ROW_BLOCK):
                act_ref[r0:r0 + FFN_ROW_BLOCK, lo:lo + LANES] = (
                    _gelu_tanh(conv(buf, 0, s, lo, r0)) * conv(buf, 1, s, lo, r0)).astype(BF16)
    y = jnp.dot(act_ref[...], wdn_ref[...], preferred_element_type=F32)
    ms = jnp.mean(y * y, axis=-1, keepdims=True)
    y = y * lax.rsqrt(ms + RMS_EPS) * gpost_ref[...]
    o_ref[0] = x + mod_ref[0, 5:6, :] * y


def _const_spec(shape):
    nd = len(shape)
    return pl.BlockSpec(shape, lambda *_: (0,) * nd, pipeline_mode=pl.Buffered(1))


def kernel(x, c, ctx, c_ctx, w_mod, b_mod, norm_mix_pre, norm_mix_post, norm_ffn_pre, norm_ffn_post, w_in, b_in, na_rpb, hy_conv_w, hy_conv_b, hy_filt_w1, hy_filt_b1, hy_filt_w2, hy_filt_b2, hy_filt_w3, hy_filt_b3, hy_sin_freq, hy_decay, hy_skip, w_o_na, w_o_hy, w_out, ffn_w_up, ffn_conv_w, ffn_conv_b, ffn_w_down):
    batch, seq, d = x.shape
    n_ctx = ctx.shape[1]
    assert d == D_MODEL and 2 * seq == FFT_N and seq == GRID_W * GRID_W and batch % 2 == 0
    assert w_mod.shape[0] == 1, "single-layer block"
    n_tiles = seq // TOKEN_TILE
    d_in = w_in.shape[2]
    row2 = lambda a: a.reshape(1, -1)

    c_all = jnp.zeros((8, d), F32).at[:batch].set(c).at[batch].set(c_ctx)
    mod_n = 1024
    mod = pl.pallas_call(
        _mod_kernel,
        grid=(N_MOD * d // mod_n,),
        in_specs=[_const_spec((8, d)),
                  pl.BlockSpec((d, mod_n), lambda j: (0, j)),
                  pl.BlockSpec((1, mod_n), lambda j: (0, j))],
        out_specs=pl.BlockSpec((8, mod_n), lambda j: (0, j)),
        out_shape=jax.ShapeDtypeStruct((8, N_MOD * d), F32),
        compiler_params=_cparams(("arbitrary",)),
        name="mod",
    )(c_all, w_mod[0], row2(b_mod[0]))
    mod_lat = jnp.pad(mod[:batch].reshape(batch, N_MOD, d), ((0, 0), (0, 8 - N_MOD), (0, 0)))
    mod_ctx = jnp.pad(mod[batch].reshape(N_MOD, d), ((0, 8 - N_MOD), (0, 0)))

    w_rows = d // N_HEADS
    bias, w_in_b = pl.pallas_call(
        _attn_bias_kernel,
        grid=(N_HEADS,),
        in_specs=[pl.BlockSpec(memory_space=pltpu.SMEM), pl.BlockSpec((w_rows, d_in), lambda h: (h, 0))],
        out_specs=[pl.BlockSpec((3, 1, ATTN_KEYS, ATTN_Q), lambda h: (0, h, 0, 0)),
                   pl.BlockSpec((w_rows, d_in), lambda h: (h, 0))],
        out_shape=[jax.ShapeDtypeStruct((3, N_HEADS, ATTN_KEYS, ATTN_Q), F32),
                   jax.ShapeDtypeStruct((d, d_in), BF16)],
        scratch_shapes=[pltpu.VMEM((N_BIAS_ROWS, GRID_W, LANES), F32)],
        compiler_params=_cparams(("arbitrary",)),
        name="attn_bias",
    )(na_rpb[0].reshape(-1), w_in[0])
    b_in_r = row2(b_in[0])
    g_mix_pre = row2(norm_mix_pre[0])

    k_ctx, v_ctx = pl.pallas_call(
        _ctx_kv_kernel,
        grid=(batch,),
        in_specs=[pl.BlockSpec((1, n_ctx, d), lambda b: (b, 0, 0)),
                  _const_spec((8, d)), _const_spec((1, d)),
                  _const_spec((d, 2 * D_ATTN)), _const_spec((1, 2 * D_ATTN))],
        out_specs=[pl.BlockSpec((1, n_ctx, D_ATTN), lambda b: (b, 0, 0)),
                   pl.BlockSpec((1, D_ATTN, n_ctx), lambda b: (b, 0, 0))],
        out_shape=[jax.ShapeDtypeStruct((batch, n_ctx, D_ATTN), BF16),
                   jax.ShapeDtypeStruct((batch, D_ATTN, n_ctx), BF16)],
        compiler_params=_cparams(("arbitrary",)),
        name="ctx_kv",
    )(ctx, mod_ctx, g_mix_pre, w_in_b[:, D_ATTN:3 * D_ATTN], b_in_r[:, D_ATTN:3 * D_ATTN])

    cos_t, sin_t = _rope_tables(seq)
    tok = lambda w: pl.BlockSpec((1, TOKEN_TILE, w), lambda b, i: (b, i, 0))
    mod_spec = pl.BlockSpec((1, 8, d), lambda b, i: (b, 0, 0))
    rope_spec = pl.BlockSpec((TOKEN_TILE, LANES), lambda b, i: (i, 0))
    halo_blocks = TOKEN_TILE // HALO
    n_halo_blocks = seq // HALO
    prev_spec = lambda w: pl.BlockSpec((1, HALO, w), lambda b, i: (b, jnp.maximum(i * halo_blocks - 1, 0), 0))
    next_spec = lambda w: pl.BlockSpec(
        (1, HALO, w), lambda b, i: (b, jnp.minimum((i + 1) * halo_blocks, n_halo_blocks - 1), 0))
    pitched_spec = pl.BlockSpec((1, TILE_GROUPS * SEQ_PITCH, D_HYENA), lambda b, i: (b, i, 0))
    pitched_shape = jax.ShapeDtypeStruct((batch, PITCHED_ROWS, D_HYENA), F32)
    later_weights = [ffn_w_up[0], ffn_w_down[0], w_out[0], w_o_na[0], w_o_hy[0]]
    slab_specs = [
        pl.BlockSpec((w.shape[0] // n, w.shape[1]), lambda b, i, n=n: (jnp.minimum(b * n_tiles + i, n - 1), 0))
        for w, n in zip(later_weights, LATER_WEIGHT_SLABS)]
    q_rot, q_plain, k_rot, v_lat, gates, u_p, x0, w_up_c, w_dn_c, w_out_b, w_o_na_b, w_o_hy_b = pl.pallas_call(
        _in_proj_kernel,
        grid=(batch, n_tiles),
        in_specs=[tok(d), prev_spec(d), next_spec(d), mod_spec, _const_spec((1, d)), _const_spec((d, d_in)),
                  _const_spec((1, d_in)), rope_spec, rope_spec,
                  _const_spec((3, 3 * D_HYENA)), _const_spec((1, 3 * D_HYENA))] + slab_specs,
        out_specs=[tok(D_ATTN)] * 3
        + [pl.BlockSpec((1, TOKEN_TILE // ATTN_KEY_CHUNK, D_ATTN, ATTN_KEY_CHUNK), lambda b, i: (b, i, 0, 0)),
           tok(2 * d), pitched_spec, tok(D_HYENA)] + slab_specs,
        out_shape=[jax.ShapeDtypeStruct((batch, seq, D_ATTN), BF16)] * 3
        + [jax.ShapeDtypeStruct((batch, seq // ATTN_KEY_CHUNK, D_ATTN, ATTN_KEY_CHUNK), BF16),
           jax.ShapeDtypeStruct((batch, seq, 2 * d), BF16), pitched_shape,
           jax.ShapeDtypeStruct((batch, seq, D_HYENA), BF16)]
        + [jax.ShapeDtypeStruct(w.shape, BF16) for w in later_weights],
        scratch_shapes=[pltpu.VMEM((3 * D_HYENA // LANES, TOKEN_TILE + 2 * HALO, LANES), F32)],
        compiler_params=_cparams(("arbitrary", "arbitrary")),
        name="in_proj",
    )(x, x, x, mod_lat, g_mix_pre, w_in_b, b_in_r, jnp.asarray(cos_t), jnp.asarray(sin_t),
      hy_conv_w[0], row2(hy_conv_b[0]), *later_weights)

    n_steps = GRID_W // ATTN_ROWS_PER_STEP // ATTN_GROUPS_PER_STEP
    q_spec = pl.BlockSpec((1, ATTN_GROUPS_PER_STEP * ATTN_Q, D_ATTN), lambda b, g: (b, g, 0))
    full = lambda n: pl.BlockSpec((1, n, D_ATTN), lambda b, g: (b, 0, 0))
    bias_spec = _const_spec((3, N_HEADS, ATTN_KEYS, ATTN_Q))
    vt_spec = pl.BlockSpec((1, seq // ATTN_KEY_CHUNK, D_ATTN, ATTN_KEY_CHUNK), lambda b, g: (b, 0, 0, 0))
    vct_spec = pl.BlockSpec((1, D_ATTN, n_ctx), lambda b, g: (b, 0, 0))
    y_na = pl.pallas_call(
        _attn_kernel,
        grid=(batch, n_steps),
        in_specs=[q_spec, q_spec, full(seq), vt_spec, full(n_ctx), vct_spec, bias_spec],
        out_specs=q_spec,
        out_shape=jax.ShapeDtypeStruct((batch, seq, D_ATTN), BF16),
        compiler_params=_cparams(("arbitrary", "arbitrary")),
        name="attn",
    )(q_rot, q_plain, k_rot, v_lat, k_ctx, v_ctx, bias)

    z_t, aux_t = _filter_tables(seq)
    filt_n2 = FILT_ROWS // FFT_HALF_N1
    hid = FILTER_HIDDEN
    w1_pad = jnp.pad(hy_filt_w1[0], ((0, hid - POS_FEATS), (0, 0)))
    block_diag = lambda w: jnp.zeros((2 * hid, 2 * hid), F32).at[:hid, :hid].set(w).at[hid:, hid:].set(w)
    twice = lambda v: jnp.tile(v, (1, 2))
    w3 = hy_filt_w3[0]
    w3_parts = jnp.stack([jnp.concatenate([w3, jnp.zeros_like(w3)], axis=0),
                          jnp.concatenate([jnp.zeros_like(w3), w3], axis=0)])
    k_circ, k_asum = pl.pallas_call(
        _filt_kernel,
        grid=(2, seq // FILT_ROWS),
        in_specs=[pl.BlockSpec((1, FILT_ROWS // 2, 2 * hid), lambda hf, i: (hf, i, 0)),
                  pl.BlockSpec((1, FILT_ROWS, aux_t.shape[2]), lambda hf, i: (hf, i, 0)),
                  _const_spec((2 * hid, 2 * hid)), _const_spec((1, 2 * hid)),
                  _const_spec((2 * hid, 2 * hid)), _const_spec((1, 2 * hid)),
                  pl.BlockSpec((2, 2 * hid, D_HYENA), lambda hf, i: (0, 0, hf)),
                  pl.BlockSpec((1, D_HYENA), lambda hf, i: (0, hf)),
                  _const_spec((2, 2 * hid)),
                  pl.BlockSpec((1, 1, D_HYENA), lambda hf, i: (hf, 0, 0))],
        out_specs=[pl.BlockSpec((filt_n2, FFT_HALF_N1, D_HYENA), lambda hf, i: (i, hf, 0)),
                   pl.BlockSpec((1, 8, D_HYENA), lambda hf, i: (hf, 0, 0))],
        out_shape=[jax.ShapeDtypeStruct((FFT_N2, FFT_N1, D_HYENA), F32),
                   jax.ShapeDtypeStruct((2, 8, D_HYENA), F32)],
        compiler_params=_cparams(("arbitrary", "arbitrary")),
        name="filt",
    )(jnp.asarray(z_t), jnp.asarray(aux_t), block_diag(w1_pad), twice(row2(hy_filt_b1[0])),
      block_diag(hy_filt_w2[0]), twice(row2(hy_filt_b2[0])), w3_parts, row2(hy_filt_b3[0]),
      twice(hy_sin_freq[0]), hy_decay[0].reshape(2, 1, D_HYENA))
    k_circ = k_circ.reshape(FFT_N, D_HYENA)

    mats = _fft_matrices()
    mats = {k: jnp.asarray(v).astype(BF16) for k, v in mats.items()}
    w2_all = mats["w2"]
    step_mat_spec = pl.BlockSpec((FFT_K1_PER_STEP, 2 * FFT_N2, 2 * FFT_N2), lambda cb, s: (s, 0, 0))
    filt_slabs = 2
    kf = pl.pallas_call(
        _filt_fft_kernel,
        grid=(D_HYENA // (filt_slabs * LANES), FFT_STEPS),
        in_specs=[pl.BlockSpec((FFT_N, filt_slabs * LANES), lambda cb, s: (0, cb)),
                  pl.BlockSpec((2, 8, filt_slabs * LANES), lambda cb, s: (0, 0, cb)),
                  _const_spec((2 * FFT_N1, FFT_N1)), step_mat_spec],
        out_specs=pl.BlockSpec((FFT_K1_PER_STEP, 2 * FFT_N2, filt_slabs * LANES), lambda cb, s: (s, 0, cb)),
        out_shape=jax.ShapeDtypeStruct((FFT_N1, 2 * FFT_N2, D_HYENA), BF16),
        scratch_shapes=[pltpu.VMEM((filt_slabs, FFT_N2 * FFT_PITCH, LANES), F32)],
        compiler_params=_cparams(("arbitrary", "arbitrary")),
        name="filt_fft",
    )(k_circ, k_asum, mats["w1_real"], w2_all)

    n_pairs = batch // 2
    pair_block = ((batch, PITCHED_ROWS, LANES), lambda cb, s: (0, 0, cb))
    pair_spec = pl.BlockSpec(*pair_block, pipeline_mode=pl.Buffered(1))
    y_conv = pl.pallas_call(
        _hy_conv_kernel,
        grid=(D_HYENA // LANES, FFT_STEPS),
        in_specs=[pl.BlockSpec(*pair_block),
                  pl.BlockSpec((FFT_K1_PER_STEP, 2 * FFT_N2, LANES), lambda cb, s: (s, 0, cb)),
                  _const_spec((2 * FFT_N1, FFT_N1)), step_mat_spec,
                  _const_spec((FFT_N1, 2 * FFT_N1))],
        out_specs=pair_spec,
        out_shape=pitched_shape,
        scratch_shapes=[pltpu.VMEM((n_pairs, FFT_N2 * FFT_PITCH, LANES), F32)],
        compiler_params=_cparams(("arbitrary", "arbitrary")),
        name="hy_conv",
    )(u_p, kf, mats["w1_data"], w2_all, mats["v1"])

    mix = pl.pallas_call(
        _merge_kernel,
        grid=(batch, n_tiles),
        in_specs=[tok(D_ATTN), tok(D_HYENA), pitched_spec, pitched_spec, tok(2 * d),
                  _const_spec((1, D_HYENA)), _const_spec((1, d)),
                  _const_spec((D_ATTN, d)), _const_spec((D_HYENA, d)), _const_spec((d, d))],
        out_specs=tok(d),
        out_shape=jax.ShapeDtypeStruct((batch, seq, d), BF16),
        compiler_params=_cparams(("arbitrary", "arbitrary")),
        name="merge",
    )(y_na, x0, u_p, y_conv, gates, row2(hy_skip[0]), row2(norm_mix_post[0]),
      w_o_na_b, w_o_hy_b, w_out_b)

    conv_w_c = ffn_conv_w[0]
    conv_b_c = row2(ffn_conv_b[0])
    mix_halo = 2 * HALO
    mix_blocks = TOKEN_TILE // mix_halo
    mix_prev = pl.BlockSpec((1, mix_halo, d), lambda b, i: (b, jnp.maximum(i * mix_blocks - 1, 0), 0))
    mix_next = pl.BlockSpec((1, mix_halo, d),
                            lambda b, i: (b, jnp.minimum((i + 1) * mix_blocks, seq // mix_halo - 1), 0))
    out = pl.pallas_call(
        _ffn_kernel,
        grid=(batch, n_tiles),
        in_specs=[tok(d), prev_spec(d), next_spec(d), tok(d), mix_prev, mix_next,
                  mod_spec, _const_spec((1, d)), _const_spec((1, d)),
                  _const_spec(w_up_c.shape), _const_spec(conv_w_c.shape), _const_spec(conv_b_c.shape),
                  _const_spec(w_dn_c.shape)],
        out_specs=tok(d),
        out_shape=jax.ShapeDtypeStruct((batch, seq, d), F32),
        scratch_shapes=[pltpu.VMEM((2, 2, FF_CHUNK // LANES, TOKEN_TILE + 2 * HALO, LANES), F32),
                        pltpu.VMEM((TOKEN_TILE, D_FF), BF16)],
        compiler_params=_cparams(("arbitrary", "arbitrary")),
        name="ffn",
    )(x, x, x, mix, mix, mix, mod_lat, row2(norm_ffn_pre[0]), row2(norm_ffn_post[0]),
      w_up_c, conv_w_c, conv_b_c, w_dn_c)
    return out
```

```python
import functools
import math

import jax
import jax.numpy as jnp
import numpy as np
from jax import lax
from jax.experimental import pallas as pl
from jax.experimental.pallas import tpu as pltpu

F32 = jnp.float32
BF16 = jnp.bfloat16

D_MODEL = 1024
N_HEADS = 8
HEAD_DIM = 64
D_ATTN = N_HEADS * HEAD_DIM
D_HYENA = 512
GRID_W = 64
WIN_ROWS = 8
WIN_COLS = 16
POS_BANDS = 16
POS_FEATS = 1 + 2 * POS_BANDS
FILTER_HIDDEN = 64
D_FF = 2816
N_MOD = 6
ROPE_BASE = 10000.0
RMS_EPS = 1e-6
NEG_BIAS = -1e30
LOG2_E = math.log2(math.e)

LANES = 128
VMEM_LIMIT_BYTES = 58 * 1024 * 1024

FFT_N1 = 64
FFT_N2 = 128
FFT_N = FFT_N1 * FFT_N2
FFT_HALF_N1 = FFT_N1 // 2
FFT_PITCH = 2 * FFT_N1 + 4
FFT_K1_PER_STEP = 32
FFT_UNROLL = 8
FFT_STEPS = FFT_N1 // FFT_K1_PER_STEP

TOKEN_TILE = 512
FILT_ROWS = 1024
SEQ_PITCH = FFT_N2 + 8
PITCHED_ROWS = FFT_HALF_N1 * SEQ_PITCH
TILE_GROUPS = TOKEN_TILE // FFT_N2
ATTN_ROWS_PER_STEP = 4
ATTN_Q = ATTN_ROWS_PER_STEP * GRID_W
ATTN_KEY_ROWS = 12
ATTN_KEYS = ATTN_KEY_ROWS * GRID_W
ATTN_KEY_CHUNK = 256
ATTN_GROUPS_PER_STEP = 8
FF_CHUNK = 768
LATER_WEIGHT_SLABS = (32, 16, 32, 32, 32)
HALO = 8
FFN_ROW_BLOCK = 64


def _cparams(sem):
    return pltpu.CompilerParams(dimension_semantics=sem, vmem_limit_bytes=VMEM_LIMIT_BYTES)


@functools.lru_cache(maxsize=None)
def _rope_tables(seq):
    pos = np.arange(seq)
    row = pos // GRID_W
    col = pos % GRID_W
    n_pairs = HEAD_DIM // 4
    inv = ROPE_BASE ** (-np.arange(n_pairs, dtype=np.float64) / n_pairs)
    lane = np.arange(LANES) % HEAD_DIM
    p = np.where(lane[None, :] < HEAD_DIM // 2, row[:, None], col[:, None]).astype(np.float64)
    ang = p * inv[lane % n_pairs][None, :]
    sign = np.where((lane % (2 * n_pairs)) < n_pairs, -1.0, 1.0)
    return np.cos(ang).astype(np.float32), (np.sin(ang) * sign[None, :]).astype(np.float32)


@functools.lru_cache(maxsize=None)
def _filter_tables(seq):
    assert 2 * seq == FFT_N
    half, n2, n1 = np.meshgrid(np.arange(2), np.arange(FFT_N2), np.arange(FFT_HALF_N1), indexing="ij")
    n = FFT_N2 * (half * FFT_HALF_N1 + n1) + n2
    fwd = n < seq
    m = n - seq
    valid = fwd | (m >= 1)
    pos = np.where(valid, np.where(fwd, n, seq - m), 0).astype(np.float64).reshape(2, seq)
    t = pos / max(seq - 1, 1)
    bands = np.linspace(1e-4, POS_BANDS - 1, POS_BANDS)
    ang = (2.0 * math.pi / seq) * pos[..., None] * bands
    z = np.zeros((2, seq, 64), np.float64)
    z[..., 0] = t
    z[..., 1:1 + POS_BANDS] = np.cos(ang)
    z[..., 1 + POS_BANDS:POS_FEATS] = -np.sin(ang)
    aux = np.zeros((2, seq, 8), np.float64)
    aux[..., 0] = t
    aux[..., 1] = valid.reshape(2, seq)
    zp = z.reshape(2, seq // FILT_ROWS, 2, FILT_ROWS // 2, 64).transpose(0, 1, 3, 2, 4).reshape(2, seq // 2, 128)
    return zp.astype(np.float32), aux.astype(np.float32)


def _realify(m):
    return np.block([[m.real, -m.imag], [m.imag, m.real]])


@functools.lru_cache(maxsize=None)
def _fft_matrices():
    n1 = np.arange(FFT_N1)
    n2 = np.arange(FFT_N2)
    k1 = np.arange(FFT_N1)
    k2 = np.arange(FFT_N2)
    f1 = np.exp(-2j * np.pi * np.outer(k1, n1) / FFT_N1)
    w1_data = _realify(f1[:, :FFT_HALF_N1])
    w1_real = np.concatenate([f1.real, f1.imag], axis=0)
    v1 = _realify(np.conj(f1.T)[:FFT_HALF_N1, :] / FFT_N)
    f2 = np.exp(-2j * np.pi * np.outer(k2, n2) / FFT_N2)
    tw = np.exp(-2j * np.pi * np.outer(k1, n2) / FFT_N)
    w2 = np.stack([_realify(f2 * tw[a][None, :]) for a in range(FFT_N1)])
    return {k: v.astype(np.float32) for k, v in
            dict(w1_data=w1_data, w1_real=w1_real, v1=v1, w2=w2).items()}


N_BIAS_ROWS = 2 * WIN_ROWS - 1
N_BIAS_COLS = 2 * WIN_COLS - 1


def _attn_bias_row_index():
    rows = GRID_W
    groups = (0, 2, rows // ATTN_ROWS_PER_STEP - 1)
    dr = np.full((3, ATTN_ROWS_PER_STEP, ATTN_KEY_ROWS), -1, np.int32)
    for v, g in enumerate(groups):
        ws = min(max(ATTN_ROWS_PER_STEP * g - WIN_ROWS // 2, 0), rows - ATTN_KEY_ROWS)
        for i in range(ATTN_ROWS_PER_STEP):
            r = ATTN_ROWS_PER_STEP * g + i
            r_start = min(max(r - WIN_ROWS // 2, 0), rows - WIN_ROWS)
            for j in range(ATTN_KEY_ROWS):
                kr = ws + j
                if r_start <= kr < r_start + WIN_ROWS:
                    dr[v, i, j] = kr - r + (WIN_ROWS - 1)
    return dr


def _attn_bias_kernel(rpb_ref, w_ref, o_ref, wb_ref, t_ref):
    wb_ref[...] = w_ref[...].astype(BF16)
    head = pl.program_id(0)
    kc = lax.broadcasted_iota(jnp.int32, (GRID_W, LANES), 0)
    lane = lax.broadcasted_iota(jnp.int32, (GRID_W, LANES), 1)
    qc = lane % GRID_W
    c_start = jnp.clip(qc - WIN_COLS // 2, 0, GRID_W - WIN_COLS)
    col_in = (kc >= c_start) & (kc < c_start + WIN_COLS)
    dc = jnp.clip(kc - qc, 1 - WIN_COLS, WIN_COLS - 1) + (WIN_COLS - 1)
    base = head * (N_BIAS_ROWS * N_BIAS_COLS)
    for r in range(N_BIAS_ROWS):
        t = jnp.full((GRID_W, LANES), NEG_BIAS, F32)
        for cidx in range(N_BIAS_COLS):
            t = jnp.where(col_in & (dc == cidx), rpb_ref[base + r * N_BIAS_COLS + cidx] * LOG2_E, t)
        t_ref[r] = t
    dr = _attn_bias_row_index()
    low_half = lane < GRID_W
    masked = jnp.full((GRID_W, LANES), NEG_BIAS, F32)
    for v in range(3):
        for i in range(0, ATTN_ROWS_PER_STEP, 2):
            for j in range(ATTN_KEY_ROWS):
                lo = t_ref[int(dr[v, i, j])] if dr[v, i, j] >= 0 else masked
                hi = t_ref[int(dr[v, i + 1, j])] if dr[v, i + 1, j] >= 0 else masked
                o_ref[v, 0, j * GRID_W:(j + 1) * GRID_W, i * GRID_W:(i + 2) * GRID_W] = jnp.where(
                    low_half, lo, hi)


def _mod_kernel(c_ref, w_ref, b_ref, o_ref):
    c = c_ref[...]
    s = c * jax.nn.sigmoid(c)
    o_ref[...] = jnp.dot(s, w_ref[...], precision=lax.Precision.HIGHEST,
                         preferred_element_type=F32) + b_ref[...]


def _norm_modulate(x, gain, shift, scale):
    ms = jnp.mean(x * x, axis=-1, keepdims=True)
    y = x * lax.rsqrt(ms + RMS_EPS) * gain
    return y * (1.0 + scale) + shift


def _rope(t, cos, sin_signed):
    n_pairs = HEAD_DIM // 4
    lane = lax.broadcasted_iota(jnp.int32, t.shape, 1)
    first = (lane % (2 * n_pairs)) < n_pairs
    partner = jnp.where(first, pltpu.roll(t, LANES - n_pairs, 1), pltpu.roll(t, n_pairs, 1))
    return t * cos + partner * sin_signed


def _in_proj_kernel(x_ref, prev_ref, next_ref, mod_ref, g_ref, w_ref, b_ref, cos_ref, sin_ref, cw_ref, cb_ref,
                    *rest):
    n_cast = len(LATER_WEIGHT_SLABS)
    cast_in, rest = rest[:n_cast], rest[n_cast:]
    (qr_ref, qp_ref, kr_ref, v_ref, gt_ref, u_ref, x0_ref), rest = rest[:7], rest[7:]
    cast_out, (hy_ref,) = rest[:n_cast], rest[n_cast:]
    for src, dst in zip(cast_in, cast_out):
        dst[...] = src[...].astype(BF16)
    i = pl.program_id(1)
    n_tiles = pl.num_programs(1)
    xx = jnp.concatenate([prev_ref[0], x_ref[0], next_ref[0]], axis=0)
    h_ext = _norm_modulate(xx, g_ref[...], mod_ref[0, 0:1, :], mod_ref[0, 1:2, :]).astype(BF16)
    h = h_ext[HALO:HALO + TOKEN_TILE]
    cos = cos_ref[...]
    sin = sin_ref[...]

    def proj(lo, hi, rows=h):
        return jnp.dot(rows, w_ref[:, lo:hi], preferred_element_type=F32) + b_ref[:, lo:hi]

    q = proj(0, D_ATTN) * (HEAD_DIM ** -0.5 * LOG2_E)
    qp_ref[0] = q.astype(BF16)
    k = proj(D_ATTN, 2 * D_ATTN)
    for c in range(D_ATTN // LANES):
        lanes = slice(c * LANES, (c + 1) * LANES)
        qr_ref[0, :, lanes] = _rope(q[:, lanes], cos, sin).astype(BF16)
        kr_ref[0, :, lanes] = _rope(k[:, lanes], cos, sin).astype(BF16)
    v_t = proj(2 * D_ATTN, 3 * D_ATTN).T.astype(BF16)
    for c in range(TOKEN_TILE // ATTN_KEY_CHUNK):
        v_ref[0, c] = v_t[:, c * ATTN_KEY_CHUNK:(c + 1) * ATTN_KEY_CHUNK]
    hy_lo = 3 * D_ATTN
    gl_lo = hy_lo + 3 * D_HYENA
    for c in range(4):
        w = D_MODEL // 2
        gt_ref[0, :, c * w:(c + 1) * w] = jax.nn.sigmoid(
            proj(gl_lo + c * w, gl_lo + (c + 1) * w)).astype(BF16)

    slabs_per_part = D_HYENA // LANES
    for c in range(3):
        hy = proj(hy_lo + c * D_HYENA, hy_lo + (c + 1) * D_HYENA, h_ext)
        for s in range(slabs_per_part):
            hy_ref[c * slabs_per_part + s] = hy[:, s * LANES:(s + 1) * LANES]
    zero_row = jnp.zeros((1, LANES), F32)

    @pl.when(i == 0)
    def _():
        for s in range(3 * slabs_per_part):
            hy_ref[s, HALO - 1:HALO, :] = zero_row

    @pl.when(i == n_tiles - 1)
    def _():
        for s in range(3 * slabs_per_part):
            hy_ref[s, HALO + TOKEN_TILE:HALO + TOKEN_TILE + 1, :] = zero_row

    def conv(s, r0, rows):
        lanes = slice(s * LANES, (s + 1) * LANES)
        tap = lambda j: hy_ref[s, pl.ds(HALO - 1 + j + r0, rows, stride=1), :]
        return (tap(0) * cw_ref[0:1, lanes] + tap(1) * cw_ref[1:2, lanes] + tap(2) * cw_ref[2:3, lanes]
                + cb_ref[:, lanes])

    for s in range(slabs_per_part):
        lanes = slice(s * LANES, (s + 1) * LANES)
        for j in range(TILE_GROUPS):
            r0 = j * FFT_N2
            x0_ref[0, r0:r0 + FFT_N2, lanes] = conv(s, r0, FFT_N2).astype(BF16)
            u_ref[0, j * SEQ_PITCH:j * SEQ_PITCH + FFT_N2, lanes] = (
                conv(slabs_per_part + s, r0, FFT_N2) * conv(2 * slabs_per_part + s, r0, FFT_N2))
            u_ref[0, j * SEQ_PITCH + FFT_N2:(j + 1) * SEQ_PITCH, lanes] = jnp.zeros((SEQ_PITCH - FFT_N2, LANES), F32)


def _ctx_kv_kernel(x_ref, mod_ref, g_ref, w_ref, b_ref, k_ref, v_ref):
    h = _norm_modulate(x_ref[0], g_ref[...], mod_ref[0:1, :], mod_ref[1:2, :]).astype(BF16)
    kv = jnp.dot(h, w_ref[...], preferred_element_type=F32) + b_ref[...]
    k_ref[0] = kv[:, :D_ATTN].astype(BF16)
    v_ref[0] = kv[:, D_ATTN:].T.astype(BF16)


def _attn_window_start(g):
    return jnp.clip(ATTN_ROWS_PER_STEP * g - WIN_ROWS // 2, 0, GRID_W - ATTN_KEY_ROWS)


def _attn_kernel(qr_ref, qp_ref, k_ref, vt_ref, kc_ref, vct_ref, bias_ref, o_ref):
    step = pl.program_id(1)
    n_groups = GRID_W // ATTN_ROWS_PER_STEP
    nt = (((1,), (1,)), ((), ()))
    quad_w = 4 * HEAD_DIM
    lane = lax.broadcasted_iota(jnp.int32, (1, quad_w), 1)
    zero = jnp.zeros((), BF16)

    def group_params(sub):
        g = step * ATTN_GROUPS_PER_STEP + sub
        win = _attn_window_start(g)
        key0 = pl.multiple_of(win * GRID_W, ATTN_KEY_CHUNK)
        chunk0 = win // (ATTN_KEY_CHUNK // GRID_W)
        variant = (g > 0).astype(jnp.int32) + (g == n_groups - 1).astype(jnp.int32)
        return key0, chunk0, variant

    params = [group_params(sub) for sub in range(ATTN_GROUPS_PER_STEP)]

    def scores(item):
        sub, head = item
        key0, _, variant = params[sub]
        qrows = slice(sub * ATTN_Q, (sub + 1) * ATTN_Q)
        quad, hh = divmod(head, 4)
        ql = slice(quad * quad_w, (quad + 1) * quad_w)
        mine = (lane >= hh * HEAD_DIM) & (lane < (hh + 1) * HEAD_DIM)
        s_nb = lax.dot_general(k_ref[0, pl.ds(key0, ATTN_KEYS), ql], jnp.where(mine, qr_ref[0, qrows, ql], zero), nt,
                               preferred_element_type=F32)
        s_cx = lax.dot_general(kc_ref[0, :, ql], jnp.where(mine, qp_ref[0, qrows, ql], zero), nt,
                               preferred_element_type=F32)
        return s_nb + bias_ref[variant, head], s_cx

    def with_ones(v):
        return jnp.concatenate([v, jnp.ones((16, v.shape[1]), BF16)], axis=0)

    def probs(s_nb, s_cx):
        m = jnp.maximum(jnp.max(s_nb, axis=0, keepdims=True), jnp.max(s_cx, axis=0, keepdims=True))
        return jnp.exp2(s_nb - m).astype(BF16), jnp.exp2(s_cx - m).astype(BF16)

    def values(item, p_nb, p_cx):
        sub, head = item
        chunk0 = params[sub][1]
        rows = slice(head * HEAD_DIM, (head + 1) * HEAD_DIM)
        v_win = jnp.concatenate([vt_ref[0, chunk0 + c, rows, :] for c in range(ATTN_KEYS // ATTN_KEY_CHUNK)],
                                axis=1)
        o = (jnp.dot(with_ones(v_win), p_nb, preferred_element_type=F32)
             + jnp.dot(with_ones(vct_ref[0, rows, :]), p_cx, preferred_element_type=F32))
        return o[:HEAD_DIM] / o[HEAD_DIM:HEAD_DIM + 1]

    items = [(sub, head) for sub in range(ATTN_GROUPS_PER_STEP) for head in range(N_HEADS)]
    outs = []
    s_q = {0: scores(items[0]), 1: scores(items[1])}
    p_q = {0: probs(*s_q.pop(0))}
    for n, item in enumerate(items):
        if n + 2 < len(items):
            s_q[n + 2] = scores(items[n + 2])
        if n + 1 < len(items):
            p_q[n + 1] = probs(*s_q.pop(n + 1))
        outs.append(values(item, *p_q.pop(n)))
    for sub in range(ATTN_GROUPS_PER_STEP):
        o_ref[0, sub * ATTN_Q:(sub + 1) * ATTN_Q, :] = jnp.concatenate(
            outs[sub * N_HEADS:(sub + 1) * N_HEADS], axis=0).T.astype(BF16)


def _filt_kernel(z_ref, aux_ref, w1_ref, b1_ref, w2_ref, b2_ref, w3_ref, b3_ref, freq_ref, decay_ref,
                 o_ref, asum_ref):
    i = pl.program_id(1)
    hp = lax.Precision.HIGHEST
    h = jnp.sin(freq_ref[0:1, :] * (jnp.dot(z_ref[0], w1_ref[...], precision=hp,
                                            preferred_element_type=F32) + b1_ref[...]))
    h = jnp.sin(freq_ref[1:2, :] * (jnp.dot(h, w2_ref[...], precision=hp,
                                            preferred_element_type=F32) + b2_ref[...]))
    taps = jnp.concatenate([jnp.dot(h, w3_ref[part], precision=hp, preferred_element_type=F32)
                            for part in range(2)], axis=0) + b3_ref[...]
    t = aux_ref[0, :, 0:1]
    valid = aux_ref[0, :, 1:2] > 0.5
    k = jnp.where(valid, taps * jnp.exp(-t * jnp.abs(decay_ref[0])), 0.0)
    part = jnp.sum(jnp.abs(k), axis=0, keepdims=True)

    @pl.when(i == 0)
    def _():
        asum_ref[...] = jnp.zeros_like(asum_ref)

    asum_ref[0] += jnp.broadcast_to(part, asum_ref.shape[1:])
    o_ref[...] = k.reshape(o_ref.shape)


def _fft_stage1(load_group, w1_ref, a_ref):
    n_slabs = a_ref.shape[0]

    def body(n2, carry):
        x = load_group(n2)
        a = jnp.dot(w1_ref[...], x, preferred_element_type=F32)
        for s in range(n_slabs):
            a_ref[s, pl.ds(n2 * FFT_PITCH, 2 * FFT_N1, stride=1), :] = a[:, s * LANES:(s + 1) * LANES]
        return carry

    lax.fori_loop(0, FFT_N2, body, 0, unroll=FFT_UNROLL)


def _fft_load_k1(a_ref, k1):
    parts = []
    for off in (0, FFT_N1):
        parts.append(jnp.concatenate(
            [a_ref[s, pl.ds(k1 + off, FFT_N2, stride=FFT_PITCH), :] for s in range(a_ref.shape[0])], axis=1))
    return jnp.concatenate(parts, axis=0)


def _filt_fft_kernel(k_ref, asum_ref, w1_ref, w2_ref, o_ref, a_ref):
    step = pl.program_id(1)

    @pl.when(step == 0)
    def _():
        norm = asum_ref[0, 0:1, :] + asum_ref[1, 0:1, :]

        def load_group(n2):
            return (k_ref[pl.ds(pl.multiple_of(n2 * FFT_N1, FFT_N1), FFT_N1), :] / norm).astype(BF16)
        _fft_stage1(load_group, w1_ref, a_ref)

    for j in range(FFT_K1_PER_STEP):
        b = _fft_load_k1(a_ref, step * FFT_K1_PER_STEP + j).astype(BF16)
        o_ref[j] = jnp.dot(w2_ref[j], b, preferred_element_type=F32).astype(BF16)


def _hy_conv_kernel(u_ref, kf_ref, w1_ref, w2_ref, v1_ref, o_ref, a_ref):
    step = pl.program_id(1)
    n_slabs = a_ref.shape[0]

    def tokens_of(member, s, n2):
        return (2 * s + member, pl.ds(n2, FFT_HALF_N1, stride=SEQ_PITCH), slice(None))

    @pl.when(step == 0)
    def _():
        for sample in range(o_ref.shape[0]):
            for grp in range(FFT_HALF_N1):
                o_ref[sample, grp * SEQ_PITCH + FFT_N2:(grp + 1) * SEQ_PITCH, :] = jnp.zeros(
                    (SEQ_PITCH - FFT_N2, LANES), F32)

        def load_group(n2):
            return jnp.concatenate(
                [jnp.concatenate([u_ref[tokens_of(member, s, n2)] for s in range(n_slabs)], axis=1)
                 for member in range(2)], axis=0).astype(BF16)
        _fft_stage1(load_group, w1_ref, a_ref)

    for j in range(FFT_K1_PER_STEP):
        k1 = step * FFT_K1_PER_STEP + j
        b = _fft_load_k1(a_ref, k1).astype(BF16)
        x = jnp.dot(w2_ref[j], b, preferred_element_type=F32)
        kf = kf_ref[j].astype(F32)
        kf = jnp.concatenate([kf] * n_slabs, axis=1)
        xr, xi = x[:FFT_N2], x[FFT_N2:]
        kr, ki = kf[:FFT_N2], kf[FFT_N2:]
        y = jnp.concatenate([xr * kr - xi * ki, xr * ki + xi * kr], axis=0).astype(BF16)
        d = lax.dot_general(w2_ref[j], y, (((0,), (0,)), ((), ())),
                            preferred_element_type=F32)
        for s in range(n_slabs):
            lanes = slice(s * LANES, (s + 1) * LANES)
            a_ref[s, pl.ds(k1, FFT_N2, stride=FFT_PITCH), :] = d[:FFT_N2, lanes]
            a_ref[s, pl.ds(k1 + FFT_N1, FFT_N2, stride=FFT_PITCH), :] = d[FFT_N2:, lanes]

    @pl.when(step == FFT_STEPS - 1)
    def _():
        def body(n2, carry):
            d = jnp.concatenate([a_ref[s, pl.ds(n2 * FFT_PITCH, 2 * FFT_N1, stride=1), :] for s in range(n_slabs)],
                                axis=1)
            y = jnp.dot(v1_ref[...], d.astype(BF16), preferred_element_type=F32)
            for s in range(n_slabs):
                for member in range(2):
                    o_ref[tokens_of(member, s, n2)] = y[member * FFT_HALF_N1:(member + 1) * FFT_HALF_N1,
                                                        s * LANES:(s + 1) * LANES]
            return carry

        lax.fori_loop(0, FFT_N2, body, 0, unroll=FFT_UNROLL)


def _from_pitched(ref):
    return jnp.concatenate([ref[0, j * SEQ_PITCH:j * SEQ_PITCH + FFT_N2, :] for j in range(TILE_GROUPS)], axis=0)


def _merge_kernel(yna_ref, x0_ref, u_ref, yc_ref, gt_ref, skip_ref, gpost_ref, wna_ref, why_ref, wout_ref, o_ref):
    y_hy = (x0_ref[0].astype(F32) * (_from_pitched(yc_ref) + _from_pitched(u_ref) * skip_ref[...])).astype(BF16)
    a = jnp.dot(yna_ref[0], wna_ref[...], preferred_element_type=F32)
    b = jnp.dot(y_hy, why_ref[...], preferred_element_type=F32)
    g_na = gt_ref[0, :, :D_MODEL].astype(F32)
    g_hy = gt_ref[0, :, D_MODEL:].astype(F32)
    m = (g_na * a + g_hy * b).astype(BF16)
    o = jnp.dot(m, wout_ref[...], preferred_element_type=F32)
    ms = jnp.mean(o * o, axis=-1, keepdims=True)
    o_ref[0] = (o * lax.rsqrt(ms + RMS_EPS) * gpost_ref[...]).astype(BF16)


def _gelu_tanh(a):
    return 0.5 * a * (1.0 + jnp.tanh(math.sqrt(2.0 / math.pi) * (a + 0.044715 * (a * a * a))))


def _ffn_kernel(x_ref, prev_ref, next_ref, mix_ref, mix_prev_ref, mix_next_ref, mod_ref, gpre_ref, gpost_ref,
                wup_ref, cw_ref, cb_ref, wdn_ref, o_ref, u_ref, act_ref):
    i = pl.program_id(1)
    n_tiles = pl.num_programs(1)
    mix_halo = mix_prev_ref.shape[1]
    mix = jnp.concatenate([mix_prev_ref[0, mix_halo - HALO:, :], mix_ref[0], mix_next_ref[0, :HALO, :]], axis=0)
    xx = (jnp.concatenate([prev_ref[0], x_ref[0], next_ref[0]], axis=0)
          + mod_ref[0, 2:3, :] * mix.astype(F32))
    x = xx[HALO:HALO + TOKEN_TILE]
    h = _norm_modulate(xx, gpre_ref[...], mod_ref[0, 3:4, :], mod_ref[0, 4:5, :])
    row = lax.broadcasted_iota(jnp.int32, (xx.shape[0], 1), 0)
    inside = ((row >= HALO) | (i > 0)) & ((row < HALO + TOKEN_TILE) | (i < n_tiles - 1))
    h = jnp.where(inside, h, 0.0).astype(BF16)
    starts = list(range(0, D_FF, FF_CHUNK))
    widths = [min(FF_CHUNK, D_FF - lo) for lo in starts]

    def conv(buf, half, s, lo, r0):
        lanes = slice(half * D_FF + lo, half * D_FF + lo + LANES)
        tap = lambda j: u_ref[buf, half, s, pl.ds(HALO - 1 + j + r0, FFN_ROW_BLOCK, stride=1), :]
        return (tap(0) * cw_ref[0:1, lanes] + tap(1) * cw_ref[1:2, lanes] + tap(2) * cw_ref[2:3, lanes]
                + cb_ref[:, lanes])

    def up_project(c):
        for half in range(2):
            lo = half * D_FF + starts[c]
            u = jnp.dot(h, wup_ref[:, lo:lo + widths[c]], preferred_element_type=F32)
            for s in range(widths[c] // LANES):
                u_ref[c % 2, half, s] = u[:, s * LANES:(s + 1) * LANES]

    up_project(0)
    for c in range(len(starts)):
        buf = c % 2
        if c + 1 < len(starts):
            up_project(c + 1)
        for s in range(widths[c] // LANES):
            lo = starts[c] + s * LANES
            for r0 in range(0, TOKEN_TILE, FFN_ROW_BLOCK):
                act_ref[r0:r0 + FFN_ROW_BLOCK, lo:lo + LANES] = (
                    _gelu_tanh(conv(buf, 0, s, lo, r0)) * conv(buf, 1, s, lo, r0)).astype(BF16)
    y = jnp.dot(act_ref[...], wdn_ref[...], preferred_element_type=F32)
    ms = jnp.mean(y * y, axis=-1, keepdims=True)
    y = y * lax.rsqrt(ms + RMS_EPS) * gpost_ref[...]
    o_ref[0] = x + mod_ref[0, 5:6, :] * y


def _const_spec(shape):
    nd = len(shape)
    return pl.BlockSpec(shape, lambda *_: (0,) * nd, pipeline_mode=pl.Buffered(1))


def kernel(x, c, ctx, c_ctx, w_mod, b_mod, norm_mix_pre, norm_mix_post, norm_ffn_pre, norm_ffn_post, w_in, b_in, na_rpb, hy_conv_w, hy_conv_b, hy_filt_w1, hy_filt_b1, hy_filt_w2, hy_filt_b2, hy_filt_w3, hy_filt_b3, hy_sin_freq, hy_decay, hy_skip, w_o_na, w_o_hy, w_out, ffn_w_up, ffn_conv_w, ffn_conv_b, ffn_w_down):
    batch, seq, d = x.shape
    n_ctx = ctx.shape[1]
    assert d == D_MODEL and 2 * seq == FFT_N and seq == GRID_W * GRID_W and batch % 2 == 0
    assert w_mod.shape[0] == 1, "single-layer block"
    n_tiles = seq // TOKEN_TILE
    d_in = w_in.shape[2]
    row2 = lambda a: a.reshape(1, -1)

    c_all = jnp.zeros((8, d), F32).at[:batch].set(c).at[batch].set(c_ctx)
    mod_n = 1024
    mod = pl.pallas_call(
        _mod_kernel,
        grid=(N_MOD * d // mod_n,),
        in_specs=[_const_spec((8, d)),
                  pl.BlockSpec((d, mod_n), lambda j: (0, j)),
                  pl.BlockSpec((1, mod_n), lambda j: (0, j))],
        out_specs=pl.BlockSpec((8, mod_n), lambda j: (0, j)),
        out_shape=jax.ShapeDtypeStruct((8, N_MOD * d), F32),
        compiler_params=_cparams(("arbitrary",)),
        name="mod",
    )(c_all, w_mod[0], row2(b_mod[0]))
    mod_lat = jnp.pad(mod[:batch].reshape(batch, N_MOD, d), ((0, 0), (0, 8 - N_MOD), (0, 0)))
    mod_ctx = jnp.pad(mod[batch].reshape(N_MOD, d), ((0, 8 - N_MOD), (0, 0)))

    w_rows = d // N_HEADS
    bias, w_in_b = pl.pallas_call(
        _attn_bias_kernel,
        grid=(N_HEADS,),
        in_specs=[pl.BlockSpec(memory_space=pltpu.SMEM), pl.BlockSpec((w_rows, d_in), lambda h: (h, 0))],
        out_specs=[pl.BlockSpec((3, 1, ATTN_KEYS, ATTN_Q), lambda h: (0, h, 0, 0)),
                   pl.BlockSpec((w_rows, d_in), lambda h: (h, 0))],
        out_shape=[jax.ShapeDtypeStruct((3, N_HEADS, ATTN_KEYS, ATTN_Q), F32),
                   jax.ShapeDtypeStruct((d, d_in), BF16)],
        scratch_shapes=[pltpu.VMEM((N_BIAS_ROWS, GRID_W, LANES), F32)],
        compiler_params=_cparams(("arbitrary",)),
        name="attn_bias",
    )(na_rpb[0].reshape(-1), w_in[0])
    b_in_r = row2(b_in[0])
    g_mix_pre = row2(norm_mix_pre[0])

    k_ctx, v_ctx = pl.pallas_call(
        _ctx_kv_kernel,
        grid=(batch,),
        in_specs=[pl.BlockSpec((1, n_ctx, d), lambda b: (b, 0, 0)),
                  _const_spec((8, d)), _const_spec((1, d)),
                  _const_spec((d, 2 * D_ATTN)), _const_spec((1, 2 * D_ATTN))],
        out_specs=[pl.BlockSpec((1, n_ctx, D_ATTN), lambda b: (b, 0, 0)),
                   pl.BlockSpec((1, D_ATTN, n_ctx), lambda b: (b, 0, 0))],
        out_shape=[jax.ShapeDtypeStruct((batch, n_ctx, D_ATTN), BF16),
                   jax.ShapeDtypeStruct((batch, D_ATTN, n_ctx), BF16)],
        compiler_params=_cparams(("arbitrary",)),
        name="ctx_kv",
    )(ctx, mod_ctx, g_mix_pre, w_in_b[:, D_ATTN:3 * D_ATTN], b_in_r[:, D_ATTN:3 * D_ATTN])

    cos_t, sin_t = _rope_tables(seq)
    tok = lambda w: pl.BlockSpec((1, TOKEN_TILE, w), lambda b, i: (b, i, 0))
    mod_spec = pl.BlockSpec((1, 8, d), lambda b, i: (b, 0, 0))
    rope_spec = pl.BlockSpec((TOKEN_TILE, LANES), lambda b, i: (i, 0))
    halo_blocks = TOKEN_TILE // HALO
    n_halo_blocks = seq // HALO
    prev_spec = lambda w: pl.BlockSpec((1, HALO, w), lambda b, i: (b, jnp.maximum(i * halo_blocks - 1, 0), 0))
    next_spec = lambda w: pl.BlockSpec(
        (1, HALO, w), lambda b, i: (b, jnp.minimum((i + 1) * halo_blocks, n_halo_blocks - 1), 0))
    pitched_spec = pl.BlockSpec((1, TILE_GROUPS * SEQ_PITCH, D_HYENA), lambda b, i: (b, i, 0))
    pitched_shape = jax.ShapeDtypeStruct((batch, PITCHED_ROWS, D_HYENA), F32)
    later_weights = [ffn_w_up[0], ffn_w_down[0], w_out[0], w_o_na[0], w_o_hy[0]]
    slab_specs = [
        pl.BlockSpec((w.shape[0] // n, w.shape[1]), lambda b, i, n=n: (jnp.minimum(b * n_tiles + i, n - 1), 0))
        for w, n in zip(later_weights, LATER_WEIGHT_SLABS)]
    q_rot, q_plain, k_rot, v_lat, gates, u_p, x0, w_up_c, w_dn_c, w_out_b, w_o_na_b, w_o_hy_b = pl.pallas_call(
        _in_proj_kernel,
        grid=(batch, n_tiles),
        in_specs=[tok(d), prev_spec(d), next_spec(d), mod_spec, _const_spec((1, d)), _const_spec((d, d_in)),
                  _const_spec((1, d_in)), rope_spec, rope_spec,
                  _const_spec((3, 3 * D_HYENA)), _const_spec((1, 3 * D_HYENA))] + slab_specs,
        out_specs=[tok(D_ATTN)] * 3
        + [pl.BlockSpec((1, TOKEN_TILE // ATTN_KEY_CHUNK, D_ATTN, ATTN_KEY_CHUNK), lambda b, i: (b, i, 0, 0)),
           tok(2 * d), pitched_spec, tok(D_HYENA)] + slab_specs,
        out_shape=[jax.ShapeDtypeStruct((batch, seq, D_ATTN), BF16)] * 3
        + [jax.ShapeDtypeStruct((batch, seq // ATTN_KEY_CHUNK, D_ATTN, ATTN_KEY_CHUNK), BF16),
           jax.ShapeDtypeStruct((batch, seq, 2 * d), BF16), pitched_shape,
           jax.ShapeDtypeStruct((batch, seq, D_HYENA), BF16)]
        + [jax.ShapeDtypeStruct(w.shape, BF16) for w in later_weights],
        scratch_shapes=[pltpu.VMEM((3 * D_HYENA // LANES, TOKEN_TILE + 2 * HALO, LANES), F32)],
        compiler_params=_cparams(("arbitrary", "arbitrary")),
        name="in_proj",
    )(x, x, x, mod_lat, g_mix_pre, w_in_b, b_in_r, jnp.asarray(cos_t), jnp.asarray(sin_t),
      hy_conv_w[0], row2(hy_conv_b[0]), *later_weights)

    n_steps = GRID_W // ATTN_ROWS_PER_STEP // ATTN_GROUPS_PER_STEP
    q_spec = pl.BlockSpec((1, ATTN_GROUPS_PER_STEP * ATTN_Q, D_ATTN), lambda b, g: (b, g, 0))
    full = lambda n: pl.BlockSpec((1, n, D_ATTN), lambda b, g: (b, 0, 0))
    bias_spec = _const_spec((3, N_HEADS, ATTN_KEYS, ATTN_Q))
    vt_spec = pl.BlockSpec((1, seq // ATTN_KEY_CHUNK, D_ATTN, ATTN_KEY_CHUNK), lambda b, g: (b, 0, 0, 0))
    vct_spec = pl.BlockSpec((1, D_ATTN, n_ctx), lambda b, g: (b, 0, 0))
    y_na = pl.pallas_call(
        _attn_kernel,
        grid=(batch, n_steps),
        in_specs=[q_spec, q_spec, full(seq), vt_spec, full(n_ctx), vct_spec, bias_spec],
        out_specs=q_spec,
        out_shape=jax.ShapeDtypeStruct((batch, seq, D_ATTN), BF16),
        compiler_params=_cparams(("arbitrary", "arbitrary")),
        name="attn",
    )(q_rot, q_plain, k_rot, v_lat, k_ctx, v_ctx, bias)

    z_t, aux_t = _filter_tables(seq)
    filt_n2 = FILT_ROWS // FFT_HALF_N1
    hid = FILTER_HIDDEN
    w1_pad = jnp.pad(hy_filt_w1[0], ((0, hid - POS_FEATS), (0, 0)))
    block_diag = lambda w: jnp.zeros((2 * hid, 2 * hid), F32).at[:hid, :hid].set(w).at[hid:, hid:].set(w)
    twice = lambda v: jnp.tile(v, (1, 2))
    w3 = hy_filt_w3[0]
    w3_parts = jnp.stack([jnp.concatenate([w3, jnp.zeros_like(w3)], axis=0),
                          jnp.concatenate([jnp.zeros_like(w3), w3], axis=0)])
    k_circ, k_asum = pl.pallas_call(
        _filt_kernel,
        grid=(2, seq // FILT_ROWS),
        in_specs=[pl.BlockSpec((1, FILT_ROWS // 2, 2 * hid), lambda hf, i: (hf, i, 0)),
                  pl.BlockSpec((1, FILT_ROWS, aux_t.shape[2]), lambda hf, i: (hf, i, 0)),
                  _const_spec((2 * hid, 2 * hid)), _const_spec((1, 2 * hid)),
                  _const_spec((2 * hid, 2 * hid)), _const_spec((1, 2 * hid)),
                  pl.BlockSpec((2, 2 * hid, D_HYENA), lambda hf, i: (0, 0, hf)),
                  pl.BlockSpec((1, D_HYENA), lambda hf, i: (0, hf)),
                  _const_spec((2, 2 * hid)),
                  pl.BlockSpec((1, 1, D_HYENA), lambda hf, i: (hf, 0, 0))],
        out_specs=[pl.BlockSpec((filt_n2, FFT_HALF_N1, D_HYENA), lambda hf, i: (i, hf, 0)),
                   pl.BlockSpec((1, 8, D_HYENA), lambda hf, i: (hf, 0, 0))],
        out_shape=[jax.ShapeDtypeStruct((FFT_N2, FFT_N1, D_HYENA), F32),
                   jax.ShapeDtypeStruct((2, 8, D_HYENA), F32)],
        compiler_params=_cparams(("arbitrary", "arbitrary")),
        name="filt",
    )(jnp.asarray(z_t), jnp.asarray(aux_t), block_diag(w1_pad), twice(row2(hy_filt_b1[0])),
      block_diag(hy_filt_w2[0]), twice(row2(hy_filt_b2[0])), w3_parts, row2(hy_filt_b3[0]),
      twice(hy_sin_freq[0]), hy_decay[0].reshape(2, 1, D_HYENA))
    k_circ = k_circ.reshape(FFT_N, D_HYENA)

    mats = _fft_matrices()
    mats = {k: jnp.asarray(v).astype(BF16) for k, v in mats.items()}
    w2_all = mats["w2"]
    step_mat_spec = pl.BlockSpec((FFT_K1_PER_STEP, 2 * FFT_N2, 2 * FFT_N2), lambda cb, s: (s, 0, 0))
    filt_slabs = 2
    kf = pl.pallas_call(
        _filt_fft_kernel,
        grid=(D_HYENA // (filt_slabs * LANES), FFT_STEPS),
        in_specs=[pl.BlockSpec((FFT_N, filt_slabs * LANES), lambda cb, s: (0, cb)),
                  pl.BlockSpec((2, 8, filt_slabs * LANES), lambda cb, s: (0, 0, cb)),
                  _const_spec((2 * FFT_N1, FFT_N1)), step_mat_spec],
        out_specs=pl.BlockSpec((FFT_K1_PER_STEP, 2 * FFT_N2, filt_slabs * LANES), lambda cb, s: (s, 0, cb)),
        out_shape=jax.ShapeDtypeStruct((FFT_N1, 2 * FFT_N2, D_HYENA), BF16),
        scratch_shapes=[pltpu.VMEM((filt_slabs, FFT_N2 * FFT_PITCH, LANES), F32)],
        compiler_params=_cparams(("arbitrary", "arbitrary")),
        name="filt_fft",
    )(k_circ, k_asum, mats["w1_real"], w2_all)

    n_pairs = batch // 2
    pair_block = ((batch, PITCHED_ROWS, LANES), lambda cb, s: (0, 0, cb))
    pair_spec = pl.BlockSpec(*pair_block, pipeline_mode=pl.Buffered(1))
    y_conv = pl.pallas_call(
        _hy_conv_kernel,
        grid=(D_HYENA // LANES, FFT_STEPS),
        in_specs=[pl.BlockSpec(*pair_block),
                  pl.BlockSpec((FFT_K1_PER_STEP, 2 * FFT_N2, LANES), lambda cb, s: (s, 0, cb)),
                  _const_spec((2 * FFT_N1, FFT_N1)), step_mat_spec,
                  _const_spec((FFT_N1, 2 * FFT_N1))],
        out_specs=pair_spec,
        out_shape=pitched_shape,
        scratch_shapes=[pltpu.VMEM((n_pairs, FFT_N2 * FFT_PITCH, LANES), F32)],
        compiler_params=_cparams(("arbitrary", "arbitrary")),
        name="hy_conv",
    )(u_p, kf, mats["w1_data"], w2_all, mats["v1"])

    mix = pl.pallas_call(
        _merge_kernel,
        grid=(batch, n_tiles),
        in_specs=[tok(D_ATTN), tok(D_HYENA), pitched_spec, pitched_spec, tok(2 * d),
                  _const_spec((1, D_HYENA)), _const_spec((1, d)),
                  _const_spec((D_ATTN, d)), _const_spec((D_HYENA, d)), _const_spec((d, d))],
        out_specs=tok(d),
        out_shape=jax.ShapeDtypeStruct((batch, seq, d), BF16),
        compiler_params=_cparams(("arbitrary", "arbitrary")),
        name="merge",
    )(y_na, x0, u_p, y_conv, gates, row2(hy_skip[0]), row2(norm_mix_post[0]),
      w_o_na_b, w_o_hy_b, w_out_b)

    conv_w_c = ffn_conv_w[0]
    conv_b_c = row2(ffn_conv_b[0])
    mix_halo = 2 * HALO
    mix_blocks = TOKEN_TILE // mix_halo
    mix_prev = pl.BlockSpec((1, mix_halo, d), lambda b, i: (b, jnp.maximum(i * mix_blocks - 1, 0), 0))
    mix_next = pl.BlockSpec((1, mix_halo, d),
                            lambda b, i: (b, jnp.minimum((i + 1) * mix_blocks, seq // mix_halo - 1), 0))
    out = pl.pallas_call(
        _ffn_kernel,
        grid=(batch, n_tiles),
        in_specs=[tok(d), prev_spec(d), next_spec(d), tok(d), mix_prev, mix_next,
                  mod_spec, _const_spec((1, d)), _const_spec((1, d)),
                  _const_spec(w_up_c.shape), _const_spec(conv_w_c.shape), _const_spec(conv_b_c.shape),
                  _const_spec(w_dn_c.shape)],
        out_specs=tok(d),
        out_shape=jax.ShapeDtypeStruct((batch, seq, d), F32),
        scratch_shapes=[pltpu.VMEM((2, 2, FF_CHUNK // LANES, TOKEN_TILE + 2 * HALO, LANES), F32),
                        pltpu.VMEM((TOKEN_TILE, D_FF), BF16)],
        compiler_params=_cparams(("arbitrary", "arbitrary")),
        name="ffn",
    )(x, x, x, mix, mix, mix, mod_lat, row2(norm_ffn_pre[0]), row2(norm_ffn_post[0]),
      w_up_c, conv_w_c, conv_b_c, w_dn_c)
    return out
```

```python
import functools
import math

import jax
import jax.numpy as jnp
import numpy as np
from jax import lax
from jax.experimental import pallas as pl
from jax.experimental.pallas import tpu as pltpu

F32 = jnp.float32
BF16 = jnp.bfloat16

D_MODEL = 1024
N_HEADS = 8
HEAD_DIM = 64
D_ATTN = N_HEADS * HEAD_DIM
D_HYENA = 512
GRID_W = 64
WIN_ROWS = 8
WIN_COLS = 16
POS_BANDS = 16
POS_FEATS = 1 + 2 * POS_BANDS
FILTER_HIDDEN = 64
D_FF = 2816
N_MOD = 6
ROPE_BASE = 10000.0
RMS_EPS = 1e-6
NEG_BIAS = -1e30
LOG2_E = math.log2(math.e)

LANES = 128
VMEM_LIMIT_BYTES = 58 * 1024 * 1024

FFT_N1 = 64
FFT_N2 = 128
FFT_N = FFT_N1 * FFT_N2
FFT_HALF_N1 = FFT_N1 // 2
FFT_PITCH = 2 * FFT_N1 + 4
FFT_K1_PER_STEP = 32
FFT_UNROLL = 8
FFT_STEPS = FFT_N1 // FFT_K1_PER_STEP

TOKEN_TILE = 512
FILT_ROWS = 1024
SEQ_PITCH = FFT_N2 + 8
PITCHED_ROWS = FFT_HALF_N1 * SEQ_PITCH
TILE_GROUPS = TOKEN_TILE // FFT_N2
ATTN_ROWS_PER_STEP = 4
ATTN_Q = ATTN_ROWS_PER_STEP * GRID_W
ATTN_KEY_ROWS = 12
ATTN_KEYS = ATTN_KEY_ROWS * GRID_W
ATTN_KEY_CHUNK = 256
ATTN_GROUPS_PER_STEP = 4
FF_CHUNK = 768
LATER_WEIGHT_SLABS = (32, 16, 32, 32, 32)
HALO = 8
FFN_ROW_BLOCK = 64


def _cparams(sem):
    return pltpu.CompilerParams(dimension_semantics=sem, vmem_limit_bytes=VMEM_LIMIT_BYTES)


@functools.lru_cache(maxsize=None)
def _rope_tables(seq):
    pos = np.arange(seq)
    row = pos // GRID_W
    col = pos % GRID_W
    n_pairs = HEAD_DIM // 4
    inv = ROPE_BASE ** (-np.arange(n_pairs, dtype=np.float64) / n_pairs)
    lane = np.arange(LANES) % HEAD_DIM
    p = np.where(lane[None, :] < HEAD_DIM // 2, row[:, None], col[:, None]).astype(np.float64)
    ang = p * inv[lane % n_pairs][None, :]
    sign = np.where((lane % (2 * n_pairs)) < n_pairs, -1.0, 1.0)
    return np.cos(ang).astype(np.float32), (np.sin(ang) * sign[None, :]).astype(np.float32)


@functools.lru_cache(maxsize=None)
def _filter_tables(seq):
    assert 2 * seq == FFT_N
    half, n2, n1 = np.meshgrid(np.arange(2), np.arange(FFT_N2), np.arange(FFT_HALF_N1), indexing="ij")
    n = FFT_N2 * (half * FFT_HALF_N1 + n1) + n2
    fwd = n < seq
    m = n - seq
    valid = fwd | (m >= 1)
    pos = np.where(valid, np.where(fwd, n, seq - m), 0).astype(np.float64).reshape(2, seq)
    t = pos / max(seq - 1, 1)
    bands = np.linspace(1e-4, POS_BANDS - 1, POS_BANDS)
    ang = (2.0 * math.pi / seq) * pos[..., None] * bands
    z = np.zeros((2, seq, 64), np.float64)
    z[..., 0] = t
    z[..., 1:1 + POS_BANDS] = np.cos(ang)
    z[..., 1 + POS_BANDS:POS_FEATS] = -np.sin(ang)
    aux = np.zeros((2, seq, 8), np.float64)
    aux[..., 0] = t
    aux[..., 1] = valid.reshape(2, seq)
    zp = z.reshape(2, seq // FILT_ROWS, 2, FILT_ROWS // 2, 64).transpose(0, 1, 3, 2, 4).reshape(2, seq // 2, 128)
    return zp.astype(np.float32), aux.astype(np.float32)


def _realify(m):
    return np.block([[m.real, -m.imag], [m.imag, m.real]])


@functools.lru_cache(maxsize=None)
def _fft_matrices():
    n1 = np.arange(FFT_N1)
    n2 = np.arange(FFT_N2)
    k1 = np.arange(FFT_N1)
    k2 = np.arange(FFT_N2)
    f1 = np.exp(-2j * np.pi * np.outer(k1, n1) / FFT_N1)
    w1_data = _realify(f1[:, :FFT_HALF_N1])
    w1_real = np.concatenate([f1.real, f1.imag], axis=0)
    v1 = _realify(np.conj(f1.T)[:FFT_HALF_N1, :] / FFT_N)
    f2 = np.exp(-2j * np.pi * np.outer(k2, n2) / FFT_N2)
    tw = np.exp(-2j * np.pi * np.outer(k1, n2) / FFT_N)
    w2 = np.stack([_realify(f2 * tw[a][None, :]) for a in range(FFT_N1)])
    return {k: v.astype(np.float32) for k, v in
            dict(w1_data=w1_data, w1_real=w1_real, v1=v1, w2=w2).items()}


N_BIAS_ROWS = 2 * WIN_ROWS - 1
N_BIAS_COLS = 2 * WIN_COLS - 1


def _attn_bias_row_index():
    rows = GRID_W
    groups = (0, 2, rows // ATTN_ROWS_PER_STEP - 1)
    dr = np.full((3, ATTN_ROWS_PER_STEP, ATTN_KEY_ROWS), -1, np.int32)
    for v, g in enumerate(groups):
        ws = min(max(ATTN_ROWS_PER_STEP * g - WIN_ROWS // 2, 0), rows - ATTN_KEY_ROWS)
        for i in range(ATTN_ROWS_PER_STEP):
            r = ATTN_ROWS_PER_STEP * g + i
            r_start = min(max(r - WIN_ROWS // 2, 0), rows - WIN_ROWS)
            for j in range(ATTN_KEY_ROWS):
                kr = ws + j
                if r_start <= kr < r_start + WIN_ROWS:
                    dr[v, i, j] = kr - r + (WIN_ROWS - 1)
    return dr


def _attn_bias_kernel(rpb_ref, w_ref, o_ref, wb_ref, t_ref):
    wb_ref[...] = w_ref[...].astype(BF16)
    head = pl.program_id(0)
    kc = lax.broadcasted_iota(jnp.int32, (GRID_W, LANES), 0)
    lane = lax.broadcasted_iota(jnp.int32, (GRID_W, LANES), 1)
    qc = lane % GRID_W
    c_start = jnp.clip(qc - WIN_COLS // 2, 0, GRID_W - WIN_COLS)
    col_in = (kc >= c_start) & (kc < c_start + WIN_COLS)
    dc = jnp.clip(kc - qc, 1 - WIN_COLS, WIN_COLS - 1) + (WIN_COLS - 1)
    base = head * (N_BIAS_ROWS * N_BIAS_COLS)
    for r in range(N_BIAS_ROWS):
        t = jnp.full((GRID_W, LANES), NEG_BIAS, F32)
        for cidx in range(N_BIAS_COLS):
            t = jnp.where(col_in & (dc == cidx), rpb_ref[base + r * N_BIAS_COLS + cidx] * LOG2_E, t)
        t_ref[r] = t
    dr = _attn_bias_row_index()
    low_half = lane < GRID_W
    masked = jnp.full((GRID_W, LANES), NEG_BIAS, F32)
    for v in range(3):
        for i in range(0, ATTN_ROWS_PER_STEP, 2):
            for j in range(ATTN_KEY_ROWS):
                lo = t_ref[int(dr[v, i, j])] if dr[v, i, j] >= 0 else masked
                hi = t_ref[int(dr[v, i + 1, j])] if dr[v, i + 1, j] >= 0 else masked
                o_ref[v, 0, j * GRID_W:(j + 1) * GRID_W, i * GRID_W:(i + 2) * GRID_W] = jnp.where(
                    low_half, lo, hi)


def _mod_kernel(c_ref, w_ref, b_ref, o_ref):
    c = c_ref[...]
    s = c * jax.nn.sigmoid(c)
    o_ref[...] = jnp.dot(s, w_ref[...], precision=lax.Precision.HIGHEST,
                         preferred_element_type=F32) + b_ref[...]


def _norm_modulate(x, gain, shift, scale):
    ms = jnp.mean(x * x, axis=-1, keepdims=True)
    y = x * lax.rsqrt(ms + RMS_EPS) * gain
    return y * (1.0 + scale) + shift


def _rope(t, cos, sin_signed):
    n_pairs = HEAD_DIM // 4
    lane = lax.broadcasted_iota(jnp.int32, t.shape, 1)
    first = (lane % (2 * n_pairs)) < n_pairs
    partner = jnp.where(first, pltpu.roll(t, LANES - n_pairs, 1), pltpu.roll(t, n_pairs, 1))
    return t * cos + partner * sin_signed


def _in_proj_kernel(x_ref, prev_ref, next_ref, mod_ref, g_ref, w_ref, b_ref, cos_ref, sin_ref, cw_ref, cb_ref,
                    *rest):
    n_cast = len(LATER_WEIGHT_SLABS)
    cast_in, rest = rest[:n_cast], rest[n_cast:]
    (qr_ref, qp_ref, kr_ref, v_ref, gt_ref, u_ref, x0_ref), rest = rest[:7], rest[7:]
    cast_out, (hy_ref,) = rest[:n_cast], rest[n_cast:]
    for src, dst in zip(cast_in, cast_out):
        dst[...] = src[...].astype(BF16)
    i = pl.program_id(1)
    n_tiles = pl.num_programs(1)
    xx = jnp.concatenate([prev_ref[0], x_ref[0], next_ref[0]], axis=0)
    h_ext = _norm_modulate(xx, g_ref[...], mod_ref[0, 0:1, :], mod_ref[0, 1:2, :]).astype(BF16)
    h = h_ext[HALO:HALO + TOKEN_TILE]
    cos = cos_ref[...]
    sin = sin_ref[...]

    def proj(lo, hi, rows=h):
        return jnp.dot(rows, w_ref[:, lo:hi], preferred_element_type=F32) + b_ref[:, lo:hi]

    q = proj(0, D_ATTN) * (HEAD_DIM ** -0.5 * LOG2_E)
    qp_ref[0] = q.astype(BF16)
    k = proj(D_ATTN, 2 * D_ATTN)
    for c in range(D_ATTN // LANES):
        lanes = slice(c * LANES, (c + 1) * LANES)
        qr_ref[0, :, lanes] = _rope(q[:, lanes], cos, sin).astype(BF16)
        kr_ref[0, :, lanes] = _rope(k[:, lanes], cos, sin).astype(BF16)
    v_t = proj(2 * D_ATTN, 3 * D_ATTN).T.astype(BF16)
    for c in range(TOKEN_TILE // ATTN_KEY_CHUNK):
        v_ref[0, c] = v_t[:, c * ATTN_KEY_CHUNK:(c + 1) * ATTN_KEY_CHUNK]
    hy_lo = 3 * D_ATTN
    gl_lo = hy_lo + 3 * D_HYENA
    for c in range(4):
        w = D_MODEL // 2
        gt_ref[0, :, c * w:(c + 1) * w] = jax.nn.sigmoid(
            proj(gl_lo + c * w, gl_lo + (c + 1) * w)).astype(BF16)

    slabs_per_part = D_HYENA // LANES
    for c in range(3):
        hy = proj(hy_lo + c * D_HYENA, hy_lo + (c + 1) * D_HYENA, h_ext)
        for s in range(slabs_per_part):
            hy_ref[c * slabs_per_part + s] = hy[:, s * LANES:(s + 1) * LANES]
    zero_row = jnp.zeros((1, LANES), F32)

    @pl.when(i == 0)
    def _():
        for s in range(3 * slabs_per_part):
            hy_ref[s, HALO - 1:HALO, :] = zero_row

    @pl.when(i == n_tiles - 1)
    def _():
        for s in range(3 * slabs_per_part):
            hy_ref[s, HALO + TOKEN_TILE:HALO + TOKEN_TILE + 1, :] = zero_row

    def conv(s, r0, rows):
        lanes = slice(s * LANES, (s + 1) * LANES)
        tap = lambda j: hy_ref[s, pl.ds(HALO - 1 + j + r0, rows, stride=1), :]
        return (tap(0) * cw_ref[0:1, lanes] + tap(1) * cw_ref[1:2, lanes] + tap(2) * cw_ref[2:3, lanes]
                + cb_ref[:, lanes])

    for s in range(slabs_per_part):
        lanes = slice(s * LANES, (s + 1) * LANES)
        for j in range(TILE_GROUPS):
            r0 = j * FFT_N2
            x0_ref[0, r0:r0 + FFT_N2, lanes] = conv(s, r0, FFT_N2).astype(BF16)
            u_ref[0, j * SEQ_PITCH:j * SEQ_PITCH + FFT_N2, lanes] = (
                conv(slabs_per_part + s, r0, FFT_N2) * conv(2 * slabs_per_part + s, r0, FFT_N2))
            u_ref[0, j * SEQ_PITCH + FFT_N2:(j + 1) * SEQ_PITCH, lanes] = jnp.zeros((SEQ_PITCH - FFT_N2, LANES), F32)


def _ctx_kv_kernel(x_ref, mod_ref, g_ref, w_ref, b_ref, k_ref, v_ref):
    h = _norm_modulate(x_ref[0], g_ref[...], mod_ref[0:1, :], mod_ref[1:2, :]).astype(BF16)
    kv = jnp.dot(h, w_ref[...], preferred_element_type=F32) + b_ref[...]
    k_ref[0] = kv[:, :D_ATTN].astype(BF16)
    v_ref[0] = kv[:, D_ATTN:].T.astype(BF16)


def _attn_window_start(g):
    return jnp.clip(ATTN_ROWS_PER_STEP * g - WIN_ROWS // 2, 0, GRID_W - ATTN_KEY_ROWS)


def _attn_kernel(qr_ref, qp_ref, k_ref, vt_ref, kc_ref, vct_ref, bias_ref, o_ref):
    step = pl.program_id(1)
    n_groups = GRID_W // ATTN_ROWS_PER_STEP
    nt = (((1,), (1,)), ((), ()))
    quad_w = 4 * HEAD_DIM
    lane = lax.broadcasted_iota(jnp.int32, (1, quad_w), 1)
    zero = jnp.zeros((), BF16)

    def group_params(sub):
        g = step * ATTN_GROUPS_PER_STEP + sub
        win = _attn_window_start(g)
        key0 = pl.multiple_of(win * GRID_W, ATTN_KEY_CHUNK)
        chunk0 = win // (ATTN_KEY_CHUNK // GRID_W)
        variant = (g > 0).astype(jnp.int32) + (g == n_groups - 1).astype(jnp.int32)
        return key0, chunk0, variant

    params = [group_params(sub) for sub in range(ATTN_GROUPS_PER_STEP)]

    def scores(item):
        sub, head = item
        key0, _, variant = params[sub]
        qrows = slice(sub * ATTN_Q, (sub + 1) * ATTN_Q)
        quad, hh = divmod(head, 4)
        ql = slice(quad * quad_w, (quad + 1) * quad_w)
        mine = (lane >= hh * HEAD_DIM) & (lane < (hh + 1) * HEAD_DIM)
        s_nb = lax.dot_general(k_ref[0, pl.ds(key0, ATTN_KEYS), ql], jnp.where(mine, qr_ref[0, qrows, ql], zero), nt,
                               preferred_element_type=F32)
        s_cx = lax.dot_general(kc_ref[0, :, ql], jnp.where(mine, qp_ref[0, qrows, ql], zero), nt,
                               preferred_element_type=F32)
        return s_nb + bias_ref[variant, head], s_cx

    def with_ones(v):
        return jnp.concatenate([v, jnp.ones((16, v.shape[1]), BF16)], axis=0)

    def probs(s_nb, s_cx):
        m = jnp.maximum(jnp.max(s_nb, axis=0, keepdims=True), jnp.max(s_cx, axis=0, keepdims=True))
        return jnp.exp2(s_nb - m).astype(BF16), jnp.exp2(s_cx - m).astype(BF16)

    def values(item, p_nb, p_cx):
        sub, head = item
        chunk0 = params[sub][1]
        rows = slice(head * HEAD_DIM, (head + 1) * HEAD_DIM)
        v_win = jnp.concatenate([vt_ref[0, chunk0 + c, rows, :] for c in range(ATTN_KEYS // ATTN_KEY_CHUNK)],
                                axis=1)
        o = (jnp.dot(with_ones(v_win), p_nb, preferred_element_type=F32)
             + jnp.dot(with_ones(vct_ref[0, rows, :]), p_cx, preferred_element_type=F32))
        return o[:HEAD_DIM] / o[HEAD_DIM:HEAD_DIM + 1]

    items = [(sub, head) for sub in range(ATTN_GROUPS_PER_STEP) for head in range(N_HEADS)]
    outs = []
    s_q = {0: scores(items[0]), 1: scores(items[1])}
    p_q = {0: probs(*s_q.pop(0))}
    for n, item in enumerate(items):
        if n + 2 < len(items):
            s_q[n + 2] = scores(items[n + 2])
        if n + 1 < len(items):
            p_q[n + 1] = probs(*s_q.pop(n + 1))
        outs.append(values(item, *p_q.pop(n)))
    for sub in range(ATTN_GROUPS_PER_STEP):
        o_ref[0, sub * ATTN_Q:(sub + 1) * ATTN_Q, :] = jnp.concatenate(
            outs[sub * N_HEADS:(sub + 1) * N_HEADS], axis=0).T.astype(BF16)


def _filt_kernel(z_ref, aux_ref, w1_ref, b1_ref, w2_ref, b2_ref, w3_ref, b3_ref, freq_ref, decay_ref,
                 o_ref, asum_ref):
    i = pl.program_id(1)
    hp = lax.Precision.HIGHEST
    h = jnp.sin(freq_ref[0:1, :] * (jnp.dot(z_ref[0], w1_ref[...], precision=hp,
                                            preferred_element_type=F32) + b1_ref[...]))
    h = jnp.sin(freq_ref[1:2, :] * (jnp.dot(h, w2_ref[...], precision=hp,
                                            preferred_element_type=F32) + b2_ref[...]))
    taps = jnp.concatenate([jnp.dot(h, w3_ref[part], precision=hp, preferred_element_type=F32)
                            for part in range(2)], axis=0) + b3_ref[...]
    t = aux_ref[0, :, 0:1]
    valid = aux_ref[0, :, 1:2] > 0.5
    k = jnp.where(valid, taps * jnp.exp(-t * jnp.abs(decay_ref[0])), 0.0)
    part = jnp.sum(jnp.abs(k), axis=0, keepdims=True)

    @pl.when(i == 0)
    def _():
        asum_ref[...] = jnp.zeros_like(asum_ref)

    asum_ref[0] += jnp.broadcast_to(part, asum_ref.shape[1:])
    o_ref[...] = k.reshape(o_ref.shape)


def _fft_stage1(load_group, w1_ref, a_ref):
    n_slabs = a_ref.shape[0]

    def body(n2, carry):
        x = load_group(n2)
        a = jnp.dot(w1_ref[...], x, preferred_element_type=F32)
        for s in range(n_slabs):
            a_ref[s, pl.ds(n2 * FFT_PITCH, 2 * FFT_N1, stride=1), :] = a[:, s * LANES:(s + 1) * LANES]
        return carry

    lax.fori_loop(0, FFT_N2, body, 0, unroll=FFT_UNROLL)


def _fft_load_k1(a_ref, k1):
    parts = []
    for off in (0, FFT_N1):
        parts.append(jnp.concatenate(
            [a_ref[s, pl.ds(k1 + off, FFT_N2, stride=FFT_PITCH), :] for s in range(a_ref.shape[0])], axis=1))
    return jnp.concatenate(parts, axis=0)


def _filt_fft_kernel(k_ref, asum_ref, w1_ref, w2_ref, o_ref, a_ref):
    step = pl.program_id(1)

    @pl.when(step == 0)
    def _():
        norm = asum_ref[0, 0:1, :] + asum_ref[1, 0:1, :]

        def load_group(n2):
            return (k_ref[pl.ds(pl.multiple_of(n2 * FFT_N1, FFT_N1), FFT_N1), :] / norm).astype(BF16)
        _fft_stage1(load_group, w1_ref, a_ref)

    for j in range(FFT_K1_PER_STEP):
        b = _fft_load_k1(a_ref, step * FFT_K1_PER_STEP + j).astype(BF16)
        o_ref[j] = jnp.dot(w2_ref[j], b, preferred_element_type=F32).astype(BF16)


def _hy_conv_kernel(u_ref, kf_ref, w1_ref, w2_ref, v1_ref, o_ref, a_ref):
    step = pl.program_id(1)
    n_slabs = a_ref.shape[0]

    def tokens_of(member, s, n2):
        return (2 * s + member, pl.ds(n2, FFT_HALF_N1, stride=SEQ_PITCH), slice(None))

    @pl.when(step == 0)
    def _():
        for sample in range(o_ref.shape[0]):
            for grp in range(FFT_HALF_N1):
                o_ref[sample, grp * SEQ_PITCH + FFT_N2:(grp + 1) * SEQ_PITCH, :] = jnp.zeros(
                    (SEQ_PITCH - FFT_N2, LANES), F32)

        def load_group(n2):
            return jnp.concatenate(
                [jnp.concatenate([u_ref[tokens_of(member, s, n2)] for s in range(n_slabs)], axis=1)
                 for member in range(2)], axis=0).astype(BF16)
        _fft_stage1(load_group, w1_ref, a_ref)

    for j in range(FFT_K1_PER_STEP):
        k1 = step * FFT_K1_PER_STEP + j
        b = _fft_load_k1(a_ref, k1).astype(BF16)
        x = jnp.dot(w2_ref[j], b, preferred_element_type=F32)
        kf = kf_ref[j].astype(F32)
        kf = jnp.concatenate([kf] * n_slabs, axis=1)
        xr, xi = x[:FFT_N2], x[FFT_N2:]
        kr, ki = kf[:FFT_N2], kf[FFT_N2:]
        y = jnp.concatenate([xr * kr - xi * ki, xr * ki + xi * kr], axis=0).astype(BF16)
        d = lax.dot_general(w2_ref[j], y, (((0,), (0,)), ((), ())),
                            preferred_element_type=F32)
        for s in range(n_slabs):
            lanes = slice(s * LANES, (s + 1) * LANES)
            a_ref[s, pl.ds(k1, FFT_N2, stride=FFT_PITCH), :] = d[:FFT_N2, lanes]
            a_ref[s, pl.ds(k1 + FFT_N1, FFT_N2, stride=FFT_PITCH), :] = d[FFT_N2:, lanes]

    @pl.when(step == FFT_STEPS - 1)
    def _():
        def body(n2, carry):
            d = jnp.concatenate([a_ref[s, pl.ds(n2 * FFT_PITCH, 2 * FFT_N1, stride=1), :] for s in range(n_slabs)],
                                axis=1)
            y = jnp.dot(v1_ref[...], d.astype(BF16), preferred_element_type=F32)
            for s in range(n_slabs):
                for member in range(2):
                    o_ref[tokens_of(member, s, n2)] = y[member * FFT_HALF_N1:(member + 1) * FFT_HALF_N1,
                                                        s * LANES:(s + 1) * LANES]
            return carry

        lax.fori_loop(0, FFT_N2, body, 0, unroll=FFT_UNROLL)


def _from_pitched(ref):
    return jnp.concatenate([ref[0, j * SEQ_PITCH:j * SEQ_PITCH + FFT_N2, :] for j in range(TILE_GROUPS)], axis=0)


def _merge_kernel(yna_ref, x0_ref, u_ref, yc_ref, gt_ref, skip_ref, gpost_ref, wna_ref, why_ref, wout_ref, o_ref):
    y_hy = (x0_ref[0].astype(F32) * (_from_pitched(yc_ref) + _from_pitched(u_ref) * skip_ref[...])).astype(BF16)
    a = jnp.dot(yna_ref[0], wna_ref[...], preferred_element_type=F32)
    b = jnp.dot(y_hy, why_ref[...], preferred_element_type=F32)
    g_na = gt_ref[0, :, :D_MODEL].astype(F32)
    g_hy = gt_ref[0, :, D_MODEL:].astype(F32)
    m = (g_na * a + g_hy * b).astype(BF16)
    o = jnp.dot(m, wout_ref[...], preferred_element_type=F32)
    ms = jnp.mean(o * o, axis=-1, keepdims=True)
    o_ref[0] = (o * lax.rsqrt(ms + RMS_EPS) * gpost_ref[...]).astype(BF16)


def _gelu_tanh(a):
    return 0.5 * a * (1.0 + jnp.tanh(math.sqrt(2.0 / math.pi) * (a + 0.044715 * (a * a * a))))


def _ffn_kernel(x_ref, prev_ref, next_ref, mix_ref, mix_prev_ref, mix_next_ref, mod_ref, gpre_ref, gpost_ref,
                wup_ref, cw_ref, cb_ref, wdn_ref, o_ref, u_ref, act_ref):
    i = pl.program_id(1)
    n_tiles = pl.num_programs(1)
    mix_halo = mix_prev_ref.shape[1]
    mix = jnp.concatenate([mix_prev_ref[0, mix_halo - HALO:, :], mix_ref[0], mix_next_ref[0, :HALO, :]], axis=0)
    xx = (jnp.concatenate([prev_ref[0], x_ref[0], next_ref[0]], axis=0)
          + mod_ref[0, 2:3, :] * mix.astype(F32))
    x = xx[HALO:HALO + TOKEN_TILE]
    h = _norm_modulate(xx, gpre_ref[...], mod_ref[0, 3:4, :], mod_ref[0, 4:5, :])
    row = lax.broadcasted_iota(jnp.int32, (xx.shape[0], 1), 0)
    inside = ((row >= HALO) | (i > 0)) & ((row < HALO + TOKEN_TILE) | (i < n_tiles - 1))
    h = jnp.where(inside, h, 0.0).astype(BF16)
    starts = list(range(0, D_FF, FF_CHUNK))
    widths = [min(FF_CHUNK, D_FF - lo) for lo in starts]

    def conv(buf, half, s, lo, r0):
        lanes = slice(half * D_FF + lo, half * D_FF + lo + LANES)
        tap = lambda j: u_ref[buf, half, s, pl.ds(HALO - 1 + j + r0, FFN_ROW_BLOCK, stride=1), :]
        return (tap(0) * cw_ref[0:1, lanes] + tap(1) * cw_ref[1:2, lanes] + tap(2) * cw_ref[2:3, lanes]
                + cb_ref[:, lanes])

    def up_project(c):
        for half in range(2):
            lo = half * D_FF + starts[c]
            u = jnp.dot(h, wup_ref[:, lo:lo + widths[c]], preferred_element_type=F32)
            for s in range(widths[c] // LANES):
                u_ref[c % 2, half, s] = u[:, s * LANES:(s + 1) * LANES]

    up_project(0)
    for c in range(len(starts)):
        buf = c % 2
        if c + 1 < len(starts):
            up_project(c + 1)
        for s in range(widths[c] // LANES):
            lo = starts[c] + s * LANES
            for r0 in range(0, TOKEN_TILE, FFN_ROW_BLOCK):
                act_ref[r0:r0 + FFN_ROW_BLOCK, lo:lo + LANES] = (
                    _gelu_tanh(conv(buf, 0, s, lo, r0)) * conv(buf, 1, s, lo, r0)).astype(BF16)
    y = jnp.dot(act_ref[...], wdn_ref[...], preferred_element_type=F32)
    ms = jnp.mean(y * y, axis=-1, keepdims=True)
    y = y * lax.rsqrt(ms + RMS_EPS) * gpost_ref[...]
    o_ref[0] = x + mod_ref[0, 5:6, :] * y


def _const_spec(shape):
    nd = len(shape)
    return pl.BlockSpec(shape, lambda *_: (0,) * nd, pipeline_mode=pl.Buffered(1))


def kernel(x, c, ctx, c_ctx, w_mod, b_mod, norm_mix_pre, norm_mix_post, norm_ffn_pre, norm_ffn_post, w_in, b_in, na_rpb, hy_conv_w, hy_conv_b, hy_filt_w1, hy_filt_b1, hy_filt_w2, hy_filt_b2, hy_filt_w3, hy_filt_b3, hy_sin_freq, hy_decay, hy_skip, w_o_na, w_o_hy, w_out, ffn_w_up, ffn_conv_w, ffn_conv_b, ffn_w_down):
    batch, seq, d = x.shape
    n_ctx = ctx.shape[1]
    assert d == D_MODEL and 2 * seq == FFT_N and seq == GRID_W * GRID_W and batch % 2 == 0
    assert w_mod.shape[0] == 1, "single-layer block"
    n_tiles = seq // TOKEN_TILE
    d_in = w_in.shape[2]
    row2 = lambda a: a.reshape(1, -1)

    c_all = jnp.zeros((8, d), F32).at[:batch].set(c).at[batch].set(c_ctx)
    mod_n = 1024
    mod = pl.pallas_call(
        _mod_kernel,
        grid=(N_MOD * d // mod_n,),
        in_specs=[_const_spec((8, d)),
                  pl.BlockSpec((d, mod_n), lambda j: (0, j)),
                  pl.BlockSpec((1, mod_n), lambda j: (0, j))],
        out_specs=pl.BlockSpec((8, mod_n), lambda j: (0, j)),
        out_shape=jax.ShapeDtypeStruct((8, N_MOD * d), F32),
        compiler_params=_cparams(("arbitrary",)),
        name="mod",
    )(c_all, w_mod[0], row2(b_mod[0]))
    mod_lat = jnp.pad(mod[:batch].reshape(batch, N_MOD, d), ((0, 0), (0, 8 - N_MOD), (0, 0)))
    mod_ctx = jnp.pad(mod[batch].reshape(N_MOD, d), ((0, 8 - N_MOD), (0, 0)))

    w_rows = d // N_HEADS
    bias, w_in_b = pl.pallas_call(
        _attn_bias_kernel,
        grid=(N_HEADS,),
        in_specs=[pl.BlockSpec(memory_space=pltpu.SMEM), pl.BlockSpec((w_rows, d_in), lambda h: (h, 0))],
        out_specs=[pl.BlockSpec((3, 1, ATTN_KEYS, ATTN_Q), lambda h: (0, h, 0, 0)),
                   pl.BlockSpec((w_rows, d_in), lambda h: (h, 0))],
        out_shape=[jax.ShapeDtypeStruct((3, N_HEADS, ATTN_KEYS, ATTN_Q), F32),
                   jax.ShapeDtypeStruct((d, d_in), BF16)],
        scratch_shapes=[pltpu.VMEM((N_BIAS_ROWS, GRID_W, LANES), F32)],
        compiler_params=_cparams(("arbitrary",)),
        name="attn_bias",
    )(na_rpb[0].reshape(-1), w_in[0])
    b_in_r = row2(b_in[0])
    g_mix_pre = row2(norm_mix_pre[0])

    k_ctx, v_ctx = pl.pallas_call(
        _ctx_kv_kernel,
        grid=(batch,),
        in_specs=[pl.BlockSpec((1, n_ctx, d), lambda b: (b, 0, 0)),
                  _const_spec((8, d)), _const_spec((1, d)),
                  _const_spec((d, 2 * D_ATTN)), _const_spec((1, 2 * D_ATTN))],
        out_specs=[pl.BlockSpec((1, n_ctx, D_ATTN), lambda b: (b, 0, 0)),
                   pl.BlockSpec((1, D_ATTN, n_ctx), lambda b: (b, 0, 0))],
        out_shape=[jax.ShapeDtypeStruct((batch, n_ctx, D_ATTN), BF16),
                   jax.ShapeDtypeStruct((batch, D_ATTN, n_ctx), BF16)],
        compiler_params=_cparams(("arbitrary",)),
        name="ctx_kv",
    )(ctx, mod_ctx, g_mix_pre, w_in_b[:, D_ATTN:3 * D_ATTN], b_in_r[:, D_ATTN:3 * D_ATTN])

    cos_t, sin_t = _rope_tables(seq)
    tok = lambda w: pl.BlockSpec((1, TOKEN_TILE, w), lambda b, i: (b, i, 0))
    mod_spec = pl.BlockSpec((1, 8, d), lambda b, i: (b, 0, 0))
    rope_spec = pl.BlockSpec((TOKEN_TILE, LANES), lambda b, i: (i, 0))
    halo_blocks = TOKEN_TILE // HALO
    n_halo_blocks = seq // HALO
    prev_spec = lambda w: pl.BlockSpec((1, HALO, w), lambda b, i: (b, jnp.maximum(i * halo_blocks - 1, 0), 0))
    next_spec = lambda w: pl.BlockSpec(
        (1, HALO, w), lambda b, i: (b, jnp.minimum((i + 1) * halo_blocks, n_halo_blocks - 1), 0))
    pitched_spec = pl.BlockSpec((1, TILE_GROUPS * SEQ_PITCH, D_HYENA), lambda b, i: (b, i, 0))
    pitched_shape = jax.ShapeDtypeStruct((batch, PITCHED_ROWS, D_HYENA), F32)
    later_weights = [ffn_w_up[0], ffn_w_down[0], w_out[0], w_o_na[0], w_o_hy[0]]
    slab_specs = [
        pl.BlockSpec((w.shape[0] // n, w.shape[1]), lambda b, i, n=n: (jnp.minimum(b * n_tiles + i, n - 1), 0))
        for w, n in zip(later_weights, LATER_WEIGHT_SLABS)]
    q_rot, q_plain, k_rot, v_lat, gates, u_p, x0, w_up_c, w_dn_c, w_out_b, w_o_na_b, w_o_hy_b = pl.pallas_call(
        _in_proj_kernel,
        grid=(batch, n_tiles),
        in_specs=[tok(d), prev_spec(d), next_spec(d), mod_spec, _const_spec((1, d)), _const_spec((d, d_in)),
                  _const_spec((1, d_in)), rope_spec, rope_spec,
                  _const_spec((3, 3 * D_HYENA)), _const_spec((1, 3 * D_HYENA))] + slab_specs,
        out_specs=[tok(D_ATTN)] * 3
        + [pl.BlockSpec((1, TOKEN_TILE // ATTN_KEY_CHUNK, D_ATTN, ATTN_KEY_CHUNK), lambda b, i: (b, i, 0, 0)),
           tok(2 * d), pitched_spec, tok(D_HYENA)] + slab_specs,
        out_shape=[jax.ShapeDtypeStruct((batch, seq, D_ATTN), BF16)] * 3
        + [jax.ShapeDtypeStruct((batch, seq // ATTN_KEY_CHUNK, D_ATTN, ATTN_KEY_CHUNK), BF16),
           jax.ShapeDtypeStruct((batch, seq, 2 * d), BF16), pitched_shape,
           jax.ShapeDtypeStruct((batch, seq, D_HYENA), BF16)]
        + [jax.ShapeDtypeStruct(w.shape, BF16) for w in later_weights],
        scratch_shapes=[pltpu.VMEM((3 * D_HYENA // LANES, TOKEN_TILE + 2 * HALO, LANES), F32)],
        compiler_params=_cparams(("arbitrary", "arbitrary")),
        name="in_proj",
    )(x, x, x, mod_lat, g_mix_pre, w_in_b, b_in_r, jnp.asarray(cos_t), jnp.asarray(sin_t),
      hy_conv_w[0], row2(hy_conv_b[0]), *later_weights)

    n_steps = GRID_W // ATTN_ROWS_PER_STEP // ATTN_GROUPS_PER_STEP
    q_spec = pl.BlockSpec((1, ATTN_GROUPS_PER_STEP * ATTN_Q, D_ATTN), lambda b, g: (b, g, 0))
    full = lambda n: pl.BlockSpec((1, n, D_ATTN), lambda b, g: (b, 0, 0))
    bias_spec = _const_spec((3, N_HEADS, ATTN_KEYS, ATTN_Q))
    vt_spec = pl.BlockSpec((1, seq // ATTN_KEY_CHUNK, D_ATTN, ATTN_KEY_CHUNK), lambda b, g: (b, 0, 0, 0))
    vct_spec = pl.BlockSpec((1, D_ATTN, n_ctx), lambda b, g: (b, 0, 0))
    y_na = pl.pallas_call(
        _attn_kernel,
        grid=(batch, n_steps),
        in_specs=[q_spec, q_spec, full(seq), vt_spec, full(n_ctx), vct_spec, bias_spec],
        out_specs=q_spec,
        out_shape=jax.ShapeDtypeStruct((batch, seq, D_ATTN), BF16),
        compiler_params=_cparams(("arbitrary", "arbitrary")),
        name="attn",
    )(q_rot, q_plain, k_rot, v_lat, k_ctx, v_ctx, bias)

    z_t, aux_t = _filter_tables(seq)
    filt_n2 = FILT_ROWS // FFT_HALF_N1
    hid = FILTER_HIDDEN
    w1_pad = jnp.pad(hy_filt_w1[0], ((0, hid - POS_FEATS), (0, 0)))
    block_diag = lambda w: jnp.zeros((2 * hid, 2 * hid), F32).at[:hid, :hid].set(w).at[hid:, hid:].set(w)
    twice = lambda v: jnp.tile(v, (1, 2))
    w3 = hy_filt_w3[0]
    w3_parts = jnp.stack([jnp.concatenate([w3, jnp.zeros_like(w3)], axis=0),
                          jnp.concatenate([jnp.zeros_like(w3), w3], axis=0)])
    k_circ, k_asum = pl.pallas_call(
        _filt_kernel,
        grid=(2, seq // FILT_ROWS),
        in_specs=[pl.BlockSpec((1, FILT_ROWS // 2, 2 * hid), lambda hf, i: (hf, i, 0)),
                  pl.BlockSpec((1, FILT_ROWS, aux_t.shape[2]), lambda hf, i: (hf, i, 0)),
                  _const_spec((2 * hid, 2 * hid)), _const_spec((1, 2 * hid)),
                  _const_spec((2 * hid, 2 * hid)), _const_spec((1, 2 * hid)),
                  pl.BlockSpec((2, 2 * hid, D_HYENA), lambda hf, i: (0, 0, hf)),
                  pl.BlockSpec((1, D_HYENA), lambda hf, i: (0, hf)),
                  _const_spec((2, 2 * hid)),
                  pl.BlockSpec((1, 1, D_HYENA), lambda hf, i: (hf, 0, 0))],
        out_specs=[pl.BlockSpec((filt_n2, FFT_HALF_N1, D_HYENA), lambda hf, i: (i, hf, 0)),
                   pl.BlockSpec((1, 8, D_HYENA), lambda hf, i: (hf, 0, 0))],
        out_shape=[jax.ShapeDtypeStruct((FFT_N2, FFT_N1, D_HYENA), F32),
                   jax.ShapeDtypeStruct((2, 8, D_HYENA), F32)],
        compiler_params=_cparams(("arbitrary", "arbitrary")),
        name="filt",
    )(jnp.asarray(z_t), jnp.asarray(aux_t), block_diag(w1_pad), twice(row2(hy_filt_b1[0])),
      block_diag(hy_filt_w2[0]), twice(row2(hy_filt_b2[0])), w3_parts, row2(hy_filt_b3[0]),
      twice(hy_sin_freq[0]), hy_decay[0].reshape(2, 1, D_HYENA))
    k_circ = k_circ.reshape(FFT_N, D_HYENA)

    mats = _fft_matrices()
    mats = {k: jnp.asarray(v).astype(BF16) for k, v in mats.items()}
    w2_all = mats["w2"]
    step_mat_spec = pl.BlockSpec((FFT_K1_PER_STEP, 2 * FFT_N2, 2 * FFT_N2), lambda cb, s: (s, 0, 0))
    filt_slabs = 2
    kf = pl.pallas_call(
        _filt_fft_kernel,
        grid=(D_HYENA // (filt_slabs * LANES), FFT_STEPS),
        in_specs=[pl.BlockSpec((FFT_N, filt_slabs * LANES), lambda cb, s: (0, cb)),
                  pl.BlockSpec((2, 8, filt_slabs * LANES), lambda cb, s: (0, 0, cb)),
                  _const_spec((2 * FFT_N1, FFT_N1)), step_mat_spec],
        out_specs=pl.BlockSpec((FFT_K1_PER_STEP, 2 * FFT_N2, filt_slabs * LANES), lambda cb, s: (s, 0, cb)),
        out_shape=jax.ShapeDtypeStruct((FFT_N1, 2 * FFT_N2, D_HYENA), BF16),
        scratch_shapes=[pltpu.VMEM((filt_slabs, FFT_N2 * FFT_PITCH, LANES), F32)],
        compiler_params=_cparams(("arbitrary", "arbitrary")),
        name="filt_fft",
    )(k_circ, k_asum, mats["w1_real"], w2_all)

    n_pairs = batch // 2
    pair_block = ((batch, PITCHED_ROWS, LANES), lambda cb, s: (0, 0, cb))
    pair_spec = pl.BlockSpec(*pair_block, pipeline_mode=pl.Buffered(1))
    y_conv = pl.pallas_call(
        _hy_conv_kernel,
        grid=(D_HYENA // LANES, FFT_STEPS),
        in_specs=[pl.BlockSpec(*pair_block),
                  pl.BlockSpec((FFT_K1_PER_STEP, 2 * FFT_N2, LANES), lambda cb, s: (s, 0, cb)),
                  _const_spec((2 * FFT_N1, FFT_N1)), step_mat_spec,
                  _const_spec((FFT_N1, 2 * FFT_N1))],
        out_specs=pair_spec,
        out_shape=pitched_shape,
        scratch_shapes=[pltpu.VMEM((n_pairs, FFT_N2 * FFT_PITCH, LANES), F32)],
        compiler_params=_cparams(("arbitrary", "arbitrary")),
        name="hy_conv",
    )(u_p, kf, mats["w1_data"], w2_all, mats["v1"])

    mix = pl.pallas_call(
        _merge_kernel,
        grid=(batch, n_tiles),
        in_specs=[tok(D_ATTN), tok(D_HYENA), pitched_spec, pitched_spec, tok(2 * d),
                  _const_spec((1, D_HYENA)), _const_spec((1, d)),
                  _const_spec((D_ATTN, d)), _const_spec((D_HYENA, d)), _const_spec((d, d))],
        out_specs=tok(d),
        out_shape=jax.ShapeDtypeStruct((batch, seq, d), BF16),
        compiler_params=_cparams(("arbitrary", "arbitrary")),
        name="merge",
    )(y_na, x0, u_p, y_conv, gates, row2(hy_skip[0]), row2(norm_mix_post[0]),
      w_o_na_b, w_o_hy_b, w_out_b)

    conv_w_c = ffn_conv_w[0]
    conv_b_c = row2(ffn_conv_b[0])
    mix_halo = 2 * HALO
    mix_blocks = TOKEN_TILE // mix_halo
    mix_prev = pl.BlockSpec((1, mix_halo, d), lambda b, i: (b, jnp.maximum(i * mix_blocks - 1, 0), 0))
    mix_next = pl.BlockSpec((1, mix_halo, d),
                            lambda b, i: (b, jnp.minimum((i + 1) * mix_blocks, seq // mix_halo - 1), 0))
    out = pl.pallas_call(
        _ffn_kernel,
        grid=(batch, n_tiles),
        in_specs=[tok(d), prev_spec(d), next_spec(d), tok(d), mix_prev, mix_next,
                  mod_spec, _const_spec((1, d)), _const_spec((1, d)),
                  _const_spec(w_up_c.shape), _const_spec(conv_w_c.shape), _const_spec(conv_b_c.shape),
                  _const_spec(w_dn_c.shape)],
        out_specs=tok(d),
        out_shape=jax.ShapeDtypeStruct((batch, seq, d), F32),
        scratch_shapes=[pltpu.VMEM((2, 2, FF_CHUNK // LANES, TOKEN_TILE + 2 * HALO, LANES), F32),
                        pltpu.VMEM((TOKEN_TILE, D_FF), BF16)],
        compiler_params=_cparams(("arbitrary", "arbitrary")),
        name="ffn",
    )(x, x, x, mix, mix, mix, mod_lat, row2(norm_ffn_pre[0]), row2(norm_ffn_post[0]),
      w_up_c, conv_w_c, conv_b_c, w_dn_c)
    return out
```

```python
import functools
import math

import jax
import jax.numpy as jnp
import numpy as np
from jax import lax
from jax.experimental import pallas as pl
from jax.experimental.pallas import tpu as pltpu

F32 = jnp.float32
BF16 = jnp.bfloat16

D_MODEL = 1024
N_HEADS = 8
HEAD_DIM = 64
D_ATTN = N_HEADS * HEAD_DIM
D_HYENA = 512
GRID_W = 64
WIN_ROWS = 8
WIN_COLS = 16
POS_BANDS = 16
POS_FEATS = 1 + 2 * POS_BANDS
FILTER_HIDDEN = 64
D_FF = 2816
N_MOD = 6
ROPE_BASE = 10000.0
RMS_EPS = 1e-6
NEG_BIAS = -1e30
LOG2_E = math.log2(math.e)

LANES = 128
VMEM_LIMIT_BYTES = 58 * 1024 * 1024

FFT_N1 = 64
FFT_N2 = 128
FFT_N = FFT_N1 * FFT_N2
FFT_HALF_N1 = FFT_N1 // 2
FFT_PITCH = 2 * FFT_N1 + 4
FFT_K1_PER_STEP = 32
FFT_UNROLL = 16
FFT_STEPS = FFT_N1 // FFT_K1_PER_STEP

TOKEN_TILE = 512
FILT_ROWS = 1024
SEQ_PITCH = FFT_N2 + 8
PITCHED_ROWS = FFT_HALF_N1 * SEQ_PITCH
TILE_GROUPS = TOKEN_TILE // FFT_N2
ATTN_ROWS_PER_STEP = 4
ATTN_Q = ATTN_ROWS_PER_STEP * GRID_W
ATTN_KEY_ROWS = 12
ATTN_KEYS = ATTN_KEY_ROWS * GRID_W
ATTN_KEY_CHUNK = 256
ATTN_GROUPS_PER_STEP = 4
FF_CHUNK = 768
LATER_WEIGHT_SLABS = (32, 16, 32, 32, 32)
HALO = 8
FFN_ROW_BLOCK = 64


def _cparams(sem):
    return pltpu.CompilerParams(dimension_semantics=sem, vmem_limit_bytes=VMEM_LIMIT_BYTES)


@functools.lru_cache(maxsize=None)
def _rope_tables(seq):
    pos = np.arange(seq)
    row = pos // GRID_W
    col = pos % GRID_W
    n_pairs = HEAD_DIM // 4
    inv = ROPE_BASE ** (-np.arange(n_pairs, dtype=np.float64) / n_pairs)
    lane = np.arange(LANES) % HEAD_DIM
    p = np.where(lane[None, :] < HEAD_DIM // 2, row[:, None], col[:, None]).astype(np.float64)
    ang = p * inv[lane % n_pairs][None, :]
    sign = np.where((lane % (2 * n_pairs)) < n_pairs, -1.0, 1.0)
    return np.cos(ang).astype(np.float32), (np.sin(ang) * sign[None, :]).astype(np.float32)


@functools.lru_cache(maxsize=None)
def _filter_tables(seq):
    assert 2 * seq == FFT_N
    half, n2, n1 = np.meshgrid(np.arange(2), np.arange(FFT_N2), np.arange(FFT_HALF_N1), indexing="ij")
    n = FFT_N2 * (half * FFT_HALF_N1 + n1) + n2
    fwd = n < seq
    m = n - seq
    valid = fwd | (m >= 1)
    pos = np.where(valid, np.where(fwd, n, seq - m), 0).astype(np.float64).reshape(2, seq)
    t = pos / max(seq - 1, 1)
    bands = np.linspace(1e-4, POS_BANDS - 1, POS_BANDS)
    ang = (2.0 * math.pi / seq) * pos[..., None] * bands
    z = np.zeros((2, seq, 64), np.float64)
    z[..., 0] = t
    z[..., 1:1 + POS_BANDS] = np.cos(ang)
    z[..., 1 + POS_BANDS:POS_FEATS] = -np.sin(ang)
    aux = np.zeros((2, seq, 8), np.float64)
    aux[..., 0] = t
    aux[..., 1] = valid.reshape(2, seq)
    zp = z.reshape(2, seq // FILT_ROWS, 2, FILT_ROWS // 2, 64).transpose(0, 1, 3, 2, 4).reshape(2, seq // 2, 128)
    return zp.astype(np.float32), aux.astype(np.float32)


def _realify(m):
    return np.block([[m.real, -m.imag], [m.imag, m.real]])


@functools.lru_cache(maxsize=None)
def _fft_matrices():
    n1 = np.arange(FFT_N1)
    n2 = np.arange(FFT_N2)
    k1 = np.arange(FFT_N1)
    k2 = np.arange(FFT_N2)
    f1 = np.exp(-2j * np.pi * np.outer(k1, n1) / FFT_N1)
    w1_data = _realify(f1[:, :FFT_HALF_N1])
    w1_real = np.concatenate([f1.real, f1.imag], axis=0)
    v1 = _realify(np.conj(f1.T)[:FFT_HALF_N1, :] / FFT_N)
    f2 = np.exp(-2j * np.pi * np.outer(k2, n2) / FFT_N2)
    tw = np.exp(-2j * np.pi * np.outer(k1, n2) / FFT_N)
    w2 = np.stack([_realify(f2 * tw[a][None, :]) for a in range(FFT_N1)])
    return {k: v.astype(np.float32) for k, v in
            dict(w1_data=w1_data, w1_real=w1_real, v1=v1, w2=w2).items()}


N_BIAS_ROWS = 2 * WIN_ROWS - 1
N_BIAS_COLS = 2 * WIN_COLS - 1


def _attn_bias_row_index():
    rows = GRID_W
    groups = (0, 2, rows // ATTN_ROWS_PER_STEP - 1)
    dr = np.full((3, ATTN_ROWS_PER_STEP, ATTN_KEY_ROWS), -1, np.int32)
    for v, g in enumerate(groups):
        ws = min(max(ATTN_ROWS_PER_STEP * g - WIN_ROWS // 2, 0), rows - ATTN_KEY_ROWS)
        for i in range(ATTN_ROWS_PER_STEP):
            r = ATTN_ROWS_PER_STEP * g + i
            r_start = min(max(r - WIN_ROWS // 2, 0), rows - WIN_ROWS)
            for j in range(ATTN_KEY_ROWS):
                kr = ws + j
                if r_start <= kr < r_start + WIN_ROWS:
                    dr[v, i, j] = kr - r + (WIN_ROWS - 1)
    return dr


def _attn_bias_kernel(rpb_ref, w_ref, o_ref, wb_ref, t_ref):
    wb_ref[...] = w_ref[...].astype(BF16)
    head = pl.program_id(0)
    kc = lax.broadcasted_iota(jnp.int32, (GRID_W, LANES), 0)
    lane = lax.broadcasted_iota(jnp.int32, (GRID_W, LANES), 1)
    qc = lane % GRID_W
    c_start = jnp.clip(qc - WIN_COLS // 2, 0, GRID_W - WIN_COLS)
    col_in = (kc >= c_start) & (kc < c_start + WIN_COLS)
    dc = jnp.clip(kc - qc, 1 - WIN_COLS, WIN_COLS - 1) + (WIN_COLS - 1)
    base = head * (N_BIAS_ROWS * N_BIAS_COLS)
    for r in range(N_BIAS_ROWS):
        t = jnp.full((GRID_W, LANES), NEG_BIAS, F32)
        for cidx in range(N_BIAS_COLS):
            t = jnp.where(col_in & (dc == cidx), rpb_ref[base + r * N_BIAS_COLS + cidx] * LOG2_E, t)
        t_ref[r] = t
    dr = _attn_bias_row_index()
    low_half = lane < GRID_W
    masked = jnp.full((GRID_W, LANES), NEG_BIAS, F32)
    for v in range(3):
        for i in range(0, ATTN_ROWS_PER_STEP, 2):
            for j in range(ATTN_KEY_ROWS):
                lo = t_ref[int(dr[v, i, j])] if dr[v, i, j] >= 0 else masked
                hi = t_ref[int(dr[v, i + 1, j])] if dr[v, i + 1, j] >= 0 else masked
                o_ref[v, 0, j * GRID_W:(j + 1) * GRID_W, i * GRID_W:(i + 2) * GRID_W] = jnp.where(
                    low_half, lo, hi)


def _mod_kernel(c_ref, w_ref, b_ref, o_ref):
    c = c_ref[...]
    s = c * jax.nn.sigmoid(c)
    o_ref[...] = jnp.dot(s, w_ref[...], precision=lax.Precision.HIGHEST,
                         preferred_element_type=F32) + b_ref[...]


def _norm_modulate(x, gain, shift, scale):
    ms = jnp.mean(x * x, axis=-1, keepdims=True)
    y = x * lax.rsqrt(ms + RMS_EPS) * gain
    return y * (1.0 + scale) + shift


def _rope(t, cos, sin_signed):
    n_pairs = HEAD_DIM // 4
    lane = lax.broadcasted_iota(jnp.int32, t.shape, 1)
    first = (lane % (2 * n_pairs)) < n_pairs
    partner = jnp.where(first, pltpu.roll(t, LANES - n_pairs, 1), pltpu.roll(t, n_pairs, 1))
    return t * cos + partner * sin_signed


def _in_proj_kernel(x_ref, prev_ref, next_ref, mod_ref, g_ref, w_ref, b_ref, cos_ref, sin_ref, cw_ref, cb_ref,
                    *rest):
    n_cast = len(LATER_WEIGHT_SLABS)
    cast_in, rest = rest[:n_cast], rest[n_cast:]
    (qr_ref, qp_ref, kr_ref, v_ref, gt_ref, u_ref, x0_ref), rest = rest[:7], rest[7:]
    cast_out, (hy_ref,) = rest[:n_cast], rest[n_cast:]
    for src, dst in zip(cast_in, cast_out):
        dst[...] = src[...].astype(BF16)
    i = pl.program_id(1)
    n_tiles = pl.num_programs(1)
    xx = jnp.concatenate([prev_ref[0], x_ref[0], next_ref[0]], axis=0)
    h_ext = _norm_modulate(xx, g_ref[...], mod_ref[0, 0:1, :], mod_ref[0, 1:2, :]).astype(BF16)
    h = h_ext[HALO:HALO + TOKEN_TILE]
    cos = cos_ref[...]
    sin = sin_ref[...]

    def proj(lo, hi, rows=h):
        return jnp.dot(rows, w_ref[:, lo:hi], preferred_element_type=F32) + b_ref[:, lo:hi]

    q = proj(0, D_ATTN) * (HEAD_DIM ** -0.5 * LOG2_E)
    qp_ref[0] = q.astype(BF16)
    k = proj(D_ATTN, 2 * D_ATTN)
    for c in range(D_ATTN // LANES):
        lanes = slice(c * LANES, (c + 1) * LANES)
        qr_ref[0, :, lanes] = _rope(q[:, lanes], cos, sin).astype(BF16)
        kr_ref[0, :, lanes] = _rope(k[:, lanes], cos, sin).astype(BF16)
    v_t = proj(2 * D_ATTN, 3 * D_ATTN).T.astype(BF16)
    for c in range(TOKEN_TILE // ATTN_KEY_CHUNK):
        v_ref[0, c] = v_t[:, c * ATTN_KEY_CHUNK:(c + 1) * ATTN_KEY_CHUNK]
    hy_lo = 3 * D_ATTN
    gl_lo = hy_lo + 3 * D_HYENA
    for c in range(4):
        w = D_MODEL // 2
        gt_ref[0, :, c * w:(c + 1) * w] = jax.nn.sigmoid(
            proj(gl_lo + c * w, gl_lo + (c + 1) * w)).astype(BF16)

    slabs_per_part = D_HYENA // LANES
    for c in range(3):
        hy = proj(hy_lo + c * D_HYENA, hy_lo + (c + 1) * D_HYENA, h_ext)
        for s in range(slabs_per_part):
            hy_ref[c * slabs_per_part + s] = hy[:, s * LANES:(s + 1) * LANES]
    zero_row = jnp.zeros((1, LANES), F32)

    @pl.when(i == 0)
    def _():
        for s in range(3 * slabs_per_part):
            hy_ref[s, HALO - 1:HALO, :] = zero_row

    @pl.when(i == n_tiles - 1)
    def _():
        for s in range(3 * slabs_per_part):
            hy_ref[s, HALO + TOKEN_TILE:HALO + TOKEN_TILE + 1, :] = zero_row

    def conv(s, r0, rows):
        lanes = slice(s * LANES, (s + 1) * LANES)
        tap = lambda j: hy_ref[s, pl.ds(HALO - 1 + j + r0, rows, stride=1), :]
        return (tap(0) * cw_ref[0:1, lanes] + tap(1) * cw_ref[1:2, lanes] + tap(2) * cw_ref[2:3, lanes]
                + cb_ref[:, lanes])

    for s in range(slabs_per_part):
        lanes = slice(s * LANES, (s + 1) * LANES)
        for j in range(TILE_GROUPS):
            r0 = j * FFT_N2
            x0_ref[0, r0:r0 + FFT_N2, lanes] = conv(s, r0, FFT_N2).astype(BF16)
            u_ref[0, j * SEQ_PITCH:j * SEQ_PITCH + FFT_N2, lanes] = (
                conv(slabs_per_part + s, r0, FFT_N2) * conv(2 * slabs_per_part + s, r0, FFT_N2))
            u_ref[0, j * SEQ_PITCH + FFT_N2:(j + 1) * SEQ_PITCH, lanes] = jnp.zeros((SEQ_PITCH - FFT_N2, LANES), F32)


def _ctx_kv_kernel(x_ref, mod_ref, g_ref, w_ref, b_ref, k_ref, v_ref):
    h = _norm_modulate(x_ref[0], g_ref[...], mod_ref[0:1, :], mod_ref[1:2, :]).astype(BF16)
    kv = jnp.dot(h, w_ref[...], preferred_element_type=F32) + b_ref[...]
    k_ref[0] = kv[:, :D_ATTN].astype(BF16)
    v_ref[0] = kv[:, D_ATTN:].T.astype(BF16)


def _attn_window_start(g):
    return jnp.clip(ATTN_ROWS_PER_STEP * g - WIN_ROWS // 2, 0, GRID_W - ATTN_KEY_ROWS)


def _attn_kernel(qr_ref, qp_ref, k_ref, vt_ref, kc_ref, vct_ref, bias_ref, o_ref):
    step = pl.program_id(1)
    n_groups = GRID_W // ATTN_ROWS_PER_STEP
    nt = (((1,), (1,)), ((), ()))
    quad_w = 4 * HEAD_DIM
    lane = lax.broadcasted_iota(jnp.int32, (1, quad_w), 1)
    zero = jnp.zeros((), BF16)

    def group_params(sub):
        g = step * ATTN_GROUPS_PER_STEP + sub
        win = _attn_window_start(g)
        key0 = pl.multiple_of(win * GRID_W, ATTN_KEY_CHUNK)
        chunk0 = win // (ATTN_KEY_CHUNK // GRID_W)
        variant = (g > 0).astype(jnp.int32) + (g == n_groups - 1).astype(jnp.int32)
        return key0, chunk0, variant

    params = [group_params(sub) for sub in range(ATTN_GROUPS_PER_STEP)]

    def scores(item):
        sub, head = item
        key0, _, variant = params[sub]
        qrows = slice(sub * ATTN_Q, (sub + 1) * ATTN_Q)
        quad, hh = divmod(head, 4)
        ql = slice(quad * quad_w, (quad + 1) * quad_w)
        mine = (lane >= hh * HEAD_DIM) & (lane < (hh + 1) * HEAD_DIM)
        s_nb = lax.dot_general(k_ref[0, pl.ds(key0, ATTN_KEYS), ql], jnp.where(mine, qr_ref[0, qrows, ql], zero), nt,
                               preferred_element_type=F32)
        s_cx = lax.dot_general(kc_ref[0, :, ql], jnp.where(mine, qp_ref[0, qrows, ql], zero), nt,
                               preferred_element_type=F32)
        return s_nb + bias_ref[variant, head], s_cx

    def with_ones(v):
        return jnp.concatenate([v, jnp.ones((16, v.shape[1]), BF16)], axis=0)

    def probs(s_nb, s_cx):
        m = jnp.maximum(jnp.max(s_nb, axis=0, keepdims=True), jnp.max(s_cx, axis=0, keepdims=True))
        return jnp.exp2(s_nb - m).astype(BF16), jnp.exp2(s_cx - m).astype(BF16)

    def values(item, p_nb, p_cx):
        sub, head = item
        chunk0 = params[sub][1]
        rows = slice(head * HEAD_DIM, (head + 1) * HEAD_DIM)
        v_win = jnp.concatenate([vt_ref[0, chunk0 + c, rows, :] for c in range(ATTN_KEYS // ATTN_KEY_CHUNK)],
                                axis=1)
        o = (jnp.dot(with_ones(v_win), p_nb, preferred_element_type=F32)
             + jnp.dot(with_ones(vct_ref[0, rows, :]), p_cx, preferred_element_type=F32))
        return o[:HEAD_DIM] / o[HEAD_DIM:HEAD_DIM + 1]

    items = [(sub, head) for sub in range(ATTN_GROUPS_PER_STEP) for head in range(N_HEADS)]
    outs = []
    s_q = {0: scores(items[0]), 1: scores(items[1])}
    p_q = {0: probs(*s_q.pop(0))}
    for n, item in enumerate(items):
        if n + 2 < len(items):
            s_q[n + 2] = scores(items[n + 2])
        if n + 1 < len(items):
            p_q[n + 1] = probs(*s_q.pop(n + 1))
        outs.append(values(item, *p_q.pop(n)))
    for sub in range(ATTN_GROUPS_PER_STEP):
        o_ref[0, sub * ATTN_Q:(sub + 1) * ATTN_Q, :] = jnp.concatenate(
            outs[sub * N_HEADS:(sub + 1) * N_HEADS], axis=0).T.astype(BF16)


def _filt_kernel(z_ref, aux_ref, w1_ref, b1_ref, w2_ref, b2_ref, w3_ref, b3_ref, freq_ref, decay_ref,
                 o_ref, asum_ref):
    i = pl.program_id(1)
    hp = lax.Precision.HIGHEST
    h = jnp.sin(freq_ref[0:1, :] * (jnp.dot(z_ref[0], w1_ref[...], precision=hp,
                                            preferred_element_type=F32) + b1_ref[...]))
    h = jnp.sin(freq_ref[1:2, :] * (jnp.dot(h, w2_ref[...], precision=hp,
                                            preferred_element_type=F32) + b2_ref[...]))
    taps = jnp.concatenate([jnp.dot(h, w3_ref[part], precision=hp, preferred_element_type=F32)
                            for part in range(2)], axis=0) + b3_ref[...]
    t = aux_ref[0, :, 0:1]
    valid = aux_ref[0, :, 1:2] > 0.5
    k = jnp.where(valid, taps * jnp.exp(-t * jnp.abs(decay_ref[0])), 0.0)
    part = jnp.sum(jnp.abs(k), axis=0, keepdims=True)

    @pl.when(i == 0)
    def _():
        asum_ref[...] = jnp.zeros_like(asum_ref)

    asum_ref[0] += jnp.broadcast_to(part, asum_ref.shape[1:])
    o_ref[...] = k.reshape(o_ref.shape)


def _fft_stage1(load_group, w1_ref, a_ref):
    n_slabs = a_ref.shape[0]

    def body(n2, carry):
        x = load_group(n2)
        a = jnp.dot(w1_ref[...], x, preferred_element_type=F32)
        for s in range(n_slabs):
            a_ref[s, pl.ds(n2 * FFT_PITCH, 2 * FFT_N1, stride=1), :] = a[:, s * LANES:(s + 1) * LANES]
        return carry

    lax.fori_loop(0, FFT_N2, body, 0, unroll=FFT_UNROLL)


def _fft_load_k1(a_ref, k1):
    parts = []
    for off in (0, FFT_N1):
        parts.append(jnp.concatenate(
            [a_ref[s, pl.ds(k1 + off, FFT_N2, stride=FFT_PITCH), :] for s in range(a_ref.shape[0])], axis=1))
    return jnp.concatenate(parts, axis=0)


def _filt_fft_kernel(k_ref, asum_ref, w1_ref, w2_ref, o_ref, a_ref):
    step = pl.program_id(1)

    @pl.when(step == 0)
    def _():
        norm = asum_ref[0, 0:1, :] + asum_ref[1, 0:1, :]

        def load_group(n2):
            return (k_ref[pl.ds(pl.multiple_of(n2 * FFT_N1, FFT_N1), FFT_N1), :] / norm).astype(BF16)
        _fft_stage1(load_group, w1_ref, a_ref)

    for j in range(FFT_K1_PER_STEP):
        b = _fft_load_k1(a_ref, step * FFT_K1_PER_STEP + j).astype(BF16)
        o_ref[j] = jnp.dot(w2_ref[j], b, preferred_element_type=F32).astype(BF16)


def _hy_conv_kernel(u_ref, kf_ref, w1_ref, w2_ref, v1_ref, o_ref, a_ref):
    step = pl.program_id(1)
    n_slabs = a_ref.shape[0]

    def tokens_of(member, s, n2):
        return (2 * s + member, pl.ds(n2, FFT_HALF_N1, stride=SEQ_PITCH), slice(None))

    @pl.when(step == 0)
    def _():
        for sample in range(o_ref.shape[0]):
            for grp in range(FFT_HALF_N1):
                o_ref[sample, grp * SEQ_PITCH + FFT_N2:(grp + 1) * SEQ_PITCH, :] = jnp.zeros(
                    (SEQ_PITCH - FFT_N2, LANES), F32)

        def load_group(n2):
            return jnp.concatenate(
                [jnp.concatenate([u_ref[tokens_of(member, s, n2)] for s in range(n_slabs)], axis=1)
                 for member in range(2)], axis=0).astype(BF16)
        _fft_stage1(load_group, w1_ref, a_ref)

    for j in range(FFT_K1_PER_STEP):
        k1 = step * FFT_K1_PER_STEP + j
        b = _fft_load_k1(a_ref, k1).astype(BF16)
        x = jnp.dot(w2_ref[j], b, preferred_element_type=F32)
        kf = kf_ref[j].astype(F32)
        kf = jnp.concatenate([kf] * n_slabs, axis=1)
        xr, xi = x[:FFT_N2], x[FFT_N2:]
        kr, ki = kf[:FFT_N2], kf[FFT_N2:]
        y = jnp.concatenate([xr * kr - xi * ki, xr * ki + xi * kr], axis=0).astype(BF16)
        d = lax.dot_general(w2_ref[j], y, (((0,), (0,)), ((), ())),
                            preferred_element_type=F32)
        for s in range(n_slabs):
            lanes = slice(s * LANES, (s + 1) * LANES)
            a_ref[s, pl.ds(k1, FFT_N2, stride=FFT_PITCH), :] = d[:FFT_N2, lanes]
            a_ref[s, pl.ds(k1 + FFT_N1, FFT_N2, stride=FFT_PITCH), :] = d[FFT_N2:, lanes]

    @pl.when(step == FFT_STEPS - 1)
    def _():
        def body(n2, carry):
            d = jnp.concatenate([a_ref[s, pl.ds(n2 * FFT_PITCH, 2 * FFT_N1, stride=1), :] for s in range(n_slabs)],
                                axis=1)
            y = jnp.dot(v1_ref[...], d.astype(BF16), preferred_element_type=F32)
            for s in range(n_slabs):
                for member in range(2):
                    o_ref[tokens_of(member, s, n2)] = y[member * FFT_HALF_N1:(member + 1) * FFT_HALF_N1,
                                                        s * LANES:(s + 1) * LANES]
            return carry

        lax.fori_loop(0, FFT_N2, body, 0, unroll=FFT_UNROLL)


def _from_pitched(ref):
    return jnp.concatenate([ref[0, j * SEQ_PITCH:j * SEQ_PITCH + FFT_N2, :] for j in range(TILE_GROUPS)], axis=0)


def _merge_kernel(yna_ref, x0_ref, u_ref, yc_ref, gt_ref, skip_ref, gpost_ref, wna_ref, why_ref, wout_ref, o_ref):
    y_hy = (x0_ref[0].astype(F32) * (_from_pitched(yc_ref) + _from_pitched(u_ref) * skip_ref[...])).astype(BF16)
    a = jnp.dot(yna_ref[0], wna_ref[...], preferred_element_type=F32)
    b = jnp.dot(y_hy, why_ref[...], preferred_element_type=F32)
    g_na = gt_ref[0, :, :D_MODEL].astype(F32)
    g_hy = gt_ref[0, :, D_MODEL:].astype(F32)
    m = (g_na * a + g_hy * b).astype(BF16)
    o = jnp.dot(m, wout_ref[...], preferred_element_type=F32)
    ms = jnp.mean(o * o, axis=-1, keepdims=True)
    o_ref[0] = (o * lax.rsqrt(ms + RMS_EPS) * gpost_ref[...]).astype(BF16)


def _gelu_tanh(a):
    return 0.5 * a * (1.0 + jnp.tanh(math.sqrt(2.0 / math.pi) * (a + 0.044715 * (a * a * a))))


def _ffn_kernel(x_ref, prev_ref, next_ref, mix_ref, mix_prev_ref, mix_next_ref, mod_ref, gpre_ref, gpost_ref,
                wup_ref, cw_ref, cb_ref, wdn_ref, o_ref, u_ref, act_ref):
    i = pl.program_id(1)
    n_tiles = pl.num_programs(1)
    mix_halo = mix_prev_ref.shape[1]
    mix = jnp.concatenate([mix_prev_ref[0, mix_halo - HALO:, :], mix_ref[0], mix_next_ref[0, :HALO, :]], axis=0)
    xx = (jnp.concatenate([prev_ref[0], x_ref[0], next_ref[0]], axis=0)
          + mod_ref[0, 2:3, :] * mix.astype(F32))
    x = xx[HALO:HALO + TOKEN_TILE]
    h = _norm_modulate(xx, gpre_ref[...], mod_ref[0, 3:4, :], mod_ref[0, 4:5, :])
    row = lax.broadcasted_iota(jnp.int32, (xx.shape[0], 1), 0)
    inside = ((row >= HALO) | (i > 0)) & ((row < HALO + TOKEN_TILE) | (i < n_tiles - 1))
    h = jnp.where(inside, h, 0.0).astype(BF16)
    starts = list(range(0, D_FF, FF_CHUNK))
    widths = [min(FF_CHUNK, D_FF - lo) for lo in starts]

    def conv(buf, half, s, lo, r0):
        lanes = slice(half * D_FF + lo, half * D_FF + lo + LANES)
        tap = lambda j: u_ref[buf, half, s, pl.ds(HALO - 1 + j + r0, FFN_ROW_BLOCK, stride=1), :]
        return (tap(0) * cw_ref[0:1, lanes] + tap(1) * cw_ref[1:2, lanes] + tap(2) * cw_ref[2:3, lanes]
                + cb_ref[:, lanes])

    def up_project(c):
        for half in range(2):
            lo = half * D_FF + starts[c]
            u = jnp.dot(h, wup_ref[:, lo:lo + widths[c]], preferred_element_type=F32)
            for s in range(widths[c] // LANES):
                u_ref[c % 2, half, s] = u[:, s * LANES:(s + 1) * LANES]

    up_project(0)
    for c in range(len(starts)):
        buf = c % 2
        if c + 1 < len(starts):
            up_project(c + 1)
        for s in range(widths[c] // LANES):
            lo = starts[c] + s * LANES
            for r0 in range(0, TOKEN_TILE, FFN_ROW_BLOCK):
                act_ref[r0:r0 + FFN_ROW_BLOCK, lo:lo + LANES] = (
                    _gelu_tanh(conv(buf, 0, s, lo, r0)) * conv(buf, 1, s, lo, r0)).astype(BF16)
    y = jnp.dot(act_ref[...], wdn_ref[...], preferred_element_type=F32)
    ms = jnp.mean(y * y, axis=-1, keepdims=True)
    y = y * lax.rsqrt(ms + RMS_EPS) * gpost_ref[...]
    o_ref[0] = x + mod_ref[0, 5:6, :] * y


def _const_spec(shape):
    nd = len(shape)
    return pl.BlockSpec(shape, lambda *_: (0,) * nd, pipeline_mode=pl.Buffered(1))


def kernel(x, c, ctx, c_ctx, w_mod, b_mod, norm_mix_pre, norm_mix_post, norm_ffn_pre, norm_ffn_post, w_in, b_in, na_rpb, hy_conv_w, hy_conv_b, hy_filt_w1, hy_filt_b1, hy_filt_w2, hy_filt_b2, hy_filt_w3, hy_filt_b3, hy_sin_freq, hy_decay, hy_skip, w_o_na, w_o_hy, w_out, ffn_w_up, ffn_conv_w, ffn_conv_b, ffn_w_down):
    batch, seq, d = x.shape
    n_ctx = ctx.shape[1]
    assert d == D_MODEL and 2 * seq == FFT_N and seq == GRID_W * GRID_W and batch % 2 == 0
    assert w_mod.shape[0] == 1, "single-layer block"
    n_tiles = seq // TOKEN_TILE
    d_in = w_in.shape[2]
    row2 = lambda a: a.reshape(1, -1)

    c_all = jnp.zeros((8, d), F32).at[:batch].set(c).at[batch].set(c_ctx)
    mod_n = 1024
    mod = pl.pallas_call(
        _mod_kernel,
        grid=(N_MOD * d // mod_n,),
        in_specs=[_const_spec((8, d)),
                  pl.BlockSpec((d, mod_n), lambda j: (0, j)),
                  pl.BlockSpec((1, mod_n), lambda j: (0, j))],
        out_specs=pl.BlockSpec((8, mod_n), lambda j: (0, j)),
        out_shape=jax.ShapeDtypeStruct((8, N_MOD * d), F32),
        compiler_params=_cparams(("arbitrary",)),
        name="mod",
    )(c_all, w_mod[0], row2(b_mod[0]))
    mod_lat = jnp.pad(mod[:batch].reshape(batch, N_MOD, d), ((0, 0), (0, 8 - N_MOD), (0, 0)))
    mod_ctx = jnp.pad(mod[batch].reshape(N_MOD, d), ((0, 8 - N_MOD), (0, 0)))

    w_rows = d // N_HEADS
    bias, w_in_b = pl.pallas_call(
        _attn_bias_kernel,
        grid=(N_HEADS,),
        in_specs=[pl.BlockSpec(memory_space=pltpu.SMEM), pl.BlockSpec((w_rows, d_in), lambda h: (h, 0))],
        out_specs=[pl.BlockSpec((3, 1, ATTN_KEYS, ATTN_Q), lambda h: (0, h, 0, 0)),
                   pl.BlockSpec((w_rows, d_in), lambda h: (h, 0))],
        out_shape=[jax.ShapeDtypeStruct((3, N_HEADS, ATTN_KEYS, ATTN_Q), F32),
                   jax.ShapeDtypeStruct((d, d_in), BF16)],
        scratch_shapes=[pltpu.VMEM((N_BIAS_ROWS, GRID_W, LANES), F32)],
        compiler_params=_cparams(("arbitrary",)),
        name="attn_bias",
    )(na_rpb[0].reshape(-1), w_in[0])
    b_in_r = row2(b_in[0])
    g_mix_pre = row2(norm_mix_pre[0])

    k_ctx, v_ctx = pl.pallas_call(
        _ctx_kv_kernel,
        grid=(batch,),
        in_specs=[pl.BlockSpec((1, n_ctx, d), lambda b: (b, 0, 0)),
                  _const_spec((8, d)), _const_spec((1, d)),
                  _const_spec((d, 2 * D_ATTN)), _const_spec((1, 2 * D_ATTN))],
        out_specs=[pl.BlockSpec((1, n_ctx, D_ATTN), lambda b: (b, 0, 0)),
                   pl.BlockSpec((1, D_ATTN, n_ctx), lambda b: (b, 0, 0))],
        out_shape=[jax.ShapeDtypeStruct((batch, n_ctx, D_ATTN), BF16),
                   jax.ShapeDtypeStruct((batch, D_ATTN, n_ctx), BF16)],
        compiler_params=_cparams(("arbitrary",)),
        name="ctx_kv",
    )(ctx, mod_ctx, g_mix_pre, w_in_b[:, D_ATTN:3 * D_ATTN], b_in_r[:, D_ATTN:3 * D_ATTN])

    cos_t, sin_t = _rope_tables(seq)
    tok = lambda w: pl.BlockSpec((1, TOKEN_TILE, w), lambda b, i: (b, i, 0))
    mod_spec = pl.BlockSpec((1, 8, d), lambda b, i: (b, 0, 0))
    rope_spec = pl.BlockSpec((TOKEN_TILE, LANES), lambda b, i: (i, 0))
    halo_blocks = TOKEN_TILE // HALO
    n_halo_blocks = seq // HALO
    prev_spec = lambda w: pl.BlockSpec((1, HALO, w), lambda b, i: (b, jnp.maximum(i * halo_blocks - 1, 0), 0))
    next_spec = lambda w: pl.BlockSpec(
        (1, HALO, w), lambda b, i: (b, jnp.minimum((i + 1) * halo_blocks, n_halo_blocks - 1), 0))
    pitched_spec = pl.BlockSpec((1, TILE_GROUPS * SEQ_PITCH, D_HYENA), lambda b, i: (b, i, 0))
    pitched_shape = jax.ShapeDtypeStruct((batch, PITCHED_ROWS, D_HYENA), F32)
    later_weights = [ffn_w_up[0], ffn_w_down[0], w_out[0], w_o_na[0], w_o_hy[0]]
    slab_specs = [
        pl.BlockSpec((w.shape[0] // n, w.shape[1]), lambda b, i, n=n: (jnp.minimum(b * n_tiles + i, n - 1), 0))
        for w, n in zip(later_weights, LATER_WEIGHT_SLABS)]
    q_rot, q_plain, k_rot, v_lat, gates, u_p, x0, w_up_c, w_dn_c, w_out_b, w_o_na_b, w_o_hy_b = pl.pallas_call(
        _in_proj_kernel,
        grid=(batch, n_tiles),
        in_specs=[tok(d), prev_spec(d), next_spec(d), mod_spec, _const_spec((1, d)), _const_spec((d, d_in)),
                  _const_spec((1, d_in)), rope_spec, rope_spec,
                  _const_spec((3, 3 * D_HYENA)), _const_spec((1, 3 * D_HYENA))] + slab_specs,
        out_specs=[tok(D_ATTN)] * 3
        + [pl.BlockSpec((1, TOKEN_TILE // ATTN_KEY_CHUNK, D_ATTN, ATTN_KEY_CHUNK), lambda b, i: (b, i, 0, 0)),
           tok(2 * d), pitched_spec, tok(D_HYENA)] + slab_specs,
        out_shape=[jax.ShapeDtypeStruct((batch, seq, D_ATTN), BF16)] * 3
        + [jax.ShapeDtypeStruct((batch, seq // ATTN_KEY_CHUNK, D_ATTN, ATTN_KEY_CHUNK), BF16),
           jax.ShapeDtypeStruct((batch, seq, 2 * d), BF16), pitched_shape,
           jax.ShapeDtypeStruct((batch, seq, D_HYENA), BF16)]
        + [jax.ShapeDtypeStruct(w.shape, BF16) for w in later_weights],
        scratch_shapes=[pltpu.VMEM((3 * D_HYENA // LANES, TOKEN_TILE + 2 * HALO, LANES), F32)],
        compiler_params=_cparams(("arbitrary", "arbitrary")),
        name="in_proj",
    )(x, x, x, mod_lat, g_mix_pre, w_in_b, b_in_r, jnp.asarray(cos_t), jnp.asarray(sin_t),
      hy_conv_w[0], row2(hy_conv_b[0]), *later_weights)

    n_steps = GRID_W // ATTN_ROWS_PER_STEP // ATTN_GROUPS_PER_STEP
    q_spec = pl.BlockSpec((1, ATTN_GROUPS_PER_STEP * ATTN_Q, D_ATTN), lambda b, g: (b, g, 0))
    full = lambda n: pl.BlockSpec((1, n, D_ATTN), lambda b, g: (b, 0, 0))
    bias_spec = _const_spec((3, N_HEADS, ATTN_KEYS, ATTN_Q))
    vt_spec = pl.BlockSpec((1, seq // ATTN_KEY_CHUNK, D_ATTN, ATTN_KEY_CHUNK), lambda b, g: (b, 0, 0, 0))
    vct_spec = pl.BlockSpec((1, D_ATTN, n_ctx), lambda b, g: (b, 0, 0))
    y_na = pl.pallas_call(
        _attn_kernel,
        grid=(batch, n_steps),
        in_specs=[q_spec, q_spec, full(seq), vt_spec, full(n_ctx), vct_spec, bias_spec],
        out_specs=q_spec,
        out_shape=jax.ShapeDtypeStruct((batch, seq, D_ATTN), BF16),
        compiler_params=_cparams(("arbitrary", "arbitrary")),
        name="attn",
    )(q_rot, q_plain, k_rot, v_lat, k_ctx, v_ctx, bias)

    z_t, aux_t = _filter_tables(seq)
    filt_n2 = FILT_ROWS // FFT_HALF_N1
    hid = FILTER_HIDDEN
    w1_pad = jnp.pad(hy_filt_w1[0], ((0, hid - POS_FEATS), (0, 0)))
    block_diag = lambda w: jnp.zeros((2 * hid, 2 * hid), F32).at[:hid, :hid].set(w).at[hid:, hid:].set(w)
    twice = lambda v: jnp.tile(v, (1, 2))
    w3 = hy_filt_w3[0]
    w3_parts = jnp.stack([jnp.concatenate([w3, jnp.zeros_like(w3)], axis=0),
                          jnp.concatenate([jnp.zeros_like(w3), w3], axis=0)])
    k_circ, k_asum = pl.pallas_call(
        _filt_kernel,
        grid=(2, seq // FILT_ROWS),
        in_specs=[pl.BlockSpec((1, FILT_ROWS // 2, 2 * hid), lambda hf, i: (hf, i, 0)),
                  pl.BlockSpec((1, FILT_ROWS, aux_t.shape[2]), lambda hf, i: (hf, i, 0)),
                  _const_spec((2 * hid, 2 * hid)), _const_spec((1, 2 * hid)),
                  _const_spec((2 * hid, 2 * hid)), _const_spec((1, 2 * hid)),
                  pl.BlockSpec((2, 2 * hid, D_HYENA), lambda hf, i: (0, 0, hf)),
                  pl.BlockSpec((1, D_HYENA), lambda hf, i: (0, hf)),
                  _const_spec((2, 2 * hid)),
                  pl.BlockSpec((1, 1, D_HYENA), lambda hf, i: (hf, 0, 0))],
        out_specs=[pl.BlockSpec((filt_n2, FFT_HALF_N1, D_HYENA), lambda hf, i: (i, hf, 0)),
                   pl.BlockSpec((1, 8, D_HYENA), lambda hf, i: (hf, 0, 0))],
        out_shape=[jax.ShapeDtypeStruct((FFT_N2, FFT_N1, D_HYENA), F32),
                   jax.ShapeDtypeStruct((2, 8, D_HYENA), F32)],
        compiler_params=_cparams(("arbitrary", "arbitrary")),
        name="filt",
    )(jnp.asarray(z_t), jnp.asarray(aux_t), block_diag(w1_pad), twice(row2(hy_filt_b1[0])),
      block_diag(hy_filt_w2[0]), twice(row2(hy_filt_b2[0])), w3_parts, row2(hy_filt_b3[0]),
      twice(hy_sin_freq[0]), hy_decay[0].reshape(2, 1, D_HYENA))
    k_circ = k_circ.reshape(FFT_N, D_HYENA)

    mats = _fft_matrices()
    mats = {k: jnp.asarray(v).astype(BF16) for k, v in mats.items()}
    w2_all = mats["w2"]
    step_mat_spec = pl.BlockSpec((FFT_K1_PER_STEP, 2 * FFT_N2, 2 * FFT_N2), lambda cb, s: (s, 0, 0))
    filt_slabs = 2
    kf = pl.pallas_call(
        _filt_fft_kernel,
        grid=(D_HYENA // (filt_slabs * LANES), FFT_STEPS),
        in_specs=[pl.BlockSpec((FFT_N, filt_slabs * LANES), lambda cb, s: (0, cb)),
                  pl.BlockSpec((2, 8, filt_slabs * LANES), lambda cb, s: (0, 0, cb)),
                  _const_spec((2 * FFT_N1, FFT_N1)), step_mat_spec],
        out_specs=pl.BlockSpec((FFT_K1_PER_STEP, 2 * FFT_N2, filt_slabs * LANES), lambda cb, s: (s, 0, cb)),
        out_shape=jax.ShapeDtypeStruct((FFT_N1, 2 * FFT_N2, D_HYENA), BF16),
        scratch_shapes=[pltpu.VMEM((filt_slabs, FFT_N2 * FFT_PITCH, LANES), F32)],
        compiler_params=_cparams(("arbitrary", "arbitrary")),
        name="filt_fft",
    )(k_circ, k_asum, mats["w1_real"], w2_all)

    n_pairs = batch // 2
    pair_block = ((batch, PITCHED_ROWS, LANES), lambda cb, s: (0, 0, cb))
    pair_spec = pl.BlockSpec(*pair_block, pipeline_mode=pl.Buffered(1))
    y_conv = pl.pallas_call(
        _hy_conv_kernel,
        grid=(D_HYENA // LANES, FFT_STEPS),
        in_specs=[pl.BlockSpec(*pair_block),
                  pl.BlockSpec((FFT_K1_PER_STEP, 2 * FFT_N2, LANES), lambda cb, s: (s, 0, cb)),
                  _const_spec((2 * FFT_N1, FFT_N1)), step_mat_spec,
                  _const_spec((FFT_N1, 2 * FFT_N1))],
        out_specs=pair_spec,
        out_shape=pitched_shape,
        scratch_shapes=[pltpu.VMEM((n_pairs, FFT_N2 * FFT_PITCH, LANES), F32)],
        compiler_params=_cparams(("arbitrary", "arbitrary")),
        name="hy_conv",
    )(u_p, kf, mats["w1_data"], w2_all, mats["v1"])

    mix = pl.pallas_call(
        _merge_kernel,
        grid=(batch, n_tiles),
        in_specs=[tok(D_ATTN), tok(D_HYENA), pitched_spec, pitched_spec, tok(2 * d),
                  _const_spec((1, D_HYENA)), _const_spec((1, d)),
                  _const_spec((D_ATTN, d)), _const_spec((D_HYENA, d)), _const_spec((d, d))],
        out_specs=tok(d),
        out_shape=jax.ShapeDtypeStruct((batch, seq, d), BF16),
        compiler_params=_cparams(("arbitrary", "arbitrary")),
        name="merge",
    )(y_na, x0, u_p, y_conv, gates, row2(hy_skip[0]), row2(norm_mix_post[0]),
      w_o_na_b, w_o_hy_b, w_out_b)

    conv_w_c = ffn_conv_w[0]
    conv_b_c = row2(ffn_conv_b[0])
    mix_halo = 2 * HALO
    mix_blocks = TOKEN_TILE // mix_halo
    mix_prev = pl.BlockSpec((1, mix_halo, d), lambda b, i: (b, jnp.maximum(i * mix_blocks - 1, 0), 0))
    mix_next = pl.BlockSpec((1, mix_halo, d),
                            lambda b, i: (b, jnp.minimum((i + 1) * mix_blocks, seq // mix_halo - 1), 0))
    out = pl.pallas_call(
        _ffn_kernel,
        grid=(batch, n_tiles),
        in_specs=[tok(d), prev_spec(d), next_spec(d), tok(d), mix_prev, mix_next,
                  mod_spec, _const_spec((1, d)), _const_spec((1, d)),
                  _const_spec(w_up_c.shape), _const_spec(conv_w_c.shape), _const_spec(conv_b_c.shape),
                  _const_spec(w_dn_c.shape)],
        out_specs=tok(d),
        out_shape=jax.ShapeDtypeStruct((batch, seq, d), F32),
        scratch_shapes=[pltpu.VMEM((2, 2, FF_CHUNK // LANES, TOKEN_TILE + 2 * HALO, LANES), F32),
                        pltpu.VMEM((TOKEN_TILE, D_FF), BF16)],
        compiler_params=_cparams(("arbitrary", "arbitrary")),
        name="ffn",
    )(x, x, x, mix, mix, mix, mod_lat, row2(norm_ffn_pre[0]), row2(norm_ffn_post[0]),
      w_up_c, conv_w_c, conv_b_c, w_dn_c)
    return out
```

```python
import functools
import math

import jax
import jax.numpy as jnp
import numpy as np
from jax import lax
from jax.experimental import pallas as pl
from jax.experimental.pallas import tpu as pltpu

F32 = jnp.float32
BF16 = jnp.bfloat16

D_MODEL = 1024
N_HEADS = 8
HEAD_DIM = 64
D_ATTN = N_HEADS * HEAD_DIM
D_HYENA = 512
GRID_W = 64
WIN_ROWS = 8
WIN_COLS = 16
POS_BANDS = 16
POS_FEATS = 1 + 2 * POS_BANDS
FILTER_HIDDEN = 64
D_FF = 2816
N_MOD = 6
ROPE_BASE = 10000.0
RMS_EPS = 1e-6
NEG_BIAS = -1e30
LOG2_E = math.log2(math.e)

LANES = 128
VMEM_LIMIT_BYTES = 58 * 1024 * 1024

FFT_N1 = 64
FFT_N2 = 128
FFT_N = FFT_N1 * FFT_N2
FFT_HALF_N1 = FFT_N1 // 2
FFT_PITCH = 2 * FFT_N1 + 4
FFT_K1_PER_STEP = 32
FFT_UNROLL = 32
FFT_STEPS = FFT_N1 // FFT_K1_PER_STEP

TOKEN_TILE = 512
FILT_ROWS = 1024
SEQ_PITCH = FFT_N2 + 8
PITCHED_ROWS = FFT_HALF_N1 * SEQ_PITCH
TILE_GROUPS = TOKEN_TILE // FFT_N2
ATTN_ROWS_PER_STEP = 4
ATTN_Q = ATTN_ROWS_PER_STEP * GRID_W
ATTN_KEY_ROWS = 12
ATTN_KEYS = ATTN_KEY_ROWS * GRID_W
ATTN_KEY_CHUNK = 256
ATTN_GROUPS_PER_STEP = 4
FF_CHUNK = 768
LATER_WEIGHT_SLABS = (32, 16, 32, 32, 32)
HALO = 8
FFN_ROW_BLOCK = 64


def _cparams(sem):
    return pltpu.CompilerParams(dimension_semantics=sem, vmem_limit_bytes=VMEM_LIMIT_BYTES)


@functools.lru_cache(maxsize=None)
def _rope_tables(seq):
    pos = np.arange(seq)
    row = pos // GRID_W
    col = pos % GRID_W
    n_pairs = HEAD_DIM // 4
    inv = ROPE_BASE ** (-np.arange(n_pairs, dtype=np.float64) / n_pairs)
    lane = np.arange(LANES) % HEAD_DIM
    p = np.where(lane[None, :] < HEAD_DIM // 2, row[:, None], col[:, None]).astype(np.float64)
    ang = p * inv[lane % n_pairs][None, :]
    sign = np.where((lane % (2 * n_pairs)) < n_pairs, -1.0, 1.0)
    return np.cos(ang).astype(np.float32), (np.sin(ang) * sign[None, :]).astype(np.float32)


@functools.lru_cache(maxsize=None)
def _filter_tables(seq):
    assert 2 * seq == FFT_N
    half, n2, n1 = np.meshgrid(np.arange(2), np.arange(FFT_N2), np.arange(FFT_HALF_N1), indexing="ij")
    n = FFT_N2 * (half * FFT_HALF_N1 + n1) + n2
    fwd = n < seq
    m = n - seq
    valid = fwd | (m >= 1)
    pos = np.where(valid, np.where(fwd, n, seq - m), 0).astype(np.float64).reshape(2, seq)
    t = pos / max(seq - 1, 1)
    bands = np.linspace(1e-4, POS_BANDS - 1, POS_BANDS)
    ang = (2.0 * math.pi / seq) * pos[..., None] * bands
    z = np.zeros((2, seq, 64), np.float64)
    z[..., 0] = t
    z[..., 1:1 + POS_BANDS] = np.cos(ang)
    z[..., 1 + POS_BANDS:POS_FEATS] = -np.sin(ang)
    aux = np.zeros((2, seq, 8), np.float64)
    aux[..., 0] = t
    aux[..., 1] = valid.reshape(2, seq)
    zp = z.reshape(2, seq // FILT_ROWS, 2, FILT_ROWS // 2, 64).transpose(0, 1, 3, 2, 4).reshape(2, seq // 2, 128)
    return zp.astype(np.float32), aux.astype(np.float32)


def _realify(m):
    return np.block([[m.real, -m.imag], [m.imag, m.real]])


@functools.lru_cache(maxsize=None)
def _fft_matrices():
    n1 = np.arange(FFT_N1)
    n2 = np.arange(FFT_N2)
    k1 = np.arange(FFT_N1)
    k2 = np.arange(FFT_N2)
    f1 = np.exp(-2j * np.pi * np.outer(k1, n1) / FFT_N1)
    w1_data = _realify(f1[:, :FFT_HALF_N1])
    w1_real = np.concatenate([f1.real, f1.imag], axis=0)
    v1 = _realify(np.conj(f1.T)[:FFT_HALF_N1, :] / FFT_N)
    f2 = np.exp(-2j * np.pi * np.outer(k2, n2) / FFT_N2)
    tw = np.exp(-2j * np.pi * np.outer(k1, n2) / FFT_N)
    w2 = np.stack([_realify(f2 * tw[a][None, :]) for a in range(FFT_N1)])
    return {k: v.astype(np.float32) for k, v in
            dict(w1_data=w1_data, w1_real=w1_real, v1=v1, w2=w2).items()}


N_BIAS_ROWS = 2 * WIN_ROWS - 1
N_BIAS_COLS = 2 * WIN_COLS - 1


def _attn_bias_row_index():
    rows = GRID_W
    groups = (0, 2, rows // ATTN_ROWS_PER_STEP - 1)
    dr = np.full((3, ATTN_ROWS_PER_STEP, ATTN_KEY_ROWS), -1, np.int32)
    for v, g in enumerate(groups):
        ws = min(max(ATTN_ROWS_PER_STEP * g - WIN_ROWS // 2, 0), rows - ATTN_KEY_ROWS)
        for i in range(ATTN_ROWS_PER_STEP):
            r = ATTN_ROWS_PER_STEP * g + i
            r_start = min(max(r - WIN_ROWS // 2, 0), rows - WIN_ROWS)
            for j in range(ATTN_KEY_ROWS):
                kr = ws + j
                if r_start <= kr < r_start + WIN_ROWS:
                    dr[v, i, j] = kr - r + (WIN_ROWS - 1)
    return dr


def _attn_bias_kernel(rpb_ref, w_ref, o_ref, wb_ref, t_ref):
    wb_ref[...] = w_ref[...].astype(BF16)
    head = pl.program_id(0)
    kc = lax.broadcasted_iota(jnp.int32, (GRID_W, LANES), 0)
    lane = lax.broadcasted_iota(jnp.int32, (GRID_W, LANES), 1)
    qc = lane % GRID_W
    c_start = jnp.clip(qc - WIN_COLS // 2, 0, GRID_W - WIN_COLS)
    col_in = (kc >= c_start) & (kc < c_start + WIN_COLS)
    dc = jnp.clip(kc - qc, 1 - WIN_COLS, WIN_COLS - 1) + (WIN_COLS - 1)
    base = head * (N_BIAS_ROWS * N_BIAS_COLS)
    for r in range(N_BIAS_ROWS):
        t = jnp.full((GRID_W, LANES), NEG_BIAS, F32)
        for cidx in range(N_BIAS_COLS):
            t = jnp.where(col_in & (dc == cidx), rpb_ref[base + r * N_BIAS_COLS + cidx] * LOG2_E, t)
        t_ref[r] = t
    dr = _attn_bias_row_index()
    low_half = lane < GRID_W
    masked = jnp.full((GRID_W, LANES), NEG_BIAS, F32)
    for v in range(3):
        for i in range(0, ATTN_ROWS_PER_STEP, 2):
            for j in range(ATTN_KEY_ROWS):
                lo = t_ref[int(dr[v, i, j])] if dr[v, i, j] >= 0 else masked
                hi = t_ref[int(dr[v, i + 1, j])] if dr[v, i + 1, j] >= 0 else masked
                o_ref[v, 0, j * GRID_W:(j + 1) * GRID_W, i * GRID_W:(i + 2) * GRID_W] = jnp.where(
                    low_half, lo, hi)


def _mod_kernel(c_ref, w_ref, b_ref, o_ref):
    c = c_ref[...]
    s = c * jax.nn.sigmoid(c)
    o_ref[...] = jnp.dot(s, w_ref[...], precision=lax.Precision.HIGHEST,
                         preferred_element_type=F32) + b_ref[...]


def _norm_modulate(x, gain, shift, scale):
    ms = jnp.mean(x * x, axis=-1, keepdims=True)
    y = x * lax.rsqrt(ms + RMS_EPS) * gain
    return y * (1.0 + scale) + shift


def _rope(t, cos, sin_signed):
    n_pairs = HEAD_DIM // 4
    lane = lax.broadcasted_iota(jnp.int32, t.shape, 1)
    first = (lane % (2 * n_pairs)) < n_pairs
    partner = jnp.where(first, pltpu.roll(t, LANES - n_pairs, 1), pltpu.roll(t, n_pairs, 1))
    return t * cos + partner * sin_signed


def _in_proj_kernel(x_ref, prev_ref, next_ref, mod_ref, g_ref, w_ref, b_ref, cos_ref, sin_ref, cw_ref, cb_ref,
                    *rest):
    n_cast = len(LATER_WEIGHT_SLABS)
    cast_in, rest = rest[:n_cast], rest[n_cast:]
    (qr_ref, qp_ref, kr_ref, v_ref, gt_ref, u_ref, x0_ref), rest = rest[:7], rest[7:]
    cast_out, (hy_ref,) = rest[:n_cast], rest[n_cast:]
    for src, dst in zip(cast_in, cast_out):
        dst[...] = src[...].astype(BF16)
    i = pl.program_id(1)
    n_tiles = pl.num_programs(1)
    xx = jnp.concatenate([prev_ref[0], x_ref[0], next_ref[0]], axis=0)
    h_ext = _norm_modulate(xx, g_ref[...], mod_ref[0, 0:1, :], mod_ref[0, 1:2, :]).astype(BF16)
    h = h_ext[HALO:HALO + TOKEN_TILE]
    cos = cos_ref[...]
    sin = sin_ref[...]

    def proj(lo, hi, rows=h):
        return jnp.dot(rows, w_ref[:, lo:hi], preferred_element_type=F32) + b_ref[:, lo:hi]

    q = proj(0, D_ATTN) * (HEAD_DIM ** -0.5 * LOG2_E)
    qp_ref[0] = q.astype(BF16)
    k = proj(D_ATTN, 2 * D_ATTN)
    for c in range(D_ATTN // LANES):
        lanes = slice(c * LANES, (c + 1) * LANES)
        qr_ref[0, :, lanes] = _rope(q[:, lanes], cos, sin).astype(BF16)
        kr_ref[0, :, lanes] = _rope(k[:, lanes], cos, sin).astype(BF16)
    v_t = proj(2 * D_ATTN, 3 * D_ATTN).T.astype(BF16)
    for c in range(TOKEN_TILE // ATTN_KEY_CHUNK):
        v_ref[0, c] = v_t[:, c * ATTN_KEY_CHUNK:(c + 1) * ATTN_KEY_CHUNK]
    hy_lo = 3 * D_ATTN
    gl_lo = hy_lo + 3 * D_HYENA
    for c in range(4):
        w = D_MODEL // 2
        gt_ref[0, :, c * w:(c + 1) * w] = jax.nn.sigmoid(
            proj(gl_lo + c * w, gl_lo + (c + 1) * w)).astype(BF16)

    slabs_per_part = D_HYENA // LANES
    for c in range(3):
        hy = proj(hy_lo + c * D_HYENA, hy_lo + (c + 1) * D_HYENA, h_ext)
        for s in range(slabs_per_part):
            hy_ref[c * slabs_per_part + s] = hy[:, s * LANES:(s + 1) * LANES]
    zero_row = jnp.zeros((1, LANES), F32)

    @pl.when(i == 0)
    def _():
        for s in range(3 * slabs_per_part):
            hy_ref[s, HALO - 1:HALO, :] = zero_row

    @pl.when(i == n_tiles - 1)
    def _():
        for s in range(3 * slabs_per_part):
            hy_ref[s, HALO + TOKEN_TILE:HALO + TOKEN_TILE + 1, :] = zero_row

    def conv(s, r0, rows):
        lanes = slice(s * LANES, (s + 1) * LANES)
        tap = lambda j: hy_ref[s, pl.ds(HALO - 1 + j + r0, rows, stride=1), :]
        return (tap(0) * cw_ref[0:1, lanes] + tap(1) * cw_ref[1:2, lanes] + tap(2) * cw_ref[2:3, lanes]
                + cb_ref[:, lanes])

    for s in range(slabs_per_part):
        lanes = slice(s * LANES, (s + 1) * LANES)
        for j in range(TILE_GROUPS):
            r0 = j * FFT_N2
            x0_ref[0, r0:r0 + FFT_N2, lanes] = conv(s, r0, FFT_N2).astype(BF16)
            u_ref[0, j * SEQ_PITCH:j * SEQ_PITCH + FFT_N2, lanes] = (
                conv(slabs_per_part + s, r0, FFT_N2) * conv(2 * slabs_per_part + s, r0, FFT_N2))
            u_ref[0, j * SEQ_PITCH + FFT_N2:(j + 1) * SEQ_PITCH, lanes] = jnp.zeros((SEQ_PITCH - FFT_N2, LANES), F32)


def _ctx_kv_kernel(x_ref, mod_ref, g_ref, w_ref, b_ref, k_ref, v_ref):
    h = _norm_modulate(x_ref[0], g_ref[...], mod_ref[0:1, :], mod_ref[1:2, :]).astype(BF16)
    kv = jnp.dot(h, w_ref[...], preferred_element_type=F32) + b_ref[...]
    k_ref[0] = kv[:, :D_ATTN].astype(BF16)
    v_ref[0] = kv[:, D_ATTN:].T.astype(BF16)


def _attn_window_start(g):
    return jnp.clip(ATTN_ROWS_PER_STEP * g - WIN_ROWS // 2, 0, GRID_W - ATTN_KEY_ROWS)


def _attn_kernel(qr_ref, qp_ref, k_ref, vt_ref, kc_ref, vct_ref, bias_ref, o_ref):
    step = pl.program_id(1)
    n_groups = GRID_W // ATTN_ROWS_PER_STEP
    nt = (((1,), (1,)), ((), ()))
    quad_w = 4 * HEAD_DIM
    lane = lax.broadcasted_iota(jnp.int32, (1, quad_w), 1)
    zero = jnp.zeros((), BF16)

    def group_params(sub):
        g = step * ATTN_GROUPS_PER_STEP + sub
        win = _attn_window_start(g)
        key0 = pl.multiple_of(win * GRID_W, ATTN_KEY_CHUNK)
        chunk0 = win // (ATTN_KEY_CHUNK // GRID_W)
        variant = (g > 0).astype(jnp.int32) + (g == n_groups - 1).astype(jnp.int32)
        return key0, chunk0, variant

    params = [group_params(sub) for sub in range(ATTN_GROUPS_PER_STEP)]

    def scores(item):
        sub, head = item
        key0, _, variant = params[sub]
        qrows = slice(sub * ATTN_Q, (sub + 1) * ATTN_Q)
        quad, hh = divmod(head, 4)
        ql = slice(quad * quad_w, (quad + 1) * quad_w)
        mine = (lane >= hh * HEAD_DIM) & (lane < (hh + 1) * HEAD_DIM)
        s_nb = lax.dot_general(k_ref[0, pl.ds(key0, ATTN_KEYS), ql], jnp.where(mine, qr_ref[0, qrows, ql], zero), nt,
                               preferred_element_type=F32)
        s_cx = lax.dot_general(kc_ref[0, :, ql], jnp.where(mine, qp_ref[0, qrows, ql], zero), nt,
                               preferred_element_type=F32)
        return s_nb + bias_ref[variant, head], s_cx

    def with_ones(v):
        return jnp.concatenate([v, jnp.ones((16, v.shape[1]), BF16)], axis=0)

    def probs(s_nb, s_cx):
        m = jnp.maximum(jnp.max(s_nb, axis=0, keepdims=True), jnp.max(s_cx, axis=0, keepdims=True))
        return jnp.exp2(s_nb - m).astype(BF16), jnp.exp2(s_cx - m).astype(BF16)

    def values(item, p_nb, p_cx):
        sub, head = item
        chunk0 = params[sub][1]
        rows = slice(head * HEAD_DIM, (head + 1) * HEAD_DIM)
        v_win = jnp.concatenate([vt_ref[0, chunk0 + c, rows, :] for c in range(ATTN_KEYS // ATTN_KEY_CHUNK)],
                                axis=1)
        o = (jnp.dot(with_ones(v_win), p_nb, preferred_element_type=F32)
             + jnp.dot(with_ones(vct_ref[0, rows, :]), p_cx, preferred_element_type=F32))
        return o[:HEAD_DIM] / o[HEAD_DIM:HEAD_DIM + 1]

    items = [(sub, head) for sub in range(ATTN_GROUPS_PER_STEP) for head in range(N_HEADS)]
    outs = []
    s_q = {0: scores(items[0]), 1: scores(items[1])}
    p_q = {0: probs(*s_q.pop(0))}
    for n, item in enumerate(items):
        if n + 2 < len(items):
            s_q[n + 2] = scores(items[n + 2])
        if n + 1 < len(items):
            p_q[n + 1] = probs(*s_q.pop(n + 1))
        outs.append(values(item, *p_q.pop(n)))
    for sub in range(ATTN_GROUPS_PER_STEP):
        o_ref[0, sub * ATTN_Q:(sub + 1) * ATTN_Q, :] = jnp.concatenate(
            outs[sub * N_HEADS:(sub + 1) * N_HEADS], axis=0).T.astype(BF16)


def _filt_kernel(z_ref, aux_ref, w1_ref, b1_ref, w2_ref, b2_ref, w3_ref, b3_ref, freq_ref, decay_ref,
                 o_ref, asum_ref):
    i = pl.program_id(1)
    hp = lax.Precision.HIGHEST
    h = jnp.sin(freq_ref[0:1, :] * (jnp.dot(z_ref[0], w1_ref[...], precision=hp,
                                            preferred_element_type=F32) + b1_ref[...]))
    h = jnp.sin(freq_ref[1:2, :] * (jnp.dot(h, w2_ref[...], precision=hp,
                                            preferred_element_type=F32) + b2_ref[...]))
    taps = jnp.concatenate([jnp.dot(h, w3_ref[part], precision=hp, preferred_element_type=F32)
                            for part in range(2)], axis=0) + b3_ref[...]
    t = aux_ref[0, :, 0:1]
    valid = aux_ref[0, :, 1:2] > 0.5
    k = jnp.where(valid, taps * jnp.exp(-t * jnp.abs(decay_ref[0])), 0.0)
    part = jnp.sum(jnp.abs(k), axis=0, keepdims=True)

    @pl.when(i == 0)
    def _():
        asum_ref[...] = jnp.zeros_like(asum_ref)

    asum_ref[0] += jnp.broadcast_to(part, asum_ref.shape[1:])
    o_ref[...] = k.reshape(o_ref.shape)


def _fft_stage1(load_group, w1_ref, a_ref):
    n_slabs = a_ref.shape[0]

    def body(n2, carry):
        x = load_group(n2)
        a = jnp.dot(w1_ref[...], x, preferred_element_type=F32)
        for s in range(n_slabs):
            a_ref[s, pl.ds(n2 * FFT_PITCH, 2 * FFT_N1, stride=1), :] = a[:, s * LANES:(s + 1) * LANES]
        return carry

    lax.fori_loop(0, FFT_N2, body, 0, unroll=FFT_UNROLL)


def _fft_load_k1(a_ref, k1):
    parts = []
    for off in (0, FFT_N1):
        parts.append(jnp.concatenate(
            [a_ref[s, pl.ds(k1 + off, FFT_N2, stride=FFT_PITCH), :] for s in range(a_ref.shape[0])], axis=1))
    return jnp.concatenate(parts, axis=0)


def _filt_fft_kernel(k_ref, asum_ref, w1_ref, w2_ref, o_ref, a_ref):
    step = pl.program_id(1)

    @pl.when(step == 0)
    def _():
        norm = asum_ref[0, 0:1, :] + asum_ref[1, 0:1, :]

        def load_group(n2):
            return (k_ref[pl.ds(pl.multiple_of(n2 * FFT_N1, FFT_N1), FFT_N1), :] / norm).astype(BF16)
        _fft_stage1(load_group, w1_ref, a_ref)

    for j in range(FFT_K1_PER_STEP):
        b = _fft_load_k1(a_ref, step * FFT_K1_PER_STEP + j).astype(BF16)
        o_ref[j] = jnp.dot(w2_ref[j], b, preferred_element_type=F32).astype(BF16)


def _hy_conv_kernel(u_ref, kf_ref, w1_ref, w2_ref, v1_ref, o_ref, a_ref):
    step = pl.program_id(1)
    n_slabs = a_ref.shape[0]

    def tokens_of(member, s, n2):
        return (2 * s + member, pl.ds(n2, FFT_HALF_N1, stride=SEQ_PITCH), slice(None))

    @pl.when(step == 0)
    def _():
        for sample in range(o_ref.shape[0]):
            for grp in range(FFT_HALF_N1):
                o_ref[sample, grp * SEQ_PITCH + FFT_N2:(grp + 1) * SEQ_PITCH, :] = jnp.zeros(
                    (SEQ_PITCH - FFT_N2, LANES), F32)

        def load_group(n2):
            return jnp.concatenate(
                [jnp.concatenate([u_ref[tokens_of(member, s, n2)] for s in range(n_slabs)], axis=1)
                 for member in range(2)], axis=0).astype(BF16)
        _fft_stage1(load_group, w1_ref, a_ref)

    for j in range(FFT_K1_PER_STEP):
        k1 = step * FFT_K1_PER_STEP + j
        b = _fft_load_k1(a_ref, k1).astype(BF16)
        x = jnp.dot(w2_ref[j], b, preferred_element_type=F32)
        kf = kf_ref[j].astype(F32)
        kf = jnp.concatenate([kf] * n_slabs, axis=1)
        xr, xi = x[:FFT_N2], x[FFT_N2:]
        kr, ki = kf[:FFT_N2], kf[FFT_N2:]
        y = jnp.concatenate([xr * kr - xi * ki, xr * ki + xi * kr], axis=0).astype(BF16)
        d = lax.dot_general(w2_ref[j], y, (((0,), (0,)), ((), ())),
                            preferred_element_type=F32)
        for s in range(n_slabs):
            lanes = slice(s * LANES, (s + 1) * LANES)
            a_ref[s, pl.ds(k1, FFT_N2, stride=FFT_PITCH), :] = d[:FFT_N2, lanes]
            a_ref[s, pl.ds(k1 + FFT_N1, FFT_N2, stride=FFT_PITCH), :] = d[FFT_N2:, lanes]

    @pl.when(step == FFT_STEPS - 1)
    def _():
        def body(n2, carry):
            d = jnp.concatenate([a_ref[s, pl.ds(n2 * FFT_PITCH, 2 * FFT_N1, stride=1), :] for s in range(n_slabs)],
                                axis=1)
            y = jnp.dot(v1_ref[...], d.astype(BF16), preferred_element_type=F32)
            for s in range(n_slabs):
                for member in range(2):
                    o_ref[tokens_of(member, s, n2)] = y[member * FFT_HALF_N1:(member + 1) * FFT_HALF_N1,
                                                        s * LANES:(s + 1) * LANES]
            return carry

        lax.fori_loop(0, FFT_N2, body, 0, unroll=FFT_UNROLL)


def _from_pitched(ref):
    return jnp.concatenate([ref[0, j * SEQ_PITCH:j * SEQ_PITCH + FFT_N2, :] for j in range(TILE_GROUPS)], axis=0)


def _merge_kernel(yna_ref, x0_ref, u_ref, yc_ref, gt_ref, skip_ref, gpost_ref, wna_ref, why_ref, wout_ref, o_ref):
    y_hy = (x0_ref[0].astype(F32) * (_from_pitched(yc_ref) + _from_pitched(u_ref) * skip_ref[...])).astype(BF16)
    a = jnp.dot(yna_ref[0], wna_ref[...], preferred_element_type=F32)
    b = jnp.dot(y_hy, why_ref[...], preferred_element_type=F32)
    g_na = gt_ref[0, :, :D_MODEL].astype(F32)
    g_hy = gt_ref[0, :, D_MODEL:].astype(F32)
    m = (g_na * a + g_hy * b).astype(BF16)
    o = jnp.dot(m, wout_ref[...], preferred_element_type=F32)
    ms = jnp.mean(o * o, axis=-1, keepdims=True)
    o_ref[0] = (o * lax.rsqrt(ms + RMS_EPS) * gpost_ref[...]).astype(BF16)


def _gelu_tanh(a):
    return 0.5 * a * (1.0 + jnp.tanh(math.sqrt(2.0 / math.pi) * (a + 0.044715 * (a * a * a))))


def _ffn_kernel(x_ref, prev_ref, next_ref, mix_ref, mix_prev_ref, mix_next_ref, mod_ref, gpre_ref, gpost_ref,
                wup_ref, cw_ref, cb_ref, wdn_ref, o_ref, u_ref, act_ref):
    i = pl.program_id(1)
    n_tiles = pl.num_programs(1)
    mix_halo = mix_prev_ref.shape[1]
    mix = jnp.concatenate([mix_prev_ref[0, mix_halo - HALO:, :], mix_ref[0], mix_next_ref[0, :HALO, :]], axis=0)
    xx = (jnp.concatenate([prev_ref[0], x_ref[0], next_ref[0]], axis=0)
          + mod_ref[0, 2:3, :] * mix.astype(F32))
    x = xx[HALO:HALO + TOKEN_TILE]
    h = _norm_modulate(xx, gpre_ref[...], mod_ref[0, 3:4, :], mod_ref[0, 4:5, :])
    row = lax.broadcasted_iota(jnp.int32, (xx.shape[0], 1), 0)
    inside = ((row >= HALO) | (i > 0)) & ((row < HALO + TOKEN_TILE) | (i < n_tiles - 1))
    h = jnp.where(inside, h, 0.0).astype(BF16)
    starts = list(range(0, D_FF, FF_CHUNK))
    widths = [min(FF_CHUNK, D_FF - lo) for lo in starts]

    def conv(buf, half, s, lo, r0):
        lanes = slice(half * D_FF + lo, half * D_FF + lo + LANES)
        tap = lambda j: u_ref[buf, half, s, pl.ds(HALO - 1 + j + r0, FFN_ROW_BLOCK, stride=1), :]
        return (tap(0) * cw_ref[0:1, lanes] + tap(1) * cw_ref[1:2, lanes] + tap(2) * cw_ref[2:3, lanes]
                + cb_ref[:, lanes])

    def up_project(c):
        for half in range(2):
            lo = half * D_FF + starts[c]
            u = jnp.dot(h, wup_ref[:, lo:lo + widths[c]], preferred_element_type=F32)
            for s in range(widths[c] // LANES):
                u_ref[c % 2, half, s] = u[:, s * LANES:(s + 1) * LANES]

    up_project(0)
    for c in range(len(starts)):
        buf = c % 2
        if c + 1 < len(starts):
            up_project(c + 1)
        for s in range(widths[c] // LANES):
            lo = starts[c] + s * LANES
            for r0 in range(0, TOKEN_TILE, FFN_ROW_BLOCK):
                act_ref[r0:r0 + FFN_ROW_BLOCK, lo:lo + LANES] = (
                    _gelu_tanh(conv(buf, 0, s, lo, r0)) * conv(buf, 1, s, lo, r0)).astype(BF16)
    y = jnp.dot(act_ref[...], wdn_ref[...], preferred_element_type=F32)
    ms = jnp.mean(y * y, axis=-1, keepdims=True)
    y = y * lax.rsqrt(ms + RMS_EPS) * gpost_ref[...]
    o_ref[0] = x + mod_ref[0, 5:6, :] * y


def _const_spec(shape):
    nd = len(shape)
    return pl.BlockSpec(shape, lambda *_: (0,) * nd, pipeline_mode=pl.Buffered(1))


def kernel(x, c, ctx, c_ctx, w_mod, b_mod, norm_mix_pre, norm_mix_post, norm_ffn_pre, norm_ffn_post, w_in, b_in, na_rpb, hy_conv_w, hy_conv_b, hy_filt_w1, hy_filt_b1, hy_filt_w2, hy_filt_b2, hy_filt_w3, hy_filt_b3, hy_sin_freq, hy_decay, hy_skip, w_o_na, w_o_hy, w_out, ffn_w_up, ffn_conv_w, ffn_conv_b, ffn_w_down):
    batch, seq, d = x.shape
    n_ctx = ctx.shape[1]
    assert d == D_MODEL and 2 * seq == FFT_N and seq == GRID_W * GRID_W and batch % 2 == 0
    assert w_mod.shape[0] == 1, "single-layer block"
    n_tiles = seq // TOKEN_TILE
    d_in = w_in.shape[2]
    row2 = lambda a: a.reshape(1, -1)

    c_all = jnp.zeros((8, d), F32).at[:batch].set(c).at[batch].set(c_ctx)
    mod_n = 1024
    mod = pl.pallas_call(
        _mod_kernel,
        grid=(N_MOD * d // mod_n,),
        in_specs=[_const_spec((8, d)),
                  pl.BlockSpec((d, mod_n), lambda j: (0, j)),
                  pl.BlockSpec((1, mod_n), lambda j: (0, j))],
        out_specs=pl.BlockSpec((8, mod_n), lambda j: (0, j)),
        out_shape=jax.ShapeDtypeStruct((8, N_MOD * d), F32),
        compiler_params=_cparams(("arbitrary",)),
        name="mod",
    )(c_all, w_mod[0], row2(b_mod[0]))
    mod_lat = jnp.pad(mod[:batch].reshape(batch, N_MOD, d), ((0, 0), (0, 8 - N_MOD), (0, 0)))
    mod_ctx = jnp.pad(mod[batch].reshape(N_MOD, d), ((0, 8 - N_MOD), (0, 0)))

    w_rows = d // N_HEADS
    bias, w_in_b = pl.pallas_call(
        _attn_bias_kernel,
        grid=(N_HEADS,),
        in_specs=[pl.BlockSpec(memory_space=pltpu.SMEM), pl.BlockSpec((w_rows, d_in), lambda h: (h, 0))],
        out_specs=[pl.BlockSpec((3, 1, ATTN_KEYS, ATTN_Q), lambda h: (0, h, 0, 0)),
                   pl.BlockSpec((w_rows, d_in), lambda h: (h, 0))],
        out_shape=[jax.ShapeDtypeStruct((3, N_HEADS, ATTN_KEYS, ATTN_Q), F32),
                   jax.ShapeDtypeStruct((d, d_in), BF16)],
        scratch_shapes=[pltpu.VMEM((N_BIAS_ROWS, GRID_W, LANES), F32)],
        compiler_params=_cparams(("arbitrary",)),
        name="attn_bias",
    )(na_rpb[0].reshape(-1), w_in[0])
    b_in_r = row2(b_in[0])
    g_mix_pre = row2(norm_mix_pre[0])

    k_ctx, v_ctx = pl.pallas_call(
        _ctx_kv_kernel,
        grid=(batch,),
        in_specs=[pl.BlockSpec((1, n_ctx, d), lambda b: (b, 0, 0)),
                  _const_spec((8, d)), _const_spec((1, d)),
                  _const_spec((d, 2 * D_ATTN)), _const_spec((1, 2 * D_ATTN))],
        out_specs=[pl.BlockSpec((1, n_ctx, D_ATTN), lambda b: (b, 0, 0)),
                   pl.BlockSpec((1, D_ATTN, n_ctx), lambda b: (b, 0, 0))],
        out_shape=[jax.ShapeDtypeStruct((batch, n_ctx, D_ATTN), BF16),
                   jax.ShapeDtypeStruct((batch, D_ATTN, n_ctx), BF16)],
        compiler_params=_cparams(("arbitrary",)),
        name="ctx_kv",
    )(ctx, mod_ctx, g_mix_pre, w_in_b[:, D_ATTN:3 * D_ATTN], b_in_r[:, D_ATTN:3 * D_ATTN])

    cos_t, sin_t = _rope_tables(seq)
    tok = lambda w: pl.BlockSpec((1, TOKEN_TILE, w), lambda b, i: (b, i, 0))
    mod_spec = pl.BlockSpec((1, 8, d), lambda b, i: (b, 0, 0))
    rope_spec = pl.BlockSpec((TOKEN_TILE, LANES), lambda b, i: (i, 0))
    halo_blocks = TOKEN_TILE // HALO
    n_halo_blocks = seq // HALO
    prev_spec = lambda w: pl.BlockSpec((1, HALO, w), lambda b, i: (b, jnp.maximum(i * halo_blocks - 1, 0), 0))
    next_spec = lambda w: pl.BlockSpec(
        (1, HALO, w), lambda b, i: (b, jnp.minimum((i + 1) * halo_blocks, n_halo_blocks - 1), 0))
    pitched_spec = pl.BlockSpec((1, TILE_GROUPS * SEQ_PITCH, D_HYENA), lambda b, i: (b, i, 0))
    pitched_shape = jax.ShapeDtypeStruct((batch, PITCHED_ROWS, D_HYENA), F32)
    later_weights = [ffn_w_up[0], ffn_w_down[0], w_out[0], w_o_na[0], w_o_hy[0]]
    slab_specs = [
        pl.BlockSpec((w.shape[0] // n, w.shape[1]), lambda b, i, n=n: (jnp.minimum(b * n_tiles + i, n - 1), 0))
        for w, n in zip(later_weights, LATER_WEIGHT_SLABS)]
    q_rot, q_plain, k_rot, v_lat, gates, u_p, x0, w_up_c, w_dn_c, w_out_b, w_o_na_b, w_o_hy_b = pl.pallas_call(
        _in_proj_kernel,
        grid=(batch, n_tiles),
        in_specs=[tok(d), prev_spec(d), next_spec(d), mod_spec, _const_spec((1, d)), _const_spec((d, d_in)),
                  _const_spec((1, d_in)), rope_spec, rope_spec,
                  _const_spec((3, 3 * D_HYENA)), _const_spec((1, 3 * D_HYENA))] + slab_specs,
        out_specs=[tok(D_ATTN)] * 3
        + [pl.BlockSpec((1, TOKEN_TILE // ATTN_KEY_CHUNK, D_ATTN, ATTN_KEY_CHUNK), lambda b, i: (b, i, 0, 0)),
           tok(2 * d), pitched_spec, tok(D_HYENA)] + slab_specs,
        out_shape=[jax.ShapeDtypeStruct((batch, seq, D_ATTN), BF16)] * 3
        + [jax.ShapeDtypeStruct((batch, seq // ATTN_KEY_CHUNK, D_ATTN, ATTN_KEY_CHUNK), BF16),
           jax.ShapeDtypeStruct((batch, seq, 2 * d), BF16), pitched_shape,
           jax.ShapeDtypeStruct((batch, seq, D_HYENA), BF16)]
        + [jax.ShapeDtypeStruct(w.shape, BF16) for w in later_weights],
        scratch_shapes=[pltpu.VMEM((3 * D_HYENA // LANES, TOKEN_TILE + 2 * HALO, LANES), F32)],
        compiler_params=_cparams(("arbitrary", "arbitrary")),
        name="in_proj",
    )(x, x, x, mod_lat, g_mix_pre, w_in_b, b_in_r, jnp.asarray(cos_t), jnp.asarray(sin_t),
      hy_conv_w[0], row2(hy_conv_b[0]), *later_weights)

    n_steps = GRID_W // ATTN_ROWS_PER_STEP // ATTN_GROUPS_PER_STEP
    q_spec = pl.BlockSpec((1, ATTN_GROUPS_PER_STEP * ATTN_Q, D_ATTN), lambda b, g: (b, g, 0))
    full = lambda n: pl.BlockSpec((1, n, D_ATTN), lambda b, g: (b, 0, 0))
    bias_spec = _const_spec((3, N_HEADS, ATTN_KEYS, ATTN_Q))
    vt_spec = pl.BlockSpec((1, seq // ATTN_KEY_CHUNK, D_ATTN, ATTN_KEY_CHUNK), lambda b, g: (b, 0, 0, 0))
    vct_spec = pl.BlockSpec((1, D_ATTN, n_ctx), lambda b, g: (b, 0, 0))
    y_na = pl.pallas_call(
        _attn_kernel,
        grid=(batch, n_steps),
        in_specs=[q_spec, q_spec, full(seq), vt_spec, full(n_ctx), vct_spec, bias_spec],
        out_specs=q_spec,
        out_shape=jax.ShapeDtypeStruct((batch, seq, D_ATTN), BF16),
        compiler_params=_cparams(("arbitrary", "arbitrary")),
        name="attn",
    )(q_rot, q_plain, k_rot, v_lat, k_ctx, v_ctx, bias)

    z_t, aux_t = _filter_tables(seq)
    filt_n2 = FILT_ROWS // FFT_HALF_N1
    hid = FILTER_HIDDEN
    w1_pad = jnp.pad(hy_filt_w1[0], ((0, hid - POS_FEATS), (0, 0)))
    block_diag = lambda w: jnp.zeros((2 * hid, 2 * hid), F32).at[:hid, :hid].set(w).at[hid:, hid:].set(w)
    twice = lambda v: jnp.tile(v, (1, 2))
    w3 = hy_filt_w3[0]
    w3_parts = jnp.stack([jnp.concatenate([w3, jnp.zeros_like(w3)], axis=0),
                          jnp.concatenate([jnp.zeros_like(w3), w3], axis=0)])
    k_circ, k_asum = pl.pallas_call(
        _filt_kernel,
        grid=(2, seq // FILT_ROWS),
        in_specs=[pl.BlockSpec((1, FILT_ROWS // 2, 2 * hid), lambda hf, i: (hf, i, 0)),
                  pl.BlockSpec((1, FILT_ROWS, aux_t.shape[2]), lambda hf, i: (hf, i, 0)),
                  _const_spec((2 * hid, 2 * hid)), _const_spec((1, 2 * hid)),
                  _const_spec((2 * hid, 2 * hid)), _const_spec((1, 2 * hid)),
                  pl.BlockSpec((2, 2 * hid, D_HYENA), lambda hf, i: (0, 0, hf)),
                  pl.BlockSpec((1, D_HYENA), lambda hf, i: (0, hf)),
                  _const_spec((2, 2 * hid)),
                  pl.BlockSpec((1, 1, D_HYENA), lambda hf, i: (hf, 0, 0))],
        out_specs=[pl.BlockSpec((filt_n2, FFT_HALF_N1, D_HYENA), lambda hf, i: (i, hf, 0)),
                   pl.BlockSpec((1, 8, D_HYENA), lambda hf, i: (hf, 0, 0))],
        out_shape=[jax.ShapeDtypeStruct((FFT_N2, FFT_N1, D_HYENA), F32),
                   jax.ShapeDtypeStruct((2, 8, D_HYENA), F32)],
        compiler_params=_cparams(("arbitrary", "arbitrary")),
        name="filt",
    )(jnp.asarray(z_t), jnp.asarray(aux_t), block_diag(w1_pad), twice(row2(hy_filt_b1[0])),
      block_diag(hy_filt_w2[0]), twice(row2(hy_filt_b2[0])), w3_parts, row2(hy_filt_b3[0]),
      twice(hy_sin_freq[0]), hy_decay[0].reshape(2, 1, D_HYENA))
    k_circ = k_circ.reshape(FFT_N, D_HYENA)

    mats = _fft_matrices()
    mats = {k: jnp.asarray(v).astype(BF16) for k, v in mats.items()}
    w2_all = mats["w2"]
    step_mat_spec = pl.BlockSpec((FFT_K1_PER_STEP, 2 * FFT_N2, 2 * FFT_N2), lambda cb, s: (s, 0, 0))
    filt_slabs = 2
    kf = pl.pallas_call(
        _filt_fft_kernel,
        grid=(D_HYENA // (filt_slabs * LANES), FFT_STEPS),
        in_specs=[pl.BlockSpec((FFT_N, filt_slabs * LANES), lambda cb, s: (0, cb)),
                  pl.BlockSpec((2, 8, filt_slabs * LANES), lambda cb, s: (0, 0, cb)),
                  _const_spec((2 * FFT_N1, FFT_N1)), step_mat_spec],
        out_specs=pl.BlockSpec((FFT_K1_PER_STEP, 2 * FFT_N2, filt_slabs * LANES), lambda cb, s: (s, 0, cb)),
        out_shape=jax.ShapeDtypeStruct((FFT_N1, 2 * FFT_N2, D_HYENA), BF16),
        scratch_shapes=[pltpu.VMEM((filt_slabs, FFT_N2 * FFT_PITCH, LANES), F32)],
        compiler_params=_cparams(("arbitrary", "arbitrary")),
        name="filt_fft",
    )(k_circ, k_asum, mats["w1_real"], w2_all)

    n_pairs = batch // 2
    pair_block = ((batch, PITCHED_ROWS, LANES), lambda cb, s: (0, 0, cb))
    pair_spec = pl.BlockSpec(*pair_block, pipeline_mode=pl.Buffered(1))
    y_conv = pl.pallas_call(
        _hy_conv_kernel,
        grid=(D_HYENA // LANES, FFT_STEPS),
        in_specs=[pl.BlockSpec(*pair_block),
                  pl.BlockSpec((FFT_K1_PER_STEP, 2 * FFT_N2, LANES), lambda cb, s: (s, 0, cb)),
                  _const_spec((2 * FFT_N1, FFT_N1)), step_mat_spec,
                  _const_spec((FFT_N1, 2 * FFT_N1))],
        out_specs=pair_spec,
        out_shape=pitched_shape,
        scratch_shapes=[pltpu.VMEM((n_pairs, FFT_N2 * FFT_PITCH, LANES), F32)],
        compiler_params=_cparams(("arbitrary", "arbitrary")),
        name="hy_conv",
    )(u_p, kf, mats["w1_data"], w2_all, mats["v1"])

    mix = pl.pallas_call(
        _merge_kernel,
        grid=(batch, n_tiles),
        in_specs=[tok(D_ATTN), tok(D_HYENA), pitched_spec, pitched_spec, tok(2 * d),
                  _const_spec((1, D_HYENA)), _const_spec((1, d)),
                  _const_spec((D_ATTN, d)), _const_spec((D_HYENA, d)), _const_spec((d, d))],
        out_specs=tok(d),
        out_shape=jax.ShapeDtypeStruct((batch, seq, d), BF16),
        compiler_params=_cparams(("arbitrary", "arbitrary")),
        name="merge",
    )(y_na, x0, u_p, y_conv, gates, row2(hy_skip[0]), row2(norm_mix_post[0]),
      w_o_na_b, w_o_hy_b, w_out_b)

    conv_w_c = ffn_conv_w[0]
    conv_b_c = row2(ffn_conv_b[0])
    mix_halo = 2 * HALO
    mix_blocks = TOKEN_TILE // mix_halo
    mix_prev = pl.BlockSpec((1, mix_halo, d), lambda b, i: (b, jnp.maximum(i * mix_blocks - 1, 0), 0))
    mix_next = pl.BlockSpec((1, mix_halo, d),
                            lambda b, i: (b, jnp.minimum((i + 1) * mix_blocks, seq // mix_halo - 1), 0))
    out = pl.pallas_call(
        _ffn_kernel,
        grid=(batch, n_tiles),
        in_specs=[tok(d), prev_spec(d), next_spec(d), tok(d), mix_prev, mix_next,
                  mod_spec, _const_spec((1, d)), _const_spec((1, d)),
                  _const_spec(w_up_c.shape), _const_spec(conv_w_c.shape), _const_spec(conv_b_c.shape),
                  _const_spec(w_dn_c.shape)],
        out_specs=tok(d),
        out_shape=jax.ShapeDtypeStruct((batch, seq, d), F32),
        scratch_shapes=[pltpu.VMEM((2, 2, FF_CHUNK // LANES, TOKEN_TILE + 2 * HALO, LANES), F32),
                        pltpu.VMEM((TOKEN_TILE, D_FF), BF16)],
        compiler_params=_cparams(("arbitrary", "arbitrary")),
        name="ffn",
    )(x, x, x, mix, mix, mix, mod_lat, row2(norm_ffn_pre[0]), row2(norm_ffn_post[0]),
      w_up_c, conv_w_c, conv_b_c, w_dn_c)
    return out
```

```python
import functools
import math

import jax
import jax.numpy as jnp
import numpy as np
from jax import lax
from jax.experimental import pallas as pl
from jax.experimental.pallas import tpu as pltpu

F32 = jnp.float32
BF16 = jnp.bfloat16

D_MODEL = 1024
N_HEADS = 8
HEAD_DIM = 64
D_ATTN = N_HEADS * HEAD_DIM
D_HYENA = 512
GRID_W = 64
WIN_ROWS = 8
WIN_COLS = 16
POS_BANDS = 16
POS_FEATS = 1 + 2 * POS_BANDS
FILTER_HIDDEN = 64
D_FF = 2816
N_MOD = 6
ROPE_BASE = 10000.0
RMS_EPS = 1e-6
NEG_BIAS = -1e30
LOG2_E = math.log2(math.e)

LANES = 128
VMEM_LIMIT_BYTES = 58 * 1024 * 1024

FFT_N1 = 64
FFT_N2 = 128
FFT_N = FFT_N1 * FFT_N2
FFT_HALF_N1 = FFT_N1 // 2
FFT_PITCH = 2 * FFT_N1 + 4
FFT_K1_PER_STEP = 32
FFT_UNROLL = 64
FFT_STEPS = FFT_N1 // FFT_K1_PER_STEP

TOKEN_TILE = 512
FILT_ROWS = 1024
SEQ_PITCH = FFT_N2 + 8
PITCHED_ROWS = FFT_HALF_N1 * SEQ_PITCH
TILE_GROUPS = TOKEN_TILE // FFT_N2
ATTN_ROWS_PER_STEP = 4
ATTN_Q = ATTN_ROWS_PER_STEP * GRID_W
ATTN_KEY_ROWS = 12
ATTN_KEYS = ATTN_KEY_ROWS * GRID_W
ATTN_KEY_CHUNK = 256
ATTN_GROUPS_PER_STEP = 4
FF_CHUNK = 768
LATER_WEIGHT_SLABS = (32, 16, 32, 32, 32)
HALO = 8
FFN_ROW_BLOCK = 64


def _cparams(sem):
    return pltpu.CompilerParams(dimension_semantics=sem, vmem_limit_bytes=VMEM_LIMIT_BYTES)


@functools.lru_cache(maxsize=None)
def _rope_tables(seq):
    pos = np.arange(seq)
    row = pos // GRID_W
    col = pos % GRID_W
    n_pairs = HEAD_DIM // 4
    inv = ROPE_BASE ** (-np.arange(n_pairs, dtype=np.float64) / n_pairs)
    lane = np.arange(LANES) % HEAD_DIM
    p = np.where(lane[None, :] < HEAD_DIM // 2, row[:, None], col[:, None]).astype(np.float64)
    ang = p * inv[lane % n_pairs][None, :]
    sign = np.where((lane % (2 * n_pairs)) < n_pairs, -1.0, 1.0)
    return np.cos(ang).astype(np.float32), (np.sin(ang) * sign[None, :]).astype(np.float32)


@functools.lru_cache(maxsize=None)
def _filter_tables(seq):
    assert 2 * seq == FFT_N
    half, n2, n1 = np.meshgrid(np.arange(2), np.arange(FFT_N2), np.arange(FFT_HALF_N1), indexing="ij")
    n = FFT_N2 * (half * FFT_HALF_N1 + n1) + n2
    fwd = n < seq
    m = n - seq
    valid = fwd | (m >= 1)
    pos = np.where(valid, np.where(fwd, n, seq - m), 0).astype(np.float64).reshape(2, seq)
    t = pos / max(seq - 1, 1)
    bands = np.linspace(1e-4, POS_BANDS - 1, POS_BANDS)
    ang = (2.0 * math.pi / seq) * pos[..., None] * bands
    z = np.zeros((2, seq, 64), np.float64)
    z[..., 0] = t
    z[..., 1:1 + POS_BANDS] = np.cos(ang)
    z[..., 1 + POS_BANDS:POS_FEATS] = -np.sin(ang)
    aux = np.zeros((2, seq, 8), np.float64)
    aux[..., 0] = t
    aux[..., 1] = valid.reshape(2, seq)
    zp = z.reshape(2, seq // FILT_ROWS, 2, FILT_ROWS // 2, 64).transpose(0, 1, 3, 2, 4).reshape(2, seq // 2, 128)
    return zp.astype(np.float32), aux.astype(np.float32)


def _realify(m):
    return np.block([[m.real, -m.imag], [m.imag, m.real]])


@functools.lru_cache(maxsize=None)
def _fft_matrices():
    n1 = np.arange(FFT_N1)
    n2 = np.arange(FFT_N2)
    k1 = np.arange(FFT_N1)
    k2 = np.arange(FFT_N2)
    f1 = np.exp(-2j * np.pi * np.outer(k1, n1) / FFT_N1)
    w1_data = _realify(f1[:, :FFT_HALF_N1])
    w1_real = np.concatenate([f1.real, f1.imag], axis=0)
    v1 = _realify(np.conj(f1.T)[:FFT_HALF_N1, :] / FFT_N)
    f2 = np.exp(-2j * np.pi * np.outer(k2, n2) / FFT_N2)
    tw = np.exp(-2j * np.pi * np.outer(k1, n2) / FFT_N)
    w2 = np.stack([_realify(f2 * tw[a][None, :]) for a in range(FFT_N1)])
    return {k: v.astype(np.float32) for k, v in
            dict(w1_data=w1_data, w1_real=w1_real, v1=v1, w2=w2).items()}


N_BIAS_ROWS = 2 * WIN_ROWS - 1
N_BIAS_COLS = 2 * WIN_COLS - 1


def _attn_bias_row_index():
    rows = GRID_W
    groups = (0, 2, rows // ATTN_ROWS_PER_STEP - 1)
    dr = np.full((3, ATTN_ROWS_PER_STEP, ATTN_KEY_ROWS), -1, np.int32)
    for v, g in enumerate(groups):
        ws = min(max(ATTN_ROWS_PER_STEP * g - WIN_ROWS // 2, 0), rows - ATTN_KEY_ROWS)
        for i in range(ATTN_ROWS_PER_STEP):
            r = ATTN_ROWS_PER_STEP * g + i
            r_start = min(max(r - WIN_ROWS // 2, 0), rows - WIN_ROWS)
            for j in range(ATTN_KEY_ROWS):
                kr = ws + j
                if r_start <= kr < r_start + WIN_ROWS:
                    dr[v, i, j] = kr - r + (WIN_ROWS - 1)
    return dr


def _attn_bias_kernel(rpb_ref, w_ref, o_ref, wb_ref, t_ref):
    wb_ref[...] = w_ref[...].astype(BF16)
    head = pl.program_id(0)
    kc = lax.broadcasted_iota(jnp.int32, (GRID_W, LANES), 0)
    lane = lax.broadcasted_iota(jnp.int32, (GRID_W, LANES), 1)
    qc = lane % GRID_W
    c_start = jnp.clip(qc - WIN_COLS // 2, 0, GRID_W - WIN_COLS)
    col_in = (kc >= c_start) & (kc < c_start + WIN_COLS)
    dc = jnp.clip(kc - qc, 1 - WIN_COLS, WIN_COLS - 1) + (WIN_COLS - 1)
    base = head * (N_BIAS_ROWS * N_BIAS_COLS)
    for r in range(N_BIAS_ROWS):
        t = jnp.full((GRID_W, LANES), NEG_BIAS, F32)
        for cidx in range(N_BIAS_COLS):
            t = jnp.where(col_in & (dc == cidx), rpb_ref[base + r * N_BIAS_COLS + cidx] * LOG2_E, t)
        t_ref[r] = t
    dr = _attn_bias_row_index()
    low_half = lane < GRID_W
    masked = jnp.full((GRID_W, LANES), NEG_BIAS, F32)
    for v in range(3):
        for i in range(0, ATTN_ROWS_PER_STEP, 2):
            for j in range(ATTN_KEY_ROWS):
                lo = t_ref[int(dr[v, i, j])] if dr[v, i, j] >= 0 else masked
                hi = t_ref[int(dr[v, i + 1, j])] if dr[v, i + 1, j] >= 0 else masked
                o_ref[v, 0, j * GRID_W:(j + 1) * GRID_W, i * GRID_W:(i + 2) * GRID_W] = jnp.where(
                    low_half, lo, hi)


def _mod_kernel(c_ref, w_ref, b_ref, o_ref):
    c = c_ref[...]
    s = c * jax.nn.sigmoid(c)
    o_ref[...] = jnp.dot(s, w_ref[...], precision=lax.Precision.HIGHEST,
                         preferred_element_type=F32) + b_ref[...]


def _norm_modulate(x, gain, shift, scale):
    ms = jnp.mean(x * x, axis=-1, keepdims=True)
    y = x * lax.rsqrt(ms + RMS_EPS) * gain
    return y * (1.0 + scale) + shift


def _rope(t, cos, sin_signed):
    n_pairs = HEAD_DIM // 4
    lane = lax.broadcasted_iota(jnp.int32, t.shape, 1)
    first = (lane % (2 * n_pairs)) < n_pairs
    partner = jnp.where(first, pltpu.roll(t, LANES - n_pairs, 1), pltpu.roll(t, n_pairs, 1))
    return t * cos + partner * sin_signed


def _in_proj_kernel(x_ref, prev_ref, next_ref, mod_ref, g_ref, w_ref, b_ref, cos_ref, sin_ref, cw_ref, cb_ref,
                    *rest):
    n_cast = len(LATER_WEIGHT_SLABS)
    cast_in, rest = rest[:n_cast], rest[n_cast:]
    (qr_ref, qp_ref, kr_ref, v_ref, gt_ref, u_ref, x0_ref), rest = rest[:7], rest[7:]
    cast_out, (hy_ref,) = rest[:n_cast], rest[n_cast:]
    for src, dst in zip(cast_in, cast_out):
        dst[...] = src[...].astype(BF16)
    i = pl.program_id(1)
    n_tiles = pl.num_programs(1)
    xx = jnp.concatenate([prev_ref[0], x_ref[0], next_ref[0]], axis=0)
    h_ext = _norm_modulate(xx, g_ref[...], mod_ref[0, 0:1, :], mod_ref[0, 1:2, :]).astype(BF16)
    h = h_ext[HALO:HALO + TOKEN_TILE]
    cos = cos_ref[...]
    sin = sin_ref[...]

    def proj(lo, hi, rows=h):
        return jnp.dot(rows, w_ref[:, lo:hi], preferred_element_type=F32) + b_ref[:, lo:hi]

    q = proj(0, D_ATTN) * (HEAD_DIM ** -0.5 * LOG2_E)
    qp_ref[0] = q.astype(BF16)
    k = proj(D_ATTN, 2 * D_ATTN)
    for c in range(D_ATTN // LANES):
        lanes = slice(c * LANES, (c + 1) * LANES)
        qr_ref[0, :, lanes] = _rope(q[:, lanes], cos, sin).astype(BF16)
        kr_ref[0, :, lanes] = _rope(k[:, lanes], cos, sin).astype(BF16)
    v_t = proj(2 * D_ATTN, 3 * D_ATTN).T.astype(BF16)
    for c in range(TOKEN_TILE // ATTN_KEY_CHUNK):
        v_ref[0, c] = v_t[:, c * ATTN_KEY_CHUNK:(c + 1) * ATTN_KEY_CHUNK]
    hy_lo = 3 * D_ATTN
    gl_lo = hy_lo + 3 * D_HYENA
    for c in range(4):
        w = D_MODEL // 2
        gt_ref[0, :, c * w:(c + 1) * w] = jax.nn.sigmoid(
            proj(gl_lo + c * w, gl_lo + (c + 1) * w)).astype(BF16)

    slabs_per_part = D_HYENA // LANES
    for c in range(3):
        hy = proj(hy_lo + c * D_HYENA, hy_lo + (c + 1) * D_HYENA, h_ext)
        for s in range(slabs_per_part):
            hy_ref[c * slabs_per_part + s] = hy[:, s * LANES:(s + 1) * LANES]
    zero_row = jnp.zeros((1, LANES), F32)

    @pl.when(i == 0)
    def _():
        for s in range(3 * slabs_per_part):
            hy_ref[s, HALO - 1:HALO, :] = zero_row

    @pl.when(i == n_tiles - 1)
    def _():
        for s in range(3 * slabs_per_part):
            hy_ref[s, HALO + TOKEN_TILE:HALO + TOKEN_TILE + 1, :] = zero_row

    def conv(s, r0, rows):
        lanes = slice(s * LANES, (s + 1) * LANES)
        tap = lambda j: hy_ref[s, pl.ds(HALO - 1 + j + r0, rows, stride=1), :]
        return (tap(0) * cw_ref[0:1, lanes] + tap(1) * cw_ref[1:2, lanes] + tap(2) * cw_ref[2:3, lanes]
                + cb_ref[:, lanes])

    for s in range(slabs_per_part):
        lanes = slice(s * LANES, (s + 1) * LANES)
        for j in range(TILE_GROUPS):
            r0 = j * FFT_N2
            x0_ref[0, r0:r0 + FFT_N2, lanes] = conv(s, r0, FFT_N2).astype(BF16)
            u_ref[0, j * SEQ_PITCH:j * SEQ_PITCH + FFT_N2, lanes] = (
                conv(slabs_per_part + s, r0, FFT_N2) * conv(2 * slabs_per_part + s, r0, FFT_N2))
            u_ref[0, j * SEQ_PITCH + FFT_N2:(j + 1) * SEQ_PITCH, lanes] = jnp.zeros((SEQ_PITCH - FFT_N2, LANES), F32)


def _ctx_kv_kernel(x_ref, mod_ref, g_ref, w_ref, b_ref, k_ref, v_ref):
    h = _norm_modulate(x_ref[0], g_ref[...], mod_ref[0:1, :], mod_ref[1:2, :]).astype(BF16)
    kv = jnp.dot(h, w_ref[...], preferred_element_type=F32) + b_ref[...]
    k_ref[0] = kv[:, :D_ATTN].astype(BF16)
    v_ref[0] = kv[:, D_ATTN:].T.astype(BF16)


def _attn_window_start(g):
    return jnp.clip(ATTN_ROWS_PER_STEP * g - WIN_ROWS // 2, 0, GRID_W - ATTN_KEY_ROWS)


def _attn_kernel(qr_ref, qp_ref, k_ref, vt_ref, kc_ref, vct_ref, bias_ref, o_ref):
    step = pl.program_id(1)
    n_groups = GRID_W // ATTN_ROWS_PER_STEP
    nt = (((1,), (1,)), ((), ()))
    quad_w = 4 * HEAD_DIM
    lane = lax.broadcasted_iota(jnp.int32, (1, quad_w), 1)
    zero = jnp.zeros((), BF16)

    def group_params(sub):
        g = step * ATTN_GROUPS_PER_STEP + sub
        win = _attn_window_start(g)
        key0 = pl.multiple_of(win * GRID_W, ATTN_KEY_CHUNK)
        chunk0 = win // (ATTN_KEY_CHUNK // GRID_W)
        variant = (g > 0).astype(jnp.int32) + (g == n_groups - 1).astype(jnp.int32)
        return key0, chunk0, variant

    params = [group_params(sub) for sub in range(ATTN_GROUPS_PER_STEP)]

    def scores(item):
        sub, head = item
        key0, _, variant = params[sub]
        qrows = slice(sub * ATTN_Q, (sub + 1) * ATTN_Q)
        quad, hh = divmod(head, 4)
        ql = slice(quad * quad_w, (quad + 1) * quad_w)
        mine = (lane >= hh * HEAD_DIM) & (lane < (hh + 1) * HEAD_DIM)
        s_nb = lax.dot_general(k_ref[0, pl.ds(key0, ATTN_KEYS), ql], jnp.where(mine, qr_ref[0, qrows, ql], zero), nt,
                               preferred_element_type=F32)
        s_cx = lax.dot_general(kc_ref[0, :, ql], jnp.where(mine, qp_ref[0, qrows, ql], zero), nt,
                               preferred_element_type=F32)
        return s_nb + bias_ref[variant, head], s_cx

    def with_ones(v):
        return jnp.concatenate([v, jnp.ones((16, v.shape[1]), BF16)], axis=0)

    def probs(s_nb, s_cx):
        m = jnp.maximum(jnp.max(s_nb, axis=0, keepdims=True), jnp.max(s_cx, axis=0, keepdims=True))
        return jnp.exp2(s_nb - m).astype(BF16), jnp.exp2(s_cx - m).astype(BF16)

    def values(item, p_nb, p_cx):
        sub, head = item
        chunk0 = params[sub][1]
        rows = slice(head * HEAD_DIM, (head + 1) * HEAD_DIM)
        v_win = jnp.concatenate([vt_ref[0, chunk0 + c, rows, :] for c in range(ATTN_KEYS // ATTN_KEY_CHUNK)],
                                axis=1)
        o = (jnp.dot(with_ones(v_win), p_nb, preferred_element_type=F32)
             + jnp.dot(with_ones(vct_ref[0, rows, :]), p_cx, preferred_element_type=F32))
        return o[:HEAD_DIM] / o[HEAD_DIM:HEAD_DIM + 1]

    items = [(sub, head) for sub in range(ATTN_GROUPS_PER_STEP) for head in range(N_HEADS)]
    outs = []
    s_q = {0: scores(items[0]), 1: scores(items[1])}
    p_q = {0: probs(*s_q.pop(0))}
    for n, item in enumerate(items):
        if n + 2 < len(items):
            s_q[n + 2] = scores(items[n + 2])
        if n + 1 < len(items):
            p_q[n + 1] = probs(*s_q.pop(n + 1))
        outs.append(values(item, *p_q.pop(n)))
    for sub in range(ATTN_GROUPS_PER_STEP):
        o_ref[0, sub * ATTN_Q:(sub + 1) * ATTN_Q, :] = jnp.concatenate(
            outs[sub * N_HEADS:(sub + 1) * N_HEADS], axis=0).T.astype(BF16)


def _filt_kernel(z_ref, aux_ref, w1_ref, b1_ref, w2_ref, b2_ref, w3_ref, b3_ref, freq_ref, decay_ref,
                 o_ref, asum_ref):
    i = pl.program_id(1)
    hp = lax.Precision.HIGHEST
    h = jnp.sin(freq_ref[0:1, :] * (jnp.dot(z_ref[0], w1_ref[...], precision=hp,
                                            preferred_element_type=F32) + b1_ref[...]))
    h = jnp.sin(freq_ref[1:2, :] * (jnp.dot(h, w2_ref[...], precision=hp,
                                            preferred_element_type=F32) + b2_ref[...]))
    taps = jnp.concatenate([jnp.dot(h, w3_ref[part], precision=hp, preferred_element_type=F32)
                            for part in range(2)], axis=0) + b3_ref[...]
    t = aux_ref[0, :, 0:1]
    valid = aux_ref[0, :, 1:2] > 0.5
    k = jnp.where(valid, taps * jnp.exp(-t * jnp.abs(decay_ref[0])), 0.0)
    part = jnp.sum(jnp.abs(k), axis=0, keepdims=True)

    @pl.when(i == 0)
    def _():
        asum_ref[...] = jnp.zeros_like(asum_ref)

    asum_ref[0] += jnp.broadcast_to(part, asum_ref.shape[1:])
    o_ref[...] = k.reshape(o_ref.shape)


def _fft_stage1(load_group, w1_ref, a_ref):
    n_slabs = a_ref.shape[0]

    def body(n2, carry):
        x = load_group(n2)
        a = jnp.dot(w1_ref[...], x, preferred_element_type=F32)
        for s in range(n_slabs):
            a_ref[s, pl.ds(n2 * FFT_PITCH, 2 * FFT_N1, stride=1), :] = a[:, s * LANES:(s + 1) * LANES]
        return carry

    lax.fori_loop(0, FFT_N2, body, 0, unroll=FFT_UNROLL)


def _fft_load_k1(a_ref, k1):
    parts = []
    for off in (0, FFT_N1):
        parts.append(jnp.concatenate(
            [a_ref[s, pl.ds(k1 + off, FFT_N2, stride=FFT_PITCH), :] for s in range(a_ref.shape[0])], axis=1))
    return jnp.concatenate(parts, axis=0)


def _filt_fft_kernel(k_ref, asum_ref, w1_ref, w2_ref, o_ref, a_ref):
    step = pl.program_id(1)

    @pl.when(step == 0)
    def _():
        norm = asum_ref[0, 0:1, :] + asum_ref[1, 0:1, :]

        def load_group(n2):
            return (k_ref[pl.ds(pl.multiple_of(n2 * FFT_N1, FFT_N1), FFT_N1), :] / norm).astype(BF16)
        _fft_stage1(load_group, w1_ref, a_ref)

    for j in range(FFT_K1_PER_STEP):
        b = _fft_load_k1(a_ref, step * FFT_K1_PER_STEP + j).astype(BF16)
        o_ref[j] = jnp.dot(w2_ref[j], b, preferred_element_type=F32).astype(BF16)


def _hy_conv_kernel(u_ref, kf_ref, w1_ref, w2_ref, v1_ref, o_ref, a_ref):
    step = pl.program_id(1)
    n_slabs = a_ref.shape[0]

    def tokens_of(member, s, n2):
        return (2 * s + member, pl.ds(n2, FFT_HALF_N1, stride=SEQ_PITCH), slice(None))

    @pl.when(step == 0)
    def _():
        for sample in range(o_ref.shape[0]):
            for grp in range(FFT_HALF_N1):
                o_ref[sample, grp * SEQ_PITCH + FFT_N2:(grp + 1) * SEQ_PITCH, :] = jnp.zeros(
                    (SEQ_PITCH - FFT_N2, LANES), F32)

        def load_group(n2):
            return jnp.concatenate(
                [jnp.concatenate([u_ref[tokens_of(member, s, n2)] for s in range(n_slabs)], axis=1)
                 for member in range(2)], axis=0).astype(BF16)
        _fft_stage1(load_group, w1_ref, a_ref)

    for j in range(FFT_K1_PER_STEP):
        k1 = step * FFT_K1_PER_STEP + j
        b = _fft_load_k1(a_ref, k1).astype(BF16)
        x = jnp.dot(w2_ref[j], b, preferred_element_type=F32)
        kf = kf_ref[j].astype(F32)
        kf = jnp.concatenate([kf] * n_slabs, axis=1)
        xr, xi = x[:FFT_N2], x[FFT_N2:]
        kr, ki = kf[:FFT_N2], kf[FFT_N2:]
        y = jnp.concatenate([xr * kr - xi * ki, xr * ki + xi * kr], axis=0).astype(BF16)
        d = lax.dot_general(w2_ref[j], y, (((0,), (0,)), ((), ())),
                            preferred_element_type=F32)
        for s in range(n_slabs):
            lanes = slice(s * LANES, (s + 1) * LANES)
            a_ref[s, pl.ds(k1, FFT_N2, stride=FFT_PITCH), :] = d[:FFT_N2, lanes]
            a_ref[s, pl.ds(k1 + FFT_N1, FFT_N2, stride=FFT_PITCH), :] = d[FFT_N2:, lanes]

    @pl.when(step == FFT_STEPS - 1)
    def _():
        def body(n2, carry):
            d = jnp.concatenate([a_ref[s, pl.ds(n2 * FFT_PITCH, 2 * FFT_N1, stride=1), :] for s in range(n_slabs)],
                                axis=1)
            y = jnp.dot(v1_ref[...], d.astype(BF16), preferred_element_type=F32)
            for s in range(n_slabs):
                for member in range(2):
                    o_ref[tokens_of(member, s, n2)] = y[member * FFT_HALF_N1:(member + 1) * FFT_HALF_N1,
                                                        s * LANES:(s + 1) * LANES]
            return carry

        lax.fori_loop(0, FFT_N2, body, 0, unroll=FFT_UNROLL)


def _from_pitched(ref):
    return jnp.concatenate([ref[0, j * SEQ_PITCH:j * SEQ_PITCH + FFT_N2, :] for j in range(TILE_GROUPS)], axis=0)


def _merge_kernel(yna_ref, x0_ref, u_ref, yc_ref, gt_ref, skip_ref, gpost_ref, wna_ref, why_ref, wout_ref, o_ref):
    y_hy = (x0_ref[0].astype(F32) * (_from_pitched(yc_ref) + _from_pitched(u_ref) * skip_ref[...])).astype(BF16)
    a = jnp.dot(yna_ref[0], wna_ref[...], preferred_element_type=F32)
    b = jnp.dot(y_hy, why_ref[...], preferred_element_type=F32)
    g_na = gt_ref[0, :, :D_MODEL].astype(F32)
    g_hy = gt_ref[0, :, D_MODEL:].astype(F32)
    m = (g_na * a + g_hy * b).astype(BF16)
    o = jnp.dot(m, wout_ref[...], preferred_element_type=F32)
    ms = jnp.mean(o * o, axis=-1, keepdims=True)
    o_ref[0] = (o * lax.rsqrt(ms + RMS_EPS) * gpost_ref[...]).astype(BF16)


def _gelu_tanh(a):
    return 0.5 * a * (1.0 + jnp.tanh(math.sqrt(2.0 / math.pi) * (a + 0.044715 * (a * a * a))))


def _ffn_kernel(x_ref, prev_ref, next_ref, mix_ref, mix_prev_ref, mix_next_ref, mod_ref, gpre_ref, gpost_ref,
                wup_ref, cw_ref, cb_ref, wdn_ref, o_ref, u_ref, act_ref):
    i = pl.program_id(1)
    n_tiles = pl.num_programs(1)
    mix_halo = mix_prev_ref.shape[1]
    mix = jnp.concatenate([mix_prev_ref[0, mix_halo - HALO:, :], mix_ref[0], mix_next_ref[0, :HALO, :]], axis=0)
    xx = (jnp.concatenate([prev_ref[0], x_ref[0], next_ref[0]], axis=0)
          + mod_ref[0, 2:3, :] * mix.astype(F32))
    x = xx[HALO:HALO + TOKEN_TILE]
    h = _norm_modulate(xx, gpre_ref[...], mod_ref[0, 3:4, :], mod_ref[0, 4:5, :])
    row = lax.broadcasted_iota(jnp.int32, (xx.shape[0], 1), 0)
    inside = ((row >= HALO) | (i > 0)) & ((row < HALO + TOKEN_TILE) | (i < n_tiles - 1))
    h = jnp.where(inside, h, 0.0).astype(BF16)
    starts = list(range(0, D_FF, FF_CHUNK))
    widths = [min(FF_CHUNK, D_FF - lo) for lo in starts]

    def conv(buf, half, s, lo, r0):
        lanes = slice(half * D_FF + lo, half * D_FF + lo + LANES)
        tap = lambda j: u_ref[buf, half, s, pl.ds(HALO - 1 + j + r0, FFN_ROW_BLOCK, stride=1), :]
        return (tap(0) * cw_ref[0:1, lanes] + tap(1) * cw_ref[1:2, lanes] + tap(2) * cw_ref[2:3, lanes]
                + cb_ref[:, lanes])

    def up_project(c):
        for half in range(2):
            lo = half * D_FF + starts[c]
            u = jnp.dot(h, wup_ref[:, lo:lo + widths[c]], preferred_element_type=F32)
            for s in range(widths[c] // LANES):
                u_ref[c % 2, half, s] = u[:, s * LANES:(s + 1) * LANES]

    up_project(0)
    for c in range(len(starts)):
        buf = c % 2
        if c + 1 < len(starts):
            up_project(c + 1)
        for s in range(widths[c] // LANES):
            lo = starts[c] + s * LANES
            for r0 in range(0, TOKEN_TILE, FFN_ROW_BLOCK):
                act_ref[r0:r0 + FFN_ROW_BLOCK, lo:lo + LANES] = (
                    _gelu_tanh(conv(buf, 0, s, lo, r0)) * conv(buf, 1, s, lo, r0)).astype(BF16)
    y = jnp.dot(act_ref[...], wdn_ref[...], preferred_element_type=F32)
    ms = jnp.mean(y * y, axis=-1, keepdims=True)
    y = y * lax.rsqrt(ms + RMS_EPS) * gpost_ref[...]
    o_ref[0] = x + mod_ref[0, 5:6, :] * y


def _const_spec(shape):
    nd = len(shape)
    return pl.BlockSpec(shape, lambda *_: (0,) * nd, pipeline_mode=pl.Buffered(1))


def kernel(x, c, ctx, c_ctx, w_mod, b_mod, norm_mix_pre, norm_mix_post, norm_ffn_pre, norm_ffn_post, w_in, b_in, na_rpb, hy_conv_w, hy_conv_b, hy_filt_w1, hy_filt_b1, hy_filt_w2, hy_filt_b2, hy_filt_w3, hy_filt_b3, hy_sin_freq, hy_decay, hy_skip, w_o_na, w_o_hy, w_out, ffn_w_up, ffn_conv_w, ffn_conv_b, ffn_w_down):
    batch, seq, d = x.shape
    n_ctx = ctx.shape[1]
    assert d == D_MODEL and 2 * seq == FFT_N and seq == GRID_W * GRID_W and batch % 2 == 0
    assert w_mod.shape[0] == 1, "single-layer block"
    n_tiles = seq // TOKEN_TILE
    d_in = w_in.shape[2]
    row2 = lambda a: a.reshape(1, -1)

    c_all = jnp.zeros((8, d), F32).at[:batch].set(c).at[batch].set(c_ctx)
    mod_n = 1024
    mod = pl.pallas_call(
        _mod_kernel,
        grid=(N_MOD * d // mod_n,),
        in_specs=[_const_spec((8, d)),
                  pl.BlockSpec((d, mod_n), lambda j: (0, j)),
                  pl.BlockSpec((1, mod_n), lambda j: (0, j))],
        out_specs=pl.BlockSpec((8, mod_n), lambda j: (0, j)),
        out_shape=jax.ShapeDtypeStruct((8, N_MOD * d), F32),
        compiler_params=_cparams(("arbitrary",)),
        name="mod",
    )(c_all, w_mod[0], row2(b_mod[0]))
    mod_lat = jnp.pad(mod[:batch].reshape(batch, N_MOD, d), ((0, 0), (0, 8 - N_MOD), (0, 0)))
    mod_ctx = jnp.pad(mod[batch].reshape(N_MOD, d), ((0, 8 - N_MOD), (0, 0)))

    w_rows = d // N_HEADS
    bias, w_in_b = pl.pallas_call(
        _attn_bias_kernel,
        grid=(N_HEADS,),
        in_specs=[pl.BlockSpec(memory_space=pltpu.SMEM), pl.BlockSpec((w_rows, d_in), lambda h: (h, 0))],
        out_specs=[pl.BlockSpec((3, 1, ATTN_KEYS, ATTN_Q), lambda h: (0, h, 0, 0)),
                   pl.BlockSpec((w_rows, d_in), lambda h: (h, 0))],
        out_shape=[jax.ShapeDtypeStruct((3, N_HEADS, ATTN_KEYS, ATTN_Q), F32),
                   jax.ShapeDtypeStruct((d, d_in), BF16)],
        scratch_shapes=[pltpu.VMEM((N_BIAS_ROWS, GRID_W, LANES), F32)],
        compiler_params=_cparams(("arbitrary",)),
        name="attn_bias",
    )(na_rpb[0].reshape(-1), w_in[0])
    b_in_r = row2(b_in[0])
    g_mix_pre = row2(norm_mix_pre[0])

    k_ctx, v_ctx = pl.pallas_call(
        _ctx_kv_kernel,
        grid=(batch,),
        in_specs=[pl.BlockSpec((1, n_ctx, d), lambda b: (b, 0, 0)),
                  _const_spec((8, d)), _const_spec((1, d)),
                  _const_spec((d, 2 * D_ATTN)), _const_spec((1, 2 * D_ATTN))],
        out_specs=[pl.BlockSpec((1, n_ctx, D_ATTN), lambda b: (b, 0, 0)),
                   pl.BlockSpec((1, D_ATTN, n_ctx), lambda b: (b, 0, 0))],
        out_shape=[jax.ShapeDtypeStruct((batch, n_ctx, D_ATTN), BF16),
                   jax.ShapeDtypeStruct((batch, D_ATTN, n_ctx), BF16)],
        compiler_params=_cparams(("arbitrary",)),
        name="ctx_kv",
    )(ctx, mod_ctx, g_mix_pre, w_in_b[:, D_ATTN:3 * D_ATTN], b_in_r[:, D_ATTN:3 * D_ATTN])

    cos_t, sin_t = _rope_tables(seq)
    tok = lambda w: pl.BlockSpec((1, TOKEN_TILE, w), lambda b, i: (b, i, 0))
    mod_spec = pl.BlockSpec((1, 8, d), lambda b, i: (b, 0, 0))
    rope_spec = pl.BlockSpec((TOKEN_TILE, LANES), lambda b, i: (i, 0))
    halo_blocks = TOKEN_TILE // HALO
    n_halo_blocks = seq // HALO
    prev_spec = lambda w: pl.BlockSpec((1, HALO, w), lambda b, i: (b, jnp.maximum(i * halo_blocks - 1, 0), 0))
    next_spec = lambda w: pl.BlockSpec(
        (1, HALO, w), lambda b, i: (b, jnp.minimum((i + 1) * halo_blocks, n_halo_blocks - 1), 0))
    pitched_spec = pl.BlockSpec((1, TILE_GROUPS * SEQ_PITCH, D_HYENA), lambda b, i: (b, i, 0))
    pitched_shape = jax.ShapeDtypeStruct((batch, PITCHED_ROWS, D_HYENA), F32)
    later_weights = [ffn_w_up[0], ffn_w_down[0], w_out[0], w_o_na[0], w_o_hy[0]]
    slab_specs = [
        pl.BlockSpec((w.shape[0] // n, w.shape[1]), lambda b, i, n=n: (jnp.minimum(b * n_tiles + i, n - 1), 0))
        for w, n in zip(later_weights, LATER_WEIGHT_SLABS)]
    q_rot, q_plain, k_rot, v_lat, gates, u_p, x0, w_up_c, w_dn_c, w_out_b, w_o_na_b, w_o_hy_b = pl.pallas_call(
        _in_proj_kernel,
        grid=(batch, n_tiles),
        in_specs=[tok(d), prev_spec(d), next_spec(d), mod_spec, _const_spec((1, d)), _const_spec((d, d_in)),
                  _const_spec((1, d_in)), rope_spec, rope_spec,
                  _const_spec((3, 3 * D_HYENA)), _const_spec((1, 3 * D_HYENA))] + slab_specs,
        out_specs=[tok(D_ATTN)] * 3
        + [pl.BlockSpec((1, TOKEN_TILE // ATTN_KEY_CHUNK, D_ATTN, ATTN_KEY_CHUNK), lambda b, i: (b, i, 0, 0)),
           tok(2 * d), pitched_spec, tok(D_HYENA)] + slab_specs,
        out_shape=[jax.ShapeDtypeStruct((batch, seq, D_ATTN), BF16)] * 3
        + [jax.ShapeDtypeStruct((batch, seq // ATTN_KEY_CHUNK, D_ATTN, ATTN_KEY_CHUNK), BF16),
           jax.ShapeDtypeStruct((batch, seq, 2 * d), BF16), pitched_shape,
           jax.ShapeDtypeStruct((batch, seq, D_HYENA), BF16)]
        + [jax.ShapeDtypeStruct(w.shape, BF16) for w in later_weights],
        scratch_shapes=[pltpu.VMEM((3 * D_HYENA // LANES, TOKEN_TILE + 2 * HALO, LANES), F32)],
        compiler_params=_cparams(("arbitrary", "arbitrary")),
        name="in_proj",
    )(x, x, x, mod_lat, g_mix_pre, w_in_b, b_in_r, jnp.asarray(cos_t), jnp.asarray(sin_t),
      hy_conv_w[0], row2(hy_conv_b[0]), *later_weights)

    n_steps = GRID_W // ATTN_ROWS_PER_STEP // ATTN_GROUPS_PER_STEP
    q_spec = pl.BlockSpec((1, ATTN_GROUPS_PER_STEP * ATTN_Q, D_ATTN), lambda b, g: (b, g, 0))
    full = lambda n: pl.BlockSpec((1, n, D_ATTN), lambda b, g: (b, 0, 0))
    bias_spec = _const_spec((3, N_HEADS, ATTN_KEYS, ATTN_Q))
    vt_spec = pl.BlockSpec((1, seq // ATTN_KEY_CHUNK, D_ATTN, ATTN_KEY_CHUNK), lambda b, g: (b, 0, 0, 0))
    vct_spec = pl.BlockSpec((1, D_ATTN, n_ctx), lambda b, g: (b, 0, 0))
    y_na = pl.pallas_call(
        _attn_kernel,
        grid=(batch, n_steps),
        in_specs=[q_spec, q_spec, full(seq), vt_spec, full(n_ctx), vct_spec, bias_spec],
        out_specs=q_spec,
        out_shape=jax.ShapeDtypeStruct((batch, seq, D_ATTN), BF16),
        compiler_params=_cparams(("arbitrary", "arbitrary")),
        name="attn",
    )(q_rot, q_plain, k_rot, v_lat, k_ctx, v_ctx, bias)

    z_t, aux_t = _filter_tables(seq)
    filt_n2 = FILT_ROWS // FFT_HALF_N1
    hid = FILTER_HIDDEN
    w1_pad = jnp.pad(hy_filt_w1[0], ((0, hid - POS_FEATS), (0, 0)))
    block_diag = lambda w: jnp.zeros((2 * hid, 2 * hid), F32).at[:hid, :hid].set(w).at[hid:, hid:].set(w)
    twice = lambda v: jnp.tile(v, (1, 2))
    w3 = hy_filt_w3[0]
    w3_parts = jnp.stack([jnp.concatenate([w3, jnp.zeros_like(w3)], axis=0),
                          jnp.concatenate([jnp.zeros_like(w3), w3], axis=0)])
    k_circ, k_asum = pl.pallas_call(
        _filt_kernel,
        grid=(2, seq // FILT_ROWS),
        in_specs=[pl.BlockSpec((1, FILT_ROWS // 2, 2 * hid), lambda hf, i: (hf, i, 0)),
                  pl.BlockSpec((1, FILT_ROWS, aux_t.shape[2]), lambda hf, i: (hf, i, 0)),
                  _const_spec((2 * hid, 2 * hid)), _const_spec((1, 2 * hid)),
                  _const_spec((2 * hid, 2 * hid)), _const_spec((1, 2 * hid)),
                  pl.BlockSpec((2, 2 * hid, D_HYENA), lambda hf, i: (0, 0, hf)),
                  pl.BlockSpec((1, D_HYENA), lambda hf, i: (0, hf)),
                  _const_spec((2, 2 * hid)),
                  pl.BlockSpec((1, 1, D_HYENA), lambda hf, i: (hf, 0, 0))],
        out_specs=[pl.BlockSpec((filt_n2, FFT_HALF_N1, D_HYENA), lambda hf, i: (i, hf, 0)),
                   pl.BlockSpec((1, 8, D_HYENA), lambda hf, i: (hf, 0, 0))],
        out_shape=[jax.ShapeDtypeStruct((FFT_N2, FFT_N1, D_HYENA), F32),
                   jax.ShapeDtypeStruct((2, 8, D_HYENA), F32)],
        compiler_params=_cparams(("arbitrary", "arbitrary")),
        name="filt",
    )(jnp.asarray(z_t), jnp.asarray(aux_t), block_diag(w1_pad), twice(row2(hy_filt_b1[0])),
      block_diag(hy_filt_w2[0]), twice(row2(hy_filt_b2[0])), w3_parts, row2(hy_filt_b3[0]),
      twice(hy_sin_freq[0]), hy_decay[0].reshape(2, 1, D_HYENA))
    k_circ = k_circ.reshape(FFT_N, D_HYENA)

    mats = _fft_matrices()
    mats = {k: jnp.asarray(v).astype(BF16) for k, v in mats.items()}
    w2_all = mats["w2"]
    step_mat_spec = pl.BlockSpec((FFT_K1_PER_STEP, 2 * FFT_N2, 2 * FFT_N2), lambda cb, s: (s, 0, 0))
    filt_slabs = 2
    kf = pl.pallas_call(
        _filt_fft_kernel,
        grid=(D_HYENA // (filt_slabs * LANES), FFT_STEPS),
        in_specs=[pl.BlockSpec((FFT_N, filt_slabs * LANES), lambda cb, s: (0, cb)),
                  pl.BlockSpec((2, 8, filt_slabs * LANES), lambda cb, s: (0, 0, cb)),
                  _const_spec((2 * FFT_N1, FFT_N1)), step_mat_spec],
        out_specs=pl.BlockSpec((FFT_K1_PER_STEP, 2 * FFT_N2, filt_slabs * LANES), lambda cb, s: (s, 0, cb)),
        out_shape=jax.ShapeDtypeStruct((FFT_N1, 2 * FFT_N2, D_HYENA), BF16),
        scratch_shapes=[pltpu.VMEM((filt_slabs, FFT_N2 * FFT_PITCH, LANES), F32)],
        compiler_params=_cparams(("arbitrary", "arbitrary")),
        name="filt_fft",
    )(k_circ, k_asum, mats["w1_real"], w2_all)

    n_pairs = batch // 2
    pair_block = ((batch, PITCHED_ROWS, LANES), lambda cb, s: (0, 0, cb))
    pair_spec = pl.BlockSpec(*pair_block, pipeline_mode=pl.Buffered(1))
    y_conv = pl.pallas_call(
        _hy_conv_kernel,
        grid=(D_HYENA // LANES, FFT_STEPS),
        in_specs=[pl.BlockSpec(*pair_block),
                  pl.BlockSpec((FFT_K1_PER_STEP, 2 * FFT_N2, LANES), lambda cb, s: (s, 0, cb)),
                  _const_spec((2 * FFT_N1, FFT_N1)), step_mat_spec,
                  _const_spec((FFT_N1, 2 * FFT_N1))],
        out_specs=pair_spec,
        out_shape=pitched_shape,
        scratch_shapes=[pltpu.VMEM((n_pairs, FFT_N2 * FFT_PITCH, LANES), F32)],
        compiler_params=_cparams(("arbitrary", "arbitrary")),
        name="hy_conv",
    )(u_p, kf, mats["w1_data"], w2_all, mats["v1"])

    mix = pl.pallas_call(
        _merge_kernel,
        grid=(batch, n_tiles),
        in_specs=[tok(D_ATTN), tok(D_HYENA), pitched_spec, pitched_spec, tok(2 * d),
                  _const_spec((1, D_HYENA)), _const_spec((1, d)),
                  _const_spec((D_ATTN, d)), _const_spec((D_HYENA, d)), _const_spec((d, d))],
        out_specs=tok(d),
        out_shape=jax.ShapeDtypeStruct((batch, seq, d), BF16),
        compiler_params=_cparams(("arbitrary", "arbitrary")),
        name="merge",
    )(y_na, x0, u_p, y_conv, gates, row2(hy_skip[0]), row2(norm_mix_post[0]),
      w_o_na_b, w_o_hy_b, w_out_b)

    conv_w_c = ffn_conv_w[0]
    conv_b_c = row2(ffn_conv_b[0])
    mix_halo = 2 * HALO
    mix_blocks = TOKEN_TILE // mix_halo
    mix_prev = pl.BlockSpec((1, mix_halo, d), lambda b, i: (b, jnp.maximum(i * mix_blocks - 1, 0), 0))
    mix_next = pl.BlockSpec((1, mix_halo, d),
                            lambda b, i: (b, jnp.minimum((i + 1) * mix_blocks, seq // mix_halo - 1), 0))
    out = pl.pallas_call(
        _ffn_kernel,
        grid=(batch, n_tiles),
        in_specs=[tok(d), prev_spec(d), next_spec(d), tok(d), mix_prev, mix_next,
                  mod_spec, _const_spec((1, d)), _const_spec((1, d)),
                  _const_spec(w_up_c.shape), _const_spec(conv_w_c.shape), _const_spec(conv_b_c.shape),
                  _const_spec(w_dn_c.shape)],
        out_specs=tok(d),
        out_shape=jax.ShapeDtypeStruct((batch, seq, d), F32),
        scratch_shapes=[pltpu.VMEM((2, 2, FF_CHUNK // LANES, TOKEN_TILE + 2 * HALO, LANES), F32),
                        pltpu.VMEM((TOKEN_TILE, D_FF), BF16)],
        compiler_params=_cparams(("arbitrary", "arbitrary")),
        name="ffn",
    )(x, x, x, mix, mix, mix, mod_lat, row2(norm_ffn_pre[0]), row2(norm_ffn_post[0]),
      w_up_c, conv_w_c, conv_b_c, w_dn_c)
    return out
```
